```python
import jax
import jax.numpy as jnp
from jax import lax
import numpy as np

D_MODEL = 1024
BATCH = 4
SEQ = 8192
DEPTH = 4

GRID_W = 64
CTX_LEN = 256
N_MIXERS = 3
NORM_EPS = 1e-6
N_MOD = 6

NA_HEADS = 16
NA_HEAD_DIM = D_MODEL // NA_HEADS
NA_WIN_ROWS = 8
NA_WIN_COLS = 16

POOL_WINDOWS = (2, 4, 8, 16)
POOL_GROUPS = len(POOL_WINDOWS)
POOL_GROUP_DIM = D_MODEL // POOL_GROUPS

MLSTM_INNER = 2 * D_MODEL
MLSTM_HEADS = 4
MLSTM_HEAD_DIM = MLSTM_INNER // MLSTM_HEADS
MLSTM_CONV = 4
MLSTM_QKV_BLOCK = 4
MLSTM_QKV_NBLK = MLSTM_INNER // MLSTM_QKV_BLOCK
MLSTM_CHUNK = 64

FFN_DENSE = 2816
N_EXPERTS = 8
TOP_K = 2
FFN_EXPERT = 3584
MOE_BLOCK = 128

N_NA = (DEPTH + 2) // 3
N_POOL = (DEPTH + 1) // 3
N_MLSTM = DEPTH // 3
N_DENSE = (DEPTH + 1) // 2
N_MOE = DEPTH // 2

kernel_name = 'hybrid_natten_pool_mlstm_moe_flow_trunk'


def rmsnorm(x, g):
    xf = x.astype(jnp.float32)
    y = xf * lax.rsqrt(jnp.mean(xf * xf, axis=-1, keepdims=True) + NORM_EPS)
    return (y * g.astype(jnp.float32)).astype(x.dtype)


def adaln(cond, w, b):
    m = jax.nn.silu(cond) @ w + b
    m = m.reshape(m.shape[:-1] + (N_MOD, D_MODEL))
    return [m[..., j, :][..., None, :] for j in range(N_MOD)]


def modulate(h, shift, scale):
    return h * (1.0 + scale) + shift


def swiglu(h, w1, w3, w2):
    return (jax.nn.silu(h @ w1) * (h @ w3)) @ w2


def context_attention(q, k, v):
    s = jnp.einsum('bqhd,bkhd->bhqk', q, k).astype(jnp.float32) * (NA_HEAD_DIM ** -0.5)
    p = jax.nn.softmax(s, axis=-1).astype(v.dtype)
    return jnp.einsum('bhqk,bkhd->bqhd', p, v)


def neighbourhood_attention(q, k, v, k_ctx, v_ctx, rpb):
    B, L, H, dh = q.shape
    rows = L // GRID_W
    kh = min(NA_WIN_ROWS, rows)
    kw = NA_WIN_COLS
    grid = lambda t: t.reshape(B, rows, GRID_W, H, dh)
    qg, kg, vg = grid(q), grid(k), grid(v)
    col = jnp.arange(GRID_W)
    c0 = jnp.clip(col - kw // 2, 0, GRID_W - kw)
    col_ok = (col[None, :] >= c0[:, None]) & (col[None, :] < c0[:, None] + kw)
    dcol = jnp.clip(col[None, :] - col[:, None], 1 - kw, kw - 1) + (NA_WIN_COLS - 1)
    scale = NA_HEAD_DIM ** -0.5

    def row_block(r):
        r0 = jnp.clip(r - kh // 2, 0, rows - kh)
        q_r = lax.dynamic_index_in_dim(qg, r, axis=1, keepdims=False)
        k_r = lax.dynamic_slice_in_dim(kg, r0, kh, axis=1)
        v_r = lax.dynamic_slice_in_dim(vg, r0, kh, axis=1)
        drow = r0 + jnp.arange(kh) - r + (NA_WIN_ROWS - 1)
        bias = rpb[:, drow[None, :, None], dcol[:, None, :]]
        s_loc = jnp.einsum('bqhd,bikhd->bhqik', q_r, k_r).astype(jnp.float32) * scale + bias.astype(jnp.float32)
        s_loc = jnp.where(col_ok[:, None, :], s_loc, -jnp.inf).reshape(B, H, GRID_W, kh * GRID_W)
        s_ctx = jnp.einsum('bqhd,bchd->bhqc', q_r, k_ctx).astype(jnp.float32) * scale
        p = jax.nn.softmax(jnp.concatenate([s_loc, s_ctx], axis=-1), axis=-1).astype(v.dtype)
        p_loc = p[..., :kh * GRID_W].reshape(B, H, GRID_W, kh, GRID_W)
        return (jnp.einsum('bhqik,bikhd->bqhd', p_loc, v_r)
                + jnp.einsum('bhqc,bchd->bqhd', p[..., kh * GRID_W:], v_ctx))

    out = lax.map(row_block, jnp.arange(rows))
    return jnp.moveaxis(out, 0, 1).reshape(B, L, H, dh)


def na_mixer(a_lat, a_ctx, w_qkv, b_qkv, rpb, w_out, b_out, with_ctx_out):
    B, L, D = a_lat.shape
    C = a_ctx.shape[1]
    qkv = (a_lat @ w_qkv + b_qkv).reshape(B, L, 3, NA_HEADS, NA_HEAD_DIM)
    kv_c = (a_ctx @ w_qkv[:, D:] + b_qkv[D:]).reshape(B, C, 2, NA_HEADS, NA_HEAD_DIM)
    k_c, v_c = kv_c[:, :, 0], kv_c[:, :, 1]
    o_lat = neighbourhood_attention(qkv[:, :, 0], qkv[:, :, 1], qkv[:, :, 2], k_c, v_c, rpb)
    y_lat = o_lat.reshape(B, L, D) @ w_out + b_out
    if not with_ctx_out:
        return y_lat, None
    q_c = (a_ctx @ w_qkv[:, :D] + b_qkv[:D]).reshape(B, C, NA_HEADS, NA_HEAD_DIM)
    o_ctx = context_attention(q_c, k_c, v_c)
    return y_lat, o_ctx.reshape(B, C, D) @ w_out + b_out


def multiscale_pool(h, w_pool, pool_scale):
    B, L, D = h.shape
    hf = h.astype(jnp.float32)
    csum = jnp.concatenate([jnp.zeros((B, 1, D), jnp.float32), jnp.cumsum(hf, axis=1)], axis=1)
    t = jnp.arange(L)
    parts = []
    for g, w in enumerate(POOL_WINDOWS):
        lo = jnp.maximum(t - w // 2, 0)
        hi = jnp.minimum(t + w // 2, L)
        sl = slice(g * POOL_GROUP_DIM, (g + 1) * POOL_GROUP_DIM)
        cg = csum[..., sl]
        mean = (jnp.take(cg, hi, axis=1) - jnp.take(cg, lo, axis=1)) / (hi - lo).astype(jnp.float32)[None, :, None]
        parts.append(mean - hf[..., sl])
    pooled = jnp.stack(parts, axis=2).astype(h.dtype)
    y = jnp.einsum('blgc,gcd->blgd', pooled, w_pool).reshape(B, L, D)
    return y * pool_scale


def mlstm_features(h, w_up, conv_w, conv_b, w_q, w_k, w_v, w_gates, b_gates):
    B, L, _ = h.shape
    xm, z = jnp.split(h @ w_up, 2, axis=-1)
    left = MLSTM_CONV // 2
    xp = jnp.pad(xm, ((0, 0), (left, MLSTM_CONV - 1 - left), (0, 0)))
    xc = conv_b
    for j in range(MLSTM_CONV):
        xc = xc + xp[:, j:j + L] * conv_w[j]
    xc = jax.nn.silu(xc)

    def headwise(t, w):
        tb = t.reshape(B, L, MLSTM_QKV_NBLK, MLSTM_QKV_BLOCK)
        return jnp.einsum('blnc,ncd->blnd', tb, w).reshape(B, L, MLSTM_INNER)

    q = headwise(xc, w_q)
    k = headwise(xc, w_k)
    v = headwise(xm, w_v)
    gates = (jnp.concatenate([q, k, v], axis=-1) @ w_gates + b_gates).reshape(B, L, 4, MLSTM_HEADS)
    heads = lambda t: t.reshape(B, L, MLSTM_HEADS, MLSTM_HEAD_DIM)
    return heads(q), heads(k), heads(v), gates, xc, z


def mlstm_zero_state(B):
    return (jnp.zeros((B, MLSTM_HEADS, MLSTM_HEAD_DIM, MLSTM_HEAD_DIM), jnp.float32),
            jnp.zeros((B, MLSTM_HEADS, MLSTM_HEAD_DIM), jnp.float32),
            jnp.zeros((B, MLSTM_HEADS), jnp.float32))


def mlstm_scan(q, k, v, i_pre, f_pre, state):
    B, L, NH, dh = q.shape
    T = MLSTM_CHUNK
    nc = L // T
    f32 = jnp.float32

    def chunks(t):
        return jnp.moveaxis(t.astype(f32).reshape((B, nc, T) + t.shape[2:]), 1, 0)

    xs = (chunks(q), chunks(k) * (dh ** -0.5), chunks(v), chunks(i_pre),
          chunks(jax.nn.log_sigmoid(f_pre.astype(f32))))
    causal = jnp.tril(jnp.ones((T, T), dtype=bool))

    def step(carry, inp):
        C, n, m = carry
        qc, kc, vc, ic, lfc = inp
        b = jnp.moveaxis(jnp.cumsum(lfc, axis=1), 1, 2)
        ig = jnp.moveaxis(ic, 1, 2)
        dmat = jnp.where(causal, b[..., :, None] - b[..., None, :] + ig[..., None, :], -jnp.inf)
        inter = b + m[..., None]
        m_t = jnp.maximum(inter, jnp.max(dmat, axis=-1))
        a = jnp.einsum('bthd,bshd->bhts', qc, kc) * jnp.exp(dmat - m_t[..., None])
        w_int = jnp.exp(inter - m_t)
        num = (jnp.einsum('bhts,bshd->bthd', a, vc)
               + jnp.einsum('bhde,bthe->bthd', C, qc) * jnp.moveaxis(w_int, 1, 2)[..., None])
        den = jnp.moveaxis(jnp.sum(a, axis=-1) + w_int * jnp.einsum('bhe,bthe->bht', n, qc), 1, 2)
        floor = jnp.exp(-jnp.moveaxis(m_t, 1, 2))
        hc = num / jnp.maximum(jnp.abs(den), floor)[..., None]
        b_end = b[..., -1]
        g = b_end[..., None] - b + ig
        m_new = jnp.maximum(b_end + m, jnp.max(g, axis=-1))
        decay = jnp.exp(b_end + m - m_new)
        wg = jnp.moveaxis(jnp.exp(g - m_new[..., None]), 1, 2)[..., None]
        C_new = decay[..., None, None] * C + jnp.einsum('bshd,bshe->bhde', vc * wg, kc)
        n_new = decay[..., None] * n + jnp.sum(kc * wg, axis=1)
        return (C_new, n_new, m_new), hc

    state, hs = lax.scan(step, state, xs)
    h = jnp.moveaxis(hs, 0, 1).reshape(B, L, NH, dh).astype(q.dtype)
    return h, state


def mlstm_bidir(feats, st_fwd, st_bwd):
    q, k, v, gates = feats[0], feats[1], feats[2], feats[3]
    flip = lambda t: jnp.flip(t, axis=1)
    h_f, st_f = mlstm_scan(q, k, v, gates[:, :, 0], gates[:, :, 1], st_fwd)
    h_b, st_b = mlstm_scan(flip(q), flip(k), flip(v), flip(gates[:, :, 2]), flip(gates[:, :, 3]), st_bwd)
    return h_f + flip(h_b), st_f, st_b


def mlstm_output(h, xc, z, gn_w, skip, w_down):
    B, L = h.shape[0], h.shape[1]
    hf = h.astype(jnp.float32)
    mu = jnp.mean(hf, axis=-1, keepdims=True)
    var = jnp.mean(jnp.square(hf - mu), axis=-1, keepdims=True)
    hn = ((hf - mu) * lax.rsqrt(var + NORM_EPS)).reshape(B, L, MLSTM_INNER).astype(xc.dtype) * gn_w
    return ((hn + skip * xc) * jax.nn.silu(z)) @ w_down


def mlstm_mixer(a_lat, a_ctx, w_up, conv_w, conv_b, w_q, w_k, w_v, w_gates, b_gates,
                gn_w, skip, w_down, with_ctx_out):
    f_ctx = mlstm_features(a_ctx, w_up, conv_w, conv_b, w_q, w_k, w_v, w_gates, b_gates)
    f_lat = mlstm_features(a_lat, w_up, conv_w, conv_b, w_q, w_k, w_v, w_gates, b_gates)
    z0 = mlstm_zero_state(a_lat.shape[0])
    h_ctx, st_f, st_b = mlstm_bidir(f_ctx, z0, z0)
    h_lat, _, _ = mlstm_bidir(f_lat, st_f, st_b)
    y_lat = mlstm_output(h_lat, f_lat[4], f_lat[5], gn_w, skip, w_down)
    if not with_ctx_out:
        return y_lat, None
    return y_lat, mlstm_output(h_ctx, f_ctx[4], f_ctx[5], gn_w, skip, w_down)


def moe_swiglu(h, w_router, w1, w3, w2):
    N, D = h.shape
    logits = (h @ w_router).astype(jnp.float32)
    top_v, top_e = lax.top_k(logits, TOP_K)
    gates = jax.nn.softmax(top_v, axis=-1)
    A = N * TOP_K
    e_flat = top_e.reshape(A)
    tok = jnp.repeat(jnp.arange(N), TOP_K)
    g_flat = gates.reshape(A)
    order = jnp.argsort(e_flat)
    e_s, tok_s, g_s = e_flat[order], tok[order], g_flat[order]
    counts = jnp.bincount(e_flat, length=N_EXPERTS)
    starts = jnp.cumsum(counts) - counts
    padded = (counts + MOE_BLOCK - 1) // MOE_BLOCK * MOE_BLOCK
    pend = jnp.cumsum(padded)
    pstarts = pend - padded
    dest = pstarts[e_s] + jnp.arange(A) - starts[e_s]
    n_blocks = -(-(A + N_EXPERTS * (MOE_BLOCK - 1)) // MOE_BLOCK)
    P = n_blocks * MOE_BLOCK
    buf = jnp.zeros((P, D), h.dtype).at[dest].set(h[tok_s])
    block_e = jnp.minimum(jnp.searchsorted(pend, jnp.arange(n_blocks) * MOE_BLOCK, side='right'), N_EXPERTS - 1)

    def expert_block(args):
        xb, e = args
        return swiglu(xb, w1[e], w3[e], w2[e])

    yb = lax.map(expert_block, (buf.reshape(n_blocks, MOE_BLOCK, D), block_e)).reshape(P, D)
    return jnp.zeros((N, D), h.dtype).at[tok_s].add(yb[dest] * g_s[:, None].astype(h.dtype))


def setup_inputs(seed: int = 0) -> dict:
    key = jax.random.key(seed)
    ks = iter(jax.random.split(key, 64))
    nrm = lambda shape, s: s * jax.random.normal(next(ks), shape, jnp.float32)
    D = D_MODEL
    fbias = jnp.linspace(3.0, 6.0, MLSTM_HEADS, dtype=jnp.float32)
    zh = jnp.zeros((MLSTM_HEADS,), jnp.float32)
    gate_offset = jnp.concatenate([zh, fbias, zh, fbias])
    return {
        'x': nrm((BATCH, SEQ, D), 1.0),
        'c': nrm((BATCH, D), 1.0),
        'ctx': nrm((BATCH, CTX_LEN, D), 1.0),
        'c_ctx': nrm((D,), 1.0),
        'w_mod': nrm((DEPTH, D, N_MOD * D), 0.5 * D ** -0.5),
        'b_mod': nrm((DEPTH, N_MOD * D), 0.02),
        'norm_g': 1.0 + nrm((DEPTH, 2, D), 0.05),
        'final_g': 1.0 + nrm((D,), 0.05),
        'na_w_qkv': nrm((N_NA, D, 3 * D), D ** -0.5),
        'na_b_qkv': nrm((N_NA, 3 * D), 0.02),
        'na_rpb': nrm((N_NA, NA_HEADS, 2 * NA_WIN_ROWS - 1, 2 * NA_WIN_COLS - 1), 0.2),
        'na_w_out': nrm((N_NA, D, D), D ** -0.5),
        'na_b_out': nrm((N_NA, D), 0.02),
        'pool_w': nrm((N_POOL, POOL_GROUPS, POOL_GROUP_DIM, POOL_GROUP_DIM), POOL_GROUP_DIM ** -0.5),
        'pool_scale': 1.0 + nrm((N_POOL, D), 0.1),
        'ml_w_up': nrm((N_MLSTM, D, 2 * MLSTM_INNER), D ** -0.5),
        'ml_conv_w': nrm((N_MLSTM, MLSTM_CONV, MLSTM_INNER), MLSTM_CONV ** -0.5),
        'ml_conv_b': nrm((N_MLSTM, MLSTM_INNER), 0.02),
        'ml_w_q': nrm((N_MLSTM, MLSTM_QKV_NBLK, MLSTM_QKV_BLOCK, MLSTM_QKV_BLOCK), MLSTM_QKV_BLOCK ** -0.5),
        'ml_w_k': nrm((N_MLSTM, MLSTM_QKV_NBLK, MLSTM_QKV_BLOCK, MLSTM_QKV_BLOCK), MLSTM_QKV_BLOCK ** -0.5),
        'ml_w_v': nrm((N_MLSTM, MLSTM_QKV_NBLK, MLSTM_QKV_BLOCK, MLSTM_QKV_BLOCK), MLSTM_QKV_BLOCK ** -0.5),
        'ml_w_gates': nrm((N_MLSTM, 3 * MLSTM_INNER, 4 * MLSTM_HEADS), (3 * MLSTM_INNER) ** -0.5),
        'ml_b_gates': gate_offset + nrm((N_MLSTM, 4 * MLSTM_HEADS), 0.1),
        'ml_gn_w': 1.0 + nrm((N_MLSTM, MLSTM_INNER), 0.05),
        'ml_skip': 1.0 + nrm((N_MLSTM, MLSTM_INNER), 0.05),
        'ml_w_down': nrm((N_MLSTM, MLSTM_INNER, D), MLSTM_INNER ** -0.5),
        'ffn_w1': nrm((N_DENSE, D, FFN_DENSE), D ** -0.5),
        'ffn_w3': nrm((N_DENSE, D, FFN_DENSE), D ** -0.5),
        'ffn_w2': nrm((N_DENSE, FFN_DENSE, D), FFN_DENSE ** -0.5),
        'moe_w_router': nrm((N_MOE, D, N_EXPERTS), D ** -0.5),
        'moe_w1': nrm((N_MOE, N_EXPERTS, D, FFN_EXPERT), D ** -0.5),
        'moe_w3': nrm((N_MOE, N_EXPERTS, D, FFN_EXPERT), D ** -0.5),
        'moe_w2': nrm((N_MOE, N_EXPERTS, FFN_EXPERT, D), FFN_EXPERT ** -0.5),
    }


def reference(x, c, ctx, c_ctx, w_mod, b_mod, norm_g, final_g,
              na_w_qkv, na_b_qkv, na_rpb, na_w_out, na_b_out,
              pool_w, pool_scale,
              ml_w_up, ml_conv_w, ml_conv_b, ml_w_q, ml_w_k, ml_w_v, ml_w_gates, ml_b_gates,
              ml_gn_w, ml_skip, ml_w_down,
              ffn_w1, ffn_w3, ffn_w2,
              moe_w_router, moe_w1, moe_w3, moe_w2):
    B, L, D = x.shape
    h_lat, h_ctx = x, ctx
    for i in range(DEPTH):
        last = i == DEPTH - 1
        kind = i % N_MIXERS
        j = i // N_MIXERS
        sh1, sc1, g1, sh2, sc2, g2 = adaln(c, w_mod[i], b_mod[i])
        csh1, csc1, cg1, csh2, csc2, cg2 = adaln(c_ctx, w_mod[i], b_mod[i])

        a_lat = modulate(rmsnorm(h_lat, norm_g[i, 0]), sh1, sc1)
        if kind == 1:
            y_lat = multiscale_pool(a_lat, pool_w[j], pool_scale[j])
            if not last:
                a_ctx = modulate(rmsnorm(h_ctx, norm_g[i, 0]), csh1, csc1)
                y_ctx = multiscale_pool(a_ctx, pool_w[j], pool_scale[j])
        else:
            a_ctx = modulate(rmsnorm(h_ctx, norm_g[i, 0]), csh1, csc1)
            if kind == 0:
                y_lat, y_ctx = na_mixer(a_lat, a_ctx, na_w_qkv[j], na_b_qkv[j], na_rpb[j],
                                        na_w_out[j], na_b_out[j], not last)
            else:
                y_lat, y_ctx = mlstm_mixer(a_lat, a_ctx, ml_w_up[j], ml_conv_w[j], ml_conv_b[j],
                                           ml_w_q[j], ml_w_k[j], ml_w_v[j], ml_w_gates[j], ml_b_gates[j],
                                           ml_gn_w[j], ml_skip[j], ml_w_down[j], not last)
        h_lat = h_lat + g1 * y_lat
        if not last:
            h_ctx = h_ctx + cg1 * y_ctx

        e = i // 2
        b_lat = modulate(rmsnorm(h_lat, norm_g[i, 1]), sh2, sc2)
        if i % 2 == 0:
            f_lat = swiglu(b_lat, ffn_w1[e], ffn_w3[e], ffn_w2[e])
            if not last:
                b_ctx = modulate(rmsnorm(h_ctx, norm_g[i, 1]), csh2, csc2)
                f_ctx = swiglu(b_ctx, ffn_w1[e], ffn_w3[e], ffn_w2[e])
        else:
            if last:
                f_lat = moe_swiglu(b_lat.reshape(B * L, D), moe_w_router[e], moe_w1[e], moe_w3[e],
                                   moe_w2[e]).reshape(B, L, D)
            else:
                b_ctx = modulate(rmsnorm(h_ctx, norm_g[i, 1]), csh2, csc2)
                n_ctx = b_ctx.shape[1]
                toks = jnp.concatenate([b_lat.reshape(B * L, D), b_ctx.reshape(B * n_ctx, D)], axis=0)
                f_all = moe_swiglu(toks, moe_w_router[e], moe_w1[e], moe_w3[e], moe_w2[e])
                f_lat = f_all[:B * L].reshape(B, L, D)
                f_ctx = f_all[B * L:].reshape(B, n_ctx, D)
        h_lat = h_lat + g2 * f_lat
        if not last:
            h_ctx = h_ctx + cg2 * f_ctx
    return rmsnorm(h_lat, final_g)
```

```python
import functools

import jax
import jax.numpy as jnp
from jax import lax
from jax.experimental import pallas as pl
from jax.experimental.pallas import tpu as pltpu

F32 = jnp.float32
BF16 = jnp.bfloat16

D_MODEL = 1024
N_MOD = 6
NORM_EPS = 1e-6
GRID_W = 64
NA_HEADS = 16
NA_WIN_ROWS = 8
NA_WIN_COLS = 16
POOL_WINDOWS = (2, 4, 8, 16)
POOL_GROUP_DIM = D_MODEL // len(POOL_WINDOWS)
MLSTM_INNER = 2 * D_MODEL
MLSTM_HEADS = 4
MLSTM_HEAD_DIM = MLSTM_INNER // MLSTM_HEADS
MLSTM_CONV = 4
MLSTM_QKV_BLOCK = 4
N_EXPERTS = 8
TOP_K = 2

LANES = 128
MOD_ROWS = 8
CTX_MOD_ROW = 4
VMEM_LIMIT_BYTES = 56 * 1024 * 1024
NEG_BIG = -1e30
SCAN_CHUNK = 128
SCAN_EXT = MLSTM_HEAD_DIM + LANES
MOE_ROWS = 512


def _cparams(*sem):
    return pltpu.CompilerParams(dimension_semantics=sem, vmem_limit_bytes=VMEM_LIMIT_BYTES)


def _resident(shape, index_map):
    return pl.BlockSpec(shape, index_map, pipeline_mode=pl.Buffered(1))


def _silu(x):
    return x * jax.nn.sigmoid(x)


def _norm_mod(x, g, shift, scale):
    ms = jnp.mean(x * x, axis=-1, keepdims=True)
    y = x * lax.rsqrt(ms + NORM_EPS) * g
    return y * (1.0 + scale) + shift


def _mod_chunk(mod_ref, row, j):
    return mod_ref[pl.ds(row, 1), pl.ds(j * D_MODEL, D_MODEL)]


def _mod_spec(layer):
    return pl.BlockSpec((None, MOD_ROWS, N_MOD * D_MODEL), lambda *_: (layer, 0, 0))


def _adaln_body(c_ref, w_ref, b_ref, o_ref):
    s = _silu(c_ref[...])
    o_ref[...] = jnp.dot(s, w_ref[...], preferred_element_type=F32) + b_ref[...]


def adaln_all(cond, w_mod, b_mod):
    depth = w_mod.shape[0]
    n = N_MOD * D_MODEL
    tn = 1536
    return pl.pallas_call(
        _adaln_body,
        grid=(depth, n // tn),
        in_specs=[pl.BlockSpec((MOD_ROWS, D_MODEL), lambda l, j: (0, 0)),
                  pl.BlockSpec((None, D_MODEL, tn), lambda l, j: (l, 0, j)),
                  pl.BlockSpec((None, 1, tn), lambda l, j: (l, 0, j))],
        out_specs=pl.BlockSpec((None, MOD_ROWS, tn), lambda l, j: (l, 0, j)),
        out_shape=jax.ShapeDtypeStruct((depth, MOD_ROWS, n), F32),
        compiler_params=_cparams("arbitrary", "arbitrary"),
        name="adaln",
    )(cond, w_mod, b_mod.reshape(depth, 1, n))


def _nm_matmul_body(x_ref, mod_ref, g_ref, w_ref, b_ref, o_ref, *, tm, rows_per_batch, sh, sc, nc):
    i = pl.program_id(0)
    row = (i * tm) // rows_per_batch if rows_per_batch else CTX_MOD_ROW
    a = _norm_mod(x_ref[...], g_ref[...], _mod_chunk(mod_ref, row, sh), _mod_chunk(mod_ref, row, sc)).astype(BF16)
    n = o_ref.shape[1]
    for c in range(n // nc):
        sl = slice(c * nc, (c + 1) * nc)
        y = jnp.dot(a, w_ref[:, sl], preferred_element_type=F32) + b_ref[:, sl]
        o_ref[:, sl] = y.astype(o_ref.dtype)


def nm_matmul(x, mods, layer, g, w, bias, *, rows_per_batch, sh, sc, tm=512, nc=1024, out_dtype=BF16):
    m, n = x.shape[0], w.shape[1]
    tm = min(tm, m)
    body = functools.partial(_nm_matmul_body, tm=tm, rows_per_batch=rows_per_batch, sh=sh, sc=sc, nc=nc)
    return pl.pallas_call(
        body,
        grid=(m // tm,),
        in_specs=[pl.BlockSpec((tm, D_MODEL), lambda i: (i, 0)),
                  _mod_spec(layer),
                  pl.BlockSpec((1, D_MODEL), lambda i: (0, 0)),
                  _resident((D_MODEL, n), lambda i: (0, 0)),
                  pl.BlockSpec((1, n), lambda i: (0, 0))],
        out_specs=pl.BlockSpec((tm, n), lambda i: (i, 0)),
        out_shape=jax.ShapeDtypeStruct((m, n), out_dtype),
        compiler_params=_cparams("arbitrary"),
        name="nm_matmul",
    )(x, mods, g.reshape(1, D_MODEL), w, bias.reshape(1, n))


def _mm_res_body(a_ref, h_ref, mod_ref, w_ref, b_ref, o_ref, *, tm, rows_per_batch, gate):
    i = pl.program_id(0)
    row = (i * tm) // rows_per_batch if rows_per_batch else CTX_MOD_ROW
    y = jnp.dot(a_ref[...], w_ref[...], preferred_element_type=F32) + b_ref[...]
    o_ref[...] = h_ref[...] + _mod_chunk(mod_ref, row, gate) * y


def mm_residual(a, h, mods, layer, w, bias, *, rows_per_batch, gate, tm=512):
    m, k = a.shape
    tm = min(tm, m)
    body = functools.partial(_mm_res_body, tm=tm, rows_per_batch=rows_per_batch, gate=gate)
    return pl.pallas_call(
        body,
        grid=(m // tm,),
        in_specs=[pl.BlockSpec((tm, k), lambda i: (i, 0)),
                  pl.BlockSpec((tm, D_MODEL), lambda i: (i, 0)),
                  _mod_spec(layer),
                  _resident((k, D_MODEL), lambda i: (0, 0)),
                  pl.BlockSpec((1, D_MODEL), lambda i: (0, 0))],
        out_specs=pl.BlockSpec((tm, D_MODEL), lambda i: (i, 0)),
        out_shape=jax.ShapeDtypeStruct((m, D_MODEL), F32),
        compiler_params=_cparams("arbitrary"),
        name="mm_residual",
    )(a, h, mods, w, bias.reshape(1, D_MODEL))


def _ffn_body(h_ref, mod_ref, g_ref, w1_ref, w3_ref, w2_ref, o_ref, *, tm, rows_per_batch):
    i = pl.program_id(0)
    row = (i * tm) // rows_per_batch if rows_per_batch else CTX_MOD_ROW
    h = h_ref[...]
    a = _norm_mod(h, g_ref[...], _mod_chunk(mod_ref, row, 3), _mod_chunk(mod_ref, row, 4)).astype(BF16)
    u = jnp.dot(a, w1_ref[...], preferred_element_type=F32)
    v = jnp.dot(a, w3_ref[...], preferred_element_type=F32)
    p = (_silu(u) * v).astype(BF16)
    y = jnp.dot(p, w2_ref[...], preferred_element_type=F32)
    o_ref[...] = h + _mod_chunk(mod_ref, row, 5) * y


def ffn_dense(h, mods, layer, g, w1, w3, w2, *, rows_per_batch, tm=256):
    m = h.shape[0]
    f = w1.shape[1]
    tm = min(tm, m)
    body = functools.partial(_ffn_body, tm=tm, rows_per_batch=rows_per_batch)
    return pl.pallas_call(
        body,
        grid=(m // tm,),
        in_specs=[pl.BlockSpec((tm, D_MODEL), lambda i: (i, 0)),
                  _mod_spec(layer),
                  pl.BlockSpec((1, D_MODEL), lambda i: (0, 0)),
                  _resident((D_MODEL, f), lambda i: (0, 0)),
                  _resident((D_MODEL, f), lambda i: (0, 0)),
                  _resident((f, D_MODEL), lambda i: (0, 0))],
        out_specs=pl.BlockSpec((tm, D_MODEL), lambda i: (i, 0)),
        out_shape=jax.ShapeDtypeStruct((m, D_MODEL), F32),
        compiler_params=_cparams("arbitrary"),
        name="ffn_dense",
    )(h, mods, g.reshape(1, D_MODEL), w1, w3, w2)


def na_bias_table(rpb):
    h = rpb.shape[0]
    col = jnp.arange(GRID_W)
    c0 = jnp.clip(col - NA_WIN_COLS // 2, 0, GRID_W - NA_WIN_COLS)
    col_ok = (col[None, :] >= c0[:, None]) & (col[None, :] < c0[:, None] + NA_WIN_COLS)
    dcol = jnp.clip(col[None, :] - col[:, None], 1 - NA_WIN_COLS, NA_WIN_COLS - 1) + (NA_WIN_COLS - 1)
    delta = jnp.arange(NA_WIN_ROWS)[:, None]
    drow = jnp.arange(NA_WIN_ROWS)[None, :] - delta + (NA_WIN_ROWS - 1)
    t = rpb[:, drow[:, None, :, None], dcol[None, :, None, :]]
    t = jnp.where(col_ok[None, None, :, None, :], t.astype(F32), NEG_BIG)
    t = t.reshape(h // 2, 2, NA_WIN_ROWS, GRID_W, NA_WIN_ROWS * GRID_W)
    return t.transpose(0, 2, 1, 3, 4).reshape(h // 2, NA_WIN_ROWS, 2 * GRID_W, NA_WIN_ROWS * GRID_W)


def _stack_heads(q):
    lo = lax.broadcasted_iota(jnp.int32, q.shape, 1) < (LANES // 2)
    zero = jnp.zeros_like(q)
    return jnp.concatenate([jnp.where(lo, q, zero), jnp.where(lo, zero, q)], axis=0)


def _unstack_heads(o):
    n = o.shape[0] // 2
    lo = lax.broadcasted_iota(jnp.int32, (n, LANES), 1) < (LANES // 2)
    return jnp.where(lo, o[:n], o[n:])


_NT = (((1,), (1,)), ((), ()))


def _na_body(q_ref, k_ref, v_ref, kc_ref, vc_ref, bias_ref, o_ref, *, rows):
    kc = kc_ref[...]
    vc = vc_ref[...]
    kwin = NA_WIN_ROWS * GRID_W

    def row(r, carry):
        r0 = jnp.clip(r - NA_WIN_ROWS // 2, 0, rows - NA_WIN_ROWS)
        qoff = pl.multiple_of(r * GRID_W, GRID_W)
        koff = pl.multiple_of(r0 * GRID_W, GRID_W)
        qs = _stack_heads(q_ref[pl.ds(qoff, GRID_W), :])
        k = k_ref[pl.ds(koff, kwin), :]
        v = v_ref[pl.ds(koff, kwin), :]
        s_loc = lax.dot_general(qs, k, _NT, preferred_element_type=F32) + bias_ref[r - r0]
        s_ctx = lax.dot_general(qs, kc, _NT, preferred_element_type=F32)
        m = jnp.maximum(jnp.max(s_loc, axis=-1, keepdims=True), jnp.max(s_ctx, axis=-1, keepdims=True))
        p_loc = jnp.exp(s_loc - m)
        p_ctx = jnp.exp(s_ctx - m)
        l = jnp.sum(p_loc, axis=-1, keepdims=True) + jnp.sum(p_ctx, axis=-1, keepdims=True)
        o = (jnp.dot(p_loc.astype(BF16), v, preferred_element_type=F32)
             + jnp.dot(p_ctx.astype(BF16), vc, preferred_element_type=F32)) / l
        o_ref[pl.ds(qoff, GRID_W), :] = _unstack_heads(o).astype(o_ref.dtype)
        return carry

    lax.fori_loop(0, rows, row, 0)


def na_attention(qkv, qkv_ctx, bias):
    b, l, _ = qkv.shape
    c = qkv_ctx.shape[1]
    hp = D_MODEL // LANES
    body = functools.partial(_na_body, rows=l // GRID_W)
    return pl.pallas_call(
        body,
        grid=(b, hp),
        in_specs=[pl.BlockSpec((None, l, LANES), lambda i, j: (i, 0, j)),
                  pl.BlockSpec((None, l, LANES), lambda i, j: (i, 0, hp + j)),
                  pl.BlockSpec((None, l, LANES), lambda i, j: (i, 0, 2 * hp + j)),
                  pl.BlockSpec((None, c, LANES), lambda i, j: (i, 0, hp + j)),
                  pl.BlockSpec((None, c, LANES), lambda i, j: (i, 0, 2 * hp + j)),
                  pl.BlockSpec((None,) + bias.shape[1:], lambda i, j: (j, 0, 0, 0))],
        out_specs=pl.BlockSpec((None, l, LANES), lambda i, j: (i, 0, j)),
        out_shape=jax.ShapeDtypeStruct((b, l, D_MODEL), BF16),
        compiler_params=_cparams("arbitrary", "arbitrary"),
        name="na_attention",
    )(qkv, qkv, qkv, qkv_ctx, qkv_ctx, bias)


def _ctx_attn_body(q_ref, k_ref, v_ref, o_ref):
    qs = _stack_heads(q_ref[...])
    s = lax.dot_general(qs, k_ref[...], _NT, preferred_element_type=F32)
    p = jnp.exp(s - jnp.max(s, axis=-1, keepdims=True))
    l = jnp.sum(p, axis=-1, keepdims=True)
    o = jnp.dot(p.astype(BF16), v_ref[...], preferred_element_type=F32) / l
    o_ref[...] = _unstack_heads(o).astype(o_ref.dtype)


def ctx_attention(qkv_ctx):
    b, c, _ = qkv_ctx.shape
    hp = D_MODEL // LANES
    return pl.pallas_call(
        _ctx_attn_body,
        grid=(b, hp),
        in_specs=[pl.BlockSpec((None, c, LANES), lambda i, j: (i, 0, j)),
                  pl.BlockSpec((None, c, LANES), lambda i, j: (i, 0, hp + j)),
                  pl.BlockSpec((None, c, LANES), lambda i, j: (i, 0, 2 * hp + j))],
        out_specs=pl.BlockSpec((None, c, LANES), lambda i, j: (i, 0, j)),
        out_shape=jax.ShapeDtypeStruct((b, c, D_MODEL), BF16),
        compiler_params=_cparams("arbitrary", "arbitrary"),
        name="ctx_attention",
    )(qkv_ctx, qkv_ctx, qkv_ctx)


POOL_HALO = 8


def _pool_body(prev_ref, cur_ref, next_ref, mod_ref, g_ref, wp_ref, ps_ref, o_ref, *, tl, seq, is_ctx):
    b = pl.program_id(0)
    j = pl.program_id(1)
    row = CTX_MOD_ROW if is_ctx else b
    g = g_ref[...]
    sh = _mod_chunk(mod_ref, row, 0)
    sc = _mod_chunk(mod_ref, row, 1)
    h = cur_ref[...]
    a_cur = _norm_mod(h, g, sh, sc)
    a_prev = _norm_mod(prev_ref[...], g, sh, sc) * (j > 0).astype(F32)
    a_next = _norm_mod(next_ref[...], g, sh, sc) * (j < seq // tl - 1).astype(F32)
    ext = jnp.concatenate([a_prev, a_cur, a_next], axis=0)
    t = j * tl + lax.broadcasted_iota(jnp.int32, (tl, 1), 0)
    outs = []
    for gi, w in enumerate(POOL_WINDOWS):
        sl = slice(gi * POOL_GROUP_DIM, (gi + 1) * POOL_GROUP_DIM)
        p = ext[:, sl]
        step = 1
        while step < w:
            n = p.shape[0]
            p = p[:n - step] + p[step:]
            step *= 2
        off = POOL_HALO - w // 2
        cnt = jnp.minimum(t + w // 2, seq) - jnp.maximum(t - w // 2, 0)
        pooled = p[off:off + tl] / cnt.astype(F32) - a_cur[:, sl]
        outs.append(jnp.dot(pooled.astype(BF16), wp_ref[gi], preferred_element_type=F32))
    y = jnp.concatenate(outs, axis=1) * ps_ref[...]
    o_ref[...] = h + _mod_chunk(mod_ref, row, 2) * y


def pool_mixer(h, mods, layer, g, w_pool, pool_scale, *, is_ctx, tl=512):
    b, seq, _ = h.shape
    tl = min(tl, seq)
    nh = tl // POOL_HALO
    last = seq // POOL_HALO - 1
    body = functools.partial(_pool_body, tl=tl, seq=seq, is_ctx=is_ctx)
    return pl.pallas_call(
        body,
        grid=(b, seq // tl),
        in_specs=[pl.BlockSpec((None, POOL_HALO, D_MODEL), lambda i, j: (i, jnp.maximum(j * nh - 1, 0), 0)),
                  pl.BlockSpec((None, tl, D_MODEL), lambda i, j: (i, j, 0)),
                  pl.BlockSpec((None, POOL_HALO, D_MODEL), lambda i, j: (i, jnp.minimum((j + 1) * nh, last), 0)),
                  _mod_spec(layer),
                  pl.BlockSpec((1, D_MODEL), lambda i, j: (0, 0)),
                  pl.BlockSpec(w_pool.shape, lambda i, j: (0, 0, 0)),
                  pl.BlockSpec((1, D_MODEL), lambda i, j: (0, 0))],
        out_specs=pl.BlockSpec((None, tl, D_MODEL), lambda i, j: (i, j, 0)),
        out_shape=jax.ShapeDtypeStruct(h.shape, F32),
        compiler_params=_cparams("arbitrary", "arbitrary"),
        name="pool_mixer",
    )(h, h, h, mods, g.reshape(1, D_MODEL), w_pool, pool_scale.reshape(1, D_MODEL))


CONV_HALO = 16


def block_diag_weights(w):
    nb = LANES // MLSTM_QKV_BLOCK
    wc = w.reshape(-1, nb, MLSTM_QKV_BLOCK, MLSTM_QKV_BLOCK)
    eye = jnp.eye(nb, dtype=w.dtype)
    bd = jnp.einsum("cnij,nm->cnimj", wc, eye)
    return bd.reshape(-1, LANES, LANES)


def _ml_feat_body(prev_ref, cur_ref, next_ref, cw_ref, cb_ref, wq_ref, wk_ref, wkt_ref, wv_ref,
                  wg_ref, wgt_ref, bg_ref, bgt_ref,
                  q_ref, kt_ref, v_ref, xc_ref, g_ref, gt_ref, *, tl, seq):
    j = pl.program_id(1)
    cur = cur_ref[...]
    prev = prev_ref[...].astype(F32) * (j > 0).astype(F32)
    nxt = next_ref[...].astype(F32) * (j < seq // tl - 1).astype(F32)
    ext = jnp.concatenate([prev, cur.astype(F32), nxt], axis=0)
    left = MLSTM_CONV // 2
    xc = cb_ref[...]
    for tap in range(MLSTM_CONV):
        o = CONV_HALO - left + tap
        xc = xc + ext[o:o + tl] * cw_ref[tap:tap + 1, :]
    xc = _silu(xc)
    xcb = xc.astype(BF16)
    xc_ref[...] = xcb
    t = SCAN_CHUNK
    qscale = MLSTM_HEAD_DIM ** -0.5
    acc = jnp.zeros((tl, LANES), F32)
    acct = jnp.zeros((wgt_ref.shape[0], tl), F32)
    n_lane_blocks = MLSTM_INNER // LANES
    for c in range(n_lane_blocks):
        sl = slice(c * LANES, (c + 1) * LANES)
        xs = xcb[:, sl]
        q = jnp.dot(xs, wq_ref[c], preferred_element_type=F32)
        k = jnp.dot(xs, wk_ref[c], preferred_element_type=F32)
        v = jnp.dot(cur[:, sl], wv_ref[c], preferred_element_type=F32)
        q_ref[:, sl] = (q * qscale).astype(BF16)
        v_ref[:, sl] = v.astype(BF16)
        for cc in range(tl // t):
            kt = lax.dot_general(wkt_ref[c], xs[cc * t:(cc + 1) * t], _NT, preferred_element_type=F32)
            kt_ref[cc, sl, :] = kt.astype(BF16)
        for part, val in enumerate((q, k, v)):
            vb = val.astype(BF16)
            rows = slice(part * MLSTM_INNER + c * LANES, part * MLSTM_INNER + (c + 1) * LANES)
            acc = acc + jnp.dot(vb, wg_ref[rows, :], preferred_element_type=F32)
            acct = acct + lax.dot_general(wgt_ref[:, rows], vb, _NT, preferred_element_type=F32)
    ng = g_ref.shape[1]
    g_ref[...] = acc[:, :ng] + bg_ref[...]
    gt = acct + bgt_ref[...]
    for cc in range(tl // t):
        gt_ref[cc] = gt[:, cc * t:(cc + 1) * t]


def mlstm_features(up, conv_w, conv_b, wq_bd, wk_bd, wkt_bd, wv_bd, wg, wgt, bg, *, tl=256):
    b, seq, _ = up.shape
    tl = min(tl, seq)
    t = SCAN_CHUNK
    nh = tl // CONV_HALO
    last = seq // CONV_HALO - 1
    ng = wgt.shape[0]
    inner = MLSTM_INNER
    body = functools.partial(_ml_feat_body, tl=tl, seq=seq)
    full = lambda a: pl.BlockSpec(a.shape, lambda i, j: (0,) * a.ndim)
    cw = conv_w
    cb = conv_b.reshape(1, inner)
    bgr = bg.reshape(1, ng)
    bgc = bg.reshape(ng, 1)
    return pl.pallas_call(
        body,
        grid=(b, seq // tl),
        in_specs=[pl.BlockSpec((None, CONV_HALO, inner), lambda i, j: (i, jnp.maximum(j * nh - 1, 0), 0)),
                  pl.BlockSpec((None, tl, inner), lambda i, j: (i, j, 0)),
                  pl.BlockSpec((None, CONV_HALO, inner), lambda i, j: (i, jnp.minimum((j + 1) * nh, last), 0)),
                  full(cw), full(cb), full(wq_bd), full(wk_bd), full(wkt_bd), full(wv_bd),
                  full(wg), full(wgt), full(bgr), full(bgc)],
        out_specs=[pl.BlockSpec((None, tl, inner), lambda i, j: (i, j, 0)),
                   pl.BlockSpec((None, tl // t, inner, t), lambda i, j: (i, j, 0, 0)),
                   pl.BlockSpec((None, tl, inner), lambda i, j: (i, j, 0)),
                   pl.BlockSpec((None, tl, inner), lambda i, j: (i, j, 0)),
                   pl.BlockSpec((None, tl, ng), lambda i, j: (i, j, 0)),
                   pl.BlockSpec((None, tl // t, ng, t), lambda i, j: (i, j, 0, 0))],
        out_shape=[jax.ShapeDtypeStruct((b, seq, inner), BF16),
                   jax.ShapeDtypeStruct((b, seq // t, inner, t), BF16),
                   jax.ShapeDtypeStruct((b, seq, inner), BF16),
                   jax.ShapeDtypeStruct((b, seq, inner), BF16),
                   jax.ShapeDtypeStruct((b, seq, ng), F32),
                   jax.ShapeDtypeStruct((b, seq // t, ng, t), F32)],
        compiler_params=_cparams("arbitrary", "arbitrary"),
        name="mlstm_features",
    )(up, up, up, cw, cb, wq_bd, wk_bd, wkt_bd, wv_bd, wg, wgt, bgr, bgc)


def _log_sigmoid(x):
    return jnp.minimum(x, 0.0) - jnp.log1p(jnp.exp(-jnp.abs(x)))


def _scan_body(*refs, rev, nchunk, nblk, has_init):
    if has_init:
        q_ref, kt_ref, v_ref, g_ref, gt_ref, c0_ref, m0_ref, h_ref, cf_ref, mf_ref, c_sc, m_sc = refs
    else:
        q_ref, kt_ref, v_ref, g_ref, gt_ref, h_ref, cf_ref, mf_ref, c_sc, m_sc = refs
    hd = pl.program_id(1)
    j = pl.program_id(2)
    t = SCAN_CHUNK
    dh = MLSTM_HEAD_DIM

    @pl.when(j == 0)
    def _():
        if has_init:
            c_sc[...] = c0_ref[...]
            m_sc[...] = m0_ref[...]
        else:
            c_sc[...] = jnp.zeros_like(c_sc)
            m_sc[...] = jnp.zeros_like(m_sc)

    ci = (2 if rev else 0) * MLSTM_HEADS + hd
    cf = (3 if rev else 1) * MLSTM_HEADS + hd
    ng = g_ref.shape[1]
    lane = lax.broadcasted_iota(jnp.int32, (t, ng), 1)
    r_io = lax.broadcasted_iota(jnp.int32, (t, t), 0)
    c_io = lax.broadcasted_iota(jnp.int32, (t, t), 1)
    seen = (c_io >= r_io) if rev else (c_io <= r_io)
    seen_t = (r_io >= c_io) if rev else (r_io <= c_io)
    seen_f = seen.astype(F32)
    seen_tf = seen_t.astype(F32)
    ones_blk = jnp.ones((t, LANES), BF16)

    order = range(nchunk - 1, -1, -1) if rev else range(nchunk)
    for cc in order:
        rows = slice(cc * t, (cc + 1) * t)
        q = q_ref[rows, :]
        kt = kt_ref[cc]
        vext = jnp.concatenate([v_ref[rows, :], ones_blk], axis=1)
        g = g_ref[rows, :]
        i_col = jnp.sum(jnp.where(lane == ci, g, 0.0), axis=1, keepdims=True)
        f_col = jnp.sum(jnp.where(lane == cf, g, 0.0), axis=1, keepdims=True)
        i_row = gt_ref[cc, pl.ds(ci, 1), :]
        f_row = gt_ref[cc, pl.ds(cf, 1), :]
        lf_col = _log_sigmoid(f_col)
        lf_row = _log_sigmoid(f_row)
        b_col = jnp.sum(seen_f * lf_row, axis=1, keepdims=True)
        b_row = jnp.sum(seen_tf * lf_col, axis=0, keepdims=True)
        m_prev = m_sc[0:1, 0:1]
        dmat = jnp.where(seen, b_col - b_row + i_row, NEG_BIG)
        inter = b_col + m_prev
        m_t = jnp.maximum(inter, jnp.max(dmat, axis=1, keepdims=True))
        s = jnp.dot(q, kt, preferred_element_type=F32)
        a = (s * jnp.exp(dmat - m_t)).astype(BF16)
        w_int = jnp.exp(inter - m_t)
        cb = c_sc[...].astype(BF16)
        r = jnp.dot(a, vext, preferred_element_type=F32) + jnp.dot(q, cb, preferred_element_type=F32) * w_int
        den = r[:, dh:dh + 1]
        hc = r[:, :dh] / jnp.maximum(jnp.abs(den), jnp.exp(-m_t))
        h_ref[rows, :] = hc.astype(h_ref.dtype)
        b_end = jnp.sum(lf_row, axis=1, keepdims=True)
        g_row = b_end - b_row + i_row
        m_new = jnp.maximum(b_end + m_prev, jnp.max(g_row, axis=1, keepdims=True))
        decay = jnp.exp(b_end + m_prev - m_new)
        kw = (kt.astype(F32) * jnp.exp(g_row - m_new)).astype(BF16)
        c_sc[...] = decay * c_sc[...] + jnp.dot(kw, vext, preferred_element_type=F32)
        m_sc[...] = jnp.broadcast_to(m_new, m_sc.shape)

    @pl.when(j == nblk - 1)
    def _():
        cf_ref[...] = c_sc[...]
        mf_ref[...] = m_sc[...]


def mlstm_scan(q, kt, v, g, gt, state, *, rev, tb=512):
    b, seq, inner = q.shape
    t = SCAN_CHUNK
    tb = min(tb, seq)
    nblk = seq // tb
    nchunk = tb // t
    dh = MLSTM_HEAD_DIM
    ng = g.shape[2]
    has_init = state is not None
    blk = (lambda j: nblk - 1 - j) if rev else (lambda j: j)
    body = functools.partial(_scan_body, rev=rev, nchunk=nchunk, nblk=nblk, has_init=has_init)
    in_specs = [pl.BlockSpec((None, tb, dh), lambda i, h, j: (i, blk(j), h)),
                pl.BlockSpec((None, nchunk, dh, t), lambda i, h, j: (i, blk(j), h, 0)),
                pl.BlockSpec((None, tb, dh), lambda i, h, j: (i, blk(j), h)),
                pl.BlockSpec((None, tb, ng), lambda i, h, j: (i, blk(j), 0)),
                pl.BlockSpec((None, nchunk, ng, t), lambda i, h, j: (i, blk(j), 0, 0))]
    args = [q, kt, v, g, gt]
    st_spec_c = pl.BlockSpec((None, None, dh, SCAN_EXT), lambda i, h, j: (i, h, 0, 0))
    st_spec_m = pl.BlockSpec((None, None, 8, LANES), lambda i, h, j: (i, h, 0, 0))
    if has_init:
        in_specs += [st_spec_c, st_spec_m]
        args += list(state)
    return pl.pallas_call(
        body,
        grid=(b, MLSTM_HEADS, nblk),
        in_specs=in_specs,
        out_specs=[pl.BlockSpec((None, tb, dh), lambda i, h, j: (i, blk(j), h)), st_spec_c, st_spec_m],
        out_shape=[jax.ShapeDtypeStruct((b, seq, inner), BF16),
                   jax.ShapeDtypeStruct((b, MLSTM_HEADS, dh, SCAN_EXT), F32),
                   jax.ShapeDtypeStruct((b, MLSTM_HEADS, 8, LANES), F32)],
        scratch_shapes=[pltpu.VMEM((dh, SCAN_EXT), F32), pltpu.VMEM((8, LANES), F32)],
        compiler_params=_cparams("arbitrary", "arbitrary", "arbitrary"),
        name="mlstm_scan_bwd" if rev else "mlstm_scan_fwd",
    )(*args)


def _ml_out_body(hf_ref, hb_ref, xc_ref, z_ref, h_ref, mod_ref, gn_ref, sk_ref, w_ref, o_ref, *, tm, rows_per_batch):
    i = pl.program_id(0)
    row = (i * tm) // rows_per_batch if rows_per_batch else CTX_MOD_ROW
    hs = hf_ref[...].astype(F32) + hb_ref[...].astype(F32)
    parts = []
    for hd in range(MLSTM_HEADS):
        x = hs[:, hd * MLSTM_HEAD_DIM:(hd + 1) * MLSTM_HEAD_DIM]
        mu = jnp.mean(x, axis=-1, keepdims=True)
        xm = x - mu
        var = jnp.mean(xm * xm, axis=-1, keepdims=True)
        parts.append(xm * lax.rsqrt(var + NORM_EPS))
    hn = jnp.concatenate(parts, axis=1) * gn_ref[...]
    y = (hn + sk_ref[...] * xc_ref[...].astype(F32)) * _silu(z_ref[...].astype(F32))
    y = jnp.dot(y.astype(BF16), w_ref[...], preferred_element_type=F32)
    o_ref[...] = h_ref[...] + _mod_chunk(mod_ref, row, 2) * y


def mlstm_output(hf, hb, xc, up, h, mods, layer, gn_w, skip, w_down, *, rows_per_batch, tm=256):
    m = h.shape[0]
    tm = min(tm, m)
    inner = MLSTM_INNER
    body = functools.partial(_ml_out_body, tm=tm, rows_per_batch=rows_per_batch)
    row_spec = pl.BlockSpec((tm, inner), lambda i: (i, 0))
    return pl.pallas_call(
        body,
        grid=(m // tm,),
        in_specs=[row_spec, row_spec, row_spec,
                  pl.BlockSpec((tm, inner), lambda i: (i, 1)),
                  pl.BlockSpec((tm, D_MODEL), lambda i: (i, 0)),
                  _mod_spec(layer),
                  pl.BlockSpec((1, inner), lambda i: (0, 0)),
                  pl.BlockSpec((1, inner), lambda i: (0, 0)),
                  _resident((inner, D_MODEL), lambda i: (0, 0))],
        out_specs=pl.BlockSpec((tm, D_MODEL), lambda i: (i, 0)),
        out_shape=jax.ShapeDtypeStruct((m, D_MODEL), F32),
        compiler_params=_cparams("arbitrary"),
        name="mlstm_output",
    )(hf, hb, xc, up, h, mods, gn_w.reshape(1, inner), skip.reshape(1, inner), w_down)


def _router_body(x_ref, mod_ref, g_ref, wr_ref, a_ref, lg_ref, *, tm, rows_per_batch):
    i = pl.program_id(0)
    row = (i * tm) // rows_per_batch if rows_per_batch else CTX_MOD_ROW
    a = _norm_mod(x_ref[...], g_ref[...], _mod_chunk(mod_ref, row, 3), _mod_chunk(mod_ref, row, 4))
    a_ref[...] = a
    lg_ref[...] = jnp.dot(a, wr_ref[...], preferred_element_type=F32, precision=lax.Precision.HIGHEST)


def moe_router(x, mods, layer, g, wr_pad, *, rows_per_batch, tm=512):
    m = x.shape[0]
    tm = min(tm, m)
    body = functools.partial(_router_body, tm=tm, rows_per_batch=rows_per_batch)
    return pl.pallas_call(
        body,
        grid=(m // tm,),
        in_specs=[pl.BlockSpec((tm, D_MODEL), lambda i: (i, 0)),
                  _mod_spec(layer),
                  pl.BlockSpec((1, D_MODEL), lambda i: (0, 0)),
                  pl.BlockSpec((D_MODEL, LANES), lambda i: (0, 0))],
        out_specs=[pl.BlockSpec((tm, D_MODEL), lambda i: (i, 0)),
                   pl.BlockSpec((tm, LANES), lambda i: (i, 0))],
        out_shape=[jax.ShapeDtypeStruct((m, D_MODEL), F32),
                   jax.ShapeDtypeStruct((m, LANES), F32)],
        compiler_params=_cparams("arbitrary"),
        name="moe_router",
    )(x, mods, g.reshape(1, D_MODEL), wr_pad)


def moe_route(logits, tm):
    n = logits.shape[0]
    a_tot = n * TOP_K
    top_v, top_e = lax.top_k(logits[:, :N_EXPERTS], TOP_K)
    gates = jax.nn.softmax(top_v, axis=-1)
    e_flat = top_e.reshape(a_tot).astype(jnp.int32)
    order = jnp.argsort(e_flat).astype(jnp.int32)
    counts = jnp.sum((e_flat[:, None] == jnp.arange(N_EXPERTS, dtype=jnp.int32)[None, :]).astype(jnp.int32), axis=0)
    starts = jnp.cumsum(counts) - counts
    padded = (counts + tm - 1) // tm * tm
    pend = jnp.cumsum(padded)
    pstarts = pend - padded
    n_blocks = (a_tot + N_EXPERTS * (tm - 1)) // tm
    block_e = jnp.minimum(jnp.searchsorted(pend, jnp.arange(n_blocks, dtype=jnp.int32) * tm, side="right"),
                          N_EXPERTS - 1).astype(jnp.int32)
    p = jnp.arange(n_blocks * tm, dtype=jnp.int32)
    e_p = block_e[p // tm]
    idx = p - pstarts[e_p]
    valid = idx < counts[e_p]
    pair = order[jnp.clip(starts[e_p] + idx, 0, a_tot - 1)]
    src = jnp.where(valid, pair // TOP_K, 0)
    dst = jnp.where(valid, (pair % TOP_K) * n + pair // TOP_K, -1)
    n_used = (pend[-1] // tm).astype(jnp.int32).reshape(1)
    return gates, block_e, n_used, src.reshape(n_blocks, 1, tm), dst.reshape(n_blocks, 1, tm)


def _moe_body(be_ref, nu_ref, src_ref, dst_ref, a_hbm, w1_ref, w3_ref, w2_ref, y_hbm,
              xf_ref, xb_ref, acc_ref, gsem, ssem, *, tm, nf):
    i = pl.program_id(0)
    f = pl.program_id(1)

    def gather_copy(r, tok):
        return pltpu.make_async_copy(a_hbm.at[pl.ds(tok, 1), :], xf_ref.at[pl.ds(r, 1), :], gsem)

    def scatter_copy(r, slot):
        return pltpu.make_async_copy(acc_ref.at[pl.ds(r, 1), :], y_hbm.at[pl.ds(slot, 1), :], ssem)

    @pl.when(i < nu_ref[0])
    def _():
        @pl.when(f == 0)
        def _():
            def start(r, c):
                gather_copy(r, src_ref[0, r]).start()
                return c
            lax.fori_loop(0, tm, start, 0)

            def wait(r, c):
                gather_copy(r, 0).wait()
                return c
            lax.fori_loop(0, tm, wait, 0)
            xb_ref[...] = xf_ref[...].astype(BF16)

        x = xb_ref[...]
        u = jnp.dot(x, w1_ref[...], preferred_element_type=F32)
        v = jnp.dot(x, w3_ref[...], preferred_element_type=F32)
        p = (_silu(u) * v).astype(BF16)
        y = jnp.dot(p, w2_ref[...], preferred_element_type=F32)

        @pl.when(f == 0)
        def _():
            acc_ref[...] = y

        @pl.when(f > 0)
        def _():
            acc_ref[...] += y

        @pl.when(f == nf - 1)
        def _():
            def start(r, c):
                slot = dst_ref[0, r]

                @pl.when(slot >= 0)
                def _():
                    scatter_copy(r, slot).start()
                return c
            lax.fori_loop(0, tm, start, 0)

            def wait(r, c):
                @pl.when(dst_ref[0, r] >= 0)
                def _():
                    scatter_copy(r, 0).wait()
                return c
            lax.fori_loop(0, tm, wait, 0)


def moe_experts(a, block_e, n_used, src, dst, w1, w3, w2, *, tf=1792):
    n = a.shape[0]
    n_blocks, _, tm = src.shape
    f_dim = w1.shape[2]
    nf = f_dim // tf
    body = functools.partial(_moe_body, tm=tm, nf=nf)

    def wmap(kind):
        def index_map(i, f, be, nu):
            live = i < nu[0]
            ff = jnp.where(live, f, nf - 1)
            ii = jnp.where(live, i, nu[0] - 1)
            return (be[ii], 0, ff) if kind == "up" else (be[ii], ff, 0)
        return index_map

    idx_spec = pl.BlockSpec((None, 1, tm), lambda i, f, be, nu: (i, 0, 0), memory_space=pltpu.SMEM)
    grid_spec = pltpu.PrefetchScalarGridSpec(
        num_scalar_prefetch=2,
        grid=(n_blocks, nf),
        in_specs=[idx_spec, idx_spec,
                  pl.BlockSpec(memory_space=pl.ANY),
                  pl.BlockSpec((None, D_MODEL, tf), wmap("up")),
                  pl.BlockSpec((None, D_MODEL, tf), wmap("up")),
                  pl.BlockSpec((None, tf, D_MODEL), wmap("down"))],
        out_specs=pl.BlockSpec(memory_space=pl.ANY),
        scratch_shapes=[pltpu.VMEM((tm, D_MODEL), F32), pltpu.VMEM((tm, D_MODEL), BF16),
                        pltpu.VMEM((tm, D_MODEL), F32),
                        pltpu.SemaphoreType.DMA(()), pltpu.SemaphoreType.DMA(())],
    )
    return pl.pallas_call(
        body,
        grid_spec=grid_spec,
        out_shape=jax.ShapeDtypeStruct((TOP_K * n, D_MODEL), F32),
        compiler_params=_cparams("arbitrary", "arbitrary"),
        name="moe_experts",
    )(block_e, n_used, src, dst, a, w1, w3, w2)


def _combine_body(h_ref, y0_ref, y1_ref, gt_ref, mod_ref, fg_ref, o_ref, *, tm, rows_per_batch, final):
    i = pl.program_id(0)
    row = (i * tm) // rows_per_batch if rows_per_batch else CTX_MOD_ROW
    gt = gt_ref[...]
    f = y0_ref[...] * gt[:, 0:1] + y1_ref[...] * gt[:, 1:2]
    out = h_ref[...] + _mod_chunk(mod_ref, row, 5) * f
    if final:
        ms = jnp.mean(out * out, axis=-1, keepdims=True)
        out = out * lax.rsqrt(ms + NORM_EPS) * fg_ref[...]
    o_ref[...] = out


def moe_combine(h, y, gates, mods, layer, final_g, *, n_tok, row_off, rows_per_batch, final, tm=512):
    m = h.shape[0]
    tm = min(tm, m)
    o0 = row_off // tm
    o1 = (n_tok + row_off) // tm
    body = functools.partial(_combine_body, tm=tm, rows_per_batch=rows_per_batch, final=final)
    return pl.pallas_call(
        body,
        grid=(m // tm,),
        in_specs=[pl.BlockSpec((tm, D_MODEL), lambda i: (i, 0)),
                  pl.BlockSpec((tm, D_MODEL), lambda i: (o0 + i, 0)),
                  pl.BlockSpec((tm, D_MODEL), lambda i: (o1 + i, 0)),
                  pl.BlockSpec((tm, TOP_K), lambda i: (o0 + i, 0)),
                  _mod_spec(layer),
                  pl.BlockSpec((1, D_MODEL), lambda i: (0, 0))],
        out_specs=pl.BlockSpec((tm, D_MODEL), lambda i: (i, 0)),
        out_shape=jax.ShapeDtypeStruct((m, D_MODEL), F32),
        compiler_params=_cparams("arbitrary"),
        name="moe_combine",
    )(h, y, y, gates, mods, final_g.reshape(1, D_MODEL))


def _na_layer(h_lat, h_ctx, mods, layer, g, w_qkv, b_qkv, rpb, w_out, b_out, with_ctx_out):
    b, seq, _ = h_lat.shape
    c = h_ctx.shape[1]
    qscale = jnp.concatenate([jnp.full((D_MODEL,), (D_MODEL // NA_HEADS) ** -0.5, F32), jnp.ones((2 * D_MODEL,), F32)])
    w = (w_qkv * qscale).astype(BF16)
    bias = b_qkv * qscale
    qkv = nm_matmul(h_lat.reshape(b * seq, D_MODEL), mods, layer, g, w, bias, rows_per_batch=seq, sh=0, sc=1)
    qkv_c = nm_matmul(h_ctx.reshape(b * c, D_MODEL), mods, layer, g, w, bias, rows_per_batch=None, sh=0, sc=1)
    qkv = qkv.reshape(b, seq, 3 * D_MODEL)
    qkv_c = qkv_c.reshape(b, c, 3 * D_MODEL)
    o_lat = na_attention(qkv, qkv_c, na_bias_table(rpb))
    wo = w_out.astype(BF16)
    h_lat = mm_residual(o_lat.reshape(b * seq, D_MODEL), h_lat.reshape(b * seq, D_MODEL), mods, layer, wo, b_out,
                        rows_per_batch=seq, gate=2).reshape(b, seq, D_MODEL)
    if with_ctx_out:
        o_ctx = ctx_attention(qkv_c)
        h_ctx = mm_residual(o_ctx.reshape(b * c, D_MODEL), h_ctx.reshape(b * c, D_MODEL), mods, layer, wo, b_out,
                            rows_per_batch=None, gate=2).reshape(b, c, D_MODEL)
    return h_lat, h_ctx


def _mlstm_layer(h_lat, h_ctx, mods, layer, g, w_up, conv_w, conv_b, w_q, w_k, w_v, w_gates, b_gates,
                 gn_w, skip, w_down, with_ctx_out):
    b, seq, _ = h_lat.shape
    c = h_ctx.shape[1]
    inner = MLSTM_INNER
    wu = w_up.astype(BF16)
    zero_b = jnp.zeros((2 * inner,), F32)
    wq_bd = block_diag_weights(w_q).astype(BF16)
    wk_bd = block_diag_weights(w_k).astype(BF16)
    wkt_bd = jnp.swapaxes(wk_bd, 1, 2)
    wv_bd = block_diag_weights(w_v).astype(BF16)
    ng = w_gates.shape[1]
    wg = jnp.pad(w_gates, ((0, 0), (0, LANES - ng))).astype(BF16)
    wgt = w_gates.T.astype(BF16)
    wd = w_down.astype(BF16)

    def features(h, rows_per_batch):
        n, s, _ = h.shape
        up = nm_matmul(h.reshape(n * s, D_MODEL), mods, layer, g, wu, zero_b, rows_per_batch=rows_per_batch, sh=0, sc=1)
        up = up.reshape(n, s, 2 * inner)
        return up, mlstm_features(up, conv_w, conv_b, wq_bd, wk_bd, wkt_bd, wv_bd, wg, wgt, b_gates)

    up_c, (q_c, kt_c, v_c, xc_c, g_c, gt_c) = features(h_ctx, None)
    up_l, (q_l, kt_l, v_l, xc_l, g_l, gt_l) = features(h_lat, seq)
    hf_c, cf, mf = mlstm_scan(q_c, kt_c, v_c, g_c, gt_c, None, rev=False)
    hb_c, cb, mb = mlstm_scan(q_c, kt_c, v_c, g_c, gt_c, None, rev=True)
    hf_l, _, _ = mlstm_scan(q_l, kt_l, v_l, g_l, gt_l, (cf, mf), rev=False)
    hb_l, _, _ = mlstm_scan(q_l, kt_l, v_l, g_l, gt_l, (cb, mb), rev=True)
    flat = lambda a: a.reshape(-1, a.shape[-1])
    h_lat = mlstm_output(flat(hf_l), flat(hb_l), flat(xc_l), flat(up_l), flat(h_lat), mods, layer, gn_w, skip, wd,
                         rows_per_batch=seq).reshape(b, seq, D_MODEL)
    if with_ctx_out:
        h_ctx = mlstm_output(flat(hf_c), flat(hb_c), flat(xc_c), flat(up_c), flat(h_ctx), mods, layer, gn_w, skip, wd,
                             rows_per_batch=None).reshape(b, c, D_MODEL)
    return h_lat, h_ctx


def _moe_layer(h_lat, h_ctx, mods, layer, g, w_router, w1, w3, w2, final_g, last):
    b, seq, _ = h_lat.shape
    c = h_ctx.shape[1]
    wr = jnp.pad(w_router, ((0, 0), (0, LANES - N_EXPERTS)))
    hl = h_lat.reshape(b * seq, D_MODEL)
    a, logits = moe_router(hl, mods, layer, g, wr, rows_per_batch=seq)
    if not last:
        hc = h_ctx.reshape(b * c, D_MODEL)
        a_c, logits_c = moe_router(hc, mods, layer, g, wr, rows_per_batch=None)
        a = jnp.concatenate([a, a_c], axis=0)
        logits = jnp.concatenate([logits, logits_c], axis=0)
    n_tok = a.shape[0]
    gates, block_e, n_used, src, dst = moe_route(logits, MOE_ROWS)
    y = moe_experts(a, block_e, n_used, src, dst, w1.astype(BF16), w3.astype(BF16), w2.astype(BF16))
    h_lat = moe_combine(hl, y, gates, mods, layer, final_g, n_tok=n_tok, row_off=0, rows_per_batch=seq,
                        final=last).reshape(b, seq, D_MODEL)
    if not last:
        h_ctx = moe_combine(hc, y, gates, mods, layer, final_g, n_tok=n_tok, row_off=b * seq, rows_per_batch=None,
                            final=False).reshape(b, c, D_MODEL)
    return h_lat, h_ctx


def kernel(x, c, ctx, c_ctx, w_mod, b_mod, norm_g, final_g, na_w_qkv, na_b_qkv, na_rpb, na_w_out, na_b_out,
           pool_w, pool_scale, ml_w_up, ml_conv_w, ml_conv_b, ml_w_q, ml_w_k, ml_w_v, ml_w_gates, ml_b_gates,
           ml_gn_w, ml_skip, ml_w_down, ffn_w1, ffn_w3, ffn_w2, moe_w_router, moe_w1, moe_w3, moe_w2):
    b, seq, _ = x.shape
    n_ctx = ctx.shape[1]
    depth = w_mod.shape[0]
    assert b <= CTX_MOD_ROW
    cond = jnp.zeros((MOD_ROWS, D_MODEL), F32).at[:b].set(c).at[CTX_MOD_ROW].set(c_ctx)
    mods = adaln_all(cond, w_mod, b_mod)
    h_lat, h_ctx = x, ctx
    for i in range(depth):
        last = i == depth - 1
        kind = i % 3
        j = i // 3
        g_tok = norm_g[i, 0]
        if kind == 0:
            h_lat, h_ctx = _na_layer(h_lat, h_ctx, mods, i, g_tok, na_w_qkv[j], na_b_qkv[j], na_rpb[j],
                                     na_w_out[j], na_b_out[j], not last)
        elif kind == 1:
            wp = pool_w[j].astype(BF16)
            h_lat = pool_mixer(h_lat, mods, i, g_tok, wp, pool_scale[j], is_ctx=False)
            if not last:
                h_ctx = pool_mixer(h_ctx, mods, i, g_tok, wp, pool_scale[j], is_ctx=True)
        else:
            h_lat, h_ctx = _mlstm_layer(h_lat, h_ctx, mods, i, g_tok, ml_w_up[j], ml_conv_w[j], ml_conv_b[j],
                                        ml_w_q[j], ml_w_k[j], ml_w_v[j], ml_w_gates[j], ml_b_gates[j],
                                        ml_gn_w[j], ml_skip[j], ml_w_down[j], not last)
        e = i // 2
        g_ch = norm_g[i, 1]
        if i % 2 == 0:
            w1, w3, w2 = ffn_w1[e].astype(BF16), ffn_w3[e].astype(BF16), ffn_w2[e].astype(BF16)
            h_lat = ffn_dense(h_lat.reshape(b * seq, D_MODEL), mods, i, g_ch, w1, w3, w2,
                              rows_per_batch=seq).reshape(b, seq, D_MODEL)
            if not last:
                h_ctx = ffn_dense(h_ctx.reshape(b * n_ctx, D_MODEL), mods, i, g_ch, w1, w3, w2,
                                  rows_per_batch=None).reshape(b, n_ctx, D_MODEL)
        else:
            h_lat, h_ctx = _moe_layer(h_lat, h_ctx, mods, i, g_ch, moe_w_router[e], moe_w1[e], moe_w3[e],
                                      moe_w2[e], final_g, last)
    return h_lat
```

```python
import functools

import jax
import jax.numpy as jnp
from jax import lax
from jax.experimental import pallas as pl
from jax.experimental.pallas import tpu as pltpu

F32 = jnp.float32
BF16 = jnp.bfloat16

D_MODEL = 1024
N_MOD = 6
NORM_EPS = 1e-6
GRID_W = 64
NA_HEADS = 16
NA_WIN_ROWS = 8
NA_WIN_COLS = 16
POOL_WINDOWS = (2, 4, 8, 16)
POOL_GROUP_DIM = D_MODEL // len(POOL_WINDOWS)
MLSTM_INNER = 2 * D_MODEL
MLSTM_HEADS = 4
MLSTM_HEAD_DIM = MLSTM_INNER // MLSTM_HEADS
MLSTM_CONV = 4
MLSTM_QKV_BLOCK = 4
N_EXPERTS = 8
TOP_K = 2

LANES = 128
MOD_ROWS = 8
CTX_MOD_ROW = 4
VMEM_LIMIT_BYTES = 56 * 1024 * 1024
NEG_BIG = -1e30
SCAN_CHUNK = 128
SCAN_EXT = MLSTM_HEAD_DIM + LANES
MOE_ROWS = 512


def _cparams(*sem):
    return pltpu.CompilerParams(dimension_semantics=sem, vmem_limit_bytes=VMEM_LIMIT_BYTES)


def _resident(shape, index_map):
    return pl.BlockSpec(shape, index_map, pipeline_mode=pl.Buffered(1))


def _silu(x):
    return x * jax.nn.sigmoid(x)


def _norm_mod(x, g, shift, scale):
    ms = jnp.mean(x * x, axis=-1, keepdims=True)
    y = x * lax.rsqrt(ms + NORM_EPS) * g
    return y * (1.0 + scale) + shift


def _mod_chunk(mod_ref, row, j):
    return mod_ref[pl.ds(row, 1), pl.ds(j * D_MODEL, D_MODEL)]


def _mod_spec(layer):
    return pl.BlockSpec((None, MOD_ROWS, N_MOD * D_MODEL), lambda *_: (layer, 0, 0))


def _adaln_body(c_ref, w_ref, b_ref, o_ref):
    s = _silu(c_ref[...])
    o_ref[...] = jnp.dot(s, w_ref[...], preferred_element_type=F32) + b_ref[...]


def adaln_all(cond, w_mod, b_mod):
    depth = w_mod.shape[0]
    n = N_MOD * D_MODEL
    tn = 1536
    return pl.pallas_call(
        _adaln_body,
        grid=(depth, n // tn),
        in_specs=[pl.BlockSpec((MOD_ROWS, D_MODEL), lambda l, j: (0, 0)),
                  pl.BlockSpec((None, D_MODEL, tn), lambda l, j: (l, 0, j)),
                  pl.BlockSpec((None, 1, tn), lambda l, j: (l, 0, j))],
        out_specs=pl.BlockSpec((None, MOD_ROWS, tn), lambda l, j: (l, 0, j)),
        out_shape=jax.ShapeDtypeStruct((depth, MOD_ROWS, n), F32),
        compiler_params=_cparams("arbitrary", "arbitrary"),
        name="adaln",
    )(cond, w_mod, b_mod.reshape(depth, 1, n))


def _nm_matmul_body(x_ref, mod_ref, g_ref, w_ref, b_ref, o_ref, *, tm, rows_per_batch, sh, sc, nc):
    i = pl.program_id(0)
    row = (i * tm) // rows_per_batch if rows_per_batch else CTX_MOD_ROW
    a = _norm_mod(x_ref[...], g_ref[...], _mod_chunk(mod_ref, row, sh), _mod_chunk(mod_ref, row, sc)).astype(BF16)
    n = o_ref.shape[1]
    for c in range(n // nc):
        sl = slice(c * nc, (c + 1) * nc)
        y = jnp.dot(a, w_ref[:, sl], preferred_element_type=F32) + b_ref[:, sl]
        o_ref[:, sl] = y.astype(o_ref.dtype)


def nm_matmul(x, mods, layer, g, w, bias, *, rows_per_batch, sh, sc, tm=512, nc=1024, out_dtype=BF16):
    m, n = x.shape[0], w.shape[1]
    tm = min(tm, m)
    body = functools.partial(_nm_matmul_body, tm=tm, rows_per_batch=rows_per_batch, sh=sh, sc=sc, nc=nc)
    return pl.pallas_call(
        body,
        grid=(m // tm,),
        in_specs=[pl.BlockSpec((tm, D_MODEL), lambda i: (i, 0)),
                  _mod_spec(layer),
                  pl.BlockSpec((1, D_MODEL), lambda i: (0, 0)),
                  _resident((D_MODEL, n), lambda i: (0, 0)),
                  pl.BlockSpec((1, n), lambda i: (0, 0))],
        out_specs=pl.BlockSpec((tm, n), lambda i: (i, 0)),
        out_shape=jax.ShapeDtypeStruct((m, n), out_dtype),
        compiler_params=_cparams("arbitrary"),
        name="nm_matmul",
    )(x, mods, g.reshape(1, D_MODEL), w, bias.reshape(1, n))


def _mm_res_body(a_ref, h_ref, mod_ref, w_ref, b_ref, o_ref, *, tm, rows_per_batch, gate):
    i = pl.program_id(0)
    row = (i * tm) // rows_per_batch if rows_per_batch else CTX_MOD_ROW
    y = jnp.dot(a_ref[...], w_ref[...], preferred_element_type=F32) + b_ref[...]
    o_ref[...] = h_ref[...] + _mod_chunk(mod_ref, row, gate) * y


def mm_residual(a, h, mods, layer, w, bias, *, rows_per_batch, gate, tm=512):
    m, k = a.shape
    tm = min(tm, m)
    body = functools.partial(_mm_res_body, tm=tm, rows_per_batch=rows_per_batch, gate=gate)
    return pl.pallas_call(
        body,
        grid=(m // tm,),
        in_specs=[pl.BlockSpec((tm, k), lambda i: (i, 0)),
                  pl.BlockSpec((tm, D_MODEL), lambda i: (i, 0)),
                  _mod_spec(layer),
                  _resident((k, D_MODEL), lambda i: (0, 0)),
                  pl.BlockSpec((1, D_MODEL), lambda i: (0, 0))],
        out_specs=pl.BlockSpec((tm, D_MODEL), lambda i: (i, 0)),
        out_shape=jax.ShapeDtypeStruct((m, D_MODEL), F32),
        compiler_params=_cparams("arbitrary"),
        name="mm_residual",
    )(a, h, mods, w, bias.reshape(1, D_MODEL))


def _ffn_body(h_ref, mod_ref, g_ref, w1_ref, w3_ref, w2_ref, o_ref, *, tm, rows_per_batch):
    i = pl.program_id(0)
    row = (i * tm) // rows_per_batch if rows_per_batch else CTX_MOD_ROW
    h = h_ref[...]
    a = _norm_mod(h, g_ref[...], _mod_chunk(mod_ref, row, 3), _mod_chunk(mod_ref, row, 4)).astype(BF16)
    u = jnp.dot(a, w1_ref[...], preferred_element_type=F32)
    v = jnp.dot(a, w3_ref[...], preferred_element_type=F32)
    p = (_silu(u) * v).astype(BF16)
    y = jnp.dot(p, w2_ref[...], preferred_element_type=F32)
    o_ref[...] = h + _mod_chunk(mod_ref, row, 5) * y


def ffn_dense(h, mods, layer, g, w1, w3, w2, *, rows_per_batch, tm=256):
    m = h.shape[0]
    f = w1.shape[1]
    tm = min(tm, m)
    body = functools.partial(_ffn_body, tm=tm, rows_per_batch=rows_per_batch)
    return pl.pallas_call(
        body,
        grid=(m // tm,),
        in_specs=[pl.BlockSpec((tm, D_MODEL), lambda i: (i, 0)),
                  _mod_spec(layer),
                  pl.BlockSpec((1, D_MODEL), lambda i: (0, 0)),
                  _resident((D_MODEL, f), lambda i: (0, 0)),
                  _resident((D_MODEL, f), lambda i: (0, 0)),
                  _resident((f, D_MODEL), lambda i: (0, 0))],
        out_specs=pl.BlockSpec((tm, D_MODEL), lambda i: (i, 0)),
        out_shape=jax.ShapeDtypeStruct((m, D_MODEL), F32),
        compiler_params=_cparams("arbitrary"),
        name="ffn_dense",
    )(h, mods, g.reshape(1, D_MODEL), w1, w3, w2)


def na_bias_table(rpb):
    h = rpb.shape[0]
    col = jnp.arange(GRID_W)
    c0 = jnp.clip(col - NA_WIN_COLS // 2, 0, GRID_W - NA_WIN_COLS)
    col_ok = (col[None, :] >= c0[:, None]) & (col[None, :] < c0[:, None] + NA_WIN_COLS)
    dcol = jnp.clip(col[None, :] - col[:, None], 1 - NA_WIN_COLS, NA_WIN_COLS - 1) + (NA_WIN_COLS - 1)
    n_drow = 2 * NA_WIN_ROWS - 1
    t = jnp.where(col_ok[None, None], rpb[:, :, dcol].astype(F32), NEG_BIG)
    t = t.transpose(0, 2, 1, 3).reshape(h // 2, 2 * GRID_W, n_drow * GRID_W)
    tiles = [t[:, :, (NA_WIN_ROWS - 1 - off) * GRID_W:(2 * NA_WIN_ROWS - 1 - off) * GRID_W]
             for off in range(NA_WIN_ROWS)]
    return jnp.stack(tiles, axis=1)


def _stack_heads(q):
    lo = lax.broadcasted_iota(jnp.int32, q.shape, 1) < (LANES // 2)
    zero = jnp.zeros_like(q)
    return jnp.concatenate([jnp.where(lo, q, zero), jnp.where(lo, zero, q)], axis=0)


def _unstack_heads(o):
    n = o.shape[0] // 2
    lo = lax.broadcasted_iota(jnp.int32, (n, LANES), 1) < (LANES // 2)
    return jnp.where(lo, o[:n], o[n:])


_NT = (((1,), (1,)), ((), ()))


def _na_body(q_ref, k_ref, v_ref, kc_ref, vc_ref, bias_ref, o_ref, *, rows):
    kc = kc_ref[...]
    vc = vc_ref[...]
    kwin = NA_WIN_ROWS * GRID_W

    def row(r, carry):
        r0 = jnp.clip(r - NA_WIN_ROWS // 2, 0, rows - NA_WIN_ROWS)
        qoff = pl.multiple_of(r * GRID_W, GRID_W)
        koff = pl.multiple_of(r0 * GRID_W, GRID_W)
        qs = _stack_heads(q_ref[pl.ds(qoff, GRID_W), :])
        k = k_ref[pl.ds(koff, kwin), :]
        v = v_ref[pl.ds(koff, kwin), :]
        s_loc = lax.dot_general(qs, k, _NT, preferred_element_type=F32) + bias_ref[r - r0]
        s_ctx = lax.dot_general(qs, kc, _NT, preferred_element_type=F32)
        m = jnp.maximum(jnp.max(s_loc, axis=-1, keepdims=True), jnp.max(s_ctx, axis=-1, keepdims=True))
        p_loc = jnp.exp(s_loc - m)
        p_ctx = jnp.exp(s_ctx - m)
        l = jnp.sum(p_loc, axis=-1, keepdims=True) + jnp.sum(p_ctx, axis=-1, keepdims=True)
        o = (jnp.dot(p_loc.astype(BF16), v, preferred_element_type=F32)
             + jnp.dot(p_ctx.astype(BF16), vc, preferred_element_type=F32)) / l
        o_ref[pl.ds(qoff, GRID_W), :] = _unstack_heads(o).astype(o_ref.dtype)
        return carry

    lax.fori_loop(0, rows, row, 0, unroll=4)


def na_attention(qkv, qkv_ctx, bias):
    b, l, _ = qkv.shape
    c = qkv_ctx.shape[1]
    hp = D_MODEL // LANES
    body = functools.partial(_na_body, rows=l // GRID_W)
    return pl.pallas_call(
        body,
        grid=(b, hp),
        in_specs=[pl.BlockSpec((None, l, LANES), lambda i, j: (i, 0, j)),
                  pl.BlockSpec((None, l, LANES), lambda i, j: (i, 0, hp + j)),
                  pl.BlockSpec((None, l, LANES), lambda i, j: (i, 0, 2 * hp + j)),
                  pl.BlockSpec((None, c, LANES), lambda i, j: (i, 0, hp + j)),
                  pl.BlockSpec((None, c, LANES), lambda i, j: (i, 0, 2 * hp + j)),
                  pl.BlockSpec((None,) + bias.shape[1:], lambda i, j: (j, 0, 0, 0))],
        out_specs=pl.BlockSpec((None, l, LANES), lambda i, j: (i, 0, j)),
        out_shape=jax.ShapeDtypeStruct((b, l, D_MODEL), BF16),
        compiler_params=_cparams("arbitrary", "arbitrary"),
        name="na_attention",
    )(qkv, qkv, qkv, qkv_ctx, qkv_ctx, bias)


def _ctx_attn_body(q_ref, k_ref, v_ref, o_ref):
    qs = _stack_heads(q_ref[...])
    s = lax.dot_general(qs, k_ref[...], _NT, preferred_element_type=F32)
    p = jnp.exp(s - jnp.max(s, axis=-1, keepdims=True))
    l = jnp.sum(p, axis=-1, keepdims=True)
    o = jnp.dot(p.astype(BF16), v_ref[...], preferred_element_type=F32) / l
    o_ref[...] = _unstack_heads(o).astype(o_ref.dtype)


def ctx_attention(qkv_ctx):
    b, c, _ = qkv_ctx.shape
    hp = D_MODEL // LANES
    return pl.pallas_call(
        _ctx_attn_body,
        grid=(b, hp),
        in_specs=[pl.BlockSpec((None, c, LANES), lambda i, j: (i, 0, j)),
                  pl.BlockSpec((None, c, LANES), lambda i, j: (i, 0, hp + j)),
                  pl.BlockSpec((None, c, LANES), lambda i, j: (i, 0, 2 * hp + j))],
        out_specs=pl.BlockSpec((None, c, LANES), lambda i, j: (i, 0, j)),
        out_shape=jax.ShapeDtypeStruct((b, c, D_MODEL), BF16),
        compiler_params=_cparams("arbitrary", "arbitrary"),
        name="ctx_attention",
    )(qkv_ctx, qkv_ctx, qkv_ctx)


POOL_HALO = 8


def _pool_body(prev_ref, cur_ref, next_ref, mod_ref, g_ref, wp_ref, ps_ref, o_ref, *, tl, seq, is_ctx):
    b = pl.program_id(0)
    j = pl.program_id(1)
    row = CTX_MOD_ROW if is_ctx else b
    g = g_ref[...]
    sh = _mod_chunk(mod_ref, row, 0)
    sc = _mod_chunk(mod_ref, row, 1)
    h = cur_ref[...]
    a_cur = _norm_mod(h, g, sh, sc)
    a_prev = _norm_mod(prev_ref[...], g, sh, sc) * (j > 0).astype(F32)
    a_next = _norm_mod(next_ref[...], g, sh, sc) * (j < seq // tl - 1).astype(F32)
    ext = jnp.concatenate([a_prev, a_cur, a_next], axis=0)
    t = j * tl + lax.broadcasted_iota(jnp.int32, (tl, 1), 0)
    outs = []
    for gi, w in enumerate(POOL_WINDOWS):
        sl = slice(gi * POOL_GROUP_DIM, (gi + 1) * POOL_GROUP_DIM)
        p = ext[:, sl]
        step = 1
        while step < w:
            n = p.shape[0]
            p = p[:n - step] + p[step:]
            step *= 2
        off = POOL_HALO - w // 2
        cnt = jnp.minimum(t + w // 2, seq) - jnp.maximum(t - w // 2, 0)
        pooled = p[off:off + tl] / cnt.astype(F32) - a_cur[:, sl]
        outs.append(jnp.dot(pooled.astype(BF16), wp_ref[gi], preferred_element_type=F32))
    y = jnp.concatenate(outs, axis=1) * ps_ref[...]
    o_ref[...] = h + _mod_chunk(mod_ref, row, 2) * y


def pool_mixer(h, mods, layer, g, w_pool, pool_scale, *, is_ctx, tl=512):
    b, seq, _ = h.shape
    tl = min(tl, seq)
    nh = tl // POOL_HALO
    last = seq // POOL_HALO - 1
    body = functools.partial(_pool_body, tl=tl, seq=seq, is_ctx=is_ctx)
    return pl.pallas_call(
        body,
        grid=(b, seq // tl),
        in_specs=[pl.BlockSpec((None, POOL_HALO, D_MODEL), lambda i, j: (i, jnp.maximum(j * nh - 1, 0), 0)),
                  pl.BlockSpec((None, tl, D_MODEL), lambda i, j: (i, j, 0)),
                  pl.BlockSpec((None, POOL_HALO, D_MODEL), lambda i, j: (i, jnp.minimum((j + 1) * nh, last), 0)),
                  _mod_spec(layer),
                  pl.BlockSpec((1, D_MODEL), lambda i, j: (0, 0)),
                  pl.BlockSpec(w_pool.shape, lambda i, j: (0, 0, 0)),
                  pl.BlockSpec((1, D_MODEL), lambda i, j: (0, 0))],
        out_specs=pl.BlockSpec((None, tl, D_MODEL), lambda i, j: (i, j, 0)),
        out_shape=jax.ShapeDtypeStruct(h.shape, F32),
        compiler_params=_cparams("arbitrary", "arbitrary"),
        name="pool_mixer",
    )(h, h, h, mods, g.reshape(1, D_MODEL), w_pool, pool_scale.reshape(1, D_MODEL))


CONV_HALO = 16


def block_diag_weights(w):
    nb = LANES // MLSTM_QKV_BLOCK
    wc = w.reshape(-1, nb, MLSTM_QKV_BLOCK, MLSTM_QKV_BLOCK)
    eye = jnp.eye(nb, dtype=w.dtype)
    bd = jnp.einsum("cnij,nm->cnimj", wc, eye)
    return bd.reshape(-1, LANES, LANES)


def _ml_feat_body(prev_ref, cur_ref, next_ref, cw_ref, cb_ref, wq_ref, wk_ref, wkt_ref, wv_ref,
                  wg_ref, wgt_ref, bg_ref, bgt_ref,
                  q_ref, kt_ref, v_ref, xc_ref, g_ref, gt_ref, *, tl, seq):
    j = pl.program_id(1)
    cur = cur_ref[...]
    prev = prev_ref[...].astype(F32) * (j > 0).astype(F32)
    nxt = next_ref[...].astype(F32) * (j < seq // tl - 1).astype(F32)
    ext = jnp.concatenate([prev, cur.astype(F32), nxt], axis=0)
    left = MLSTM_CONV // 2
    xc = cb_ref[...]
    for tap in range(MLSTM_CONV):
        o = CONV_HALO - left + tap
        xc = xc + ext[o:o + tl] * cw_ref[tap:tap + 1, :]
    xc = _silu(xc)
    xcb = xc.astype(BF16)
    xc_ref[...] = xcb
    t = SCAN_CHUNK
    qscale = MLSTM_HEAD_DIM ** -0.5
    acc = jnp.zeros((tl, LANES), F32)
    acct = jnp.zeros((wgt_ref.shape[0], tl), F32)
    n_lane_blocks = MLSTM_INNER // LANES
    for c in range(n_lane_blocks):
        sl = slice(c * LANES, (c + 1) * LANES)
        xs = xcb[:, sl]
        q = jnp.dot(xs, wq_ref[c], preferred_element_type=F32)
        k = jnp.dot(xs, wk_ref[c], preferred_element_type=F32)
        v = jnp.dot(cur[:, sl], wv_ref[c], preferred_element_type=F32)
        q_ref[:, sl] = (q * qscale).astype(BF16)
        v_ref[:, sl] = v.astype(BF16)
        for cc in range(tl // t):
            kt = lax.dot_general(wkt_ref[c], xs[cc * t:(cc + 1) * t], _NT, preferred_element_type=F32)
            kt_ref[cc, sl, :] = kt.astype(BF16)
        for part, val in enumerate((q, k, v)):
            vb = val.astype(BF16)
            rows = slice(part * MLSTM_INNER + c * LANES, part * MLSTM_INNER + (c + 1) * LANES)
            acc = acc + jnp.dot(vb, wg_ref[rows, :], preferred_element_type=F32)
            acct = acct + lax.dot_general(wgt_ref[:, rows], vb, _NT, preferred_element_type=F32)
    ng = g_ref.shape[1]
    g_ref[...] = acc[:, :ng] + bg_ref[...]
    gt = acct + bgt_ref[...]
    for cc in range(tl // t):
        gt_ref[cc] = gt[:, cc * t:(cc + 1) * t]


def mlstm_features(up, conv_w, conv_b, wq_bd, wk_bd, wkt_bd, wv_bd, wg, wgt, bg, *, tl=256):
    b, seq, _ = up.shape
    tl = min(tl, seq)
    t = SCAN_CHUNK
    nh = tl // CONV_HALO
    last = seq // CONV_HALO - 1
    ng = wgt.shape[0]
    inner = MLSTM_INNER
    body = functools.partial(_ml_feat_body, tl=tl, seq=seq)
    full = lambda a: pl.BlockSpec(a.shape, lambda i, j: (0,) * a.ndim)
    cw = conv_w
    cb = conv_b.reshape(1, inner)
    bgr = bg.reshape(1, ng)
    bgc = bg.reshape(ng, 1)
    return pl.pallas_call(
        body,
        grid=(b, seq // tl),
        in_specs=[pl.BlockSpec((None, CONV_HALO, inner), lambda i, j: (i, jnp.maximum(j * nh - 1, 0), 0)),
                  pl.BlockSpec((None, tl, inner), lambda i, j: (i, j, 0)),
                  pl.BlockSpec((None, CONV_HALO, inner), lambda i, j: (i, jnp.minimum((j + 1) * nh, last), 0)),
                  full(cw), full(cb), full(wq_bd), full(wk_bd), full(wkt_bd), full(wv_bd),
                  full(wg), full(wgt), full(bgr), full(bgc)],
        out_specs=[pl.BlockSpec((None, tl, inner), lambda i, j: (i, j, 0)),
                   pl.BlockSpec((None, tl // t, inner, t), lambda i, j: (i, j, 0, 0)),
                   pl.BlockSpec((None, tl, inner), lambda i, j: (i, j, 0)),
                   pl.BlockSpec((None, tl, inner), lambda i, j: (i, j, 0)),
                   pl.BlockSpec((None, tl, ng), lambda i, j: (i, j, 0)),
                   pl.BlockSpec((None, tl // t, ng, t), lambda i, j: (i, j, 0, 0))],
        out_shape=[jax.ShapeDtypeStruct((b, seq, inner), BF16),
                   jax.ShapeDtypeStruct((b, seq // t, inner, t), BF16),
                   jax.ShapeDtypeStruct((b, seq, inner), BF16),
                   jax.ShapeDtypeStruct((b, seq, inner), BF16),
                   jax.ShapeDtypeStruct((b, seq, ng), F32),
                   jax.ShapeDtypeStruct((b, seq // t, ng, t), F32)],
        compiler_params=_cparams("arbitrary", "arbitrary"),
        name="mlstm_features",
    )(up, up, up, cw, cb, wq_bd, wk_bd, wkt_bd, wv_bd, wg, wgt, bgr, bgc)


def _log_sigmoid(x):
    return jnp.minimum(x, 0.0) - jnp.log1p(jnp.exp(-jnp.abs(x)))


def _scan_body(*refs, rev, nchunk, nblk, has_init):
    if has_init:
        q_ref, kt_ref, v_ref, g_ref, gt_ref, c0_ref, m0_ref, h_ref, cf_ref, mf_ref, c_sc, m_sc = refs
    else:
        q_ref, kt_ref, v_ref, g_ref, gt_ref, h_ref, cf_ref, mf_ref, c_sc, m_sc = refs
    hd = pl.program_id(1)
    j = pl.program_id(2)
    t = SCAN_CHUNK
    dh = MLSTM_HEAD_DIM

    @pl.when(j == 0)
    def _():
        if has_init:
            c_sc[...] = c0_ref[...]
            m_sc[...] = m0_ref[...]
        else:
            c_sc[...] = jnp.zeros_like(c_sc)
            m_sc[...] = jnp.zeros_like(m_sc)

    ci = (2 if rev else 0) * MLSTM_HEADS + hd
    cf = (3 if rev else 1) * MLSTM_HEADS + hd
    ng = g_ref.shape[1]
    lane = lax.broadcasted_iota(jnp.int32, (t, ng), 1)
    r_io = lax.broadcasted_iota(jnp.int32, (t, t), 0)
    c_io = lax.broadcasted_iota(jnp.int32, (t, t), 1)
    seen = (c_io >= r_io) if rev else (c_io <= r_io)
    seen_t = (r_io >= c_io) if rev else (r_io <= c_io)
    seen_f = seen.astype(F32)
    seen_tf = seen_t.astype(F32)
    ones_blk = jnp.ones((t, LANES), BF16)

    order = range(nchunk - 1, -1, -1) if rev else range(nchunk)
    for cc in order:
        rows = slice(cc * t, (cc + 1) * t)
        q = q_ref[rows, :]
        kt = kt_ref[cc]
        vext = jnp.concatenate([v_ref[rows, :], ones_blk], axis=1)
        g = g_ref[rows, :]
        i_col = jnp.sum(jnp.where(lane == ci, g, 0.0), axis=1, keepdims=True)
        f_col = jnp.sum(jnp.where(lane == cf, g, 0.0), axis=1, keepdims=True)
        i_row = gt_ref[cc, pl.ds(ci, 1), :]
        f_row = gt_ref[cc, pl.ds(cf, 1), :]
        lf_col = _log_sigmoid(f_col)
        lf_row = _log_sigmoid(f_row)
        b_col = jnp.sum(seen_f * lf_row, axis=1, keepdims=True)
        b_row = jnp.sum(seen_tf * lf_col, axis=0, keepdims=True)
        m_prev = m_sc[0:1, 0:1]
        dmat = jnp.where(seen, b_col - b_row + i_row, NEG_BIG)
        inter = b_col + m_prev
        m_t = jnp.maximum(inter, jnp.max(dmat, axis=1, keepdims=True))
        s = jnp.dot(q, kt, preferred_element_type=F32)
        a = (s * jnp.exp(dmat - m_t)).astype(BF16)
        w_int = jnp.exp(inter - m_t)
        cb = c_sc[...].astype(BF16)
        r = jnp.dot(a, vext, preferred_element_type=F32) + jnp.dot(q, cb, preferred_element_type=F32) * w_int
        den = r[:, dh:dh + 1]
        hc = r[:, :dh] / jnp.maximum(jnp.abs(den), jnp.exp(-m_t))
        h_ref[rows, :] = hc.astype(h_ref.dtype)
        b_end = jnp.sum(lf_row, axis=1, keepdims=True)
        g_row = b_end - b_row + i_row
        m_new = jnp.maximum(b_end + m_prev, jnp.max(g_row, axis=1, keepdims=True))
        decay = jnp.exp(b_end + m_prev - m_new)
        kw = (kt.astype(F32) * jnp.exp(g_row - m_new)).astype(BF16)
        c_sc[...] = decay * c_sc[...] + jnp.dot(kw, vext, preferred_element_type=F32)
        m_sc[...] = jnp.broadcast_to(m_new, m_sc.shape)

    @pl.when(j == nblk - 1)
    def _():
        cf_ref[...] = c_sc[...]
        mf_ref[...] = m_sc[...]


def mlstm_scan(q, kt, v, g, gt, state, *, rev, tb=512):
    b, seq, inner = q.shape
    t = SCAN_CHUNK
    tb = min(tb, seq)
    nblk = seq // tb
    nchunk = tb // t
    dh = MLSTM_HEAD_DIM
    ng = g.shape[2]
    has_init = state is not None
    blk = (lambda j: nblk - 1 - j) if rev else (lambda j: j)
    body = functools.partial(_scan_body, rev=rev, nchunk=nchunk, nblk=nblk, has_init=has_init)
    in_specs = [pl.BlockSpec((None, tb, dh), lambda i, h, j: (i, blk(j), h)),
                pl.BlockSpec((None, nchunk, dh, t), lambda i, h, j: (i, blk(j), h, 0)),
                pl.BlockSpec((None, tb, dh), lambda i, h, j: (i, blk(j), h)),
                pl.BlockSpec((None, tb, ng), lambda i, h, j: (i, blk(j), 0)),
                pl.BlockSpec((None, nchunk, ng, t), lambda i, h, j: (i, blk(j), 0, 0))]
    args = [q, kt, v, g, gt]
    st_spec_c = pl.BlockSpec((None, None, dh, SCAN_EXT), lambda i, h, j: (i, h, 0, 0))
    st_spec_m = pl.BlockSpec((None, None, 8, LANES), lambda i, h, j: (i, h, 0, 0))
    if has_init:
        in_specs += [st_spec_c, st_spec_m]
        args += list(state)
    return pl.pallas_call(
        body,
        grid=(b, MLSTM_HEADS, nblk),
        in_specs=in_specs,
        out_specs=[pl.BlockSpec((None, tb, dh), lambda i, h, j: (i, blk(j), h)), st_spec_c, st_spec_m],
        out_shape=[jax.ShapeDtypeStruct((b, seq, inner), BF16),
                   jax.ShapeDtypeStruct((b, MLSTM_HEADS, dh, SCAN_EXT), F32),
                   jax.ShapeDtypeStruct((b, MLSTM_HEADS, 8, LANES), F32)],
        scratch_shapes=[pltpu.VMEM((dh, SCAN_EXT), F32), pltpu.VMEM((8, LANES), F32)],
        compiler_params=_cparams("arbitrary", "arbitrary", "arbitrary"),
        name="mlstm_scan_bwd" if rev else "mlstm_scan_fwd",
    )(*args)


def _ml_out_body(hf_ref, hb_ref, xc_ref, z_ref, h_ref, mod_ref, gn_ref, sk_ref, w_ref, o_ref, *, tm, rows_per_batch):
    i = pl.program_id(0)
    row = (i * tm) // rows_per_batch if rows_per_batch else CTX_MOD_ROW
    hs = hf_ref[...].astype(F32) + hb_ref[...].astype(F32)
    parts = []
    for hd in range(MLSTM_HEADS):
        x = hs[:, hd * MLSTM_HEAD_DIM:(hd + 1) * MLSTM_HEAD_DIM]
        mu = jnp.mean(x, axis=-1, keepdims=True)
        xm = x - mu
        var = jnp.mean(xm * xm, axis=-1, keepdims=True)
        parts.append(xm * lax.rsqrt(var + NORM_EPS))
    hn = jnp.concatenate(parts, axis=1) * gn_ref[...]
    y = (hn + sk_ref[...] * xc_ref[...].astype(F32)) * _silu(z_ref[...].astype(F32))
    y = jnp.dot(y.astype(BF16), w_ref[...], preferred_element_type=F32)
    o_ref[...] = h_ref[...] + _mod_chunk(mod_ref, row, 2) * y


def mlstm_output(hf, hb, xc, up, h, mods, layer, gn_w, skip, w_down, *, rows_per_batch, tm=256):
    m = h.shape[0]
    tm = min(tm, m)
    inner = MLSTM_INNER
    body = functools.partial(_ml_out_body, tm=tm, rows_per_batch=rows_per_batch)
    row_spec = pl.BlockSpec((tm, inner), lambda i: (i, 0))
    return pl.pallas_call(
        body,
        grid=(m // tm,),
        in_specs=[row_spec, row_spec, row_spec,
                  pl.BlockSpec((tm, inner), lambda i: (i, 1)),
                  pl.BlockSpec((tm, D_MODEL), lambda i: (i, 0)),
                  _mod_spec(layer),
                  pl.BlockSpec((1, inner), lambda i: (0, 0)),
                  pl.BlockSpec((1, inner), lambda i: (0, 0)),
                  _resident((inner, D_MODEL), lambda i: (0, 0))],
        out_specs=pl.BlockSpec((tm, D_MODEL), lambda i: (i, 0)),
        out_shape=jax.ShapeDtypeStruct((m, D_MODEL), F32),
        compiler_params=_cparams("arbitrary"),
        name="mlstm_output",
    )(hf, hb, xc, up, h, mods, gn_w.reshape(1, inner), skip.reshape(1, inner), w_down)


def _router_body(xl_ref, xc_ref, mod_ref, g_ref, wr_ref, a_ref, lg_ref, *, tm, rows_per_batch, n_lat):
    i = pl.program_id(0)
    is_lat = i < n_lat
    row = jnp.where(is_lat, (i * tm) // rows_per_batch, CTX_MOD_ROW)
    x = jnp.where(is_lat, xl_ref[...], xc_ref[...])
    a = _norm_mod(x, g_ref[...], _mod_chunk(mod_ref, row, 3), _mod_chunk(mod_ref, row, 4))
    a_ref[...] = a
    lg_ref[...] = jnp.dot(a, wr_ref[...], preferred_element_type=F32, precision=lax.Precision.HIGHEST)


def moe_router(x_lat, x_ctx, mods, layer, g, wr_pad, *, rows_per_batch, tm=512):
    n_lat = x_lat.shape[0] // tm
    n_ctx = 0 if x_ctx is None else x_ctx.shape[0] // tm
    n = (n_lat + n_ctx) * tm
    if x_ctx is None:
        x_ctx = x_lat
    body = functools.partial(_router_body, tm=tm, rows_per_batch=rows_per_batch, n_lat=n_lat)
    return pl.pallas_call(
        body,
        grid=(n_lat + n_ctx,),
        in_specs=[pl.BlockSpec((tm, D_MODEL), lambda i: (jnp.minimum(i, n_lat - 1), 0)),
                  pl.BlockSpec((tm, D_MODEL), lambda i: (jnp.maximum(i - n_lat, 0), 0)),
                  _mod_spec(layer),
                  pl.BlockSpec((1, D_MODEL), lambda i: (0, 0)),
                  pl.BlockSpec((D_MODEL, LANES), lambda i: (0, 0))],
        out_specs=[pl.BlockSpec((tm, D_MODEL), lambda i: (i, 0)),
                   pl.BlockSpec((tm, LANES), lambda i: (i, 0))],
        out_shape=[jax.ShapeDtypeStruct((n, D_MODEL), F32),
                   jax.ShapeDtypeStruct((n, LANES), F32)],
        compiler_params=_cparams("arbitrary"),
        name="moe_router",
    )(x_lat, x_ctx, mods, g.reshape(1, D_MODEL), wr_pad)


def moe_route(logits, tm):
    n = logits.shape[0]
    a_tot = n * TOP_K
    top_v, top_e = lax.top_k(logits[:, :N_EXPERTS], TOP_K)
    gates = jax.nn.softmax(top_v, axis=-1)
    e_flat = top_e.reshape(a_tot).astype(jnp.int32)
    order = jnp.argsort(e_flat).astype(jnp.int32)
    counts = jnp.sum((e_flat[:, None] == jnp.arange(N_EXPERTS, dtype=jnp.int32)[None, :]).astype(jnp.int32), axis=0)
    starts = jnp.cumsum(counts) - counts
    padded = (counts + tm - 1) // tm * tm
    pend = jnp.cumsum(padded)
    pstarts = pend - padded
    n_blocks = (a_tot + N_EXPERTS * (tm - 1)) // tm
    blk_row = jnp.arange(n_blocks, dtype=jnp.int32) * tm
    block_e = jnp.minimum(jnp.searchsorted(pend, blk_row, side="right"), N_EXPERTS - 1).astype(jnp.int32)
    into = blk_row - pstarts[block_e]
    n_valid = jnp.clip(counts[block_e] - into, 0, tm).astype(jnp.int32)
    base = jnp.clip(starts[block_e] + into, 0, a_tot)
    order_pad = jnp.concatenate([order, jnp.zeros((tm,), jnp.int32)])
    pair = jax.vmap(lambda s: lax.dynamic_slice(order_pad, (s,), (tm,)))(base)
    valid = jnp.arange(tm, dtype=jnp.int32)[None, :] < n_valid[:, None]
    src = jnp.where(valid, pair // TOP_K, 0)
    dst = jnp.where(valid, (pair % TOP_K) * n + pair // TOP_K, 0)
    n_used = (pend[-1] // tm).astype(jnp.int32).reshape(1)
    return gates, block_e, n_used, n_valid, src.reshape(n_blocks, 1, tm), dst.reshape(n_blocks, 1, tm)


ROW_UNROLL = 8


def _for_rows(n, fn):
    full = n // ROW_UNROLL

    def group(c, carry):
        for u in range(ROW_UNROLL):
            fn(c * ROW_UNROLL + u)
        return carry
    lax.fori_loop(0, full, group, 0)

    def single(r, carry):
        fn(r)
        return carry
    lax.fori_loop(full * ROW_UNROLL, n, single, 0)


def _moe_body(be_ref, nu_ref, nv_ref, src_ref, srcn_ref, dst_ref, a_hbm, w1_ref, w3_ref, w2_ref, y_hbm,
              xf_ref, xb_ref, acc_ref, gsem, ssem, *, nf):
    i = pl.program_id(0)
    f = pl.program_id(1)
    n_used = nu_ref[0]
    slot = i % 2

    def gather_copy(s, r, tok):
        return pltpu.make_async_copy(a_hbm.at[pl.ds(tok, 1), :], xf_ref.at[s, pl.ds(r, 1), :], gsem.at[s])

    def scatter_copy(s, r, row):
        return pltpu.make_async_copy(acc_ref.at[s, pl.ds(r, 1), :], y_hbm.at[pl.ds(row, 1), :], ssem.at[s])

    def start_gather(s, idx_ref, n):
        _for_rows(n, lambda r: gather_copy(s, r, idx_ref[0, r]).start())

    def wait_gather(s, n):
        _for_rows(n, lambda r: gather_copy(s, r, 0).wait())

    def wait_scatter(s, n):
        _for_rows(n, lambda r: scatter_copy(s, r, 0).wait())

    @pl.when(i < n_used)
    def _():
        @pl.when(f == 0)
        def _():
            @pl.when(i == 0)
            def _():
                xf_ref[...] = jnp.zeros_like(xf_ref)
                start_gather(0, src_ref, nv_ref[0])

            wait_gather(slot, nv_ref[i])
            xb_ref[...] = xf_ref[slot].astype(BF16)

            @pl.when(i + 1 < n_used)
            def _():
                start_gather(1 - slot, srcn_ref, nv_ref[i + 1])

        x = xb_ref[...]
        u = jnp.dot(x, w1_ref[...], preferred_element_type=F32)
        v = jnp.dot(x, w3_ref[...], preferred_element_type=F32)
        p = (_silu(u) * v).astype(BF16)
        y = jnp.dot(p, w2_ref[...], preferred_element_type=F32)

        @pl.when(f == 0)
        def _():
            acc_ref[slot] = y

        @pl.when(f > 0)
        def _():
            acc_ref[slot] += y

        @pl.when(f == nf - 1)
        def _():
            _for_rows(nv_ref[i], lambda r: scatter_copy(slot, r, dst_ref[0, r]).start())

            @pl.when(i > 0)
            def _():
                wait_scatter(1 - slot, nv_ref[i - 1])

            @pl.when(i == n_used - 1)
            def _():
                wait_scatter(slot, nv_ref[i])


def moe_experts(a, block_e, n_used, n_valid, src, dst, w1, w3, w2, *, tf=1792):
    n = a.shape[0]
    n_blocks, _, tm = src.shape
    f_dim = w1.shape[2]
    nf = f_dim // tf
    body = functools.partial(_moe_body, nf=nf)

    def wmap(kind):
        def index_map(i, f, be, nu, nv):
            live = i < nu[0]
            ff = jnp.where(live, f, nf - 1)
            ii = jnp.where(live, i, nu[0] - 1)
            return (be[ii], 0, ff) if kind == "up" else (be[ii], ff, 0)
        return index_map

    idx_spec = pl.BlockSpec((None, 1, tm), lambda i, f, be, nu, nv: (i, 0, 0), memory_space=pltpu.SMEM)
    next_spec = pl.BlockSpec((None, 1, tm), lambda i, f, be, nu, nv: (jnp.minimum(i + 1, n_blocks - 1), 0, 0),
                             memory_space=pltpu.SMEM)
    grid_spec = pltpu.PrefetchScalarGridSpec(
        num_scalar_prefetch=3,
        grid=(n_blocks, nf),
        in_specs=[idx_spec, next_spec, idx_spec,
                  pl.BlockSpec(memory_space=pl.ANY),
                  pl.BlockSpec((None, D_MODEL, tf), wmap("up")),
                  pl.BlockSpec((None, D_MODEL, tf), wmap("up")),
                  pl.BlockSpec((None, tf, D_MODEL), wmap("down"))],
        out_specs=pl.BlockSpec(memory_space=pl.ANY),
        scratch_shapes=[pltpu.VMEM((2, tm, D_MODEL), F32), pltpu.VMEM((tm, D_MODEL), BF16),
                        pltpu.VMEM((2, tm, D_MODEL), F32),
                        pltpu.SemaphoreType.DMA((2,)), pltpu.SemaphoreType.DMA((2,))],
    )
    return pl.pallas_call(
        body,
        grid_spec=grid_spec,
        out_shape=jax.ShapeDtypeStruct((TOP_K * n, D_MODEL), F32),
        compiler_params=_cparams("arbitrary", "arbitrary"),
        name="moe_experts",
    )(block_e, n_used, n_valid, src, src, dst, a, w1, w3, w2)


def _combine_body(h_ref, y0_ref, y1_ref, gt_ref, mod_ref, fg_ref, o_ref, *, tm, rows_per_batch, final):
    i = pl.program_id(0)
    row = (i * tm) // rows_per_batch if rows_per_batch else CTX_MOD_ROW
    gt = gt_ref[...]
    f = y0_ref[...] * gt[:, 0:1] + y1_ref[...] * gt[:, 1:2]
    out = h_ref[...] + _mod_chunk(mod_ref, row, 5) * f
    if final:
        ms = jnp.mean(out * out, axis=-1, keepdims=True)
        out = out * lax.rsqrt(ms + NORM_EPS) * fg_ref[...]
    o_ref[...] = out


def moe_combine(h, y, gates, mods, layer, final_g, *, n_tok, row_off, rows_per_batch, final, tm=512):
    m = h.shape[0]
    tm = min(tm, m)
    o0 = row_off // tm
    o1 = (n_tok + row_off) // tm
    body = functools.partial(_combine_body, tm=tm, rows_per_batch=rows_per_batch, final=final)
    return pl.pallas_call(
        body,
        grid=(m // tm,),
        in_specs=[pl.BlockSpec((tm, D_MODEL), lambda i: (i, 0)),
                  pl.BlockSpec((tm, D_MODEL), lambda i: (o0 + i, 0)),
                  pl.BlockSpec((tm, D_MODEL), lambda i: (o1 + i, 0)),
                  pl.BlockSpec((tm, TOP_K), lambda i: (o0 + i, 0)),
                  _mod_spec(layer),
                  pl.BlockSpec((1, D_MODEL), lambda i: (0, 0))],
        out_specs=pl.BlockSpec((tm, D_MODEL), lambda i: (i, 0)),
        out_shape=jax.ShapeDtypeStruct((m, D_MODEL), F32),
        compiler_params=_cparams("arbitrary"),
        name="moe_combine",
    )(h, y, y, gates, mods, final_g.reshape(1, D_MODEL))


def _na_layer(h_lat, h_ctx, mods, layer, g, w_qkv, b_qkv, rpb, w_out, b_out, with_ctx_out):
    b, seq, _ = h_lat.shape
    c = h_ctx.shape[1]
    qscale = jnp.concatenate([jnp.full((D_MODEL,), (D_MODEL // NA_HEADS) ** -0.5, F32), jnp.ones((2 * D_MODEL,), F32)])
    w = (w_qkv * qscale).astype(BF16)
    bias = b_qkv * qscale
    qkv = nm_matmul(h_lat.reshape(b * seq, D_MODEL), mods, layer, g, w, bias, rows_per_batch=seq, sh=0, sc=1)
    qkv_c = nm_matmul(h_ctx.reshape(b * c, D_MODEL), mods, layer, g, w, bias, rows_per_batch=None, sh=0, sc=1)
    qkv = qkv.reshape(b, seq, 3 * D_MODEL)
    qkv_c = qkv_c.reshape(b, c, 3 * D_MODEL)
    o_lat = na_attention(qkv, qkv_c, na_bias_table(rpb))
    wo = w_out.astype(BF16)
    h_lat = mm_residual(o_lat.reshape(b * seq, D_MODEL), h_lat.reshape(b * seq, D_MODEL), mods, layer, wo, b_out,
                        rows_per_batch=seq, gate=2).reshape(b, seq, D_MODEL)
    if with_ctx_out:
        o_ctx = ctx_attention(qkv_c)
        h_ctx = mm_residual(o_ctx.reshape(b * c, D_MODEL), h_ctx.reshape(b * c, D_MODEL), mods, layer, wo, b_out,
                            rows_per_batch=None, gate=2).reshape(b, c, D_MODEL)
    return h_lat, h_ctx


def _mlstm_layer(h_lat, h_ctx, mods, layer, g, w_up, conv_w, conv_b, w_q, w_k, w_v, w_gates, b_gates,
                 gn_w, skip, w_down, with_ctx_out):
    b, seq, _ = h_lat.shape
    c = h_ctx.shape[1]
    inner = MLSTM_INNER
    wu = w_up.astype(BF16)
    zero_b = jnp.zeros((2 * inner,), F32)
    wq_bd = block_diag_weights(w_q).astype(BF16)
    wk_bd = block_diag_weights(w_k).astype(BF16)
    wkt_bd = jnp.swapaxes(wk_bd, 1, 2)
    wv_bd = block_diag_weights(w_v).astype(BF16)
    ng = w_gates.shape[1]
    wg = jnp.pad(w_gates, ((0, 0), (0, LANES - ng))).astype(BF16)
    wgt = w_gates.T.astype(BF16)
    wd = w_down.astype(BF16)

    def features(h, rows_per_batch):
        n, s, _ = h.shape
        up = nm_matmul(h.reshape(n * s, D_MODEL), mods, layer, g, wu, zero_b, rows_per_batch=rows_per_batch, sh=0, sc=1)
        up = up.reshape(n, s, 2 * inner)
        return up, mlstm_features(up, conv_w, conv_b, wq_bd, wk_bd, wkt_bd, wv_bd, wg, wgt, b_gates)

    up_c, (q_c, kt_c, v_c, xc_c, g_c, gt_c) = features(h_ctx, None)
    up_l, (q_l, kt_l, v_l, xc_l, g_l, gt_l) = features(h_lat, seq)
    hf_c, cf, mf = mlstm_scan(q_c, kt_c, v_c, g_c, gt_c, None, rev=False)
    hb_c, cb, mb = mlstm_scan(q_c, kt_c, v_c, g_c, gt_c, None, rev=True)
    hf_l, _, _ = mlstm_scan(q_l, kt_l, v_l, g_l, gt_l, (cf, mf), rev=False)
    hb_l, _, _ = mlstm_scan(q_l, kt_l, v_l, g_l, gt_l, (cb, mb), rev=True)
    flat = lambda a: a.reshape(-1, a.shape[-1])
    h_lat = mlstm_output(flat(hf_l), flat(hb_l), flat(xc_l), flat(up_l), flat(h_lat), mods, layer, gn_w, skip, wd,
                         rows_per_batch=seq).reshape(b, seq, D_MODEL)
    if with_ctx_out:
        h_ctx = mlstm_output(flat(hf_c), flat(hb_c), flat(xc_c), flat(up_c), flat(h_ctx), mods, layer, gn_w, skip, wd,
                             rows_per_batch=None).reshape(b, c, D_MODEL)
    return h_lat, h_ctx


def _moe_layer(h_lat, h_ctx, mods, layer, g, w_router, w1, w3, w2, final_g, last):
    b, seq, _ = h_lat.shape
    c = h_ctx.shape[1]
    wr = jnp.pad(w_router, ((0, 0), (0, LANES - N_EXPERTS)))
    hl = h_lat.reshape(b * seq, D_MODEL)
    hc = None if last else h_ctx.reshape(b * c, D_MODEL)
    a, logits = moe_router(hl, hc, mods, layer, g, wr, rows_per_batch=seq)
    n_tok = a.shape[0]
    gates, block_e, n_used, n_valid, src, dst = moe_route(logits, MOE_ROWS)
    y = moe_experts(a, block_e, n_used, n_valid, src, dst, w1.astype(BF16), w3.astype(BF16), w2.astype(BF16))
    h_lat = moe_combine(hl, y, gates, mods, layer, final_g, n_tok=n_tok, row_off=0, rows_per_batch=seq,
                        final=last).reshape(b, seq, D_MODEL)
    if not last:
        h_ctx = moe_combine(hc, y, gates, mods, layer, final_g, n_tok=n_tok, row_off=b * seq, rows_per_batch=None,
                            final=False).reshape(b, c, D_MODEL)
    return h_lat, h_ctx


def kernel(x, c, ctx, c_ctx, w_mod, b_mod, norm_g, final_g, na_w_qkv, na_b_qkv, na_rpb, na_w_out, na_b_out,
           pool_w, pool_scale, ml_w_up, ml_conv_w, ml_conv_b, ml_w_q, ml_w_k, ml_w_v, ml_w_gates, ml_b_gates,
           ml_gn_w, ml_skip, ml_w_down, ffn_w1, ffn_w3, ffn_w2, moe_w_router, moe_w1, moe_w3, moe_w2):
    b, seq, _ = x.shape
    n_ctx = ctx.shape[1]
    depth = w_mod.shape[0]
    assert b <= CTX_MOD_ROW
    cond = jnp.zeros((MOD_ROWS, D_MODEL), F32).at[:b].set(c).at[CTX_MOD_ROW].set(c_ctx)
    mods = adaln_all(cond, w_mod, b_mod)
    h_lat, h_ctx = x, ctx
    for i in range(depth):
        last = i == depth - 1
        kind = i % 3
        j = i // 3
        g_tok = norm_g[i, 0]
        if kind == 0:
            h_lat, h_ctx = _na_layer(h_lat, h_ctx, mods, i, g_tok, na_w_qkv[j], na_b_qkv[j], na_rpb[j],
                                     na_w_out[j], na_b_out[j], not last)
        elif kind == 1:
            wp = pool_w[j].astype(BF16)
            h_lat = pool_mixer(h_lat, mods, i, g_tok, wp, pool_scale[j], is_ctx=False)
            if not last:
                h_ctx = pool_mixer(h_ctx, mods, i, g_tok, wp, pool_scale[j], is_ctx=True)
        else:
            h_lat, h_ctx = _mlstm_layer(h_lat, h_ctx, mods, i, g_tok, ml_w_up[j], ml_conv_w[j], ml_conv_b[j],
                                        ml_w_q[j], ml_w_k[j], ml_w_v[j], ml_w_gates[j], ml_b_gates[j],
                                        ml_gn_w[j], ml_skip[j], ml_w_down[j], not last)
        e = i // 2
        g_ch = norm_g[i, 1]
        if i % 2 == 0:
            w1, w3, w2 = ffn_w1[e].astype(BF16), ffn_w3[e].astype(BF16), ffn_w2[e].astype(BF16)
            h_lat = ffn_dense(h_lat.reshape(b * seq, D_MODEL), mods, i, g_ch, w1, w3, w2,
                              rows_per_batch=seq).reshape(b, seq, D_MODEL)
            if not last:
                h_ctx = ffn_dense(h_ctx.reshape(b * n_ctx, D_MODEL), mods, i, g_ch, w1, w3, w2,
                                  rows_per_batch=None).reshape(b, n_ctx, D_MODEL)
        else:
            h_lat, h_ctx = _moe_layer(h_lat, h_ctx, mods, i, g_ch, moe_w_router[e], moe_w1[e], moe_w3[e],
                                      moe_w2[e], final_g, last)
    return h_lat
```

```python
import functools

import jax
import jax.numpy as jnp
from jax import lax
from jax.experimental import pallas as pl
from jax.experimental.pallas import tpu as pltpu

F32 = jnp.float32
BF16 = jnp.bfloat16

D_MODEL = 1024
N_MOD = 6
NORM_EPS = 1e-6
GRID_W = 64
NA_HEADS = 16
NA_WIN_ROWS = 8
NA_WIN_COLS = 16
POOL_WINDOWS = (2, 4, 8, 16)
POOL_GROUP_DIM = D_MODEL // len(POOL_WINDOWS)
MLSTM_INNER = 2 * D_MODEL
MLSTM_HEADS = 4
MLSTM_HEAD_DIM = MLSTM_INNER // MLSTM_HEADS
MLSTM_CONV = 4
MLSTM_QKV_BLOCK = 4
N_EXPERTS = 8
TOP_K = 2

LANES = 128
MOD_ROWS = 8
CTX_MOD_ROW = 4
VMEM_LIMIT_BYTES = 56 * 1024 * 1024
NEG_BIG = -1e30
SCAN_CHUNK = 128
SCAN_EXT = MLSTM_HEAD_DIM + LANES
MOE_ROWS = 512


def _cparams(*sem):
    return pltpu.CompilerParams(dimension_semantics=sem, vmem_limit_bytes=VMEM_LIMIT_BYTES)


def _resident(shape, index_map):
    return pl.BlockSpec(shape, index_map, pipeline_mode=pl.Buffered(1))


def _silu(x):
    return x * jax.nn.sigmoid(x)


def _norm_mod(x, g, shift, scale):
    ms = jnp.mean(x * x, axis=-1, keepdims=True)
    y = x * lax.rsqrt(ms + NORM_EPS) * g
    return y * (1.0 + scale) + shift


def _mod_chunk(mod_ref, row, j):
    return mod_ref[pl.ds(row, 1), pl.ds(j * D_MODEL, D_MODEL)]


def _mod_spec(layer):
    return pl.BlockSpec((None, MOD_ROWS, N_MOD * D_MODEL), lambda *_: (layer, 0, 0))


def _adaln_body(c_ref, w_ref, b_ref, o_ref):
    s = _silu(c_ref[...])
    o_ref[...] = jnp.dot(s, w_ref[...], preferred_element_type=F32) + b_ref[...]


def adaln_all(cond, w_mod, b_mod):
    depth = w_mod.shape[0]
    n = N_MOD * D_MODEL
    tn = 1536
    return pl.pallas_call(
        _adaln_body,
        grid=(depth, n // tn),
        in_specs=[pl.BlockSpec((MOD_ROWS, D_MODEL), lambda l, j: (0, 0)),
                  pl.BlockSpec((None, D_MODEL, tn), lambda l, j: (l, 0, j)),
                  pl.BlockSpec((None, 1, tn), lambda l, j: (l, 0, j))],
        out_specs=pl.BlockSpec((None, MOD_ROWS, tn), lambda l, j: (l, 0, j)),
        out_shape=jax.ShapeDtypeStruct((depth, MOD_ROWS, n), F32),
        compiler_params=_cparams("arbitrary", "arbitrary"),
        name="adaln",
    )(cond, w_mod, b_mod.reshape(depth, 1, n))


def _nm_matmul_body(x_ref, mod_ref, g_ref, w_ref, b_ref, o_ref, *, tm, rows_per_batch, sh, sc, nc):
    i = pl.program_id(0)
    row = (i * tm) // rows_per_batch if rows_per_batch else CTX_MOD_ROW
    a = _norm_mod(x_ref[...], g_ref[...], _mod_chunk(mod_ref, row, sh), _mod_chunk(mod_ref, row, sc)).astype(BF16)
    n = o_ref.shape[1]
    for c in range(n // nc):
        sl = slice(c * nc, (c + 1) * nc)
        y = jnp.dot(a, w_ref[:, sl], preferred_element_type=F32) + b_ref[:, sl]
        o_ref[:, sl] = y.astype(o_ref.dtype)


def nm_matmul(x, mods, layer, g, w, bias, *, rows_per_batch, sh, sc, tm=512, nc=1024, out_dtype=BF16):
    m, n = x.shape[0], w.shape[1]
    tm = min(tm, m)
    body = functools.partial(_nm_matmul_body, tm=tm, rows_per_batch=rows_per_batch, sh=sh, sc=sc, nc=nc)
    return pl.pallas_call(
        body,
        grid=(m // tm,),
        in_specs=[pl.BlockSpec((tm, D_MODEL), lambda i: (i, 0)),
                  _mod_spec(layer),
                  pl.BlockSpec((1, D_MODEL), lambda i: (0, 0)),
                  _resident((D_MODEL, n), lambda i: (0, 0)),
                  pl.BlockSpec((1, n), lambda i: (0, 0))],
        out_specs=pl.BlockSpec((tm, n), lambda i: (i, 0)),
        out_shape=jax.ShapeDtypeStruct((m, n), out_dtype),
        compiler_params=_cparams("arbitrary"),
        name="nm_matmul",
    )(x, mods, g.reshape(1, D_MODEL), w, bias.reshape(1, n))


def _mm_res_body(a_ref, h_ref, mod_ref, w_ref, b_ref, o_ref, *, tm, rows_per_batch, gate):
    i = pl.program_id(0)
    row = (i * tm) // rows_per_batch if rows_per_batch else CTX_MOD_ROW
    y = jnp.dot(a_ref[...], w_ref[...], preferred_element_type=F32) + b_ref[...]
    o_ref[...] = h_ref[...] + _mod_chunk(mod_ref, row, gate) * y


def mm_residual(a, h, mods, layer, w, bias, *, rows_per_batch, gate, tm=512):
    m, k = a.shape
    tm = min(tm, m)
    body = functools.partial(_mm_res_body, tm=tm, rows_per_batch=rows_per_batch, gate=gate)
    return pl.pallas_call(
        body,
        grid=(m // tm,),
        in_specs=[pl.BlockSpec((tm, k), lambda i: (i, 0)),
                  pl.BlockSpec((tm, D_MODEL), lambda i: (i, 0)),
                  _mod_spec(layer),
                  _resident((k, D_MODEL), lambda i: (0, 0)),
                  pl.BlockSpec((1, D_MODEL), lambda i: (0, 0))],
        out_specs=pl.BlockSpec((tm, D_MODEL), lambda i: (i, 0)),
        out_shape=jax.ShapeDtypeStruct((m, D_MODEL), F32),
        compiler_params=_cparams("arbitrary"),
        name="mm_residual",
    )(a, h, mods, w, bias.reshape(1, D_MODEL))


def _ffn_body(h_ref, mod_ref, g_ref, w1_ref, w3_ref, w2_ref, o_ref, *, tm, rows_per_batch):
    i = pl.program_id(0)
    row = (i * tm) // rows_per_batch if rows_per_batch else CTX_MOD_ROW
    h = h_ref[...]
    a = _norm_mod(h, g_ref[...], _mod_chunk(mod_ref, row, 3), _mod_chunk(mod_ref, row, 4)).astype(BF16)
    u = jnp.dot(a, w1_ref[...], preferred_element_type=F32)
    v = jnp.dot(a, w3_ref[...], preferred_element_type=F32)
    p = (_silu(u) * v).astype(BF16)
    y = jnp.dot(p, w2_ref[...], preferred_element_type=F32)
    o_ref[...] = h + _mod_chunk(mod_ref, row, 5) * y


def ffn_dense(h, mods, layer, g, w1, w3, w2, *, rows_per_batch, tm=256):
    m = h.shape[0]
    f = w1.shape[1]
    tm = min(tm, m)
    body = functools.partial(_ffn_body, tm=tm, rows_per_batch=rows_per_batch)
    return pl.pallas_call(
        body,
        grid=(m // tm,),
        in_specs=[pl.BlockSpec((tm, D_MODEL), lambda i: (i, 0)),
                  _mod_spec(layer),
                  pl.BlockSpec((1, D_MODEL), lambda i: (0, 0)),
                  _resident((D_MODEL, f), lambda i: (0, 0)),
                  _resident((D_MODEL, f), lambda i: (0, 0)),
                  _resident((f, D_MODEL), lambda i: (0, 0))],
        out_specs=pl.BlockSpec((tm, D_MODEL), lambda i: (i, 0)),
        out_shape=jax.ShapeDtypeStruct((m, D_MODEL), F32),
        compiler_params=_cparams("arbitrary"),
        name="ffn_dense",
    )(h, mods, g.reshape(1, D_MODEL), w1, w3, w2)


def na_bias_table(rpb):
    h = rpb.shape[0]
    col = jnp.arange(GRID_W)
    c0 = jnp.clip(col - NA_WIN_COLS // 2, 0, GRID_W - NA_WIN_COLS)
    col_ok = (col[None, :] >= c0[:, None]) & (col[None, :] < c0[:, None] + NA_WIN_COLS)
    dcol = jnp.clip(col[None, :] - col[:, None], 1 - NA_WIN_COLS, NA_WIN_COLS - 1) + (NA_WIN_COLS - 1)
    n_drow = 2 * NA_WIN_ROWS - 1
    t = jnp.where(col_ok[None, None], rpb[:, :, dcol].astype(F32), NEG_BIG)
    t = t.transpose(0, 2, 1, 3).reshape(h // 2, 2 * GRID_W, n_drow * GRID_W)
    tiles = [t[:, :, (NA_WIN_ROWS - 1 - off) * GRID_W:(2 * NA_WIN_ROWS - 1 - off) * GRID_W]
             for off in range(NA_WIN_ROWS)]
    return jnp.stack(tiles, axis=1)


def _stack_heads(q):
    lo = lax.broadcasted_iota(jnp.int32, q.shape, 1) < (LANES // 2)
    zero = jnp.zeros_like(q)
    return jnp.concatenate([jnp.where(lo, q, zero), jnp.where(lo, zero, q)], axis=0)


def _unstack_heads(o):
    n = o.shape[0] // 2
    lo = lax.broadcasted_iota(jnp.int32, (n, LANES), 1) < (LANES // 2)
    return jnp.where(lo, o[:n], o[n:])


_NT = (((1,), (1,)), ((), ()))
NA_GROUP = 4


def _na_body(q_ref, k_ref, v_ref, kc_ref, vc_ref, bias_ref, o_ref, *, rows):
    kc = kc_ref[...]
    vc = vc_ref[...]
    kwin = NA_WIN_ROWS * GRID_W

    def scores(r):
        r0 = jnp.clip(r - NA_WIN_ROWS // 2, 0, rows - NA_WIN_ROWS)
        qoff = pl.multiple_of(r * GRID_W, GRID_W)
        koff = pl.multiple_of(r0 * GRID_W, GRID_W)
        qs = _stack_heads(q_ref[pl.ds(qoff, GRID_W), :])
        k = k_ref[pl.ds(koff, kwin), :]
        s_loc = lax.dot_general(qs, k, _NT, preferred_element_type=F32) + bias_ref[r - r0]
        s_ctx = lax.dot_general(qs, kc, _NT, preferred_element_type=F32)
        return qoff, koff, s_loc, s_ctx

    def probs(s_loc, s_ctx):
        m = jnp.maximum(jnp.max(s_loc, axis=-1, keepdims=True), jnp.max(s_ctx, axis=-1, keepdims=True))
        p_loc = jnp.exp(s_loc - m)
        p_ctx = jnp.exp(s_ctx - m)
        l = jnp.sum(p_loc, axis=-1, keepdims=True) + jnp.sum(p_ctx, axis=-1, keepdims=True)
        return p_loc.astype(BF16), p_ctx.astype(BF16), l

    def group(gi, carry):
        sc = [scores(gi * NA_GROUP + u) for u in range(NA_GROUP)]
        pr = [probs(s[2], s[3]) for s in sc]
        for (qoff, koff, _, _), (p_loc, p_ctx, l) in zip(sc, pr):
            v = v_ref[pl.ds(koff, kwin), :]
            o = (jnp.dot(p_loc, v, preferred_element_type=F32)
                 + jnp.dot(p_ctx, vc, preferred_element_type=F32)) / l
            o_ref[pl.ds(qoff, GRID_W), :] = _unstack_heads(o).astype(o_ref.dtype)
        return carry

    lax.fori_loop(0, rows // NA_GROUP, group, 0)


def na_attention(qkv, qkv_ctx, bias):
    b, l, _ = qkv.shape
    c = qkv_ctx.shape[1]
    hp = D_MODEL // LANES
    body = functools.partial(_na_body, rows=l // GRID_W)
    return pl.pallas_call(
        body,
        grid=(b, hp),
        in_specs=[pl.BlockSpec((None, l, LANES), lambda i, j: (i, 0, j)),
                  pl.BlockSpec((None, l, LANES), lambda i, j: (i, 0, hp + j)),
                  pl.BlockSpec((None, l, LANES), lambda i, j: (i, 0, 2 * hp + j)),
                  pl.BlockSpec((None, c, LANES), lambda i, j: (i, 0, hp + j)),
                  pl.BlockSpec((None, c, LANES), lambda i, j: (i, 0, 2 * hp + j)),
                  pl.BlockSpec((None,) + bias.shape[1:], lambda i, j: (j, 0, 0, 0))],
        out_specs=pl.BlockSpec((None, l, LANES), lambda i, j: (i, 0, j)),
        out_shape=jax.ShapeDtypeStruct((b, l, D_MODEL), BF16),
        compiler_params=_cparams("arbitrary", "arbitrary"),
        name="na_attention",
    )(qkv, qkv, qkv, qkv_ctx, qkv_ctx, bias)


def _ctx_attn_body(q_ref, k_ref, v_ref, o_ref):
    qs = _stack_heads(q_ref[...])
    s = lax.dot_general(qs, k_ref[...], _NT, preferred_element_type=F32)
    p = jnp.exp(s - jnp.max(s, axis=-1, keepdims=True))
    l = jnp.sum(p, axis=-1, keepdims=True)
    o = jnp.dot(p.astype(BF16), v_ref[...], preferred_element_type=F32) / l
    o_ref[...] = _unstack_heads(o).astype(o_ref.dtype)


def ctx_attention(qkv_ctx):
    b, c, _ = qkv_ctx.shape
    hp = D_MODEL // LANES
    return pl.pallas_call(
        _ctx_attn_body,
        grid=(b, hp),
        in_specs=[pl.BlockSpec((None, c, LANES), lambda i, j: (i, 0, j)),
                  pl.BlockSpec((None, c, LANES), lambda i, j: (i, 0, hp + j)),
                  pl.BlockSpec((None, c, LANES), lambda i, j: (i, 0, 2 * hp + j))],
        out_specs=pl.BlockSpec((None, c, LANES), lambda i, j: (i, 0, j)),
        out_shape=jax.ShapeDtypeStruct((b, c, D_MODEL), BF16),
        compiler_params=_cparams("arbitrary", "arbitrary"),
        name="ctx_attention",
    )(qkv_ctx, qkv_ctx, qkv_ctx)


POOL_HALO = 8


def _pool_body(prev_ref, cur_ref, next_ref, mod_ref, g_ref, wp_ref, ps_ref, o_ref, *, tl, seq, is_ctx):
    b = pl.program_id(0)
    j = pl.program_id(1)
    row = CTX_MOD_ROW if is_ctx else b
    g = g_ref[...]
    sh = _mod_chunk(mod_ref, row, 0)
    sc = _mod_chunk(mod_ref, row, 1)
    h = cur_ref[...]
    a_cur = _norm_mod(h, g, sh, sc)
    a_prev = _norm_mod(prev_ref[...], g, sh, sc) * (j > 0).astype(F32)
    a_next = _norm_mod(next_ref[...], g, sh, sc) * (j < seq // tl - 1).astype(F32)
    ext = jnp.concatenate([a_prev, a_cur, a_next], axis=0)
    t = j * tl + lax.broadcasted_iota(jnp.int32, (tl, 1), 0)
    outs = []
    for gi, w in enumerate(POOL_WINDOWS):
        sl = slice(gi * POOL_GROUP_DIM, (gi + 1) * POOL_GROUP_DIM)
        p = ext[:, sl]
        step = 1
        while step < w:
            n = p.shape[0]
            p = p[:n - step] + p[step:]
            step *= 2
        off = POOL_HALO - w // 2
        cnt = jnp.minimum(t + w // 2, seq) - jnp.maximum(t - w // 2, 0)
        pooled = p[off:off + tl] / cnt.astype(F32) - a_cur[:, sl]
        outs.append(jnp.dot(pooled.astype(BF16), wp_ref[gi], preferred_element_type=F32))
    y = jnp.concatenate(outs, axis=1) * ps_ref[...]
    o_ref[...] = h + _mod_chunk(mod_ref, row, 2) * y


def pool_mixer(h, mods, layer, g, w_pool, pool_scale, *, is_ctx, tl=512):
    b, seq, _ = h.shape
    tl = min(tl, seq)
    nh = tl // POOL_HALO
    last = seq // POOL_HALO - 1
    body = functools.partial(_pool_body, tl=tl, seq=seq, is_ctx=is_ctx)
    return pl.pallas_call(
        body,
        grid=(b, seq // tl),
        in_specs=[pl.BlockSpec((None, POOL_HALO, D_MODEL), lambda i, j: (i, jnp.maximum(j * nh - 1, 0), 0)),
                  pl.BlockSpec((None, tl, D_MODEL), lambda i, j: (i, j, 0)),
                  pl.BlockSpec((None, POOL_HALO, D_MODEL), lambda i, j: (i, jnp.minimum((j + 1) * nh, last), 0)),
                  _mod_spec(layer),
                  pl.BlockSpec((1, D_MODEL), lambda i, j: (0, 0)),
                  pl.BlockSpec(w_pool.shape, lambda i, j: (0, 0, 0)),
                  pl.BlockSpec((1, D_MODEL), lambda i, j: (0, 0))],
        out_specs=pl.BlockSpec((None, tl, D_MODEL), lambda i, j: (i, j, 0)),
        out_shape=jax.ShapeDtypeStruct(h.shape, F32),
        compiler_params=_cparams("arbitrary", "arbitrary"),
        name="pool_mixer",
    )(h, h, h, mods, g.reshape(1, D_MODEL), w_pool, pool_scale.reshape(1, D_MODEL))


CONV_HALO = 16


def block_diag_weights(w):
    nb = LANES // MLSTM_QKV_BLOCK
    wc = w.reshape(-1, nb, MLSTM_QKV_BLOCK, MLSTM_QKV_BLOCK)
    eye = jnp.eye(nb, dtype=w.dtype)
    bd = jnp.einsum("cnij,nm->cnimj", wc, eye)
    return bd.reshape(-1, LANES, LANES)


def _ml_feat_body(prev_ref, cur_ref, next_ref, cw_ref, cb_ref, wq_ref, wk_ref, wkt_ref, wv_ref,
                  wg_ref, wgt_ref, bg_ref, bgt_ref,
                  q_ref, kt_ref, v_ref, xc_ref, g_ref, gt_ref, *, tl, seq):
    j = pl.program_id(1)
    cur = cur_ref[...]
    prev = prev_ref[...].astype(F32) * (j > 0).astype(F32)
    nxt = next_ref[...].astype(F32) * (j < seq // tl - 1).astype(F32)
    ext = jnp.concatenate([prev, cur.astype(F32), nxt], axis=0)
    left = MLSTM_CONV // 2
    xc = cb_ref[...]
    for tap in range(MLSTM_CONV):
        o = CONV_HALO - left + tap
        xc = xc + ext[o:o + tl] * cw_ref[tap:tap + 1, :]
    xc = _silu(xc)
    xcb = xc.astype(BF16)
    xc_ref[...] = xcb
    t = SCAN_CHUNK
    qscale = MLSTM_HEAD_DIM ** -0.5
    acc = jnp.zeros((tl, LANES), F32)
    acct = jnp.zeros((wgt_ref.shape[0], tl), F32)
    n_lane_blocks = MLSTM_INNER // LANES
    for c in range(n_lane_blocks):
        sl = slice(c * LANES, (c + 1) * LANES)
        xs = xcb[:, sl]
        q = jnp.dot(xs, wq_ref[c], preferred_element_type=F32)
        k = jnp.dot(xs, wk_ref[c], preferred_element_type=F32)
        v = jnp.dot(cur[:, sl], wv_ref[c], preferred_element_type=F32)
        q_ref[:, sl] = (q * qscale).astype(BF16)
        v_ref[:, sl] = v.astype(BF16)
        for cc in range(tl // t):
            kt = lax.dot_general(wkt_ref[c], xs[cc * t:(cc + 1) * t], _NT, preferred_element_type=F32)
            kt_ref[cc, sl, :] = kt.astype(BF16)
        for part, val in enumerate((q, k, v)):
            vb = val.astype(BF16)
            rows = slice(part * MLSTM_INNER + c * LANES, part * MLSTM_INNER + (c + 1) * LANES)
            acc = acc + jnp.dot(vb, wg_ref[rows, :], preferred_element_type=F32)
            acct = acct + lax.dot_general(wgt_ref[:, rows], vb, _NT, preferred_element_type=F32)
    ng = g_ref.shape[1]
    g_ref[...] = acc[:, :ng] + bg_ref[...]
    gt = acct + bgt_ref[...]
    for cc in range(tl // t):
        gt_ref[cc] = gt[:, cc * t:(cc + 1) * t]


def mlstm_features(up, conv_w, conv_b, wq_bd, wk_bd, wkt_bd, wv_bd, wg, wgt, bg, *, tl=256):
    b, seq, _ = up.shape
    tl = min(tl, seq)
    t = SCAN_CHUNK
    nh = tl // CONV_HALO
    last = seq // CONV_HALO - 1
    ng = wgt.shape[0]
    inner = MLSTM_INNER
    body = functools.partial(_ml_feat_body, tl=tl, seq=seq)
    full = lambda a: pl.BlockSpec(a.shape, lambda i, j: (0,) * a.ndim)
    cw = conv_w
    cb = conv_b.reshape(1, inner)
    bgr = bg.reshape(1, ng)
    bgc = bg.reshape(ng, 1)
    return pl.pallas_call(
        body,
        grid=(b, seq // tl),
        in_specs=[pl.BlockSpec((None, CONV_HALO, inner), lambda i, j: (i, jnp.maximum(j * nh - 1, 0), 0)),
                  pl.BlockSpec((None, tl, inner), lambda i, j: (i, j, 0)),
                  pl.BlockSpec((None, CONV_HALO, inner), lambda i, j: (i, jnp.minimum((j + 1) * nh, last), 0)),
                  full(cw), full(cb), full(wq_bd), full(wk_bd), full(wkt_bd), full(wv_bd),
                  full(wg), full(wgt), full(bgr), full(bgc)],
        out_specs=[pl.BlockSpec((None, tl, inner), lambda i, j: (i, j, 0)),
                   pl.BlockSpec((None, tl // t, inner, t), lambda i, j: (i, j, 0, 0)),
                   pl.BlockSpec((None, tl, inner), lambda i, j: (i, j, 0)),
                   pl.BlockSpec((None, tl, inner), lambda i, j: (i, j, 0)),
                   pl.BlockSpec((None, tl, ng), lambda i, j: (i, j, 0)),
                   pl.BlockSpec((None, tl // t, ng, t), lambda i, j: (i, j, 0, 0))],
        out_shape=[jax.ShapeDtypeStruct((b, seq, inner), BF16),
                   jax.ShapeDtypeStruct((b, seq // t, inner, t), BF16),
                   jax.ShapeDtypeStruct((b, seq, inner), BF16),
                   jax.ShapeDtypeStruct((b, seq, inner), BF16),
                   jax.ShapeDtypeStruct((b, seq, ng), F32),
                   jax.ShapeDtypeStruct((b, seq // t, ng, t), F32)],
        compiler_params=_cparams("arbitrary", "arbitrary"),
        name="mlstm_features",
    )(up, up, up, cw, cb, wq_bd, wk_bd, wkt_bd, wv_bd, wg, wgt, bgr, bgc)


def _log_sigmoid(x):
    return jnp.minimum(x, 0.0) - jnp.log1p(jnp.exp(-jnp.abs(x)))


def _scan_body(*refs, rev, nchunk, nblk, has_init):
    if has_init:
        q_ref, kt_ref, v_ref, g_ref, gt_ref, c0_ref, m0_ref, h_ref, cf_ref, mf_ref, c_sc, m_sc = refs
    else:
        q_ref, kt_ref, v_ref, g_ref, gt_ref, h_ref, cf_ref, mf_ref, c_sc, m_sc = refs
    hd = pl.program_id(1)
    j = pl.program_id(2)
    t = SCAN_CHUNK
    dh = MLSTM_HEAD_DIM

    @pl.when(j == 0)
    def _():
        if has_init:
            c_sc[...] = c0_ref[...]
            m_sc[...] = m0_ref[...]
        else:
            c_sc[...] = jnp.zeros_like(c_sc)
            m_sc[...] = jnp.zeros_like(m_sc)

    ci = (2 if rev else 0) * MLSTM_HEADS + hd
    cf = (3 if rev else 1) * MLSTM_HEADS + hd
    ng = g_ref.shape[1]
    lane = lax.broadcasted_iota(jnp.int32, (t, ng), 1)
    r_io = lax.broadcasted_iota(jnp.int32, (t, t), 0)
    c_io = lax.broadcasted_iota(jnp.int32, (t, t), 1)
    seen = (c_io >= r_io) if rev else (c_io <= r_io)
    seen_t = (r_io >= c_io) if rev else (r_io <= c_io)
    seen_f = seen.astype(F32)
    seen_tf = seen_t.astype(F32)
    ones_blk = jnp.ones((t, LANES), BF16)

    order = range(nchunk - 1, -1, -1) if rev else range(nchunk)
    for cc in order:
        rows = slice(cc * t, (cc + 1) * t)
        q = q_ref[rows, :]
        kt = kt_ref[cc]
        vext = jnp.concatenate([v_ref[rows, :], ones_blk], axis=1)
        g = g_ref[rows, :]
        i_col = jnp.sum(jnp.where(lane == ci, g, 0.0), axis=1, keepdims=True)
        f_col = jnp.sum(jnp.where(lane == cf, g, 0.0), axis=1, keepdims=True)
        i_row = gt_ref[cc, pl.ds(ci, 1), :]
        f_row = gt_ref[cc, pl.ds(cf, 1), :]
        lf_col = _log_sigmoid(f_col)
        lf_row = _log_sigmoid(f_row)
        b_col = jnp.sum(seen_f * lf_row, axis=1, keepdims=True)
        b_row = jnp.sum(seen_tf * lf_col, axis=0, keepdims=True)
        m_prev = m_sc[0:1, 0:1]
        dmat = jnp.where(seen, b_col - b_row + i_row, NEG_BIG)
        inter = b_col + m_prev
        m_t = jnp.maximum(inter, jnp.max(dmat, axis=1, keepdims=True))
        s = jnp.dot(q, kt, preferred_element_type=F32)
        a = (s * jnp.exp(dmat - m_t)).astype(BF16)
        w_int = jnp.exp(inter - m_t)
        cb = c_sc[...].astype(BF16)
        r = jnp.dot(a, vext, preferred_element_type=F32) + jnp.dot(q, cb, preferred_element_type=F32) * w_int
        den = r[:, dh:dh + 1]
        hc = r[:, :dh] / jnp.maximum(jnp.abs(den), jnp.exp(-m_t))
        h_ref[rows, :] = hc.astype(h_ref.dtype)
        b_end = jnp.sum(lf_row, axis=1, keepdims=True)
        g_row = b_end - b_row + i_row
        m_new = jnp.maximum(b_end + m_prev, jnp.max(g_row, axis=1, keepdims=True))
        decay = jnp.exp(b_end + m_prev - m_new)
        kw = (kt.astype(F32) * jnp.exp(g_row - m_new)).astype(BF16)
        c_sc[...] = decay * c_sc[...] + jnp.dot(kw, vext, preferred_element_type=F32)
        m_sc[...] = jnp.broadcast_to(m_new, m_sc.shape)

    @pl.when(j == nblk - 1)
    def _():
        cf_ref[...] = c_sc[...]
        mf_ref[...] = m_sc[...]


def mlstm_scan(q, kt, v, g, gt, state, *, rev, tb=512):
    b, seq, inner = q.shape
    t = SCAN_CHUNK
    tb = min(tb, seq)
    nblk = seq // tb
    nchunk = tb // t
    dh = MLSTM_HEAD_DIM
    ng = g.shape[2]
    has_init = state is not None
    blk = (lambda j: nblk - 1 - j) if rev else (lambda j: j)
    body = functools.partial(_scan_body, rev=rev, nchunk=nchunk, nblk=nblk, has_init=has_init)
    in_specs = [pl.BlockSpec((None, tb, dh), lambda i, h, j: (i, blk(j), h)),
                pl.BlockSpec((None, nchunk, dh, t), lambda i, h, j: (i, blk(j), h, 0)),
                pl.BlockSpec((None, tb, dh), lambda i, h, j: (i, blk(j), h)),
                pl.BlockSpec((None, tb, ng), lambda i, h, j: (i, blk(j), 0)),
                pl.BlockSpec((None, nchunk, ng, t), lambda i, h, j: (i, blk(j), 0, 0))]
    args = [q, kt, v, g, gt]
    st_spec_c = pl.BlockSpec((None, None, dh, SCAN_EXT), lambda i, h, j: (i, h, 0, 0))
    st_spec_m = pl.BlockSpec((None, None, 8, LANES), lambda i, h, j: (i, h, 0, 0))
    if has_init:
        in_specs += [st_spec_c, st_spec_m]
        args += list(state)
    return pl.pallas_call(
        body,
        grid=(b, MLSTM_HEADS, nblk),
        in_specs=in_specs,
        out_specs=[pl.BlockSpec((None, tb, dh), lambda i, h, j: (i, blk(j), h)), st_spec_c, st_spec_m],
        out_shape=[jax.ShapeDtypeStruct((b, seq, inner), BF16),
                   jax.ShapeDtypeStruct((b, MLSTM_HEADS, dh, SCAN_EXT), F32),
                   jax.ShapeDtypeStruct((b, MLSTM_HEADS, 8, LANES), F32)],
        scratch_shapes=[pltpu.VMEM((dh, SCAN_EXT), F32), pltpu.VMEM((8, LANES), F32)],
        compiler_params=_cparams("arbitrary", "arbitrary", "arbitrary"),
        name="mlstm_scan_bwd" if rev else "mlstm_scan_fwd",
    )(*args)


def _ml_out_body(hf_ref, hb_ref, xc_ref, z_ref, h_ref, mod_ref, gn_ref, sk_ref, w_ref, o_ref, *, tm, rows_per_batch):
    i = pl.program_id(0)
    row = (i * tm) // rows_per_batch if rows_per_batch else CTX_MOD_ROW
    hs = hf_ref[...].astype(F32) + hb_ref[...].astype(F32)
    parts = []
    for hd in range(MLSTM_HEADS):
        x = hs[:, hd * MLSTM_HEAD_DIM:(hd + 1) * MLSTM_HEAD_DIM]
        mu = jnp.mean(x, axis=-1, keepdims=True)
        xm = x - mu
        var = jnp.mean(xm * xm, axis=-1, keepdims=True)
        parts.append(xm * lax.rsqrt(var + NORM_EPS))
    hn = jnp.concatenate(parts, axis=1) * gn_ref[...]
    y = (hn + sk_ref[...] * xc_ref[...].astype(F32)) * _silu(z_ref[...].astype(F32))
    y = jnp.dot(y.astype(BF16), w_ref[...], preferred_element_type=F32)
    o_ref[...] = h_ref[...] + _mod_chunk(mod_ref, row, 2) * y


def mlstm_output(hf, hb, xc, up, h, mods, layer, gn_w, skip, w_down, *, rows_per_batch, tm=256):
    m = h.shape[0]
    tm = min(tm, m)
    inner = MLSTM_INNER
    body = functools.partial(_ml_out_body, tm=tm, rows_per_batch=rows_per_batch)
    row_spec = pl.BlockSpec((tm, inner), lambda i: (i, 0))
    return pl.pallas_call(
        body,
        grid=(m // tm,),
        in_specs=[row_spec, row_spec, row_spec,
                  pl.BlockSpec((tm, inner), lambda i: (i, 1)),
                  pl.BlockSpec((tm, D_MODEL), lambda i: (i, 0)),
                  _mod_spec(layer),
                  pl.BlockSpec((1, inner), lambda i: (0, 0)),
                  pl.BlockSpec((1, inner), lambda i: (0, 0)),
                  _resident((inner, D_MODEL), lambda i: (0, 0))],
        out_specs=pl.BlockSpec((tm, D_MODEL), lambda i: (i, 0)),
        out_shape=jax.ShapeDtypeStruct((m, D_MODEL), F32),
        compiler_params=_cparams("arbitrary"),
        name="mlstm_output",
    )(hf, hb, xc, up, h, mods, gn_w.reshape(1, inner), skip.reshape(1, inner), w_down)


def _router_body(xl_ref, xc_ref, mod_ref, g_ref, wr_ref, a_ref, lg_ref, *, tm, rows_per_batch, n_lat):
    i = pl.program_id(0)
    is_lat = i < n_lat
    row = jnp.where(is_lat, (i * tm) // rows_per_batch, CTX_MOD_ROW)
    x = jnp.where(is_lat, xl_ref[...], xc_ref[...])
    a = _norm_mod(x, g_ref[...], _mod_chunk(mod_ref, row, 3), _mod_chunk(mod_ref, row, 4))
    a_ref[...] = a
    lg_ref[...] = jnp.dot(a, wr_ref[...], preferred_element_type=F32, precision=lax.Precision.HIGHEST)


def moe_router(x_lat, x_ctx, mods, layer, g, wr_pad, *, rows_per_batch, tm=512):
    n_lat = x_lat.shape[0] // tm
    n_ctx = 0 if x_ctx is None else x_ctx.shape[0] // tm
    n = (n_lat + n_ctx) * tm
    if x_ctx is None:
        x_ctx = x_lat
    body = functools.partial(_router_body, tm=tm, rows_per_batch=rows_per_batch, n_lat=n_lat)
    return pl.pallas_call(
        body,
        grid=(n_lat + n_ctx,),
        in_specs=[pl.BlockSpec((tm, D_MODEL), lambda i: (jnp.minimum(i, n_lat - 1), 0)),
                  pl.BlockSpec((tm, D_MODEL), lambda i: (jnp.maximum(i - n_lat, 0), 0)),
                  _mod_spec(layer),
                  pl.BlockSpec((1, D_MODEL), lambda i: (0, 0)),
                  pl.BlockSpec((D_MODEL, LANES), lambda i: (0, 0))],
        out_specs=[pl.BlockSpec((tm, D_MODEL), lambda i: (i, 0)),
                   pl.BlockSpec((tm, LANES), lambda i: (i, 0))],
        out_shape=[jax.ShapeDtypeStruct((n, D_MODEL), F32),
                   jax.ShapeDtypeStruct((n, LANES), F32)],
        compiler_params=_cparams("arbitrary"),
        name="moe_router",
    )(x_lat, x_ctx, mods, g.reshape(1, D_MODEL), wr_pad)


def moe_route(logits, tm):
    n = logits.shape[0]
    a_tot = n * TOP_K
    top_v, top_e = lax.top_k(logits[:, :N_EXPERTS], TOP_K)
    gates = jax.nn.softmax(top_v, axis=-1)
    e_flat = top_e.reshape(a_tot).astype(jnp.int32)
    order = jnp.argsort(e_flat).astype(jnp.int32)
    counts = jnp.sum((e_flat[:, None] == jnp.arange(N_EXPERTS, dtype=jnp.int32)[None, :]).astype(jnp.int32), axis=0)
    starts = jnp.cumsum(counts) - counts
    padded = (counts + tm - 1) // tm * tm
    pend = jnp.cumsum(padded)
    pstarts = pend - padded
    n_blocks = (a_tot + N_EXPERTS * (tm - 1)) // tm
    blk_row = jnp.arange(n_blocks, dtype=jnp.int32) * tm
    block_e = jnp.minimum(jnp.searchsorted(pend, blk_row, side="right"), N_EXPERTS - 1).astype(jnp.int32)
    into = blk_row - pstarts[block_e]
    n_valid = jnp.clip(counts[block_e] - into, 0, tm).astype(jnp.int32)
    base = jnp.clip(starts[block_e] + into, 0, a_tot)
    order_pad = jnp.concatenate([order, jnp.zeros((tm,), jnp.int32)])
    pair = jax.vmap(lambda s: lax.dynamic_slice(order_pad, (s,), (tm,)))(base)
    valid = jnp.arange(tm, dtype=jnp.int32)[None, :] < n_valid[:, None]
    src = jnp.where(valid, pair // TOP_K, 0)
    dst = jnp.where(valid, (pair % TOP_K) * n + pair // TOP_K, 0)
    n_used = (pend[-1] // tm).astype(jnp.int32).reshape(1)
    return gates, block_e, n_used, n_valid, src.reshape(n_blocks, 1, tm), dst.reshape(n_blocks, 1, tm)


ROW_UNROLL = 8


def _for_rows(n, fn):
    full = n // ROW_UNROLL

    def group(c, carry):
        for u in range(ROW_UNROLL):
            fn(c * ROW_UNROLL + u)
        return carry
    lax.fori_loop(0, full, group, 0)

    def single(r, carry):
        fn(r)
        return carry
    lax.fori_loop(full * ROW_UNROLL, n, single, 0)


def _moe_body(be_ref, nu_ref, nv_ref, src_ref, srcn_ref, dst_ref, a_hbm, w1_ref, w3_ref, w2_ref, y_hbm,
              xf_ref, xb_ref, acc_ref, gsem, ssem, *, nf):
    i = pl.program_id(0)
    f = pl.program_id(1)
    n_used = nu_ref[0]
    slot = i % 2

    def gather_copy(s, r, tok):
        return pltpu.make_async_copy(a_hbm.at[pl.ds(tok, 1), :], xf_ref.at[s, pl.ds(r, 1), :], gsem.at[s])

    def scatter_copy(s, r, row):
        return pltpu.make_async_copy(acc_ref.at[s, pl.ds(r, 1), :], y_hbm.at[pl.ds(row, 1), :], ssem.at[s])

    def start_gather(s, idx_ref, n):
        _for_rows(n, lambda r: gather_copy(s, r, idx_ref[0, r]).start())

    def wait_gather(s, n):
        _for_rows(n, lambda r: gather_copy(s, r, 0).wait())

    def wait_scatter(s, n):
        _for_rows(n, lambda r: scatter_copy(s, r, 0).wait())

    @pl.when(i < n_used)
    def _():
        @pl.when(f == 0)
        def _():
            @pl.when(i == 0)
            def _():
                xf_ref[...] = jnp.zeros_like(xf_ref)
                start_gather(0, src_ref, nv_ref[0])

            wait_gather(slot, nv_ref[i])
            xb_ref[...] = xf_ref[slot].astype(BF16)

            @pl.when(i + 1 < n_used)
            def _():
                start_gather(1 - slot, srcn_ref, nv_ref[i + 1])

        x = xb_ref[...]
        u = jnp.dot(x, w1_ref[...], preferred_element_type=F32)
        v = jnp.dot(x, w3_ref[...], preferred_element_type=F32)
        p = (_silu(u) * v).astype(BF16)
        y = jnp.dot(p, w2_ref[...], preferred_element_type=F32)

        @pl.when(f == 0)
        def _():
            acc_ref[slot] = y

        @pl.when(f > 0)
        def _():
            acc_ref[slot] += y

        @pl.when(f == nf - 1)
        def _():
            _for_rows(nv_ref[i], lambda r: scatter_copy(slot, r, dst_ref[0, r]).start())

            @pl.when(i > 0)
            def _():
                wait_scatter(1 - slot, nv_ref[i - 1])

            @pl.when(i == n_used - 1)
            def _():
                wait_scatter(slot, nv_ref[i])


def moe_experts(a, block_e, n_used, n_valid, src, dst, w1, w3, w2, *, tf=1792):
    n = a.shape[0]
    n_blocks, _, tm = src.shape
    f_dim = w1.shape[2]
    nf = f_dim // tf
    body = functools.partial(_moe_body, nf=nf)

    def wmap(kind):
        def index_map(i, f, be, nu, nv):
            live = i < nu[0]
            ff = jnp.where(live, f, nf - 1)
            ii = jnp.where(live, i, nu[0] - 1)
            return (be[ii], 0, ff) if kind == "up" else (be[ii], ff, 0)
        return index_map

    idx_spec = pl.BlockSpec((None, 1, tm), lambda i, f, be, nu, nv: (i, 0, 0), memory_space=pltpu.SMEM)
    next_spec = pl.BlockSpec((None, 1, tm), lambda i, f, be, nu, nv: (jnp.minimum(i + 1, n_blocks - 1), 0, 0),
                             memory_space=pltpu.SMEM)
    grid_spec = pltpu.PrefetchScalarGridSpec(
        num_scalar_prefetch=3,
        grid=(n_blocks, nf),
        in_specs=[idx_spec, next_spec, idx_spec,
                  pl.BlockSpec(memory_space=pl.ANY),
                  pl.BlockSpec((None, D_MODEL, tf), wmap("up")),
                  pl.BlockSpec((None, D_MODEL, tf), wmap("up")),
                  pl.BlockSpec((None, tf, D_MODEL), wmap("down"))],
        out_specs=pl.BlockSpec(memory_space=pl.ANY),
        scratch_shapes=[pltpu.VMEM((2, tm, D_MODEL), F32), pltpu.VMEM((tm, D_MODEL), BF16),
                        pltpu.VMEM((2, tm, D_MODEL), F32),
                        pltpu.SemaphoreType.DMA((2,)), pltpu.SemaphoreType.DMA((2,))],
    )
    return pl.pallas_call(
        body,
        grid_spec=grid_spec,
        out_shape=jax.ShapeDtypeStruct((TOP_K * n, D_MODEL), F32),
        compiler_params=_cparams("arbitrary", "arbitrary"),
        name="moe_experts",
    )(block_e, n_used, n_valid, src, src, dst, a, w1, w3, w2)


def _combine_body(h_ref, y0_ref, y1_ref, gt_ref, mod_ref, fg_ref, o_ref, *, tm, rows_per_batch, final):
    i = pl.program_id(0)
    row = (i * tm) // rows_per_batch if rows_per_batch else CTX_MOD_ROW
    gt = gt_ref[...]
    f = y0_ref[...] * gt[:, 0:1] + y1_ref[...] * gt[:, 1:2]
    out = h_ref[...] + _mod_chunk(mod_ref, row, 5) * f
    if final:
        ms = jnp.mean(out * out, axis=-1, keepdims=True)
        out = out * lax.rsqrt(ms + NORM_EPS) * fg_ref[...]
    o_ref[...] = out


def moe_combine(h, y, gates, mods, layer, final_g, *, n_tok, row_off, rows_per_batch, final, tm=512):
    m = h.shape[0]
    tm = min(tm, m)
    o0 = row_off // tm
    o1 = (n_tok + row_off) // tm
    body = functools.partial(_combine_body, tm=tm, rows_per_batch=rows_per_batch, final=final)
    return pl.pallas_call(
        body,
        grid=(m // tm,),
        in_specs=[pl.BlockSpec((tm, D_MODEL), lambda i: (i, 0)),
                  pl.BlockSpec((tm, D_MODEL), lambda i: (o0 + i, 0)),
                  pl.BlockSpec((tm, D_MODEL), lambda i: (o1 + i, 0)),
                  pl.BlockSpec((tm, TOP_K), lambda i: (o0 + i, 0)),
                  _mod_spec(layer),
                  pl.BlockSpec((1, D_MODEL), lambda i: (0, 0))],
        out_specs=pl.BlockSpec((tm, D_MODEL), lambda i: (i, 0)),
        out_shape=jax.ShapeDtypeStruct((m, D_MODEL), F32),
        compiler_params=_cparams("arbitrary"),
        name="moe_combine",
    )(h, y, y, gates, mods, final_g.reshape(1, D_MODEL))


SORT_TOKENS = 512
ROW_GROUP = 16
EXPERT_ROWS = 16
SORT_SLOTS = -(-(TOP_K * SORT_TOKENS + N_EXPERTS * (ROW_GROUP - 1)) // LANES) * LANES
GROUPS_PER_BLOCK = MOE_ROWS // ROW_GROUP
META_SLOT0, META_SLOT1, META_GATE0, META_GATE1 = 0, 1, 2, 3


def _sort_body(xl_ref, xc_ref, mod_ref, g_ref, wrt_ref, as_ref, meta_ref, cnt_ref, *, tm, rows_per_batch, n_lat):
    i = pl.program_id(0)
    is_lat = i < n_lat
    row = jnp.where(is_lat, (i * tm) // rows_per_batch, CTX_MOD_ROW)
    x = jnp.where(is_lat, xl_ref[...], xc_ref[...])
    a = _norm_mod(x, g_ref[...], _mod_chunk(mod_ref, row, 3), _mod_chunk(mod_ref, row, 4))
    e_io = lax.broadcasted_iota(jnp.int32, (EXPERT_ROWS, tm), 0)
    lt = lax.dot_general(wrt_ref[...], a, _NT, preferred_element_type=F32, precision=lax.Precision.HIGHEST)
    lt = jnp.where(e_io < N_EXPERTS, lt, -jnp.inf)
    m0 = jnp.max(lt, axis=0, keepdims=True)
    e0 = jnp.min(jnp.where(lt == m0, e_io, EXPERT_ROWS), axis=0, keepdims=True)
    oh0 = e_io == e0
    lt1 = jnp.where(oh0, -jnp.inf, lt)
    m1 = jnp.max(lt1, axis=0, keepdims=True)
    e1 = jnp.min(jnp.where(lt1 == m1, e_io, EXPERT_ROWS), axis=0, keepdims=True)
    oh1 = e_io == e1
    ex = jnp.exp(m1 - m0)
    gate0 = 1.0 / (1.0 + ex)
    gate1 = ex / (1.0 + ex)
    oh = jnp.where(oh0, 1.0, jnp.where(oh1, 1.0, 0.0))
    n_io = lax.broadcasted_iota(jnp.int32, (tm, tm), 0)
    c_io = lax.broadcasted_iota(jnp.int32, (tm, tm), 1)
    earlier = jnp.where(n_io < c_io, 1.0, 0.0).astype(BF16)
    rank = jnp.dot(oh.astype(BF16), earlier, preferred_element_type=F32)
    cnt = jnp.sum(oh, axis=1, keepdims=True)
    padded = jnp.floor((cnt + (ROW_GROUP - 1)) * (1.0 / ROW_GROUP)) * ROW_GROUP
    r8 = lax.broadcasted_iota(jnp.int32, (EXPERT_ROWS, EXPERT_ROWS), 0)
    c8 = lax.broadcasted_iota(jnp.int32, (EXPERT_ROWS, EXPERT_ROWS), 1)
    padded_row = jnp.sum(jnp.where(r8 == c8, padded, 0.0), axis=0, keepdims=True)
    start = jnp.sum(jnp.where(c8 < r8, padded_row, 0.0), axis=1, keepdims=True)
    slot0 = jnp.sum(jnp.where(oh0, start + rank, 0.0), axis=0, keepdims=True)
    slot1 = jnp.sum(jnp.where(oh1, start + rank, 0.0), axis=0, keepdims=True)
    j_io = lax.broadcasted_iota(jnp.int32, (SORT_SLOTS, tm), 0).astype(F32)
    perm = jnp.where(j_io == slot0, 1.0, jnp.where(j_io == slot1, 1.0, 0.0)).astype(BF16)
    as_ref[...] = jnp.dot(perm, a.astype(BF16), preferred_element_type=F32).astype(BF16)
    rows = jnp.concatenate([slot0, slot1, gate0, gate1, jnp.zeros((LANES - 4, tm), F32)], axis=0)
    meta_ref[...] = rows.T
    cnt_ref[...] = jnp.concatenate([jnp.broadcast_to(padded, (EXPERT_ROWS, LANES)),
                                    jnp.broadcast_to(start, (EXPERT_ROWS, LANES))], axis=0)


def moe_sort(x_lat, x_ctx, mods, layer, g, wrt, *, rows_per_batch):
    tm = SORT_TOKENS
    n_lat = x_lat.shape[0] // tm
    n_ctx = 0 if x_ctx is None else x_ctx.shape[0] // tm
    nt = n_lat + n_ctx
    if x_ctx is None:
        x_ctx = x_lat
    body = functools.partial(_sort_body, tm=tm, rows_per_batch=rows_per_batch, n_lat=n_lat)
    return pl.pallas_call(
        body,
        grid=(nt,),
        in_specs=[pl.BlockSpec((tm, D_MODEL), lambda i: (jnp.minimum(i, n_lat - 1), 0)),
                  pl.BlockSpec((tm, D_MODEL), lambda i: (jnp.maximum(i - n_lat, 0), 0)),
                  _mod_spec(layer),
                  pl.BlockSpec((1, D_MODEL), lambda i: (0, 0)),
                  pl.BlockSpec((EXPERT_ROWS, D_MODEL), lambda i: (0, 0))],
        out_specs=[pl.BlockSpec((SORT_SLOTS, D_MODEL), lambda i: (i, 0)),
                   pl.BlockSpec((tm, LANES), lambda i: (i, 0)),
                   pl.BlockSpec((None, 2 * EXPERT_ROWS, LANES), lambda i: (i, 0, 0))],
        out_shape=[jax.ShapeDtypeStruct((nt * SORT_SLOTS, D_MODEL), BF16),
                   jax.ShapeDtypeStruct((nt * tm, LANES), F32),
                   jax.ShapeDtypeStruct((nt, 2 * EXPERT_ROWS, LANES), F32)],
        compiler_params=_cparams("arbitrary"),
        name="moe_sort",
    )(x_lat, x_ctx, mods, g.reshape(1, D_MODEL), wrt)


def moe_group_table(cnt):
    nt = cnt.shape[0]
    padded = cnt[:, :N_EXPERTS, 0].astype(jnp.int32)
    start = cnt[:, EXPERT_ROWS:EXPERT_ROWS + N_EXPERTS, 0].astype(jnp.int32)
    groups = padded // ROW_GROUP
    cum = jnp.cumsum(groups, axis=0)
    tot = cum[-1]
    blocks = (tot + GROUPS_PER_BLOCK - 1) // GROUPS_PER_BLOCK
    bend = jnp.cumsum(blocks)
    bstart = bend - blocks
    n_blocks = (nt * SORT_SLOTS // ROW_GROUP + N_EXPERTS * (GROUPS_PER_BLOCK - 1)) // GROUPS_PER_BLOCK
    bi = jnp.arange(n_blocks, dtype=jnp.int32)
    block_e = jnp.minimum(jnp.searchsorted(bend, bi, side="right"), N_EXPERTS - 1).astype(jnp.int32)
    q = (bi - bstart[block_e])[:, None] * GROUPS_PER_BLOCK + jnp.arange(GROUPS_PER_BLOCK, dtype=jnp.int32)[None, :]
    n_valid = jnp.clip(tot[block_e] - (bi - bstart[block_e]) * GROUPS_PER_BLOCK, 0, GROUPS_PER_BLOCK).astype(jnp.int32)
    cum_e = cum.T[block_e]
    tile = jnp.sum((cum_e[:, None, :] <= q[:, :, None]).astype(jnp.int32), axis=2)
    tile = jnp.minimum(tile, nt - 1)
    before = jnp.take_along_axis(cum_e - groups.T[block_e], tile, axis=1)
    first = jnp.take_along_axis(start.T[block_e], tile, axis=1)
    rows = tile * SORT_SLOTS + first + (q - before) * ROW_GROUP
    valid = jnp.arange(GROUPS_PER_BLOCK, dtype=jnp.int32)[None, :] < n_valid[:, None]
    rows = jnp.where(valid, rows, 0).astype(jnp.int32)
    n_used = bend[-1].astype(jnp.int32).reshape(1)
    return block_e, n_used, n_valid, rows.reshape(n_blocks, 1, GROUPS_PER_BLOCK)


def _moe2_body(be_ref, nu_ref, nv_ref, row_ref, rown_ref, as_hbm, w1_ref, w3_ref, w2_ref, ys_hbm,
               x_ref, acc_ref, y_ref, gsem, ssem, *, nf):
    i = pl.program_id(0)
    f = pl.program_id(1)
    n_used = nu_ref[0]
    slot = i % 2

    def gather_copy(s, gidx, row):
        row = pl.multiple_of(row, ROW_GROUP)
        dst = pl.multiple_of(gidx * ROW_GROUP, ROW_GROUP)
        return pltpu.make_async_copy(as_hbm.at[pl.ds(row, ROW_GROUP), :], x_ref.at[s, pl.ds(dst, ROW_GROUP), :],
                                     gsem.at[s])

    def scatter_copy(s, gidx, row):
        row = pl.multiple_of(row, ROW_GROUP)
        src = pl.multiple_of(gidx * ROW_GROUP, ROW_GROUP)
        return pltpu.make_async_copy(y_ref.at[s, pl.ds(src, ROW_GROUP), :], ys_hbm.at[pl.ds(row, ROW_GROUP), :],
                                     ssem.at[s])

    def loop(n, fn):
        def body(r, c):
            fn(r)
            return c
        lax.fori_loop(0, n, body, 0)

    @pl.when(i < n_used)
    def _():
        @pl.when(f == 0)
        def _():
            @pl.when(i == 0)
            def _():
                x_ref[...] = jnp.zeros_like(x_ref)
                loop(nv_ref[0], lambda r: gather_copy(0, r, row_ref[0, r]).start())

            loop(nv_ref[i], lambda r: gather_copy(slot, r, 0).wait())

            @pl.when(i + 1 < n_used)
            def _():
                loop(nv_ref[i + 1], lambda r: gather_copy(1 - slot, r, rown_ref[0, r]).start())

        x = x_ref[slot]
        u = jnp.dot(x, w1_ref[...], preferred_element_type=F32)
        v = jnp.dot(x, w3_ref[...], preferred_element_type=F32)
        p = (_silu(u) * v).astype(BF16)
        y = jnp.dot(p, w2_ref[...], preferred_element_type=F32)

        @pl.when(f == 0)
        def _():
            acc_ref[...] = y

        @pl.when(jnp.logical_and(f > 0, f < nf - 1))
        def _():
            acc_ref[...] += y

        @pl.when(f == nf - 1)
        def _():
            y_ref[slot] = (acc_ref[...] + y).astype(BF16)
            loop(nv_ref[i], lambda r: scatter_copy(slot, r, row_ref[0, r]).start())

            @pl.when(i > 0)
            def _():
                loop(nv_ref[i - 1], lambda r: scatter_copy(1 - slot, r, 0).wait())

            @pl.when(i == n_used - 1)
            def _():
                loop(nv_ref[i], lambda r: scatter_copy(slot, r, 0).wait())


def moe_experts_sorted(a_sorted, block_e, n_used, n_valid, rows, w1, w3, w2, *, tf=1792):
    n_blocks = rows.shape[0]
    tm = MOE_ROWS
    f_dim = w1.shape[2]
    nf = f_dim // tf
    assert nf >= 2
    body = functools.partial(_moe2_body, nf=nf)

    def wmap(kind):
        def index_map(i, f, be, nu, nv):
            live = i < nu[0]
            ff = jnp.where(live, f, nf - 1)
            ii = jnp.where(live, i, nu[0] - 1)
            return (be[ii], 0, ff) if kind == "up" else (be[ii], ff, 0)
        return index_map

    idx_spec = pl.BlockSpec((None, 1, GROUPS_PER_BLOCK), lambda i, f, be, nu, nv: (i, 0, 0), memory_space=pltpu.SMEM)
    next_spec = pl.BlockSpec((None, 1, GROUPS_PER_BLOCK),
                             lambda i, f, be, nu, nv: (jnp.minimum(i + 1, n_blocks - 1), 0, 0),
                             memory_space=pltpu.SMEM)
    grid_spec = pltpu.PrefetchScalarGridSpec(
        num_scalar_prefetch=3,
        grid=(n_blocks, nf),
        in_specs=[idx_spec, next_spec,
                  pl.BlockSpec(memory_space=pl.ANY),
                  pl.BlockSpec((None, D_MODEL, tf), wmap("up")),
                  pl.BlockSpec((None, D_MODEL, tf), wmap("up")),
                  pl.BlockSpec((None, tf, D_MODEL), wmap("down"))],
        out_specs=pl.BlockSpec(memory_space=pl.ANY),
        scratch_shapes=[pltpu.VMEM((2, tm, D_MODEL), BF16), pltpu.VMEM((tm, D_MODEL), F32),
                        pltpu.VMEM((2, tm, D_MODEL), BF16),
                        pltpu.SemaphoreType.DMA((2,)), pltpu.SemaphoreType.DMA((2,))],
    )
    return pl.pallas_call(
        body,
        grid_spec=grid_spec,
        out_shape=jax.ShapeDtypeStruct(a_sorted.shape, BF16),
        input_output_aliases={5: 0},
        compiler_params=_cparams("arbitrary", "arbitrary"),
        name="moe_experts",
    )(block_e, n_used, n_valid, rows, rows, a_sorted, w1, w3, w2)


def _unsort_body(h_ref, ys_ref, meta_ref, mod_ref, fg_ref, o_ref, *, tm, rows_per_batch, tile_off, final):
    i = pl.program_id(0)
    row = ((i * tm) // rows_per_batch) if rows_per_batch else CTX_MOD_ROW
    meta = meta_ref[...]
    slot0 = meta[:, META_SLOT0:META_SLOT0 + 1]
    slot1 = meta[:, META_SLOT1:META_SLOT1 + 1]
    gate0 = meta[:, META_GATE0:META_GATE0 + 1]
    gate1 = meta[:, META_GATE1:META_GATE1 + 1]
    j_io = lax.broadcasted_iota(jnp.int32, (tm, SORT_SLOTS), 1).astype(F32)
    pick = jnp.where(j_io == slot0, gate0, jnp.where(j_io == slot1, gate1, 0.0)).astype(BF16)
    fsum = jnp.dot(pick, ys_ref[...], preferred_element_type=F32)
    out = h_ref[...] + _mod_chunk(mod_ref, row, 5) * fsum
    if final:
        ms = jnp.mean(out * out, axis=-1, keepdims=True)
        out = out * lax.rsqrt(ms + NORM_EPS) * fg_ref[...]
    o_ref[...] = out


def moe_unsort_combine(h, ys, meta, mods, layer, final_g, *, tile_off, rows_per_batch, final):
    tm = SORT_TOKENS
    m = h.shape[0]
    body = functools.partial(_unsort_body, tm=tm, rows_per_batch=rows_per_batch, tile_off=tile_off, final=final)
    return pl.pallas_call(
        body,
        grid=(m // tm,),
        in_specs=[pl.BlockSpec((tm, D_MODEL), lambda i: (i, 0)),
                  pl.BlockSpec((SORT_SLOTS, D_MODEL), lambda i: (tile_off + i, 0)),
                  pl.BlockSpec((tm, LANES), lambda i: (tile_off + i, 0)),
                  _mod_spec(layer),
                  pl.BlockSpec((1, D_MODEL), lambda i: (0, 0))],
        out_specs=pl.BlockSpec((tm, D_MODEL), lambda i: (i, 0)),
        out_shape=jax.ShapeDtypeStruct((m, D_MODEL), F32),
        compiler_params=_cparams("arbitrary"),
        name="moe_combine",
    )(h, ys, meta, mods, final_g.reshape(1, D_MODEL))


def _na_layer(h_lat, h_ctx, mods, layer, g, w_qkv, b_qkv, rpb, w_out, b_out, with_ctx_out):
    b, seq, _ = h_lat.shape
    c = h_ctx.shape[1]
    qscale = jnp.concatenate([jnp.full((D_MODEL,), (D_MODEL // NA_HEADS) ** -0.5, F32), jnp.ones((2 * D_MODEL,), F32)])
    w = (w_qkv * qscale).astype(BF16)
    bias = b_qkv * qscale
    qkv = nm_matmul(h_lat.reshape(b * seq, D_MODEL), mods, layer, g, w, bias, rows_per_batch=seq, sh=0, sc=1)
    qkv_c = nm_matmul(h_ctx.reshape(b * c, D_MODEL), mods, layer, g, w, bias, rows_per_batch=None, sh=0, sc=1)
    qkv = qkv.reshape(b, seq, 3 * D_MODEL)
    qkv_c = qkv_c.reshape(b, c, 3 * D_MODEL)
    o_lat = na_attention(qkv, qkv_c, na_bias_table(rpb))
    wo = w_out.astype(BF16)
    h_lat = mm_residual(o_lat.reshape(b * seq, D_MODEL), h_lat.reshape(b * seq, D_MODEL), mods, layer, wo, b_out,
                        rows_per_batch=seq, gate=2).reshape(b, seq, D_MODEL)
    if with_ctx_out:
        o_ctx = ctx_attention(qkv_c)
        h_ctx = mm_residual(o_ctx.reshape(b * c, D_MODEL), h_ctx.reshape(b * c, D_MODEL), mods, layer, wo, b_out,
                            rows_per_batch=None, gate=2).reshape(b, c, D_MODEL)
    return h_lat, h_ctx


def _mlstm_layer(h_lat, h_ctx, mods, layer, g, w_up, conv_w, conv_b, w_q, w_k, w_v, w_gates, b_gates,
                 gn_w, skip, w_down, with_ctx_out):
    b, seq, _ = h_lat.shape
    c = h_ctx.shape[1]
    inner = MLSTM_INNER
    wu = w_up.astype(BF16)
    zero_b = jnp.zeros((2 * inner,), F32)
    wq_bd = block_diag_weights(w_q).astype(BF16)
    wk_bd = block_diag_weights(w_k).astype(BF16)
    wkt_bd = jnp.swapaxes(wk_bd, 1, 2)
    wv_bd = block_diag_weights(w_v).astype(BF16)
    ng = w_gates.shape[1]
    wg = jnp.pad(w_gates, ((0, 0), (0, LANES - ng))).astype(BF16)
    wgt = w_gates.T.astype(BF16)
    wd = w_down.astype(BF16)

    def features(h, rows_per_batch):
        n, s, _ = h.shape
        up = nm_matmul(h.reshape(n * s, D_MODEL), mods, layer, g, wu, zero_b, rows_per_batch=rows_per_batch, sh=0, sc=1)
        up = up.reshape(n, s, 2 * inner)
        return up, mlstm_features(up, conv_w, conv_b, wq_bd, wk_bd, wkt_bd, wv_bd, wg, wgt, b_gates)

    up_c, (q_c, kt_c, v_c, xc_c, g_c, gt_c) = features(h_ctx, None)
    up_l, (q_l, kt_l, v_l, xc_l, g_l, gt_l) = features(h_lat, seq)
    hf_c, cf, mf = mlstm_scan(q_c, kt_c, v_c, g_c, gt_c, None, rev=False)
    hb_c, cb, mb = mlstm_scan(q_c, kt_c, v_c, g_c, gt_c, None, rev=True)
    hf_l, _, _ = mlstm_scan(q_l, kt_l, v_l, g_l, gt_l, (cf, mf), rev=False)
    hb_l, _, _ = mlstm_scan(q_l, kt_l, v_l, g_l, gt_l, (cb, mb), rev=True)
    flat = lambda a: a.reshape(-1, a.shape[-1])
    h_lat = mlstm_output(flat(hf_l), flat(hb_l), flat(xc_l), flat(up_l), flat(h_lat), mods, layer, gn_w, skip, wd,
                         rows_per_batch=seq).reshape(b, seq, D_MODEL)
    if with_ctx_out:
        h_ctx = mlstm_output(flat(hf_c), flat(hb_c), flat(xc_c), flat(up_c), flat(h_ctx), mods, layer, gn_w, skip, wd,
                             rows_per_batch=None).reshape(b, c, D_MODEL)
    return h_lat, h_ctx


def _moe_layer(h_lat, h_ctx, mods, layer, g, w_router, w1, w3, w2, final_g, last):
    b, seq, _ = h_lat.shape
    c = h_ctx.shape[1]
    wrt = jnp.pad(w_router.T, ((0, EXPERT_ROWS - N_EXPERTS), (0, 0)))
    hl = h_lat.reshape(b * seq, D_MODEL)
    hc = None if last else h_ctx.reshape(b * c, D_MODEL)
    a_sorted, meta, cnt = moe_sort(hl, hc, mods, layer, g, wrt, rows_per_batch=seq)
    block_e, n_used, n_valid, rows = moe_group_table(cnt)
    ys = moe_experts_sorted(a_sorted, block_e, n_used, n_valid, rows,
                            w1.astype(BF16), w3.astype(BF16), w2.astype(BF16))
    h_lat = moe_unsort_combine(hl, ys, meta, mods, layer, final_g, tile_off=0, rows_per_batch=seq,
                               final=last).reshape(b, seq, D_MODEL)
    if not last:
        h_ctx = moe_unsort_combine(hc, ys, meta, mods, layer, final_g, tile_off=(b * seq) // SORT_TOKENS,
                                   rows_per_batch=None, final=False).reshape(b, c, D_MODEL)
    return h_lat, h_ctx


def kernel(x, c, ctx, c_ctx, w_mod, b_mod, norm_g, final_g, na_w_qkv, na_b_qkv, na_rpb, na_w_out, na_b_out,
           pool_w, pool_scale, ml_w_up, ml_conv_w, ml_conv_b, ml_w_q, ml_w_k, ml_w_v, ml_w_gates, ml_b_gates,
           ml_gn_w, ml_skip, ml_w_down, ffn_w1, ffn_w3, ffn_w2, moe_w_router, moe_w1, moe_w3, moe_w2):
    b, seq, _ = x.shape
    n_ctx = ctx.shape[1]
    depth = w_mod.shape[0]
    assert b <= CTX_MOD_ROW
    cond = jnp.zeros((MOD_ROWS, D_MODEL), F32).at[:b].set(c).at[CTX_MOD_ROW].set(c_ctx)
    mods = adaln_all(cond, w_mod, b_mod)
    h_lat, h_ctx = x, ctx
    for i in range(depth):
        last = i == depth - 1
        kind = i % 3
        j = i // 3
        g_tok = norm_g[i, 0]
        if kind == 0:
            h_lat, h_ctx = _na_layer(h_lat, h_ctx, mods, i, g_tok, na_w_qkv[j], na_b_qkv[j], na_rpb[j],
                                     na_w_out[j], na_b_out[j], not last)
        elif kind == 1:
            wp = pool_w[j].astype(BF16)
            h_lat = pool_mixer(h_lat, mods, i, g_tok, wp, pool_scale[j], is_ctx=False)
            if not last:
                h_ctx = pool_mixer(h_ctx, mods, i, g_tok, wp, pool_scale[j], is_ctx=True)
        else:
            h_lat, h_ctx = _mlstm_layer(h_lat, h_ctx, mods, i, g_tok, ml_w_up[j], ml_conv_w[j], ml_conv_b[j],
                                        ml_w_q[j], ml_w_k[j], ml_w_v[j], ml_w_gates[j], ml_b_gates[j],
                                        ml_gn_w[j], ml_skip[j], ml_w_down[j], not last)
        e = i // 2
        g_ch = norm_g[i, 1]
        if i % 2 == 0:
            w1, w3, w2 = ffn_w1[e].astype(BF16), ffn_w3[e].astype(BF16), ffn_w2[e].astype(BF16)
            h_lat = ffn_dense(h_lat.reshape(b * seq, D_MODEL), mods, i, g_ch, w1, w3, w2,
                              rows_per_batch=seq).reshape(b, seq, D_MODEL)
            if not last:
                h_ctx = ffn_dense(h_ctx.reshape(b * n_ctx, D_MODEL), mods, i, g_ch, w1, w3, w2,
                                  rows_per_batch=None).reshape(b, n_ctx, D_MODEL)
        else:
            h_lat, h_ctx = _moe_layer(h_lat, h_ctx, mods, i, g_ch, moe_w_router[e], moe_w1[e], moe_w3[e],
                                      moe_w2[e], final_g, last)
    return h_lat
```

```python
import functools

import jax
import jax.numpy as jnp
from jax import lax
from jax.experimental import pallas as pl
from jax.experimental.pallas import tpu as pltpu

F32 = jnp.float32
BF16 = jnp.bfloat16

D_MODEL = 1024
N_MOD = 6
NORM_EPS = 1e-6
GRID_W = 64
NA_HEADS = 16
NA_WIN_ROWS = 8
NA_WIN_COLS = 16
POOL_WINDOWS = (2, 4, 8, 16)
POOL_GROUP_DIM = D_MODEL // len(POOL_WINDOWS)
MLSTM_INNER = 2 * D_MODEL
MLSTM_HEADS = 4
MLSTM_HEAD_DIM = MLSTM_INNER // MLSTM_HEADS
MLSTM_CONV = 4
MLSTM_QKV_BLOCK = 4
N_EXPERTS = 8
TOP_K = 2

LANES = 128
MOD_ROWS = 8
CTX_MOD_ROW = 4
VMEM_LIMIT_BYTES = 56 * 1024 * 1024
NEG_BIG = -1e30
SCAN_CHUNK = 128
SCAN_STATE_ROWS = MLSTM_HEAD_DIM + 8
MOE_ROWS = 512


def _cparams(*sem):
    return pltpu.CompilerParams(dimension_semantics=sem, vmem_limit_bytes=VMEM_LIMIT_BYTES)


def _resident(shape, index_map):
    return pl.BlockSpec(shape, index_map, pipeline_mode=pl.Buffered(1))


def _silu(x):
    return x * jax.nn.sigmoid(x)


def _norm_mod(x, g, shift, scale):
    ms = jnp.mean(x * x, axis=-1, keepdims=True)
    y = x * lax.rsqrt(ms + NORM_EPS) * g
    return y * (1.0 + scale) + shift


def _mod_chunk(mod_ref, row, j):
    return mod_ref[pl.ds(row, 1), pl.ds(j * D_MODEL, D_MODEL)]


def _mod_spec(layer):
    return pl.BlockSpec((None, MOD_ROWS, N_MOD * D_MODEL), lambda *_: (layer, 0, 0))


def _adaln_body(c_ref, w_ref, b_ref, o_ref):
    s = _silu(c_ref[...])
    o_ref[...] = jnp.dot(s, w_ref[...], preferred_element_type=F32) + b_ref[...]


def adaln_all(cond, w_mod, b_mod):
    depth = w_mod.shape[0]
    n = N_MOD * D_MODEL
    tn = 1536
    return pl.pallas_call(
        _adaln_body,
        grid=(depth, n // tn),
        in_specs=[pl.BlockSpec((MOD_ROWS, D_MODEL), lambda l, j: (0, 0)),
                  pl.BlockSpec((None, D_MODEL, tn), lambda l, j: (l, 0, j)),
                  pl.BlockSpec((None, 1, tn), lambda l, j: (l, 0, j))],
        out_specs=pl.BlockSpec((None, MOD_ROWS, tn), lambda l, j: (l, 0, j)),
        out_shape=jax.ShapeDtypeStruct((depth, MOD_ROWS, n), F32),
        compiler_params=_cparams("arbitrary", "arbitrary"),
        name="adaln",
    )(cond, w_mod, b_mod.reshape(depth, 1, n))


def _nm_matmul_body(x_ref, mod_ref, g_ref, w_ref, b_ref, o_ref, *, tm, rows_per_batch, sh, sc, nc):
    i = pl.program_id(0)
    row = (i * tm) // rows_per_batch if rows_per_batch else CTX_MOD_ROW
    a = _norm_mod(x_ref[...], g_ref[...], _mod_chunk(mod_ref, row, sh), _mod_chunk(mod_ref, row, sc)).astype(BF16)
    n = o_ref.shape[1]
    for c in range(n // nc):
        sl = slice(c * nc, (c + 1) * nc)
        y = jnp.dot(a, w_ref[:, sl], preferred_element_type=F32) + b_ref[:, sl]
        o_ref[:, sl] = y.astype(o_ref.dtype)


def nm_matmul(x, mods, layer, g, w, bias, *, rows_per_batch, sh, sc, tm=512, nc=1024, out_dtype=BF16):
    m, n = x.shape[0], w.shape[1]
    tm = min(tm, m)
    body = functools.partial(_nm_matmul_body, tm=tm, rows_per_batch=rows_per_batch, sh=sh, sc=sc, nc=nc)
    return pl.pallas_call(
        body,
        grid=(m // tm,),
        in_specs=[pl.BlockSpec((tm, D_MODEL), lambda i: (i, 0)),
                  _mod_spec(layer),
                  pl.BlockSpec((1, D_MODEL), lambda i: (0, 0)),
                  _resident((D_MODEL, n), lambda i: (0, 0)),
                  pl.BlockSpec((1, n), lambda i: (0, 0))],
        out_specs=pl.BlockSpec((tm, n), lambda i: (i, 0)),
        out_shape=jax.ShapeDtypeStruct((m, n), out_dtype),
        compiler_params=_cparams("arbitrary"),
        name="nm_matmul",
    )(x, mods, g.reshape(1, D_MODEL), w, bias.reshape(1, n))


def _mm_res_body(a_ref, h_ref, mod_ref, w_ref, b_ref, o_ref, *, tm, rows_per_batch, gate):
    i = pl.program_id(0)
    row = (i * tm) // rows_per_batch if rows_per_batch else CTX_MOD_ROW
    y = jnp.dot(a_ref[...], w_ref[...], preferred_element_type=F32) + b_ref[...]
    o_ref[...] = h_ref[...] + _mod_chunk(mod_ref, row, gate) * y


def mm_residual(a, h, mods, layer, w, bias, *, rows_per_batch, gate, tm=512):
    m, k = a.shape
    tm = min(tm, m)
    body = functools.partial(_mm_res_body, tm=tm, rows_per_batch=rows_per_batch, gate=gate)
    return pl.pallas_call(
        body,
        grid=(m // tm,),
        in_specs=[pl.BlockSpec((tm, k), lambda i: (i, 0)),
                  pl.BlockSpec((tm, D_MODEL), lambda i: (i, 0)),
                  _mod_spec(layer),
                  _resident((k, D_MODEL), lambda i: (0, 0)),
                  pl.BlockSpec((1, D_MODEL), lambda i: (0, 0))],
        out_specs=pl.BlockSpec((tm, D_MODEL), lambda i: (i, 0)),
        out_shape=jax.ShapeDtypeStruct((m, D_MODEL), F32),
        compiler_params=_cparams("arbitrary"),
        name="mm_residual",
    )(a, h, mods, w, bias.reshape(1, D_MODEL))


def _ffn_body(h_ref, mod_ref, g_ref, w1_ref, w3_ref, w2_ref, o_ref, *, tm, rows_per_batch):
    i = pl.program_id(0)
    row = (i * tm) // rows_per_batch if rows_per_batch else CTX_MOD_ROW
    h = h_ref[...]
    a = _norm_mod(h, g_ref[...], _mod_chunk(mod_ref, row, 3), _mod_chunk(mod_ref, row, 4)).astype(BF16)
    u = jnp.dot(a, w1_ref[...], preferred_element_type=F32)
    v = jnp.dot(a, w3_ref[...], preferred_element_type=F32)
    p = (_silu(u) * v).astype(BF16)
    y = jnp.dot(p, w2_ref[...], preferred_element_type=F32)
    o_ref[...] = h + _mod_chunk(mod_ref, row, 5) * y


def ffn_dense(h, mods, layer, g, w1, w3, w2, *, rows_per_batch, tm=256):
    m = h.shape[0]
    f = w1.shape[1]
    tm = min(tm, m)
    body = functools.partial(_ffn_body, tm=tm, rows_per_batch=rows_per_batch)
    return pl.pallas_call(
        body,
        grid=(m // tm,),
        in_specs=[pl.BlockSpec((tm, D_MODEL), lambda i: (i, 0)),
                  _mod_spec(layer),
                  pl.BlockSpec((1, D_MODEL), lambda i: (0, 0)),
                  _resident((D_MODEL, f), lambda i: (0, 0)),
                  _resident((D_MODEL, f), lambda i: (0, 0)),
                  _resident((f, D_MODEL), lambda i: (0, 0))],
        out_specs=pl.BlockSpec((tm, D_MODEL), lambda i: (i, 0)),
        out_shape=jax.ShapeDtypeStruct((m, D_MODEL), F32),
        compiler_params=_cparams("arbitrary"),
        name="ffn_dense",
    )(h, mods, g.reshape(1, D_MODEL), w1, w3, w2)


def na_bias_table(rpb):
    h = rpb.shape[0]
    col = jnp.arange(GRID_W)
    c0 = jnp.clip(col - NA_WIN_COLS // 2, 0, GRID_W - NA_WIN_COLS)
    col_ok = (col[None, :] >= c0[:, None]) & (col[None, :] < c0[:, None] + NA_WIN_COLS)
    dcol = jnp.clip(col[None, :] - col[:, None], 1 - NA_WIN_COLS, NA_WIN_COLS - 1) + (NA_WIN_COLS - 1)
    n_drow = 2 * NA_WIN_ROWS - 1
    t = jnp.where(col_ok[None, None], rpb[:, :, dcol].astype(F32), NEG_BIG)
    t = t.transpose(0, 2, 1, 3).reshape(h // 2, 2 * GRID_W, n_drow * GRID_W)
    tiles = [t[:, :, (NA_WIN_ROWS - 1 - off) * GRID_W:(2 * NA_WIN_ROWS - 1 - off) * GRID_W]
             for off in range(NA_WIN_ROWS)]
    return jnp.stack(tiles, axis=1)


def _stack_heads(q):
    lo = lax.broadcasted_iota(jnp.int32, q.shape, 1) < (LANES // 2)
    zero = jnp.zeros_like(q)
    return jnp.concatenate([jnp.where(lo, q, zero), jnp.where(lo, zero, q)], axis=0)


def _unstack_heads(o):
    n = o.shape[0] // 2
    lo = lax.broadcasted_iota(jnp.int32, (n, LANES), 1) < (LANES // 2)
    return jnp.where(lo, o[:n], o[n:])


_NT = (((1,), (1,)), ((), ()))
NA_GROUP = 8


def _na_body(q_ref, k_ref, v_ref, kc_ref, vc_ref, bias_ref, o_ref, *, rows):
    kc = kc_ref[...]
    vc = vc_ref[...]
    kwin = NA_WIN_ROWS * GRID_W

    def scores(r):
        r0 = jnp.clip(r - NA_WIN_ROWS // 2, 0, rows - NA_WIN_ROWS)
        qoff = pl.multiple_of(r * GRID_W, GRID_W)
        koff = pl.multiple_of(r0 * GRID_W, GRID_W)
        qs = _stack_heads(q_ref[pl.ds(qoff, GRID_W), :])
        k = k_ref[pl.ds(koff, kwin), :]
        s_loc = lax.dot_general(qs, k, _NT, preferred_element_type=F32) + bias_ref[r - r0]
        s_ctx = lax.dot_general(qs, kc, _NT, preferred_element_type=F32)
        return qoff, koff, s_loc, s_ctx

    def probs(s_loc, s_ctx):
        m = jnp.maximum(jnp.max(s_loc, axis=-1, keepdims=True), jnp.max(s_ctx, axis=-1, keepdims=True))
        p_loc = jnp.exp(s_loc - m)
        p_ctx = jnp.exp(s_ctx - m)
        l = jnp.sum(p_loc, axis=-1, keepdims=True) + jnp.sum(p_ctx, axis=-1, keepdims=True)
        return p_loc.astype(BF16), p_ctx.astype(BF16), l

    def group(gi, carry):
        sc = [scores(gi * NA_GROUP + u) for u in range(NA_GROUP)]
        pr = [probs(s[2], s[3]) for s in sc]
        for (qoff, koff, _, _), (p_loc, p_ctx, l) in zip(sc, pr):
            v = v_ref[pl.ds(koff, kwin), :]
            o = (jnp.dot(p_loc, v, preferred_element_type=F32)
                 + jnp.dot(p_ctx, vc, preferred_element_type=F32)) / l
            o_ref[pl.ds(qoff, GRID_W), :] = _unstack_heads(o).astype(o_ref.dtype)
        return carry

    lax.fori_loop(0, rows // NA_GROUP, group, 0)


def na_attention(qkv, qkv_ctx, bias):
    b, l, _ = qkv.shape
    c = qkv_ctx.shape[1]
    hp = D_MODEL // LANES
    body = functools.partial(_na_body, rows=l // GRID_W)
    return pl.pallas_call(
        body,
        grid=(b, hp),
        in_specs=[pl.BlockSpec((None, l, LANES), lambda i, j: (i, 0, j)),
                  pl.BlockSpec((None, l, LANES), lambda i, j: (i, 0, hp + j)),
                  pl.BlockSpec((None, l, LANES), lambda i, j: (i, 0, 2 * hp + j)),
                  pl.BlockSpec((None, c, LANES), lambda i, j: (i, 0, hp + j)),
                  pl.BlockSpec((None, c, LANES), lambda i, j: (i, 0, 2 * hp + j)),
                  pl.BlockSpec((None,) + bias.shape[1:], lambda i, j: (j, 0, 0, 0))],
        out_specs=pl.BlockSpec((None, l, LANES), lambda i, j: (i, 0, j)),
        out_shape=jax.ShapeDtypeStruct((b, l, D_MODEL), BF16),
        compiler_params=_cparams("arbitrary", "arbitrary"),
        name="na_attention",
    )(qkv, qkv, qkv, qkv_ctx, qkv_ctx, bias)


def _ctx_attn_body(q_ref, k_ref, v_ref, o_ref):
    qs = _stack_heads(q_ref[...])
    s = lax.dot_general(qs, k_ref[...], _NT, preferred_element_type=F32)
    p = jnp.exp(s - jnp.max(s, axis=-1, keepdims=True))
    l = jnp.sum(p, axis=-1, keepdims=True)
    o = jnp.dot(p.astype(BF16), v_ref[...], preferred_element_type=F32) / l
    o_ref[...] = _unstack_heads(o).astype(o_ref.dtype)


def ctx_attention(qkv_ctx):
    b, c, _ = qkv_ctx.shape
    hp = D_MODEL // LANES
    return pl.pallas_call(
        _ctx_attn_body,
        grid=(b, hp),
        in_specs=[pl.BlockSpec((None, c, LANES), lambda i, j: (i, 0, j)),
                  pl.BlockSpec((None, c, LANES), lambda i, j: (i, 0, hp + j)),
                  pl.BlockSpec((None, c, LANES), lambda i, j: (i, 0, 2 * hp + j))],
        out_specs=pl.BlockSpec((None, c, LANES), lambda i, j: (i, 0, j)),
        out_shape=jax.ShapeDtypeStruct((b, c, D_MODEL), BF16),
        compiler_params=_cparams("arbitrary", "arbitrary"),
        name="ctx_attention",
    )(qkv_ctx, qkv_ctx, qkv_ctx)


POOL_HALO = 8


def _pool_body(prev_ref, cur_ref, next_ref, mod_ref, g_ref, wp_ref, ps_ref, o_ref, *, tl, seq, is_ctx):
    b = pl.program_id(0)
    j = pl.program_id(1)
    row = CTX_MOD_ROW if is_ctx else b
    g = g_ref[...]
    sh = _mod_chunk(mod_ref, row, 0)
    sc = _mod_chunk(mod_ref, row, 1)
    h = cur_ref[...]
    a_cur = _norm_mod(h, g, sh, sc)
    a_prev = _norm_mod(prev_ref[...], g, sh, sc) * (j > 0).astype(F32)
    a_next = _norm_mod(next_ref[...], g, sh, sc) * (j < seq // tl - 1).astype(F32)
    ext = jnp.concatenate([a_prev, a_cur, a_next], axis=0)
    t = j * tl + lax.broadcasted_iota(jnp.int32, (tl, 1), 0)
    outs = []
    for gi, w in enumerate(POOL_WINDOWS):
        sl = slice(gi * POOL_GROUP_DIM, (gi + 1) * POOL_GROUP_DIM)
        p = ext[:, sl]
        step = 1
        while step < w:
            n = p.shape[0]
            p = p[:n - step] + p[step:]
            step *= 2
        off = POOL_HALO - w // 2
        cnt = jnp.minimum(t + w // 2, seq) - jnp.maximum(t - w // 2, 0)
        pooled = p[off:off + tl] / cnt.astype(F32) - a_cur[:, sl]
        outs.append(jnp.dot(pooled.astype(BF16), wp_ref[gi], preferred_element_type=F32))
    y = jnp.concatenate(outs, axis=1) * ps_ref[...]
    o_ref[...] = h + _mod_chunk(mod_ref, row, 2) * y


def pool_mixer(h, mods, layer, g, w_pool, pool_scale, *, is_ctx, tl=512):
    b, seq, _ = h.shape
    tl = min(tl, seq)
    nh = tl // POOL_HALO
    last = seq // POOL_HALO - 1
    body = functools.partial(_pool_body, tl=tl, seq=seq, is_ctx=is_ctx)
    return pl.pallas_call(
        body,
        grid=(b, seq // tl),
        in_specs=[pl.BlockSpec((None, POOL_HALO, D_MODEL), lambda i, j: (i, jnp.maximum(j * nh - 1, 0), 0)),
                  pl.BlockSpec((None, tl, D_MODEL), lambda i, j: (i, j, 0)),
                  pl.BlockSpec((None, POOL_HALO, D_MODEL), lambda i, j: (i, jnp.minimum((j + 1) * nh, last), 0)),
                  _mod_spec(layer),
                  pl.BlockSpec((1, D_MODEL), lambda i, j: (0, 0)),
                  pl.BlockSpec(w_pool.shape, lambda i, j: (0, 0, 0)),
                  pl.BlockSpec((1, D_MODEL), lambda i, j: (0, 0))],
        out_specs=pl.BlockSpec((None, tl, D_MODEL), lambda i, j: (i, j, 0)),
        out_shape=jax.ShapeDtypeStruct(h.shape, F32),
        compiler_params=_cparams("arbitrary", "arbitrary"),
        name="pool_mixer",
    )(h, h, h, mods, g.reshape(1, D_MODEL), w_pool, pool_scale.reshape(1, D_MODEL))


CONV_HALO = 16


def block_diag_weights(w):
    nb = LANES // MLSTM_QKV_BLOCK
    wc = w.reshape(-1, nb, MLSTM_QKV_BLOCK, MLSTM_QKV_BLOCK)
    eye = jnp.eye(nb, dtype=w.dtype)
    bd = jnp.einsum("cnij,nm->cnimj", wc, eye)
    return bd.reshape(-1, LANES, LANES)


def fold_gate_weights(w_q, w_k, w_v, w_gates):
    ng = w_gates.shape[1]
    wg = w_gates.reshape(3, -1, MLSTM_QKV_BLOCK, ng)
    fold = lambda w, part: jnp.einsum("ncd,ndg->ncg", w, wg[part], precision=lax.Precision.HIGHEST).reshape(-1, ng)
    return fold(w_q, 0) + fold(w_k, 1), fold(w_v, 2)


def _ml_feat_body(prev_ref, cur_ref, next_ref, cw_ref, cb_ref, wq_ref, wk_ref, wkt_ref, wv_ref,
                  gxc_ref, gxm_ref, gxct_ref, gxmt_ref, bg_ref, bgt_ref,
                  q_ref, k_ref, kt_ref, v_ref, xc_ref, g_ref, gt_ref, ext_ref, *, tl, seq):
    j = pl.program_id(1)
    cur = cur_ref[...]
    ext_ref[0:CONV_HALO, :] = prev_ref[...].astype(F32) * (j > 0).astype(F32)
    ext_ref[CONV_HALO:CONV_HALO + tl, :] = cur.astype(F32)
    ext_ref[CONV_HALO + tl:, :] = next_ref[...].astype(F32) * (j < seq // tl - 1).astype(F32)
    left = MLSTM_CONV // 2
    xc = cb_ref[...]
    for tap in range(MLSTM_CONV):
        xc = xc + ext_ref[pl.ds(CONV_HALO - left + tap, tl), :] * cw_ref[tap:tap + 1, :]
    xc = _silu(xc)
    xcb = xc.astype(BF16)
    xc_ref[...] = xcb
    t = SCAN_CHUNK
    qscale = MLSTM_HEAD_DIM ** -0.5
    for c in range(MLSTM_INNER // LANES):
        sl = slice(c * LANES, (c + 1) * LANES)
        xs = xcb[:, sl]
        q = jnp.dot(xs, wq_ref[c], preferred_element_type=F32)
        k = jnp.dot(xs, wk_ref[c], preferred_element_type=F32)
        v = jnp.dot(cur[:, sl], wv_ref[c], preferred_element_type=F32)
        q_ref[:, sl] = (q * qscale).astype(BF16)
        k_ref[:, sl] = k.astype(BF16)
        v_ref[:, sl] = v.astype(BF16)
        for cc in range(tl // t):
            kt = lax.dot_general(wkt_ref[c], xs[cc * t:(cc + 1) * t], _NT, preferred_element_type=F32)
            kt_ref[cc, sl, :] = kt.astype(BF16)
    ng = g_ref.shape[1]
    g = (jnp.dot(xcb, gxc_ref[...], preferred_element_type=F32)
         + jnp.dot(cur, gxm_ref[...], preferred_element_type=F32))
    g_ref[...] = g[:, :ng] + bg_ref[...]
    gt = (lax.dot_general(gxct_ref[...], xcb, _NT, preferred_element_type=F32)
          + lax.dot_general(gxmt_ref[...], cur, _NT, preferred_element_type=F32)) + bgt_ref[...]
    for cc in range(tl // t):
        gt_ref[cc] = gt[:, cc * t:(cc + 1) * t]


def mlstm_features(up, conv_w, conv_b, wq_bd, wk_bd, wkt_bd, wv_bd, gxc, gxm, bg, *, tl=256):
    b, seq, _ = up.shape
    tl = min(tl, seq)
    t = SCAN_CHUNK
    nh = tl // CONV_HALO
    last = seq // CONV_HALO - 1
    ng = gxc.shape[1]
    inner = MLSTM_INNER
    body = functools.partial(_ml_feat_body, tl=tl, seq=seq)
    full = lambda a: pl.BlockSpec(a.shape, lambda i, j: (0,) * a.ndim)
    cw = conv_w
    cb = conv_b.reshape(1, inner)
    bgr = bg.reshape(1, ng)
    bgc = bg.reshape(ng, 1)
    pad = lambda w: jnp.pad(w, ((0, 0), (0, LANES - ng))).astype(BF16)
    gxc_p, gxm_p = pad(gxc), pad(gxm)
    gxc_t, gxm_t = gxc.T.astype(BF16), gxm.T.astype(BF16)
    return pl.pallas_call(
        body,
        grid=(b, seq // tl),
        in_specs=[pl.BlockSpec((None, CONV_HALO, inner), lambda i, j: (i, jnp.maximum(j * nh - 1, 0), 0)),
                  pl.BlockSpec((None, tl, inner), lambda i, j: (i, j, 0)),
                  pl.BlockSpec((None, CONV_HALO, inner), lambda i, j: (i, jnp.minimum((j + 1) * nh, last), 0)),
                  full(cw), full(cb), full(wq_bd), full(wk_bd), full(wkt_bd), full(wv_bd),
                  full(gxc_p), full(gxm_p), full(gxc_t), full(gxm_t), full(bgr), full(bgc)],
        out_specs=[pl.BlockSpec((None, tl, inner), lambda i, j: (i, j, 0)),
                   pl.BlockSpec((None, tl, inner), lambda i, j: (i, j, 0)),
                   pl.BlockSpec((None, tl // t, inner, t), lambda i, j: (i, j, 0, 0)),
                   pl.BlockSpec((None, tl, inner), lambda i, j: (i, j, 0)),
                   pl.BlockSpec((None, tl, inner), lambda i, j: (i, j, 0)),
                   pl.BlockSpec((None, tl, ng), lambda i, j: (i, j, 0)),
                   pl.BlockSpec((None, tl // t, ng, t), lambda i, j: (i, j, 0, 0))],
        out_shape=[jax.ShapeDtypeStruct((b, seq, inner), BF16),
                   jax.ShapeDtypeStruct((b, seq, inner), BF16),
                   jax.ShapeDtypeStruct((b, seq // t, inner, t), BF16),
                   jax.ShapeDtypeStruct((b, seq, inner), BF16),
                   jax.ShapeDtypeStruct((b, seq, inner), BF16),
                   jax.ShapeDtypeStruct((b, seq, ng), F32),
                   jax.ShapeDtypeStruct((b, seq // t, ng, t), F32)],
        scratch_shapes=[pltpu.VMEM((tl + 2 * CONV_HALO, inner), F32)],
        compiler_params=_cparams("arbitrary", "arbitrary"),
        name="mlstm_features",
    )(up, up, up, cw, cb, wq_bd, wk_bd, wkt_bd, wv_bd, gxc_p, gxm_p, gxc_t, gxm_t, bgr, bgc)


def _log_sigmoid(x):
    return jnp.minimum(x, 0.0) - jnp.log1p(jnp.exp(-jnp.abs(x)))


def _scan_body(*refs, rev, nchunk, nblk, has_init):
    if has_init:
        q_ref, k_ref, kt_ref, v_ref, g_ref, gt_ref, c0_ref, m0_ref, h_ref, cf_ref, mf_ref, c_sc, m_sc = refs
    else:
        q_ref, k_ref, kt_ref, v_ref, g_ref, gt_ref, h_ref, cf_ref, mf_ref, c_sc, m_sc = refs
    hd = pl.program_id(1)
    j = pl.program_id(2)
    t = SCAN_CHUNK
    dh = MLSTM_HEAD_DIM

    @pl.when(j == 0)
    def _():
        if has_init:
            c_sc[...] = c0_ref[...]
            m_sc[...] = m0_ref[...]
        else:
            c_sc[...] = jnp.zeros_like(c_sc)
            m_sc[...] = jnp.zeros_like(m_sc)

    ci = (2 if rev else 0) * MLSTM_HEADS + hd
    cf = (3 if rev else 1) * MLSTM_HEADS + hd
    ng = g_ref.shape[1]
    lane = lax.broadcasted_iota(jnp.int32, (t, ng), 1)
    r_io = lax.broadcasted_iota(jnp.int32, (t, t), 0)
    c_io = lax.broadcasted_iota(jnp.int32, (t, t), 1)
    seen = (c_io >= r_io) if rev else (c_io <= r_io)
    seen_t = (r_io >= c_io) if rev else (r_io <= c_io)
    seen_f = seen.astype(F32)
    seen_tf = seen_t.astype(F32)

    order = range(nchunk - 1, -1, -1) if rev else range(nchunk)
    for cc in order:
        rows = slice(cc * t, (cc + 1) * t)
        q = q_ref[rows, :]
        kt = kt_ref[cc]
        v = v_ref[rows, :]
        g = g_ref[rows, :]
        i_col = jnp.sum(jnp.where(lane == ci, g, 0.0), axis=1, keepdims=True)
        f_col = jnp.sum(jnp.where(lane == cf, g, 0.0), axis=1, keepdims=True)
        i_row = gt_ref[cc, pl.ds(ci, 1), :]
        f_row = gt_ref[cc, pl.ds(cf, 1), :]
        lf_col = _log_sigmoid(f_col)
        lf_row = _log_sigmoid(f_row)
        b_col = jnp.sum(seen_f * lf_row, axis=1, keepdims=True)
        b_row = jnp.sum(seen_tf * lf_col, axis=0, keepdims=True)
        m_prev = m_sc[0:1, 0:1]
        n_row = c_sc[dh:dh + 1, :]
        dmat = jnp.where(seen, b_col - b_row + i_row, NEG_BIG)
        inter = b_col + m_prev
        m_t = jnp.maximum(inter, jnp.max(dmat, axis=1, keepdims=True))
        a = jnp.dot(q, kt, preferred_element_type=F32) * jnp.exp(dmat - m_t)
        w_int = jnp.exp(inter - m_t)
        cb = c_sc[0:dh, :].astype(BF16)
        num = (jnp.dot(a.astype(BF16), v, preferred_element_type=F32)
               + jnp.dot(q, cb, preferred_element_type=F32) * w_int)
        den = (jnp.sum(a, axis=1, keepdims=True)
               + w_int * jnp.sum(q.astype(F32) * n_row, axis=1, keepdims=True))
        hc = num / jnp.maximum(jnp.abs(den), jnp.exp(-m_t))
        h_ref[rows, :] = hc.astype(h_ref.dtype)
        b_end = jnp.sum(lf_row, axis=1, keepdims=True)
        g_row = b_end - b_row + i_row
        m_new = jnp.maximum(b_end + m_prev, jnp.max(g_row, axis=1, keepdims=True))
        decay = jnp.exp(b_end + m_prev - m_new)
        kw = (kt.astype(F32) * jnp.exp(g_row - m_new)).astype(BF16)
        w_col = jnp.exp(b_end - b_col + i_col - m_new)
        c_sc[0:dh, :] = decay * c_sc[0:dh, :] + jnp.dot(kw, v, preferred_element_type=F32)
        c_sc[dh:dh + 1, :] = decay * n_row + jnp.sum(k_ref[rows, :].astype(F32) * w_col, axis=0, keepdims=True)
        m_sc[...] = jnp.broadcast_to(m_new, m_sc.shape)

    @pl.when(j == nblk - 1)
    def _():
        cf_ref[...] = c_sc[...]
        mf_ref[...] = m_sc[...]


def mlstm_scan(q, k, kt, v, g, gt, state, *, rev, tb=1024):
    b, seq, inner = q.shape
    t = SCAN_CHUNK
    tb = min(tb, seq)
    nblk = seq // tb
    nchunk = tb // t
    dh = MLSTM_HEAD_DIM
    ng = g.shape[2]
    has_init = state is not None
    blk = (lambda j: nblk - 1 - j) if rev else (lambda j: j)
    body = functools.partial(_scan_body, rev=rev, nchunk=nchunk, nblk=nblk, has_init=has_init)
    tok_spec = pl.BlockSpec((None, tb, dh), lambda i, h, j: (i, blk(j), h))
    in_specs = [tok_spec, tok_spec,
                pl.BlockSpec((None, nchunk, dh, t), lambda i, h, j: (i, blk(j), h, 0)),
                tok_spec,
                pl.BlockSpec((None, tb, ng), lambda i, h, j: (i, blk(j), 0)),
                pl.BlockSpec((None, nchunk, ng, t), lambda i, h, j: (i, blk(j), 0, 0))]
    args = [q, k, kt, v, g, gt]
    st_spec_c = pl.BlockSpec((None, None, SCAN_STATE_ROWS, dh), lambda i, h, j: (i, h, 0, 0))
    st_spec_m = pl.BlockSpec((None, None, 8, LANES), lambda i, h, j: (i, h, 0, 0))
    if has_init:
        in_specs += [st_spec_c, st_spec_m]
        args += list(state)
    return pl.pallas_call(
        body,
        grid=(b, MLSTM_HEADS, nblk),
        in_specs=in_specs,
        out_specs=[tok_spec, st_spec_c, st_spec_m],
        out_shape=[jax.ShapeDtypeStruct((b, seq, inner), BF16),
                   jax.ShapeDtypeStruct((b, MLSTM_HEADS, SCAN_STATE_ROWS, dh), F32),
                   jax.ShapeDtypeStruct((b, MLSTM_HEADS, 8, LANES), F32)],
        scratch_shapes=[pltpu.VMEM((SCAN_STATE_ROWS, dh), F32), pltpu.VMEM((8, LANES), F32)],
        compiler_params=_cparams("arbitrary", "arbitrary", "arbitrary"),
        name="mlstm_scan_bwd" if rev else "mlstm_scan_fwd",
    )(*args)


SCAN_HEADS = 2


def _scan2_body(*refs, rev, nchunk, nblk, has_init):
    if has_init:
        q_ref, k_ref, kt_ref, v_ref, g_ref, gt_ref, c0_ref, m0_ref, h_ref, cf_ref, mf_ref, c_sc, m_sc = refs
    else:
        q_ref, k_ref, kt_ref, v_ref, g_ref, gt_ref, h_ref, cf_ref, mf_ref, c_sc, m_sc = refs
    hp = pl.program_id(1)
    j = pl.program_id(2)
    t = SCAN_CHUNK
    dh = MLSTM_HEAD_DIM

    @pl.when(j == 0)
    def _():
        if has_init:
            c_sc[...] = c0_ref[...]
            m_sc[...] = m0_ref[...]
        else:
            c_sc[...] = jnp.zeros_like(c_sc)
            m_sc[...] = jnp.zeros_like(m_sc)

    ng = g_ref.shape[1]
    lane = lax.broadcasted_iota(jnp.int32, (t, ng), 1)
    r_io = lax.broadcasted_iota(jnp.int32, (t, t), 0)
    c_io = lax.broadcasted_iota(jnp.int32, (t, t), 1)
    seen = (c_io >= r_io) if rev else (c_io <= r_io)
    seen_t = (r_io >= c_io) if rev else (r_io <= c_io)
    seen_f = seen.astype(F32)
    seen_tf = seen_t.astype(F32)

    def decays(cc, u):
        hd = hp * SCAN_HEADS + u
        ci = (2 if rev else 0) * MLSTM_HEADS + hd
        cf = (3 if rev else 1) * MLSTM_HEADS + hd
        g = g_ref[cc * t:(cc + 1) * t, :]
        i_col = jnp.sum(jnp.where(lane == ci, g, 0.0), axis=1, keepdims=True)
        f_col = jnp.sum(jnp.where(lane == cf, g, 0.0), axis=1, keepdims=True)
        i_row = gt_ref[cc, pl.ds(ci, 1), :]
        f_row = gt_ref[cc, pl.ds(cf, 1), :]
        lf_col = _log_sigmoid(f_col)
        lf_row = _log_sigmoid(f_row)
        b_col = jnp.sum(seen_f * lf_row, axis=1, keepdims=True)
        b_row = jnp.sum(seen_tf * lf_col, axis=0, keepdims=True)
        m_prev = m_sc[u, 0:1, 0:1]
        dmat = jnp.where(seen, b_col - b_row + i_row, NEG_BIG)
        inter = b_col + m_prev
        m_t = jnp.maximum(inter, jnp.max(dmat, axis=1, keepdims=True))
        b_end = jnp.sum(lf_row, axis=1, keepdims=True)
        g_row = b_end - b_row + i_row
        m_new = jnp.maximum(b_end + m_prev, jnp.max(g_row, axis=1, keepdims=True))
        return dict(dexp=jnp.exp(dmat - m_t), w_int=jnp.exp(inter - m_t), floor=jnp.exp(-m_t),
                    decay=jnp.exp(b_end + m_prev - m_new), w_row=jnp.exp(g_row - m_new),
                    w_col=jnp.exp(b_end - b_col + i_col - m_new), m_new=m_new)

    def readout(cc, u, d):
        rows = slice(cc * t, (cc + 1) * t)
        cols = slice(u * dh, (u + 1) * dh)
        q = q_ref[rows, cols]
        a = jnp.dot(q, kt_ref[cc, cols, :], preferred_element_type=F32) * d["dexp"]
        n_row = c_sc[u, dh:dh + 1, :]
        cb = c_sc[u, 0:dh, :].astype(BF16)
        num = (jnp.dot(a.astype(BF16), v_ref[rows, cols], preferred_element_type=F32)
               + jnp.dot(q, cb, preferred_element_type=F32) * d["w_int"])
        den = (jnp.sum(a, axis=1, keepdims=True)
               + d["w_int"] * jnp.sum(q.astype(F32) * n_row, axis=1, keepdims=True))
        hc = num / jnp.maximum(jnp.abs(den), d["floor"])
        h_ref[rows, cols] = hc.astype(h_ref.dtype)

    def update(cc, u, d):
        rows = slice(cc * t, (cc + 1) * t)
        cols = slice(u * dh, (u + 1) * dh)
        kw = (kt_ref[cc, cols, :].astype(F32) * d["w_row"]).astype(BF16)
        n_row = c_sc[u, dh:dh + 1, :]
        c_sc[u, 0:dh, :] = (d["decay"] * c_sc[u, 0:dh, :]
                            + jnp.dot(kw, v_ref[rows, cols], preferred_element_type=F32))
        c_sc[u, dh:dh + 1, :] = d["decay"] * n_row + jnp.sum(k_ref[rows, cols].astype(F32) * d["w_col"],
                                                              axis=0, keepdims=True)
        m_sc[u] = jnp.broadcast_to(d["m_new"], m_sc.shape[1:])

    order = range(nchunk - 1, -1, -1) if rev else range(nchunk)
    for cc in order:
        ds = [decays(cc, u) for u in range(SCAN_HEADS)]
        for u in range(SCAN_HEADS):
            readout(cc, u, ds[u])
        for u in range(SCAN_HEADS):
            update(cc, u, ds[u])

    @pl.when(j == nblk - 1)
    def _():
        cf_ref[...] = c_sc[...]
        mf_ref[...] = m_sc[...]


def mlstm_scan2(q, k, kt, v, g, gt, state, *, rev, tb=512):
    b, seq, inner = q.shape
    t = SCAN_CHUNK
    tb = min(tb, seq)
    nblk = seq // tb
    nchunk = tb // t
    dh = MLSTM_HEAD_DIM
    hw = SCAN_HEADS * dh
    ng = g.shape[2]
    has_init = state is not None
    blk = (lambda j: nblk - 1 - j) if rev else (lambda j: j)
    body = functools.partial(_scan2_body, rev=rev, nchunk=nchunk, nblk=nblk, has_init=has_init)
    tok_spec = pl.BlockSpec((None, tb, hw), lambda i, h, j: (i, blk(j), h))
    in_specs = [tok_spec, tok_spec,
                pl.BlockSpec((None, nchunk, hw, t), lambda i, h, j: (i, blk(j), h, 0)),
                tok_spec,
                pl.BlockSpec((None, tb, ng), lambda i, h, j: (i, blk(j), 0)),
                pl.BlockSpec((None, nchunk, ng, t), lambda i, h, j: (i, blk(j), 0, 0))]
    args = [q, k, kt, v, g, gt]
    st_spec_c = pl.BlockSpec((None, SCAN_HEADS, SCAN_STATE_ROWS, dh), lambda i, h, j: (i, h, 0, 0))
    st_spec_m = pl.BlockSpec((None, SCAN_HEADS, 8, LANES), lambda i, h, j: (i, h, 0, 0))
    if has_init:
        in_specs += [st_spec_c, st_spec_m]
        args += list(state)
    return pl.pallas_call(
        body,
        grid=(b, MLSTM_HEADS // SCAN_HEADS, nblk),
        in_specs=in_specs,
        out_specs=[tok_spec, st_spec_c, st_spec_m],
        out_shape=[jax.ShapeDtypeStruct((b, seq, inner), BF16),
                   jax.ShapeDtypeStruct((b, MLSTM_HEADS, SCAN_STATE_ROWS, dh), F32),
                   jax.ShapeDtypeStruct((b, MLSTM_HEADS, 8, LANES), F32)],
        scratch_shapes=[pltpu.VMEM((SCAN_HEADS, SCAN_STATE_ROWS, dh), F32), pltpu.VMEM((SCAN_HEADS, 8, LANES), F32)],
        compiler_params=_cparams("arbitrary", "arbitrary", "arbitrary"),
        name="mlstm_scan_bwd" if rev else "mlstm_scan_fwd",
    )(*args)


def _ml_out_body(hf_ref, hb_ref, xc_ref, z_ref, h_ref, mod_ref, gn_ref, sk_ref, w_ref, o_ref, *, tm, rows_per_batch):
    i = pl.program_id(0)
    row = (i * tm) // rows_per_batch if rows_per_batch else CTX_MOD_ROW
    hs = hf_ref[...].astype(F32) + hb_ref[...].astype(F32)
    parts = []
    for hd in range(MLSTM_HEADS):
        x = hs[:, hd * MLSTM_HEAD_DIM:(hd + 1) * MLSTM_HEAD_DIM]
        mu = jnp.mean(x, axis=-1, keepdims=True)
        xm = x - mu
        var = jnp.mean(xm * xm, axis=-1, keepdims=True)
        parts.append(xm * lax.rsqrt(var + NORM_EPS))
    hn = jnp.concatenate(parts, axis=1) * gn_ref[...]
    y = (hn + sk_ref[...] * xc_ref[...].astype(F32)) * _silu(z_ref[...].astype(F32))
    y = jnp.dot(y.astype(BF16), w_ref[...], preferred_element_type=F32)
    o_ref[...] = h_ref[...] + _mod_chunk(mod_ref, row, 2) * y


def mlstm_output(hf, hb, xc, up, h, mods, layer, gn_w, skip, w_down, *, rows_per_batch, tm=256):
    m = h.shape[0]
    tm = min(tm, m)
    inner = MLSTM_INNER
    body = functools.partial(_ml_out_body, tm=tm, rows_per_batch=rows_per_batch)
    row_spec = pl.BlockSpec((tm, inner), lambda i: (i, 0))
    return pl.pallas_call(
        body,
        grid=(m // tm,),
        in_specs=[row_spec, row_spec, row_spec,
                  pl.BlockSpec((tm, inner), lambda i: (i, 1)),
                  pl.BlockSpec((tm, D_MODEL), lambda i: (i, 0)),
                  _mod_spec(layer),
                  pl.BlockSpec((1, inner), lambda i: (0, 0)),
                  pl.BlockSpec((1, inner), lambda i: (0, 0)),
                  _resident((inner, D_MODEL), lambda i: (0, 0))],
        out_specs=pl.BlockSpec((tm, D_MODEL), lambda i: (i, 0)),
        out_shape=jax.ShapeDtypeStruct((m, D_MODEL), F32),
        compiler_params=_cparams("arbitrary"),
        name="mlstm_output",
    )(hf, hb, xc, up, h, mods, gn_w.reshape(1, inner), skip.reshape(1, inner), w_down)


def _router_body(xl_ref, xc_ref, mod_ref, g_ref, wr_ref, a_ref, lg_ref, *, tm, rows_per_batch, n_lat):
    i = pl.program_id(0)
    is_lat = i < n_lat
    row = jnp.where(is_lat, (i * tm) // rows_per_batch, CTX_MOD_ROW)
    x = jnp.where(is_lat, xl_ref[...], xc_ref[...])
    a = _norm_mod(x, g_ref[...], _mod_chunk(mod_ref, row, 3), _mod_chunk(mod_ref, row, 4))
    a_ref[...] = a
    lg_ref[...] = jnp.dot(a, wr_ref[...], preferred_element_type=F32, precision=lax.Precision.HIGHEST)


def moe_router(x_lat, x_ctx, mods, layer, g, wr_pad, *, rows_per_batch, tm=512):
    n_lat = x_lat.shape[0] // tm
    n_ctx = 0 if x_ctx is None else x_ctx.shape[0] // tm
    n = (n_lat + n_ctx) * tm
    if x_ctx is None:
        x_ctx = x_lat
    body = functools.partial(_router_body, tm=tm, rows_per_batch=rows_per_batch, n_lat=n_lat)
    return pl.pallas_call(
        body,
        grid=(n_lat + n_ctx,),
        in_specs=[pl.BlockSpec((tm, D_MODEL), lambda i: (jnp.minimum(i, n_lat - 1), 0)),
                  pl.BlockSpec((tm, D_MODEL), lambda i: (jnp.maximum(i - n_lat, 0), 0)),
                  _mod_spec(layer),
                  pl.BlockSpec((1, D_MODEL), lambda i: (0, 0)),
                  pl.BlockSpec((D_MODEL, LANES), lambda i: (0, 0))],
        out_specs=[pl.BlockSpec((tm, D_MODEL), lambda i: (i, 0)),
                   pl.BlockSpec((tm, LANES), lambda i: (i, 0))],
        out_shape=[jax.ShapeDtypeStruct((n, D_MODEL), F32),
                   jax.ShapeDtypeStruct((n, LANES), F32)],
        compiler_params=_cparams("arbitrary"),
        name="moe_router",
    )(x_lat, x_ctx, mods, g.reshape(1, D_MODEL), wr_pad)


def moe_route(logits, tm):
    n = logits.shape[0]
    a_tot = n * TOP_K
    top_v, top_e = lax.top_k(logits[:, :N_EXPERTS], TOP_K)
    gates = jax.nn.softmax(top_v, axis=-1)
    e_flat = top_e.reshape(a_tot).astype(jnp.int32)
    order = jnp.argsort(e_flat).astype(jnp.int32)
    counts = jnp.sum((e_flat[:, None] == jnp.arange(N_EXPERTS, dtype=jnp.int32)[None, :]).astype(jnp.int32), axis=0)
    starts = jnp.cumsum(counts) - counts
    padded = (counts + tm - 1) // tm * tm
    pend = jnp.cumsum(padded)
    pstarts = pend - padded
    n_blocks = (a_tot + N_EXPERTS * (tm - 1)) // tm
    blk_row = jnp.arange(n_blocks, dtype=jnp.int32) * tm
    block_e = jnp.minimum(jnp.searchsorted(pend, blk_row, side="right"), N_EXPERTS - 1).astype(jnp.int32)
    into = blk_row - pstarts[block_e]
    n_valid = jnp.clip(counts[block_e] - into, 0, tm).astype(jnp.int32)
    base = jnp.clip(starts[block_e] + into, 0, a_tot)
    order_pad = jnp.concatenate([order, jnp.zeros((tm,), jnp.int32)])
    pair = jax.vmap(lambda s: lax.dynamic_slice(order_pad, (s,), (tm,)))(base)
    valid = jnp.arange(tm, dtype=jnp.int32)[None, :] < n_valid[:, None]
    src = jnp.where(valid, pair // TOP_K, 0)
    dst = jnp.where(valid, (pair % TOP_K) * n + pair // TOP_K, 0)
    n_used = (pend[-1] // tm).astype(jnp.int32).reshape(1)
    return gates, block_e, n_used, n_valid, src.reshape(n_blocks, 1, tm), dst.reshape(n_blocks, 1, tm)


ROW_UNROLL = 8


def _for_rows(n, fn):
    full = n // ROW_UNROLL

    def group(c, carry):
        for u in range(ROW_UNROLL):
            fn(c * ROW_UNROLL + u)
        return carry
    lax.fori_loop(0, full, group, 0)

    def single(r, carry):
        fn(r)
        return carry
    lax.fori_loop(full * ROW_UNROLL, n, single, 0)


def _moe_body(be_ref, nu_ref, nv_ref, src_ref, srcn_ref, dst_ref, a_hbm, w1_ref, w3_ref, w2_ref, y_hbm,
              xf_ref, xb_ref, acc_ref, gsem, ssem, *, nf):
    i = pl.program_id(0)
    f = pl.program_id(1)
    n_used = nu_ref[0]
    slot = i % 2

    def gather_copy(s, r, tok):
        return pltpu.make_async_copy(a_hbm.at[pl.ds(tok, 1), :], xf_ref.at[s, pl.ds(r, 1), :], gsem.at[s])

    def scatter_copy(s, r, row):
        return pltpu.make_async_copy(acc_ref.at[s, pl.ds(r, 1), :], y_hbm.at[pl.ds(row, 1), :], ssem.at[s])

    def start_gather(s, idx_ref, n):
        _for_rows(n, lambda r: gather_copy(s, r, idx_ref[0, r]).start())

    def wait_gather(s, n):
        _for_rows(n, lambda r: gather_copy(s, r, 0).wait())

    def wait_scatter(s, n):
        _for_rows(n, lambda r: scatter_copy(s, r, 0).wait())

    @pl.when(i < n_used)
    def _():
        @pl.when(f == 0)
        def _():
            @pl.when(i == 0)
            def _():
                xf_ref[...] = jnp.zeros_like(xf_ref)
                start_gather(0, src_ref, nv_ref[0])

            wait_gather(slot, nv_ref[i])
            xb_ref[...] = xf_ref[slot].astype(BF16)

            @pl.when(i + 1 < n_used)
            def _():
                start_gather(1 - slot, srcn_ref, nv_ref[i + 1])

        x = xb_ref[...]
        u = jnp.dot(x, w1_ref[...], preferred_element_type=F32)
        v = jnp.dot(x, w3_ref[...], preferred_element_type=F32)
        p = (_silu(u) * v).astype(BF16)
        y = jnp.dot(p, w2_ref[...], preferred_element_type=F32)

        @pl.when(f == 0)
        def _():
            acc_ref[slot] = y

        @pl.when(f > 0)
        def _():
            acc_ref[slot] += y

        @pl.when(f == nf - 1)
        def _():
            _for_rows(nv_ref[i], lambda r: scatter_copy(slot, r, dst_ref[0, r]).start())

            @pl.when(i > 0)
            def _():
                wait_scatter(1 - slot, nv_ref[i - 1])

            @pl.when(i == n_used - 1)
            def _():
                wait_scatter(slot, nv_ref[i])


def moe_experts(a, block_e, n_used, n_valid, src, dst, w1, w3, w2, *, tf=1792):
    n = a.shape[0]
    n_blocks, _, tm = src.shape
    f_dim = w1.shape[2]
    nf = f_dim // tf
    body = functools.partial(_moe_body, nf=nf)

    def wmap(kind):
        def index_map(i, f, be, nu, nv):
            live = i < nu[0]
            ff = jnp.where(live, f, nf - 1)
            ii = jnp.where(live, i, nu[0] - 1)
            return (be[ii], 0, ff) if kind == "up" else (be[ii], ff, 0)
        return index_map

    idx_spec = pl.BlockSpec((None, 1, tm), lambda i, f, be, nu, nv: (i, 0, 0), memory_space=pltpu.SMEM)
    next_spec = pl.BlockSpec((None, 1, tm), lambda i, f, be, nu, nv: (jnp.minimum(i + 1, n_blocks - 1), 0, 0),
                             memory_space=pltpu.SMEM)
    grid_spec = pltpu.PrefetchScalarGridSpec(
        num_scalar_prefetch=3,
        grid=(n_blocks, nf),
        in_specs=[idx_spec, next_spec, idx_spec,
                  pl.BlockSpec(memory_space=pl.ANY),
                  pl.BlockSpec((None, D_MODEL, tf), wmap("up")),
                  pl.BlockSpec((None, D_MODEL, tf), wmap("up")),
                  pl.BlockSpec((None, tf, D_MODEL), wmap("down"))],
        out_specs=pl.BlockSpec(memory_space=pl.ANY),
        scratch_shapes=[pltpu.VMEM((2, tm, D_MODEL), F32), pltpu.VMEM((tm, D_MODEL), BF16),
                        pltpu.VMEM((2, tm, D_MODEL), F32),
                        pltpu.SemaphoreType.DMA((2,)), pltpu.SemaphoreType.DMA((2,))],
    )
    return pl.pallas_call(
        body,
        grid_spec=grid_spec,
        out_shape=jax.ShapeDtypeStruct((TOP_K * n, D_MODEL), F32),
        compiler_params=_cparams("arbitrary", "arbitrary"),
        name="moe_experts",
    )(block_e, n_used, n_valid, src, src, dst, a, w1, w3, w2)


def _combine_body(h_ref, y0_ref, y1_ref, gt_ref, mod_ref, fg_ref, o_ref, *, tm, rows_per_batch, final):
    i = pl.program_id(0)
    row = (i * tm) // rows_per_batch if rows_per_batch else CTX_MOD_ROW
    gt = gt_ref[...]
    f = y0_ref[...] * gt[:, 0:1] + y1_ref[...] * gt[:, 1:2]
    out = h_ref[...] + _mod_chunk(mod_ref, row, 5) * f
    if final:
        ms = jnp.mean(out * out, axis=-1, keepdims=True)
        out = out * lax.rsqrt(ms + NORM_EPS) * fg_ref[...]
    o_ref[...] = out


def moe_combine(h, y, gates, mods, layer, final_g, *, n_tok, row_off, rows_per_batch, final, tm=512):
    m = h.shape[0]
    tm = min(tm, m)
    o0 = row_off // tm
    o1 = (n_tok + row_off) // tm
    body = functools.partial(_combine_body, tm=tm, rows_per_batch=rows_per_batch, final=final)
    return pl.pallas_call(
        body,
        grid=(m // tm,),
        in_specs=[pl.BlockSpec((tm, D_MODEL), lambda i: (i, 0)),
                  pl.BlockSpec((tm, D_MODEL), lambda i: (o0 + i, 0)),
                  pl.BlockSpec((tm, D_MODEL), lambda i: (o1 + i, 0)),
                  pl.BlockSpec((tm, TOP_K), lambda i: (o0 + i, 0)),
                  _mod_spec(layer),
                  pl.BlockSpec((1, D_MODEL), lambda i: (0, 0))],
        out_specs=pl.BlockSpec((tm, D_MODEL), lambda i: (i, 0)),
        out_shape=jax.ShapeDtypeStruct((m, D_MODEL), F32),
        compiler_params=_cparams("arbitrary"),
        name="moe_combine",
    )(h, y, y, gates, mods, final_g.reshape(1, D_MODEL))


SORT_TOKENS = 512
ROW_GROUP = 16
EXPERT_ROWS = 16
SORT_SLOTS = -(-(TOP_K * SORT_TOKENS + N_EXPERTS * (ROW_GROUP - 1)) // LANES) * LANES
GROUPS_PER_BLOCK = MOE_ROWS // ROW_GROUP
META_SLOT0, META_SLOT1, META_GATE0, META_GATE1 = 0, 1, 2, 3


def _sort_body(xl_ref, xc_ref, mod_ref, g_ref, wrt_ref, as_ref, meta_ref, cnt_ref, *, tm, rows_per_batch, n_lat):
    i = pl.program_id(0)
    is_lat = i < n_lat
    row = jnp.where(is_lat, (i * tm) // rows_per_batch, CTX_MOD_ROW)
    x = jnp.where(is_lat, xl_ref[...], xc_ref[...])
    a = _norm_mod(x, g_ref[...], _mod_chunk(mod_ref, row, 3), _mod_chunk(mod_ref, row, 4))
    e_io = lax.broadcasted_iota(jnp.int32, (EXPERT_ROWS, tm), 0)
    lt = lax.dot_general(wrt_ref[...], a, _NT, preferred_element_type=F32, precision=lax.Precision.HIGHEST)
    lt = jnp.where(e_io < N_EXPERTS, lt, -jnp.inf)
    m0 = jnp.max(lt, axis=0, keepdims=True)
    e0 = jnp.min(jnp.where(lt == m0, e_io, EXPERT_ROWS), axis=0, keepdims=True)
    oh0 = e_io == e0
    lt1 = jnp.where(oh0, -jnp.inf, lt)
    m1 = jnp.max(lt1, axis=0, keepdims=True)
    e1 = jnp.min(jnp.where(lt1 == m1, e_io, EXPERT_ROWS), axis=0, keepdims=True)
    oh1 = e_io == e1
    ex = jnp.exp(m1 - m0)
    gate0 = 1.0 / (1.0 + ex)
    gate1 = ex / (1.0 + ex)
    oh = jnp.where(oh0, 1.0, jnp.where(oh1, 1.0, 0.0))
    n_io = lax.broadcasted_iota(jnp.int32, (tm, tm), 0)
    c_io = lax.broadcasted_iota(jnp.int32, (tm, tm), 1)
    earlier = jnp.where(n_io < c_io, 1.0, 0.0).astype(BF16)
    rank = jnp.dot(oh.astype(BF16), earlier, preferred_element_type=F32)
    cnt = jnp.sum(oh, axis=1, keepdims=True)
    padded = jnp.floor((cnt + (ROW_GROUP - 1)) * (1.0 / ROW_GROUP)) * ROW_GROUP
    r8 = lax.broadcasted_iota(jnp.int32, (EXPERT_ROWS, EXPERT_ROWS), 0)
    c8 = lax.broadcasted_iota(jnp.int32, (EXPERT_ROWS, EXPERT_ROWS), 1)
    padded_row = jnp.sum(jnp.where(r8 == c8, padded, 0.0), axis=0, keepdims=True)
    start = jnp.sum(jnp.where(c8 < r8, padded_row, 0.0), axis=1, keepdims=True)
    slot0 = jnp.sum(jnp.where(oh0, start + rank, 0.0), axis=0, keepdims=True)
    slot1 = jnp.sum(jnp.where(oh1, start + rank, 0.0), axis=0, keepdims=True)
    j_io = lax.broadcasted_iota(jnp.int32, (SORT_SLOTS, tm), 0).astype(F32)
    perm = jnp.where(j_io == slot0, 1.0, jnp.where(j_io == slot1, 1.0, 0.0)).astype(BF16)
    as_ref[...] = jnp.dot(perm, a.astype(BF16), preferred_element_type=F32).astype(BF16)
    rows = jnp.concatenate([slot0, slot1, gate0, gate1, jnp.zeros((LANES - 4, tm), F32)], axis=0)
    meta_ref[...] = rows.T
    cnt_ref[...] = jnp.concatenate([jnp.broadcast_to(padded, (EXPERT_ROWS, LANES)),
                                    jnp.broadcast_to(start, (EXPERT_ROWS, LANES))], axis=0)


def moe_sort(x_lat, x_ctx, mods, layer, g, wrt, *, rows_per_batch):
    tm = SORT_TOKENS
    n_lat = x_lat.shape[0] // tm
    n_ctx = 0 if x_ctx is None else x_ctx.shape[0] // tm
    nt = n_lat + n_ctx
    if x_ctx is None:
        x_ctx = x_lat
    body = functools.partial(_sort_body, tm=tm, rows_per_batch=rows_per_batch, n_lat=n_lat)
    return pl.pallas_call(
        body,
        grid=(nt,),
        in_specs=[pl.BlockSpec((tm, D_MODEL), lambda i: (jnp.minimum(i, n_lat - 1), 0)),
                  pl.BlockSpec((tm, D_MODEL), lambda i: (jnp.maximum(i - n_lat, 0), 0)),
                  _mod_spec(layer),
                  pl.BlockSpec((1, D_MODEL), lambda i: (0, 0)),
                  pl.BlockSpec((EXPERT_ROWS, D_MODEL), lambda i: (0, 0))],
        out_specs=[pl.BlockSpec((SORT_SLOTS, D_MODEL), lambda i: (i, 0)),
                   pl.BlockSpec((tm, LANES), lambda i: (i, 0)),
                   pl.BlockSpec((None, 2 * EXPERT_ROWS, LANES), lambda i: (i, 0, 0))],
        out_shape=[jax.ShapeDtypeStruct((nt * SORT_SLOTS, D_MODEL), BF16),
                   jax.ShapeDtypeStruct((nt * tm, LANES), F32),
                   jax.ShapeDtypeStruct((nt, 2 * EXPERT_ROWS, LANES), F32)],
        compiler_params=_cparams("arbitrary"),
        name="moe_sort",
    )(x_lat, x_ctx, mods, g.reshape(1, D_MODEL), wrt)


def moe_group_table(cnt):
    nt = cnt.shape[0]
    padded = cnt[:, :N_EXPERTS, 0].astype(jnp.int32)
    start = cnt[:, EXPERT_ROWS:EXPERT_ROWS + N_EXPERTS, 0].astype(jnp.int32)
    groups = padded // ROW_GROUP
    cum = jnp.cumsum(groups, axis=0)
    tot = cum[-1]
    blocks = (tot + GROUPS_PER_BLOCK - 1) // GROUPS_PER_BLOCK
    bend = jnp.cumsum(blocks)
    bstart = bend - blocks
    n_blocks = (nt * SORT_SLOTS // ROW_GROUP + N_EXPERTS * (GROUPS_PER_BLOCK - 1)) // GROUPS_PER_BLOCK
    bi = jnp.arange(n_blocks, dtype=jnp.int32)
    block_e = jnp.minimum(jnp.searchsorted(bend, bi, side="right"), N_EXPERTS - 1).astype(jnp.int32)
    q = (bi - bstart[block_e])[:, None] * GROUPS_PER_BLOCK + jnp.arange(GROUPS_PER_BLOCK, dtype=jnp.int32)[None, :]
    n_valid = jnp.clip(tot[block_e] - (bi - bstart[block_e]) * GROUPS_PER_BLOCK, 0, GROUPS_PER_BLOCK).astype(jnp.int32)
    cum_e = cum.T[block_e]
    tile = jnp.sum((cum_e[:, None, :] <= q[:, :, None]).astype(jnp.int32), axis=2)
    tile = jnp.minimum(tile, nt - 1)
    before = jnp.take_along_axis(cum_e - groups.T[block_e], tile, axis=1)
    first = jnp.take_along_axis(start.T[block_e], tile, axis=1)
    rows = tile * SORT_SLOTS + first + (q - before) * ROW_GROUP
    valid = jnp.arange(GROUPS_PER_BLOCK, dtype=jnp.int32)[None, :] < n_valid[:, None]
    rows = jnp.where(valid, rows, 0).astype(jnp.int32)
    n_used = bend[-1].astype(jnp.int32).reshape(1)
    return block_e, n_used, n_valid, rows.reshape(n_blocks, 1, GROUPS_PER_BLOCK)


def _moe2_body(be_ref, nu_ref, nv_ref, row_ref, rown_ref, as_hbm, w1_ref, w3_ref, w2_ref, ys_hbm,
               x_ref, acc_ref, y_ref, gsem, ssem, *, nf):
    i = pl.program_id(0)
    f = pl.program_id(1)
    n_used = nu_ref[0]
    slot = i % 2

    def gather_copy(s, gidx, row):
        row = pl.multiple_of(row, ROW_GROUP)
        dst = pl.multiple_of(gidx * ROW_GROUP, ROW_GROUP)
        return pltpu.make_async_copy(as_hbm.at[pl.ds(row, ROW_GROUP), :], x_ref.at[s, pl.ds(dst, ROW_GROUP), :],
                                     gsem.at[s])

    def scatter_copy(s, gidx, row):
        row = pl.multiple_of(row, ROW_GROUP)
        src = pl.multiple_of(gidx * ROW_GROUP, ROW_GROUP)
        return pltpu.make_async_copy(y_ref.at[s, pl.ds(src, ROW_GROUP), :], ys_hbm.at[pl.ds(row, ROW_GROUP), :],
                                     ssem.at[s])

    def loop(n, fn):
        def body(r, c):
            fn(r)
            return c
        lax.fori_loop(0, n, body, 0)

    @pl.when(i < n_used)
    def _():
        @pl.when(f == 0)
        def _():
            @pl.when(i == 0)
            def _():
                x_ref[...] = jnp.zeros_like(x_ref)
                loop(nv_ref[0], lambda r: gather_copy(0, r, row_ref[0, r]).start())

            loop(nv_ref[i], lambda r: gather_copy(slot, r, 0).wait())

            @pl.when(i + 1 < n_used)
            def _():
                loop(nv_ref[i + 1], lambda r: gather_copy(1 - slot, r, rown_ref[0, r]).start())

        x = x_ref[slot]
        u = jnp.dot(x, w1_ref[...], preferred_element_type=F32)
        v = jnp.dot(x, w3_ref[...], preferred_element_type=F32)
        p = (_silu(u) * v).astype(BF16)
        y = jnp.dot(p, w2_ref[...], preferred_element_type=F32)

        @pl.when(f == 0)
        def _():
            acc_ref[...] = y

        @pl.when(jnp.logical_and(f > 0, f < nf - 1))
        def _():
            acc_ref[...] += y

        @pl.when(f == nf - 1)
        def _():
            y_ref[slot] = (acc_ref[...] + y).astype(BF16)
            loop(nv_ref[i], lambda r: scatter_copy(slot, r, row_ref[0, r]).start())

            @pl.when(i > 0)
            def _():
                loop(nv_ref[i - 1], lambda r: scatter_copy(1 - slot, r, 0).wait())

            @pl.when(i == n_used - 1)
            def _():
                loop(nv_ref[i], lambda r: scatter_copy(slot, r, 0).wait())


def moe_experts_sorted(a_sorted, block_e, n_used, n_valid, rows, w1, w3, w2, w_layer, *, tf=1792):
    n_blocks = rows.shape[0]
    tm = MOE_ROWS
    f_dim = w1.shape[3]
    nf = f_dim // tf
    assert nf >= 2
    body = functools.partial(_moe2_body, nf=nf)

    def wmap(kind):
        def index_map(i, f, be, nu, nv):
            live = i < nu[0]
            ff = jnp.where(live, f, nf - 1)
            ii = jnp.where(live, i, nu[0] - 1)
            return (w_layer, be[ii], 0, ff) if kind == "up" else (w_layer, be[ii], ff, 0)
        return index_map

    idx_spec = pl.BlockSpec((None, 1, GROUPS_PER_BLOCK), lambda i, f, be, nu, nv: (i, 0, 0), memory_space=pltpu.SMEM)
    next_spec = pl.BlockSpec((None, 1, GROUPS_PER_BLOCK),
                             lambda i, f, be, nu, nv: (jnp.minimum(i + 1, n_blocks - 1), 0, 0),
                             memory_space=pltpu.SMEM)
    grid_spec = pltpu.PrefetchScalarGridSpec(
        num_scalar_prefetch=3,
        grid=(n_blocks, nf),
        in_specs=[idx_spec, next_spec,
                  pl.BlockSpec(memory_space=pl.ANY),
                  pl.BlockSpec((None, None, D_MODEL, tf), wmap("up")),
                  pl.BlockSpec((None, None, D_MODEL, tf), wmap("up")),
                  pl.BlockSpec((None, None, tf, D_MODEL), wmap("down"))],
        out_specs=pl.BlockSpec(memory_space=pl.ANY),
        scratch_shapes=[pltpu.VMEM((2, tm, D_MODEL), BF16), pltpu.VMEM((tm, D_MODEL), F32),
                        pltpu.VMEM((2, tm, D_MODEL), BF16),
                        pltpu.SemaphoreType.DMA((2,)), pltpu.SemaphoreType.DMA((2,))],
    )
    return pl.pallas_call(
        body,
        grid_spec=grid_spec,
        out_shape=jax.ShapeDtypeStruct(a_sorted.shape, BF16),
        input_output_aliases={5: 0},
        compiler_params=_cparams("arbitrary", "arbitrary"),
        name="moe_experts",
    )(block_e, n_used, n_valid, rows, rows, a_sorted, w1, w3, w2)


def _unsort_body(h_ref, ys_ref, meta_ref, mod_ref, fg_ref, o_ref, *, tm, rows_per_batch, tile_off, final):
    i = pl.program_id(0)
    row = ((i * tm) // rows_per_batch) if rows_per_batch else CTX_MOD_ROW
    meta = meta_ref[...]
    slot0 = meta[:, META_SLOT0:META_SLOT0 + 1]
    slot1 = meta[:, META_SLOT1:META_SLOT1 + 1]
    gate0 = meta[:, META_GATE0:META_GATE0 + 1]
    gate1 = meta[:, META_GATE1:META_GATE1 + 1]
    j_io = lax.broadcasted_iota(jnp.int32, (tm, SORT_SLOTS), 1).astype(F32)
    pick = jnp.where(j_io == slot0, gate0, jnp.where(j_io == slot1, gate1, 0.0)).astype(BF16)
    fsum = jnp.dot(pick, ys_ref[...], preferred_element_type=F32)
    out = h_ref[...] + _mod_chunk(mod_ref, row, 5) * fsum
    if final:
        ms = jnp.mean(out * out, axis=-1, keepdims=True)
        out = out * lax.rsqrt(ms + NORM_EPS) * fg_ref[...]
    o_ref[...] = out


def moe_unsort_combine(h, ys, meta, mods, layer, final_g, *, tile_off, rows_per_batch, final):
    tm = SORT_TOKENS
    m = h.shape[0]
    body = functools.partial(_unsort_body, tm=tm, rows_per_batch=rows_per_batch, tile_off=tile_off, final=final)
    return pl.pallas_call(
        body,
        grid=(m // tm,),
        in_specs=[pl.BlockSpec((tm, D_MODEL), lambda i: (i, 0)),
                  pl.BlockSpec((SORT_SLOTS, D_MODEL), lambda i: (tile_off + i, 0)),
                  pl.BlockSpec((tm, LANES), lambda i: (tile_off + i, 0)),
                  _mod_spec(layer),
                  pl.BlockSpec((1, D_MODEL), lambda i: (0, 0))],
        out_specs=pl.BlockSpec((tm, D_MODEL), lambda i: (i, 0)),
        out_shape=jax.ShapeDtypeStruct((m, D_MODEL), F32),
        compiler_params=_cparams("arbitrary"),
        name="moe_combine",
    )(h, ys, meta, mods, final_g.reshape(1, D_MODEL))


def _na_layer(h_lat, h_ctx, mods, layer, g, w_qkv, b_qkv, rpb, w_out, b_out, with_ctx_out):
    b, seq, _ = h_lat.shape
    c = h_ctx.shape[1]
    qscale = jnp.concatenate([jnp.full((D_MODEL,), (D_MODEL // NA_HEADS) ** -0.5, F32), jnp.ones((2 * D_MODEL,), F32)])
    w = (w_qkv * qscale).astype(BF16)
    bias = b_qkv * qscale
    qkv = nm_matmul(h_lat.reshape(b * seq, D_MODEL), mods, layer, g, w, bias, rows_per_batch=seq, sh=0, sc=1)
    qkv_c = nm_matmul(h_ctx.reshape(b * c, D_MODEL), mods, layer, g, w, bias, rows_per_batch=None, sh=0, sc=1)
    qkv = qkv.reshape(b, seq, 3 * D_MODEL)
    qkv_c = qkv_c.reshape(b, c, 3 * D_MODEL)
    o_lat = na_attention(qkv, qkv_c, na_bias_table(rpb))
    wo = w_out.astype(BF16)
    h_lat = mm_residual(o_lat.reshape(b * seq, D_MODEL), h_lat.reshape(b * seq, D_MODEL), mods, layer, wo, b_out,
                        rows_per_batch=seq, gate=2).reshape(b, seq, D_MODEL)
    if with_ctx_out:
        o_ctx = ctx_attention(qkv_c)
        h_ctx = mm_residual(o_ctx.reshape(b * c, D_MODEL), h_ctx.reshape(b * c, D_MODEL), mods, layer, wo, b_out,
                            rows_per_batch=None, gate=2).reshape(b, c, D_MODEL)
    return h_lat, h_ctx


def _mlstm_layer(h_lat, h_ctx, mods, layer, g, w_up, conv_w, conv_b, w_q, w_k, w_v, w_gates, b_gates,
                 gn_w, skip, w_down, with_ctx_out):
    b, seq, _ = h_lat.shape
    c = h_ctx.shape[1]
    inner = MLSTM_INNER
    wu = w_up.astype(BF16)
    zero_b = jnp.zeros((2 * inner,), F32)
    wq_bd = block_diag_weights(w_q).astype(BF16)
    wk_bd = block_diag_weights(w_k).astype(BF16)
    wkt_bd = jnp.swapaxes(wk_bd, 1, 2)
    wv_bd = block_diag_weights(w_v).astype(BF16)
    gxc, gxm = fold_gate_weights(w_q, w_k, w_v, w_gates)
    wd = w_down.astype(BF16)

    def features(h, rows_per_batch):
        n, s, _ = h.shape
        up = nm_matmul(h.reshape(n * s, D_MODEL), mods, layer, g, wu, zero_b, rows_per_batch=rows_per_batch, sh=0, sc=1)
        up = up.reshape(n, s, 2 * inner)
        return up, mlstm_features(up, conv_w, conv_b, wq_bd, wk_bd, wkt_bd, wv_bd, gxc, gxm, b_gates)

    up_c, (q_c, k_c, kt_c, v_c, xc_c, g_c, gt_c) = features(h_ctx, None)
    up_l, (q_l, k_l, kt_l, v_l, xc_l, g_l, gt_l) = features(h_lat, seq)
    hf_c, cf, mf = mlstm_scan(q_c, k_c, kt_c, v_c, g_c, gt_c, None, rev=False)
    hb_c, cb, mb = mlstm_scan(q_c, k_c, kt_c, v_c, g_c, gt_c, None, rev=True)
    hf_l, _, _ = mlstm_scan(q_l, k_l, kt_l, v_l, g_l, gt_l, (cf, mf), rev=False)
    hb_l, _, _ = mlstm_scan(q_l, k_l, kt_l, v_l, g_l, gt_l, (cb, mb), rev=True)
    flat = lambda a: a.reshape(-1, a.shape[-1])
    h_lat = mlstm_output(flat(hf_l), flat(hb_l), flat(xc_l), flat(up_l), flat(h_lat), mods, layer, gn_w, skip, wd,
                         rows_per_batch=seq).reshape(b, seq, D_MODEL)
    if with_ctx_out:
        h_ctx = mlstm_output(flat(hf_c), flat(hb_c), flat(xc_c), flat(up_c), flat(h_ctx), mods, layer, gn_w, skip, wd,
                             rows_per_batch=None).reshape(b, c, D_MODEL)
    return h_lat, h_ctx


def _moe_layer(h_lat, h_ctx, mods, layer, g, w_router, w1, w3, w2, w_layer, final_g, last):
    b, seq, _ = h_lat.shape
    c = h_ctx.shape[1]
    wrt = jnp.pad(w_router.T, ((0, EXPERT_ROWS - N_EXPERTS), (0, 0)))
    hl = h_lat.reshape(b * seq, D_MODEL)
    hc = None if last else h_ctx.reshape(b * c, D_MODEL)
    a_sorted, meta, cnt = moe_sort(hl, hc, mods, layer, g, wrt, rows_per_batch=seq)
    block_e, n_used, n_valid, rows = moe_group_table(cnt)
    ys = moe_experts_sorted(a_sorted, block_e, n_used, n_valid, rows, w1, w3, w2, w_layer)
    h_lat = moe_unsort_combine(hl, ys, meta, mods, layer, final_g, tile_off=0, rows_per_batch=seq,
                               final=last).reshape(b, seq, D_MODEL)
    if not last:
        h_ctx = moe_unsort_combine(hc, ys, meta, mods, layer, final_g, tile_off=(b * seq) // SORT_TOKENS,
                                   rows_per_batch=None, final=False).reshape(b, c, D_MODEL)
    return h_lat, h_ctx


def kernel(x, c, ctx, c_ctx, w_mod, b_mod, norm_g, final_g, na_w_qkv, na_b_qkv, na_rpb, na_w_out, na_b_out,
           pool_w, pool_scale, ml_w_up, ml_conv_w, ml_conv_b, ml_w_q, ml_w_k, ml_w_v, ml_w_gates, ml_b_gates,
           ml_gn_w, ml_skip, ml_w_down, ffn_w1, ffn_w3, ffn_w2, moe_w_router, moe_w1, moe_w3, moe_w2):
    b, seq, _ = x.shape
    n_ctx = ctx.shape[1]
    depth = w_mod.shape[0]
    assert b <= CTX_MOD_ROW
    cond = jnp.zeros((MOD_ROWS, D_MODEL), F32).at[:b].set(c).at[CTX_MOD_ROW].set(c_ctx)
    mods = adaln_all(cond, w_mod, b_mod)
    moe_w1b, moe_w3b, moe_w2b = moe_w1.astype(BF16), moe_w3.astype(BF16), moe_w2.astype(BF16)
    h_lat, h_ctx = x, ctx
    for i in range(depth):
        last = i == depth - 1
        kind = i % 3
        j = i // 3
        g_tok = norm_g[i, 0]
        if kind == 0:
            h_lat, h_ctx = _na_layer(h_lat, h_ctx, mods, i, g_tok, na_w_qkv[j], na_b_qkv[j], na_rpb[j],
                                     na_w_out[j], na_b_out[j], not last)
        elif kind == 1:
            wp = pool_w[j].astype(BF16)
            h_lat = pool_mixer(h_lat, mods, i, g_tok, wp, pool_scale[j], is_ctx=False)
            if not last:
                h_ctx = pool_mixer(h_ctx, mods, i, g_tok, wp, pool_scale[j], is_ctx=True)
        else:
            h_lat, h_ctx = _mlstm_layer(h_lat, h_ctx, mods, i, g_tok, ml_w_up[j], ml_conv_w[j], ml_conv_b[j],
                                        ml_w_q[j], ml_w_k[j], ml_w_v[j], ml_w_gates[j], ml_b_gates[j],
                                        ml_gn_w[j], ml_skip[j], ml_w_down[j], not last)
        e = i // 2
        g_ch = norm_g[i, 1]
        if i % 2 == 0:
            w1, w3, w2 = ffn_w1[e].astype(BF16), ffn_w3[e].astype(BF16), ffn_w2[e].astype(BF16)
            h_lat = ffn_dense(h_lat.reshape(b * seq, D_MODEL), mods, i, g_ch, w1, w3, w2,
                              rows_per_batch=seq).reshape(b, seq, D_MODEL)
            if not last:
                h_ctx = ffn_dense(h_ctx.reshape(b * n_ctx, D_MODEL), mods, i, g_ch, w1, w3, w2,
                                  rows_per_batch=None).reshape(b, n_ctx, D_MODEL)
        else:
            h_lat, h_ctx = _moe_layer(h_lat, h_ctx, mods, i, g_ch, moe_w_router[e], moe_w1b, moe_w3b, moe_w2b, e,
                                      final_g, last)
    return h_lat
```

```python
import functools

import jax
import jax.numpy as jnp
from jax import lax
from jax.experimental import pallas as pl
from jax.experimental.pallas import tpu as pltpu

F32 = jnp.float32
BF16 = jnp.bfloat16

D_MODEL = 1024
N_MOD = 6
NORM_EPS = 1e-6
GRID_W = 64
NA_HEADS = 16
NA_WIN_ROWS = 8
NA_WIN_COLS = 16
POOL_WINDOWS = (2, 4, 8, 16)
POOL_GROUP_DIM = D_MODEL // len(POOL_WINDOWS)
MLSTM_INNER = 2 * D_MODEL
MLSTM_HEADS = 4
MLSTM_HEAD_DIM = MLSTM_INNER // MLSTM_HEADS
MLSTM_CONV = 4
MLSTM_QKV_BLOCK = 4
N_EXPERTS = 8
TOP_K = 2

LANES = 128
MOD_ROWS = 8
CTX_MOD_ROW = 4
VMEM_LIMIT_BYTES = 56 * 1024 * 1024
NEG_BIG = -1e30
SCAN_CHUNK = 256
SCAN_STATE_ROWS = MLSTM_HEAD_DIM + 8
MOE_ROWS = 1024


def _cparams(*sem):
    return pltpu.CompilerParams(dimension_semantics=sem, vmem_limit_bytes=VMEM_LIMIT_BYTES)


def _resident(shape, index_map):
    return pl.BlockSpec(shape, index_map, pipeline_mode=pl.Buffered(1))


def _silu(x):
    return x * jax.nn.sigmoid(x)


def _norm_mod(x, g, shift, scale):
    ms = jnp.mean(x * x, axis=-1, keepdims=True)
    y = x * lax.rsqrt(ms + NORM_EPS) * g
    return y * (1.0 + scale) + shift


def _mod_chunk(mod_ref, row, j):
    return mod_ref[pl.ds(row, 1), pl.ds(j * D_MODEL, D_MODEL)]


def _mod_spec(layer):
    return pl.BlockSpec((None, MOD_ROWS, N_MOD * D_MODEL), lambda *_: (layer, 0, 0))


def _adaln_body(c_ref, w_ref, b_ref, o_ref):
    s = _silu(c_ref[...])
    o_ref[...] = jnp.dot(s, w_ref[...], preferred_element_type=F32) + b_ref[...]


def adaln_all(cond, w_mod, b_mod):
    depth = w_mod.shape[0]
    n = N_MOD * D_MODEL
    tn = 1536
    return pl.pallas_call(
        _adaln_body,
        grid=(depth, n // tn),
        in_specs=[pl.BlockSpec((MOD_ROWS, D_MODEL), lambda l, j: (0, 0)),
                  pl.BlockSpec((None, D_MODEL, tn), lambda l, j: (l, 0, j)),
                  pl.BlockSpec((None, 1, tn), lambda l, j: (l, 0, j))],
        out_specs=pl.BlockSpec((None, MOD_ROWS, tn), lambda l, j: (l, 0, j)),
        out_shape=jax.ShapeDtypeStruct((depth, MOD_ROWS, n), F32),
        compiler_params=_cparams("arbitrary", "arbitrary"),
        name="adaln",
    )(cond, w_mod, b_mod.reshape(depth, 1, n))


def _nm_matmul_body(x_ref, mod_ref, g_ref, w_ref, b_ref, o_ref, *, tm, rows_per_batch, sh, sc, nc):
    i = pl.program_id(0)
    row = (i * tm) // rows_per_batch if rows_per_batch else CTX_MOD_ROW
    a = _norm_mod(x_ref[...], g_ref[...], _mod_chunk(mod_ref, row, sh), _mod_chunk(mod_ref, row, sc)).astype(BF16)
    n = o_ref.shape[1]
    for c in range(n // nc):
        sl = slice(c * nc, (c + 1) * nc)
        y = jnp.dot(a, w_ref[:, sl], preferred_element_type=F32) + b_ref[:, sl]
        o_ref[:, sl] = y.astype(o_ref.dtype)


def nm_matmul(x, mods, layer, g, w, bias, *, rows_per_batch, sh, sc, tm=512, nc=1024, out_dtype=BF16):
    m, n = x.shape[0], w.shape[1]
    tm = min(tm, m)
    body = functools.partial(_nm_matmul_body, tm=tm, rows_per_batch=rows_per_batch, sh=sh, sc=sc, nc=nc)
    return pl.pallas_call(
        body,
        grid=(m // tm,),
        in_specs=[pl.BlockSpec((tm, D_MODEL), lambda i: (i, 0)),
                  _mod_spec(layer),
                  pl.BlockSpec((1, D_MODEL), lambda i: (0, 0)),
                  _resident((D_MODEL, n), lambda i: (0, 0)),
                  pl.BlockSpec((1, n), lambda i: (0, 0))],
        out_specs=pl.BlockSpec((tm, n), lambda i: (i, 0)),
        out_shape=jax.ShapeDtypeStruct((m, n), out_dtype),
        compiler_params=_cparams("arbitrary"),
        name="nm_matmul",
    )(x, mods, g.reshape(1, D_MODEL), w, bias.reshape(1, n))


def _mm_res_body(a_ref, h_ref, mod_ref, w_ref, b_ref, o_ref, *, tm, rows_per_batch, gate):
    i = pl.program_id(0)
    row = (i * tm) // rows_per_batch if rows_per_batch else CTX_MOD_ROW
    y = jnp.dot(a_ref[...], w_ref[...], preferred_element_type=F32) + b_ref[...]
    o_ref[...] = h_ref[...] + _mod_chunk(mod_ref, row, gate) * y


def mm_residual(a, h, mods, layer, w, bias, *, rows_per_batch, gate, tm=512):
    m, k = a.shape
    tm = min(tm, m)
    body = functools.partial(_mm_res_body, tm=tm, rows_per_batch=rows_per_batch, gate=gate)
    return pl.pallas_call(
        body,
        grid=(m // tm,),
        in_specs=[pl.BlockSpec((tm, k), lambda i: (i, 0)),
                  pl.BlockSpec((tm, D_MODEL), lambda i: (i, 0)),
                  _mod_spec(layer),
                  _resident((k, D_MODEL), lambda i: (0, 0)),
                  pl.BlockSpec((1, D_MODEL), lambda i: (0, 0))],
        out_specs=pl.BlockSpec((tm, D_MODEL), lambda i: (i, 0)),
        out_shape=jax.ShapeDtypeStruct((m, D_MODEL), F32),
        compiler_params=_cparams("arbitrary"),
        name="mm_residual",
    )(a, h, mods, w, bias.reshape(1, D_MODEL))


def _ffn_body(h_ref, mod_ref, g_ref, w1_ref, w3_ref, w2_ref, o_ref, *, tm, rows_per_batch):
    i = pl.program_id(0)
    row = (i * tm) // rows_per_batch if rows_per_batch else CTX_MOD_ROW
    h = h_ref[...]
    a = _norm_mod(h, g_ref[...], _mod_chunk(mod_ref, row, 3), _mod_chunk(mod_ref, row, 4)).astype(BF16)
    u = jnp.dot(a, w1_ref[...], preferred_element_type=F32)
    v = jnp.dot(a, w3_ref[...], preferred_element_type=F32)
    p = (_silu(u) * v).astype(BF16)
    y = jnp.dot(p, w2_ref[...], preferred_element_type=F32)
    o_ref[...] = h + _mod_chunk(mod_ref, row, 5) * y


def ffn_dense(h, mods, layer, g, w1, w3, w2, *, rows_per_batch, tm=256):
    m = h.shape[0]
    f = w1.shape[1]
    tm = min(tm, m)
    body = functools.partial(_ffn_body, tm=tm, rows_per_batch=rows_per_batch)
    return pl.pallas_call(
        body,
        grid=(m // tm,),
        in_specs=[pl.BlockSpec((tm, D_MODEL), lambda i: (i, 0)),
                  _mod_spec(layer),
                  pl.BlockSpec((1, D_MODEL), lambda i: (0, 0)),
                  _resident((D_MODEL, f), lambda i: (0, 0)),
                  _resident((D_MODEL, f), lambda i: (0, 0)),
                  _resident((f, D_MODEL), lambda i: (0, 0))],
        out_specs=pl.BlockSpec((tm, D_MODEL), lambda i: (i, 0)),
        out_shape=jax.ShapeDtypeStruct((m, D_MODEL), F32),
        compiler_params=_cparams("arbitrary"),
        name="ffn_dense",
    )(h, mods, g.reshape(1, D_MODEL), w1, w3, w2)


def na_bias_table(rpb):
    h = rpb.shape[0]
    col = jnp.arange(GRID_W)
    c0 = jnp.clip(col - NA_WIN_COLS // 2, 0, GRID_W - NA_WIN_COLS)
    col_ok = (col[None, :] >= c0[:, None]) & (col[None, :] < c0[:, None] + NA_WIN_COLS)
    dcol = jnp.clip(col[None, :] - col[:, None], 1 - NA_WIN_COLS, NA_WIN_COLS - 1) + (NA_WIN_COLS - 1)
    n_drow = 2 * NA_WIN_ROWS - 1
    t = jnp.where(col_ok[None, None], rpb[:, :, dcol].astype(F32), NEG_BIG)
    t = t.transpose(0, 2, 1, 3).reshape(h // 2, 2 * GRID_W, n_drow * GRID_W)
    tiles = [t[:, :, (NA_WIN_ROWS - 1 - off) * GRID_W:(2 * NA_WIN_ROWS - 1 - off) * GRID_W]
             for off in range(NA_WIN_ROWS)]
    return jnp.stack(tiles, axis=1)


def _stack_heads(q):
    lo = lax.broadcasted_iota(jnp.int32, q.shape, 1) < (LANES // 2)
    zero = jnp.zeros_like(q)
    return jnp.concatenate([jnp.where(lo, q, zero), jnp.where(lo, zero, q)], axis=0)


def _unstack_heads(o):
    n = o.shape[0] // 2
    lo = lax.broadcasted_iota(jnp.int32, (n, LANES), 1) < (LANES // 2)
    return jnp.where(lo, o[:n], o[n:])


_NT = (((1,), (1,)), ((), ()))
NA_GROUP = 8


def _na_body(q_ref, k_ref, v_ref, kc_ref, vc_ref, bias_ref, o_ref, *, rows):
    kc = kc_ref[...]
    vc = vc_ref[...]
    kwin = NA_WIN_ROWS * GRID_W

    def scores(r):
        r0 = jnp.clip(r - NA_WIN_ROWS // 2, 0, rows - NA_WIN_ROWS)
        qoff = pl.multiple_of(r * GRID_W, GRID_W)
        koff = pl.multiple_of(r0 * GRID_W, GRID_W)
        qs = _stack_heads(q_ref[pl.ds(qoff, GRID_W), :])
        k = k_ref[pl.ds(koff, kwin), :]
        s_loc = lax.dot_general(qs, k, _NT, preferred_element_type=F32) + bias_ref[r - r0]
        s_ctx = lax.dot_general(qs, kc, _NT, preferred_element_type=F32)
        return qoff, koff, s_loc, s_ctx

    def probs(s_loc, s_ctx):
        m = jnp.maximum(jnp.max(s_loc, axis=-1, keepdims=True), jnp.max(s_ctx, axis=-1, keepdims=True))
        p_loc = jnp.exp(s_loc - m)
        p_ctx = jnp.exp(s_ctx - m)
        l = jnp.sum(p_loc, axis=-1, keepdims=True) + jnp.sum(p_ctx, axis=-1, keepdims=True)
        return p_loc.astype(BF16), p_ctx.astype(BF16), l

    def group(gi, carry):
        sc = [scores(gi * NA_GROUP + u) for u in range(NA_GROUP)]
        pr = [probs(s[2], s[3]) for s in sc]
        for (qoff, koff, _, _), (p_loc, p_ctx, l) in zip(sc, pr):
            v = v_ref[pl.ds(koff, kwin), :]
            o = (jnp.dot(p_loc, v, preferred_element_type=F32)
                 + jnp.dot(p_ctx, vc, preferred_element_type=F32)) / l
            o_ref[pl.ds(qoff, GRID_W), :] = _unstack_heads(o).astype(o_ref.dtype)
        return carry

    lax.fori_loop(0, rows // NA_GROUP, group, 0)


def na_attention(qkv, qkv_ctx, bias):
    b, l, _ = qkv.shape
    c = qkv_ctx.shape[1]
    hp = D_MODEL // LANES
    body = functools.partial(_na_body, rows=l // GRID_W)
    return pl.pallas_call(
        body,
        grid=(b, hp),
        in_specs=[pl.BlockSpec((None, l, LANES), lambda i, j: (i, 0, j)),
                  pl.BlockSpec((None, l, LANES), lambda i, j: (i, 0, hp + j)),
                  pl.BlockSpec((None, l, LANES), lambda i, j: (i, 0, 2 * hp + j)),
                  pl.BlockSpec((None, c, LANES), lambda i, j: (i, 0, hp + j)),
                  pl.BlockSpec((None, c, LANES), lambda i, j: (i, 0, 2 * hp + j)),
                  pl.BlockSpec((None,) + bias.shape[1:], lambda i, j: (j, 0, 0, 0))],
        out_specs=pl.BlockSpec((None, l, LANES), lambda i, j: (i, 0, j)),
        out_shape=jax.ShapeDtypeStruct((b, l, D_MODEL), BF16),
        compiler_params=_cparams("arbitrary", "arbitrary"),
        name="na_attention",
    )(qkv, qkv, qkv, qkv_ctx, qkv_ctx, bias)


def _ctx_attn_body(q_ref, k_ref, v_ref, o_ref):
    qs = _stack_heads(q_ref[...])
    s = lax.dot_general(qs, k_ref[...], _NT, preferred_element_type=F32)
    p = jnp.exp(s - jnp.max(s, axis=-1, keepdims=True))
    l = jnp.sum(p, axis=-1, keepdims=True)
    o = jnp.dot(p.astype(BF16), v_ref[...], preferred_element_type=F32) / l
    o_ref[...] = _unstack_heads(o).astype(o_ref.dtype)


def ctx_attention(qkv_ctx):
    b, c, _ = qkv_ctx.shape
    hp = D_MODEL // LANES
    return pl.pallas_call(
        _ctx_attn_body,
        grid=(b, hp),
        in_specs=[pl.BlockSpec((None, c, LANES), lambda i, j: (i, 0, j)),
                  pl.BlockSpec((None, c, LANES), lambda i, j: (i, 0, hp + j)),
                  pl.BlockSpec((None, c, LANES), lambda i, j: (i, 0, 2 * hp + j))],
        out_specs=pl.BlockSpec((None, c, LANES), lambda i, j: (i, 0, j)),
        out_shape=jax.ShapeDtypeStruct((b, c, D_MODEL), BF16),
        compiler_params=_cparams("arbitrary", "arbitrary"),
        name="ctx_attention",
    )(qkv_ctx, qkv_ctx, qkv_ctx)


POOL_HALO = 8


def _pool_body(prev_ref, cur_ref, next_ref, mod_ref, g_ref, wp_ref, ps_ref, o_ref, *, tl, seq, is_ctx):
    b = pl.program_id(0)
    j = pl.program_id(1)
    row = CTX_MOD_ROW if is_ctx else b
    g = g_ref[...]
    sh = _mod_chunk(mod_ref, row, 0)
    sc = _mod_chunk(mod_ref, row, 1)
    h = cur_ref[...]
    a_cur = _norm_mod(h, g, sh, sc)
    a_prev = _norm_mod(prev_ref[...], g, sh, sc) * (j > 0).astype(F32)
    a_next = _norm_mod(next_ref[...], g, sh, sc) * (j < seq // tl - 1).astype(F32)
    ext = jnp.concatenate([a_prev, a_cur, a_next], axis=0)
    t = j * tl + lax.broadcasted_iota(jnp.int32, (tl, 1), 0)
    outs = []
    for gi, w in enumerate(POOL_WINDOWS):
        sl = slice(gi * POOL_GROUP_DIM, (gi + 1) * POOL_GROUP_DIM)
        p = ext[:, sl]
        step = 1
        while step < w:
            n = p.shape[0]
            p = p[:n - step] + p[step:]
            step *= 2
        off = POOL_HALO - w // 2
        cnt = jnp.minimum(t + w // 2, seq) - jnp.maximum(t - w // 2, 0)
        pooled = p[off:off + tl] / cnt.astype(F32) - a_cur[:, sl]
        outs.append(jnp.dot(pooled.astype(BF16), wp_ref[gi], preferred_element_type=F32))
    y = jnp.concatenate(outs, axis=1) * ps_ref[...]
    o_ref[...] = h + _mod_chunk(mod_ref, row, 2) * y


def pool_mixer(h, mods, layer, g, w_pool, pool_scale, *, is_ctx, tl=512):
    b, seq, _ = h.shape
    tl = min(tl, seq)
    nh = tl // POOL_HALO
    last = seq // POOL_HALO - 1
    body = functools.partial(_pool_body, tl=tl, seq=seq, is_ctx=is_ctx)
    return pl.pallas_call(
        body,
        grid=(b, seq // tl),
        in_specs=[pl.BlockSpec((None, POOL_HALO, D_MODEL), lambda i, j: (i, jnp.maximum(j * nh - 1, 0), 0)),
                  pl.BlockSpec((None, tl, D_MODEL), lambda i, j: (i, j, 0)),
                  pl.BlockSpec((None, POOL_HALO, D_MODEL), lambda i, j: (i, jnp.minimum((j + 1) * nh, last), 0)),
                  _mod_spec(layer),
                  pl.BlockSpec((1, D_MODEL), lambda i, j: (0, 0)),
                  pl.BlockSpec(w_pool.shape, lambda i, j: (0, 0, 0)),
                  pl.BlockSpec((1, D_MODEL), lambda i, j: (0, 0))],
        out_specs=pl.BlockSpec((None, tl, D_MODEL), lambda i, j: (i, j, 0)),
        out_shape=jax.ShapeDtypeStruct(h.shape, F32),
        compiler_params=_cparams("arbitrary", "arbitrary"),
        name="pool_mixer",
    )(h, h, h, mods, g.reshape(1, D_MODEL), w_pool, pool_scale.reshape(1, D_MODEL))


CONV_HALO = 16


def block_diag_weights(w):
    nb = LANES // MLSTM_QKV_BLOCK
    wc = w.reshape(-1, nb, MLSTM_QKV_BLOCK, MLSTM_QKV_BLOCK)
    eye = jnp.eye(nb, dtype=w.dtype)
    bd = jnp.einsum("cnij,nm->cnimj", wc, eye)
    return bd.reshape(-1, LANES, LANES)


def fold_gate_weights(w_q, w_k, w_v, w_gates):
    ng = w_gates.shape[1]
    wg = w_gates.reshape(3, -1, MLSTM_QKV_BLOCK, ng)
    fold = lambda w, part: jnp.einsum("ncd,ndg->ncg", w, wg[part], precision=lax.Precision.HIGHEST).reshape(-1, ng)
    return fold(w_q, 0) + fold(w_k, 1), fold(w_v, 2)


def _ml_feat_body(prev_ref, cur_ref, next_ref, cw_ref, cb_ref, wq_ref, wk_ref, wkt_ref, wv_ref,
                  gxc_ref, gxm_ref, gxct_ref, gxmt_ref, bg_ref, bgt_ref,
                  q_ref, k_ref, kt_ref, v_ref, xc_ref, g_ref, gt_ref, ext_ref, *, tl, seq):
    j = pl.program_id(1)
    cur = cur_ref[...]
    ext_ref[0:CONV_HALO, :] = prev_ref[...].astype(F32) * (j > 0).astype(F32)
    ext_ref[CONV_HALO:CONV_HALO + tl, :] = cur.astype(F32)
    ext_ref[CONV_HALO + tl:, :] = next_ref[...].astype(F32) * (j < seq // tl - 1).astype(F32)
    left = MLSTM_CONV // 2
    xc = cb_ref[...]
    for tap in range(MLSTM_CONV):
        xc = xc + ext_ref[pl.ds(CONV_HALO - left + tap, tl), :] * cw_ref[tap:tap + 1, :]
    xc = _silu(xc)
    xcb = xc.astype(BF16)
    xc_ref[...] = xcb
    t = SCAN_CHUNK
    qscale = MLSTM_HEAD_DIM ** -0.5
    for c in range(MLSTM_INNER // LANES):
        sl = slice(c * LANES, (c + 1) * LANES)
        xs = xcb[:, sl]
        q = jnp.dot(xs, wq_ref[c], preferred_element_type=F32)
        k = jnp.dot(xs, wk_ref[c], preferred_element_type=F32)
        v = jnp.dot(cur[:, sl], wv_ref[c], preferred_element_type=F32)
        q_ref[:, sl] = (q * qscale).astype(BF16)
        k_ref[:, sl] = k.astype(BF16)
        v_ref[:, sl] = v.astype(BF16)
        for cc in range(tl // t):
            kt = lax.dot_general(wkt_ref[c], xs[cc * t:(cc + 1) * t], _NT, preferred_element_type=F32)
            kt_ref[cc, sl, :] = kt.astype(BF16)
    ng = g_ref.shape[1]
    g = (jnp.dot(xcb, gxc_ref[...], preferred_element_type=F32)
         + jnp.dot(cur, gxm_ref[...], preferred_element_type=F32))
    g_ref[...] = g[:, :ng] + bg_ref[...]
    gt = (lax.dot_general(gxct_ref[...], xcb, _NT, preferred_element_type=F32)
          + lax.dot_general(gxmt_ref[...], cur, _NT, preferred_element_type=F32)) + bgt_ref[...]
    for cc in range(tl // t):
        gt_ref[cc] = gt[:, cc * t:(cc + 1) * t]


def mlstm_features(up, conv_w, conv_b, wq_bd, wk_bd, wkt_bd, wv_bd, gxc, gxm, bg, *, tl=256):
    b, seq, _ = up.shape
    tl = min(tl, seq)
    t = SCAN_CHUNK
    nh = tl // CONV_HALO
    last = seq // CONV_HALO - 1
    ng = gxc.shape[1]
    inner = MLSTM_INNER
    body = functools.partial(_ml_feat_body, tl=tl, seq=seq)
    full = lambda a: pl.BlockSpec(a.shape, lambda i, j: (0,) * a.ndim)
    cw = conv_w
    cb = conv_b.reshape(1, inner)
    bgr = bg.reshape(1, ng)
    bgc = bg.reshape(ng, 1)
    pad = lambda w: jnp.pad(w, ((0, 0), (0, LANES - ng))).astype(BF16)
    gxc_p, gxm_p = pad(gxc), pad(gxm)
    gxc_t, gxm_t = gxc.T.astype(BF16), gxm.T.astype(BF16)
    return pl.pallas_call(
        body,
        grid=(b, seq // tl),
        in_specs=[pl.BlockSpec((None, CONV_HALO, inner), lambda i, j: (i, jnp.maximum(j * nh - 1, 0), 0)),
                  pl.BlockSpec((None, tl, inner), lambda i, j: (i, j, 0)),
                  pl.BlockSpec((None, CONV_HALO, inner), lambda i, j: (i, jnp.minimum((j + 1) * nh, last), 0)),
                  full(cw), full(cb), full(wq_bd), full(wk_bd), full(wkt_bd), full(wv_bd),
                  full(gxc_p), full(gxm_p), full(gxc_t), full(gxm_t), full(bgr), full(bgc)],
        out_specs=[pl.BlockSpec((None, tl, inner), lambda i, j: (i, j, 0)),
                   pl.BlockSpec((None, tl, inner), lambda i, j: (i, j, 0)),
                   pl.BlockSpec((None, tl // t, inner, t), lambda i, j: (i, j, 0, 0)),
                   pl.BlockSpec((None, tl, inner), lambda i, j: (i, j, 0)),
                   pl.BlockSpec((None, tl, inner), lambda i, j: (i, j, 0)),
                   pl.BlockSpec((None, tl, ng), lambda i, j: (i, j, 0)),
                   pl.BlockSpec((None, tl // t, ng, t), lambda i, j: (i, j, 0, 0))],
        out_shape=[jax.ShapeDtypeStruct((b, seq, inner), BF16),
                   jax.ShapeDtypeStruct((b, seq, inner), BF16),
                   jax.ShapeDtypeStruct((b, seq // t, inner, t), BF16),
                   jax.ShapeDtypeStruct((b, seq, inner), BF16),
                   jax.ShapeDtypeStruct((b, seq, inner), BF16),
                   jax.ShapeDtypeStruct((b, seq, ng), F32),
                   jax.ShapeDtypeStruct((b, seq // t, ng, t), F32)],
        scratch_shapes=[pltpu.VMEM((tl + 2 * CONV_HALO, inner), F32)],
        compiler_params=_cparams("arbitrary", "arbitrary"),
        name="mlstm_features",
    )(up, up, up, cw, cb, wq_bd, wk_bd, wkt_bd, wv_bd, gxc_p, gxm_p, gxc_t, gxm_t, bgr, bgc)


def _log_sigmoid(x):
    return jnp.minimum(x, 0.0) - jnp.log1p(jnp.exp(-jnp.abs(x)))


def _scan_body(*refs, rev, nchunk, nblk, has_init):
    if has_init:
        q_ref, k_ref, kt_ref, v_ref, g_ref, gt_ref, c0_ref, m0_ref, h_ref, cf_ref, mf_ref, c_sc, m_sc = refs
    else:
        q_ref, k_ref, kt_ref, v_ref, g_ref, gt_ref, h_ref, cf_ref, mf_ref, c_sc, m_sc = refs
    hd = pl.program_id(1)
    j = pl.program_id(2)
    t = SCAN_CHUNK
    dh = MLSTM_HEAD_DIM

    @pl.when(j == 0)
    def _():
        if has_init:
            c_sc[...] = c0_ref[...]
            m_sc[...] = m0_ref[...]
        else:
            c_sc[...] = jnp.zeros_like(c_sc)
            m_sc[...] = jnp.zeros_like(m_sc)

    ci = (2 if rev else 0) * MLSTM_HEADS + hd
    cf = (3 if rev else 1) * MLSTM_HEADS + hd
    ng = g_ref.shape[1]
    lane = lax.broadcasted_iota(jnp.int32, (t, ng), 1)
    r_io = lax.broadcasted_iota(jnp.int32, (t, t), 0)
    c_io = lax.broadcasted_iota(jnp.int32, (t, t), 1)
    seen = (c_io >= r_io) if rev else (c_io <= r_io)
    seen_t = (r_io >= c_io) if rev else (r_io <= c_io)
    seen_f = seen.astype(F32)
    seen_tf = seen_t.astype(F32)

    order = range(nchunk - 1, -1, -1) if rev else range(nchunk)
    for cc in order:
        rows = slice(cc * t, (cc + 1) * t)
        q = q_ref[rows, :]
        kt = kt_ref[cc]
        v = v_ref[rows, :]
        g = g_ref[rows, :]
        i_col = jnp.sum(jnp.where(lane == ci, g, 0.0), axis=1, keepdims=True)
        f_col = jnp.sum(jnp.where(lane == cf, g, 0.0), axis=1, keepdims=True)
        i_row = gt_ref[cc, pl.ds(ci, 1), :]
        f_row = gt_ref[cc, pl.ds(cf, 1), :]
        lf_col = _log_sigmoid(f_col)
        lf_row = _log_sigmoid(f_row)
        b_col = jnp.sum(seen_f * lf_row, axis=1, keepdims=True)
        b_row = jnp.sum(seen_tf * lf_col, axis=0, keepdims=True)
        m_prev = m_sc[0:1, 0:1]
        n_row = c_sc[dh:dh + 1, :]
        dmat = jnp.where(seen, b_col - b_row + i_row, NEG_BIG)
        inter = b_col + m_prev
        m_t = jnp.maximum(inter, jnp.max(dmat, axis=1, keepdims=True))
        a = jnp.dot(q, kt, preferred_element_type=F32) * jnp.exp(dmat - m_t)
        w_int = jnp.exp(inter - m_t)
        cb = c_sc[0:dh, :].astype(BF16)
        num = (jnp.dot(a.astype(BF16), v, preferred_element_type=F32)
               + jnp.dot(q, cb, preferred_element_type=F32) * w_int)
        den = (jnp.sum(a, axis=1, keepdims=True)
               + w_int * jnp.sum(q.astype(F32) * n_row, axis=1, keepdims=True))
        hc = num / jnp.maximum(jnp.abs(den), jnp.exp(-m_t))
        h_ref[rows, :] = hc.astype(h_ref.dtype)
        b_end = jnp.sum(lf_row, axis=1, keepdims=True)
        g_row = b_end - b_row + i_row
        m_new = jnp.maximum(b_end + m_prev, jnp.max(g_row, axis=1, keepdims=True))
        decay = jnp.exp(b_end + m_prev - m_new)
        kw = (kt.astype(F32) * jnp.exp(g_row - m_new)).astype(BF16)
        w_col = jnp.exp(b_end - b_col + i_col - m_new)
        c_sc[0:dh, :] = decay * c_sc[0:dh, :] + jnp.dot(kw, v, preferred_element_type=F32)
        c_sc[dh:dh + 1, :] = decay * n_row + jnp.sum(k_ref[rows, :].astype(F32) * w_col, axis=0, keepdims=True)
        m_sc[...] = jnp.broadcast_to(m_new, m_sc.shape)

    @pl.when(j == nblk - 1)
    def _():
        cf_ref[...] = c_sc[...]
        mf_ref[...] = m_sc[...]


def mlstm_scan(q, k, kt, v, g, gt, state, *, rev, tb=1024):
    b, seq, inner = q.shape
    t = SCAN_CHUNK
    tb = min(tb, seq)
    nblk = seq // tb
    nchunk = tb // t
    dh = MLSTM_HEAD_DIM
    ng = g.shape[2]
    has_init = state is not None
    blk = (lambda j: nblk - 1 - j) if rev else (lambda j: j)
    body = functools.partial(_scan_body, rev=rev, nchunk=nchunk, nblk=nblk, has_init=has_init)
    tok_spec = pl.BlockSpec((None, tb, dh), lambda i, h, j: (i, blk(j), h))
    in_specs = [tok_spec, tok_spec,
                pl.BlockSpec((None, nchunk, dh, t), lambda i, h, j: (i, blk(j), h, 0)),
                tok_spec,
                pl.BlockSpec((None, tb, ng), lambda i, h, j: (i, blk(j), 0)),
                pl.BlockSpec((None, nchunk, ng, t), lambda i, h, j: (i, blk(j), 0, 0))]
    args = [q, k, kt, v, g, gt]
    st_spec_c = pl.BlockSpec((None, None, SCAN_STATE_ROWS, dh), lambda i, h, j: (i, h, 0, 0))
    st_spec_m = pl.BlockSpec((None, None, 8, LANES), lambda i, h, j: (i, h, 0, 0))
    if has_init:
        in_specs += [st_spec_c, st_spec_m]
        args += list(state)
    return pl.pallas_call(
        body,
        grid=(b, MLSTM_HEADS, nblk),
        in_specs=in_specs,
        out_specs=[tok_spec, st_spec_c, st_spec_m],
        out_shape=[jax.ShapeDtypeStruct((b, seq, inner), BF16),
                   jax.ShapeDtypeStruct((b, MLSTM_HEADS, SCAN_STATE_ROWS, dh), F32),
                   jax.ShapeDtypeStruct((b, MLSTM_HEADS, 8, LANES), F32)],
        scratch_shapes=[pltpu.VMEM((SCAN_STATE_ROWS, dh), F32), pltpu.VMEM((8, LANES), F32)],
        compiler_params=_cparams("arbitrary", "arbitrary", "arbitrary"),
        name="mlstm_scan_bwd" if rev else "mlstm_scan_fwd",
    )(*args)


SCAN_HEADS = 2


def _scan2_body(*refs, rev, nchunk, nblk, has_init):
    if has_init:
        q_ref, k_ref, kt_ref, v_ref, g_ref, gt_ref, c0_ref, m0_ref, h_ref, cf_ref, mf_ref, c_sc, m_sc = refs
    else:
        q_ref, k_ref, kt_ref, v_ref, g_ref, gt_ref, h_ref, cf_ref, mf_ref, c_sc, m_sc = refs
    hp = pl.program_id(1)
    j = pl.program_id(2)
    t = SCAN_CHUNK
    dh = MLSTM_HEAD_DIM

    @pl.when(j == 0)
    def _():
        if has_init:
            c_sc[...] = c0_ref[...]
            m_sc[...] = m0_ref[...]
        else:
            c_sc[...] = jnp.zeros_like(c_sc)
            m_sc[...] = jnp.zeros_like(m_sc)

    ng = g_ref.shape[1]
    lane = lax.broadcasted_iota(jnp.int32, (t, ng), 1)
    r_io = lax.broadcasted_iota(jnp.int32, (t, t), 0)
    c_io = lax.broadcasted_iota(jnp.int32, (t, t), 1)
    seen = (c_io >= r_io) if rev else (c_io <= r_io)
    seen_t = (r_io >= c_io) if rev else (r_io <= c_io)
    seen_f = seen.astype(F32)
    seen_tf = seen_t.astype(F32)

    def decays(cc, u):
        hd = hp * SCAN_HEADS + u
        ci = (2 if rev else 0) * MLSTM_HEADS + hd
        cf = (3 if rev else 1) * MLSTM_HEADS + hd
        g = g_ref[cc * t:(cc + 1) * t, :]
        i_col = jnp.sum(jnp.where(lane == ci, g, 0.0), axis=1, keepdims=True)
        f_col = jnp.sum(jnp.where(lane == cf, g, 0.0), axis=1, keepdims=True)
        i_row = gt_ref[cc, pl.ds(ci, 1), :]
        f_row = gt_ref[cc, pl.ds(cf, 1), :]
        lf_col = _log_sigmoid(f_col)
        lf_row = _log_sigmoid(f_row)
        b_col = jnp.sum(seen_f * lf_row, axis=1, keepdims=True)
        b_row = jnp.sum(seen_tf * lf_col, axis=0, keepdims=True)
        m_prev = m_sc[u, 0:1, 0:1]
        dmat = jnp.where(seen, b_col - b_row + i_row, NEG_BIG)
        inter = b_col + m_prev
        m_t = jnp.maximum(inter, jnp.max(dmat, axis=1, keepdims=True))
        b_end = jnp.sum(lf_row, axis=1, keepdims=True)
        g_row = b_end - b_row + i_row
        m_new = jnp.maximum(b_end + m_prev, jnp.max(g_row, axis=1, keepdims=True))
        return dict(dexp=jnp.exp(dmat - m_t), w_int=jnp.exp(inter - m_t), floor=jnp.exp(-m_t),
                    decay=jnp.exp(b_end + m_prev - m_new), w_row=jnp.exp(g_row - m_new),
                    w_col=jnp.exp(b_end - b_col + i_col - m_new), m_new=m_new)

    def readout(cc, u, d):
        rows = slice(cc * t, (cc + 1) * t)
        cols = slice(u * dh, (u + 1) * dh)
        q = q_ref[rows, cols]
        a = jnp.dot(q, kt_ref[cc, cols, :], preferred_element_type=F32) * d["dexp"]
        n_row = c_sc[u, dh:dh + 1, :]
        cb = c_sc[u, 0:dh, :].astype(BF16)
        num = (jnp.dot(a.astype(BF16), v_ref[rows, cols], preferred_element_type=F32)
               + jnp.dot(q, cb, preferred_element_type=F32) * d["w_int"])
        den = (jnp.sum(a, axis=1, keepdims=True)
               + d["w_int"] * jnp.sum(q.astype(F32) * n_row, axis=1, keepdims=True))
        hc = num / jnp.maximum(jnp.abs(den), d["floor"])
        h_ref[rows, cols] = hc.astype(h_ref.dtype)

    def update(cc, u, d):
        rows = slice(cc * t, (cc + 1) * t)
        cols = slice(u * dh, (u + 1) * dh)
        kw = (kt_ref[cc, cols, :].astype(F32) * d["w_row"]).astype(BF16)
        n_row = c_sc[u, dh:dh + 1, :]
        c_sc[u, 0:dh, :] = (d["decay"] * c_sc[u, 0:dh, :]
                            + jnp.dot(kw, v_ref[rows, cols], preferred_element_type=F32))
        c_sc[u, dh:dh + 1, :] = d["decay"] * n_row + jnp.sum(k_ref[rows, cols].astype(F32) * d["w_col"],
                                                              axis=0, keepdims=True)
        m_sc[u] = jnp.broadcast_to(d["m_new"], m_sc.shape[1:])

    order = range(nchunk - 1, -1, -1) if rev else range(nchunk)
    for cc in order:
        ds = [decays(cc, u) for u in range(SCAN_HEADS)]
        for u in range(SCAN_HEADS):
            readout(cc, u, ds[u])
        for u in range(SCAN_HEADS):
            update(cc, u, ds[u])

    @pl.when(j == nblk - 1)
    def _():
        cf_ref[...] = c_sc[...]
        mf_ref[...] = m_sc[...]


def mlstm_scan2(q, k, kt, v, g, gt, state, *, rev, tb=512):
    b, seq, inner = q.shape
    t = SCAN_CHUNK
    tb = min(tb, seq)
    nblk = seq // tb
    nchunk = tb // t
    dh = MLSTM_HEAD_DIM
    hw = SCAN_HEADS * dh
    ng = g.shape[2]
    has_init = state is not None
    blk = (lambda j: nblk - 1 - j) if rev else (lambda j: j)
    body = functools.partial(_scan2_body, rev=rev, nchunk=nchunk, nblk=nblk, has_init=has_init)
    tok_spec = pl.BlockSpec((None, tb, hw), lambda i, h, j: (i, blk(j), h))
    in_specs = [tok_spec, tok_spec,
                pl.BlockSpec((None, nchunk, hw, t), lambda i, h, j: (i, blk(j), h, 0)),
                tok_spec,
                pl.BlockSpec((None, tb, ng), lambda i, h, j: (i, blk(j), 0)),
                pl.BlockSpec((None, nchunk, ng, t), lambda i, h, j: (i, blk(j), 0, 0))]
    args = [q, k, kt, v, g, gt]
    st_spec_c = pl.BlockSpec((None, SCAN_HEADS, SCAN_STATE_ROWS, dh), lambda i, h, j: (i, h, 0, 0))
    st_spec_m = pl.BlockSpec((None, SCAN_HEADS, 8, LANES), lambda i, h, j: (i, h, 0, 0))
    if has_init:
        in_specs += [st_spec_c, st_spec_m]
        args += list(state)
    return pl.pallas_call(
        body,
        grid=(b, MLSTM_HEADS // SCAN_HEADS, nblk),
        in_specs=in_specs,
        out_specs=[tok_spec, st_spec_c, st_spec_m],
        out_shape=[jax.ShapeDtypeStruct((b, seq, inner), BF16),
                   jax.ShapeDtypeStruct((b, MLSTM_HEADS, SCAN_STATE_ROWS, dh), F32),
                   jax.ShapeDtypeStruct((b, MLSTM_HEADS, 8, LANES), F32)],
        scratch_shapes=[pltpu.VMEM((SCAN_HEADS, SCAN_STATE_ROWS, dh), F32), pltpu.VMEM((SCAN_HEADS, 8, LANES), F32)],
        compiler_params=_cparams("arbitrary", "arbitrary", "arbitrary"),
        name="mlstm_scan_bwd" if rev else "mlstm_scan_fwd",
    )(*args)


def _ml_out_body(hf_ref, hb_ref, xc_ref, z_ref, h_ref, mod_ref, gn_ref, sk_ref, w_ref, o_ref, *, tm, rows_per_batch):
    i = pl.program_id(0)
    row = (i * tm) // rows_per_batch if rows_per_batch else CTX_MOD_ROW
    hs = hf_ref[...].astype(F32) + hb_ref[...].astype(F32)
    parts = []
    for hd in range(MLSTM_HEADS):
        x = hs[:, hd * MLSTM_HEAD_DIM:(hd + 1) * MLSTM_HEAD_DIM]
        mu = jnp.mean(x, axis=-1, keepdims=True)
        xm = x - mu
        var = jnp.mean(xm * xm, axis=-1, keepdims=True)
        parts.append(xm * lax.rsqrt(var + NORM_EPS))
    hn = jnp.concatenate(parts, axis=1) * gn_ref[...]
    y = (hn + sk_ref[...] * xc_ref[...].astype(F32)) * _silu(z_ref[...].astype(F32))
    y = jnp.dot(y.astype(BF16), w_ref[...], preferred_element_type=F32)
    o_ref[...] = h_ref[...] + _mod_chunk(mod_ref, row, 2) * y


def mlstm_output(hf, hb, xc, up, h, mods, layer, gn_w, skip, w_down, *, rows_per_batch, tm=256):
    m = h.shape[0]
    tm = min(tm, m)
    inner = MLSTM_INNER
    body = functools.partial(_ml_out_body, tm=tm, rows_per_batch=rows_per_batch)
    row_spec = pl.BlockSpec((tm, inner), lambda i: (i, 0))
    return pl.pallas_call(
        body,
        grid=(m // tm,),
        in_specs=[row_spec, row_spec, row_spec,
                  pl.BlockSpec((tm, inner), lambda i: (i, 1)),
                  pl.BlockSpec((tm, D_MODEL), lambda i: (i, 0)),
                  _mod_spec(layer),
                  pl.BlockSpec((1, inner), lambda i: (0, 0)),
                  pl.BlockSpec((1, inner), lambda i: (0, 0)),
                  _resident((inner, D_MODEL), lambda i: (0, 0))],
        out_specs=pl.BlockSpec((tm, D_MODEL), lambda i: (i, 0)),
        out_shape=jax.ShapeDtypeStruct((m, D_MODEL), F32),
        compiler_params=_cparams("arbitrary"),
        name="mlstm_output",
    )(hf, hb, xc, up, h, mods, gn_w.reshape(1, inner), skip.reshape(1, inner), w_down)


def _router_body(xl_ref, xc_ref, mod_ref, g_ref, wr_ref, a_ref, lg_ref, *, tm, rows_per_batch, n_lat):
    i = pl.program_id(0)
    is_lat = i < n_lat
    row = jnp.where(is_lat, (i * tm) // rows_per_batch, CTX_MOD_ROW)
    x = jnp.where(is_lat, xl_ref[...], xc_ref[...])
    a = _norm_mod(x, g_ref[...], _mod_chunk(mod_ref, row, 3), _mod_chunk(mod_ref, row, 4))
    a_ref[...] = a
    lg_ref[...] = jnp.dot(a, wr_ref[...], preferred_element_type=F32, precision=lax.Precision.HIGHEST)


def moe_router(x_lat, x_ctx, mods, layer, g, wr_pad, *, rows_per_batch, tm=512):
    n_lat = x_lat.shape[0] // tm
    n_ctx = 0 if x_ctx is None else x_ctx.shape[0] // tm
    n = (n_lat + n_ctx) * tm
    if x_ctx is None:
        x_ctx = x_lat
    body = functools.partial(_router_body, tm=tm, rows_per_batch=rows_per_batch, n_lat=n_lat)
    return pl.pallas_call(
        body,
        grid=(n_lat + n_ctx,),
        in_specs=[pl.BlockSpec((tm, D_MODEL), lambda i: (jnp.minimum(i, n_lat - 1), 0)),
                  pl.BlockSpec((tm, D_MODEL), lambda i: (jnp.maximum(i - n_lat, 0), 0)),
                  _mod_spec(layer),
                  pl.BlockSpec((1, D_MODEL), lambda i: (0, 0)),
                  pl.BlockSpec((D_MODEL, LANES), lambda i: (0, 0))],
        out_specs=[pl.BlockSpec((tm, D_MODEL), lambda i: (i, 0)),
                   pl.BlockSpec((tm, LANES), lambda i: (i, 0))],
        out_shape=[jax.ShapeDtypeStruct((n, D_MODEL), F32),
                   jax.ShapeDtypeStruct((n, LANES), F32)],
        compiler_params=_cparams("arbitrary"),
        name="moe_router",
    )(x_lat, x_ctx, mods, g.reshape(1, D_MODEL), wr_pad)


def moe_route(logits, tm):
    n = logits.shape[0]
    a_tot = n * TOP_K
    top_v, top_e = lax.top_k(logits[:, :N_EXPERTS], TOP_K)
    gates = jax.nn.softmax(top_v, axis=-1)
    e_flat = top_e.reshape(a_tot).astype(jnp.int32)
    order = jnp.argsort(e_flat).astype(jnp.int32)
    counts = jnp.sum((e_flat[:, None] == jnp.arange(N_EXPERTS, dtype=jnp.int32)[None, :]).astype(jnp.int32), axis=0)
    starts = jnp.cumsum(counts) - counts
    padded = (counts + tm - 1) // tm * tm
    pend = jnp.cumsum(padded)
    pstarts = pend - padded
    n_blocks = (a_tot + N_EXPERTS * (tm - 1)) // tm
    blk_row = jnp.arange(n_blocks, dtype=jnp.int32) * tm
    block_e = jnp.minimum(jnp.searchsorted(pend, blk_row, side="right"), N_EXPERTS - 1).astype(jnp.int32)
    into = blk_row - pstarts[block_e]
    n_valid = jnp.clip(counts[block_e] - into, 0, tm).astype(jnp.int32)
    base = jnp.clip(starts[block_e] + into, 0, a_tot)
    order_pad = jnp.concatenate([order, jnp.zeros((tm,), jnp.int32)])
    pair = jax.vmap(lambda s: lax.dynamic_slice(order_pad, (s,), (tm,)))(base)
    valid = jnp.arange(tm, dtype=jnp.int32)[None, :] < n_valid[:, None]
    src = jnp.where(valid, pair // TOP_K, 0)
    dst = jnp.where(valid, (pair % TOP_K) * n + pair // TOP_K, 0)
    n_used = (pend[-1] // tm).astype(jnp.int32).reshape(1)
    return gates, block_e, n_used, n_valid, src.reshape(n_blocks, 1, tm), dst.reshape(n_blocks, 1, tm)


ROW_UNROLL = 8


def _for_rows(n, fn):
    full = n // ROW_UNROLL

    def group(c, carry):
        for u in range(ROW_UNROLL):
            fn(c * ROW_UNROLL + u)
        return carry
    lax.fori_loop(0, full, group, 0)

    def single(r, carry):
        fn(r)
        return carry
    lax.fori_loop(full * ROW_UNROLL, n, single, 0)


def _moe_body(be_ref, nu_ref, nv_ref, src_ref, srcn_ref, dst_ref, a_hbm, w1_ref, w3_ref, w2_ref, y_hbm,
              xf_ref, xb_ref, acc_ref, gsem, ssem, *, nf):
    i = pl.program_id(0)
    f = pl.program_id(1)
    n_used = nu_ref[0]
    slot = i % 2

    def gather_copy(s, r, tok):
        return pltpu.make_async_copy(a_hbm.at[pl.ds(tok, 1), :], xf_ref.at[s, pl.ds(r, 1), :], gsem.at[s])

    def scatter_copy(s, r, row):
        return pltpu.make_async_copy(acc_ref.at[s, pl.ds(r, 1), :], y_hbm.at[pl.ds(row, 1), :], ssem.at[s])

    def start_gather(s, idx_ref, n):
        _for_rows(n, lambda r: gather_copy(s, r, idx_ref[0, r]).start())

    def wait_gather(s, n):
        _for_rows(n, lambda r: gather_copy(s, r, 0).wait())

    def wait_scatter(s, n):
        _for_rows(n, lambda r: scatter_copy(s, r, 0).wait())

    @pl.when(i < n_used)
    def _():
        @pl.when(f == 0)
        def _():
            @pl.when(i == 0)
            def _():
                xf_ref[...] = jnp.zeros_like(xf_ref)
                start_gather(0, src_ref, nv_ref[0])

            wait_gather(slot, nv_ref[i])
            xb_ref[...] = xf_ref[slot].astype(BF16)

            @pl.when(i + 1 < n_used)
            def _():
                start_gather(1 - slot, srcn_ref, nv_ref[i + 1])

        x = xb_ref[...]
        u = jnp.dot(x, w1_ref[...], preferred_element_type=F32)
        v = jnp.dot(x, w3_ref[...], preferred_element_type=F32)
        p = (_silu(u) * v).astype(BF16)
        y = jnp.dot(p, w2_ref[...], preferred_element_type=F32)

        @pl.when(f == 0)
        def _():
            acc_ref[slot] = y

        @pl.when(f > 0)
        def _():
            acc_ref[slot] += y

        @pl.when(f == nf - 1)
        def _():
            _for_rows(nv_ref[i], lambda r: scatter_copy(slot, r, dst_ref[0, r]).start())

            @pl.when(i > 0)
            def _():
                wait_scatter(1 - slot, nv_ref[i - 1])

            @pl.when(i == n_used - 1)
            def _():
                wait_scatter(slot, nv_ref[i])


def moe_experts(a, block_e, n_used, n_valid, src, dst, w1, w3, w2, *, tf=1792):
    n = a.shape[0]
    n_blocks, _, tm = src.shape
    f_dim = w1.shape[2]
    nf = f_dim // tf
    body = functools.partial(_moe_body, nf=nf)

    def wmap(kind):
        def index_map(i, f, be, nu, nv):
            live = i < nu[0]
            ff = jnp.where(live, f, nf - 1)
            ii = jnp.where(live, i, nu[0] - 1)
            return (be[ii], 0, ff) if kind == "up" else (be[ii], ff, 0)
        return index_map

    idx_spec = pl.BlockSpec((None, 1, tm), lambda i, f, be, nu, nv: (i, 0, 0), memory_space=pltpu.SMEM)
    next_spec = pl.BlockSpec((None, 1, tm), lambda i, f, be, nu, nv: (jnp.minimum(i + 1, n_blocks - 1), 0, 0),
                             memory_space=pltpu.SMEM)
    grid_spec = pltpu.PrefetchScalarGridSpec(
        num_scalar_prefetch=3,
        grid=(n_blocks, nf),
        in_specs=[idx_spec, next_spec, idx_spec,
                  pl.BlockSpec(memory_space=pl.ANY),
                  pl.BlockSpec((None, D_MODEL, tf), wmap("up")),
                  pl.BlockSpec((None, D_MODEL, tf), wmap("up")),
                  pl.BlockSpec((None, tf, D_MODEL), wmap("down"))],
        out_specs=pl.BlockSpec(memory_space=pl.ANY),
        scratch_shapes=[pltpu.VMEM((2, tm, D_MODEL), F32), pltpu.VMEM((tm, D_MODEL), BF16),
                        pltpu.VMEM((2, tm, D_MODEL), F32),
                        pltpu.SemaphoreType.DMA((2,)), pltpu.SemaphoreType.DMA((2,))],
    )
    return pl.pallas_call(
        body,
        grid_spec=grid_spec,
        out_shape=jax.ShapeDtypeStruct((TOP_K * n, D_MODEL), F32),
        compiler_params=_cparams("arbitrary", "arbitrary"),
        name="moe_experts",
    )(block_e, n_used, n_valid, src, src, dst, a, w1, w3, w2)


def _combine_body(h_ref, y0_ref, y1_ref, gt_ref, mod_ref, fg_ref, o_ref, *, tm, rows_per_batch, final):
    i = pl.program_id(0)
    row = (i * tm) // rows_per_batch if rows_per_batch else CTX_MOD_ROW
    gt = gt_ref[...]
    f = y0_ref[...] * gt[:, 0:1] + y1_ref[...] * gt[:, 1:2]
    out = h_ref[...] + _mod_chunk(mod_ref, row, 5) * f
    if final:
        ms = jnp.mean(out * out, axis=-1, keepdims=True)
        out = out * lax.rsqrt(ms + NORM_EPS) * fg_ref[...]
    o_ref[...] = out


def moe_combine(h, y, gates, mods, layer, final_g, *, n_tok, row_off, rows_per_batch, final, tm=512):
    m = h.shape[0]
    tm = min(tm, m)
    o0 = row_off // tm
    o1 = (n_tok + row_off) // tm
    body = functools.partial(_combine_body, tm=tm, rows_per_batch=rows_per_batch, final=final)
    return pl.pallas_call(
        body,
        grid=(m // tm,),
        in_specs=[pl.BlockSpec((tm, D_MODEL), lambda i: (i, 0)),
                  pl.BlockSpec((tm, D_MODEL), lambda i: (o0 + i, 0)),
                  pl.BlockSpec((tm, D_MODEL), lambda i: (o1 + i, 0)),
                  pl.BlockSpec((tm, TOP_K), lambda i: (o0 + i, 0)),
                  _mod_spec(layer),
                  pl.BlockSpec((1, D_MODEL), lambda i: (0, 0))],
        out_specs=pl.BlockSpec((tm, D_MODEL), lambda i: (i, 0)),
        out_shape=jax.ShapeDtypeStruct((m, D_MODEL), F32),
        compiler_params=_cparams("arbitrary"),
        name="moe_combine",
    )(h, y, y, gates, mods, final_g.reshape(1, D_MODEL))


SORT_TOKENS = 512
ROW_GROUP = 16
EXPERT_ROWS = 16
SORT_SLOTS = -(-(TOP_K * SORT_TOKENS + N_EXPERTS * (ROW_GROUP - 1)) // LANES) * LANES
GROUPS_PER_BLOCK = MOE_ROWS // ROW_GROUP
META_SLOT0, META_SLOT1, META_GATE0, META_GATE1 = 0, 1, 2, 3


def _sort_body(xl_ref, xc_ref, mod_ref, g_ref, wrt_ref, earlier_ref, as_ref, meta_ref, cnt_ref, *,
               tm, rows_per_batch, n_lat):
    i = pl.program_id(0)
    is_lat = i < n_lat
    row = jnp.where(is_lat, (i * tm) // rows_per_batch, CTX_MOD_ROW)
    x = jnp.where(is_lat, xl_ref[...], xc_ref[...])
    a = _norm_mod(x, g_ref[...], _mod_chunk(mod_ref, row, 3), _mod_chunk(mod_ref, row, 4))
    e_io = lax.broadcasted_iota(jnp.int32, (EXPERT_ROWS, tm), 0)
    ab = a.astype(BF16)
    a_rem = (a - ab.astype(F32)).astype(BF16)
    wrt = wrt_ref[...]
    l_head = lax.dot_general(wrt, ab, _NT, preferred_element_type=F32)
    lt = (l_head[:EXPERT_ROWS] + l_head[EXPERT_ROWS:]
          + lax.dot_general(wrt[:EXPERT_ROWS], a_rem, _NT, preferred_element_type=F32))
    lt = jnp.where(e_io < N_EXPERTS, lt, -jnp.inf)
    m0 = jnp.max(lt, axis=0, keepdims=True)
    e0 = jnp.min(jnp.where(lt == m0, e_io, EXPERT_ROWS), axis=0, keepdims=True)
    oh0 = e_io == e0
    lt1 = jnp.where(oh0, -jnp.inf, lt)
    m1 = jnp.max(lt1, axis=0, keepdims=True)
    e1 = jnp.min(jnp.where(lt1 == m1, e_io, EXPERT_ROWS), axis=0, keepdims=True)
    oh1 = e_io == e1
    ex = jnp.exp(m1 - m0)
    gate0 = 1.0 / (1.0 + ex)
    gate1 = ex / (1.0 + ex)
    oh = jnp.where(oh0, 1.0, jnp.where(oh1, 1.0, 0.0))
    rank = jnp.dot(oh.astype(BF16), earlier_ref[...], preferred_element_type=F32)
    cnt = jnp.sum(oh, axis=1, keepdims=True)
    padded = jnp.floor((cnt + (ROW_GROUP - 1)) * (1.0 / ROW_GROUP)) * ROW_GROUP
    r8 = lax.broadcasted_iota(jnp.int32, (EXPERT_ROWS, EXPERT_ROWS), 0)
    c8 = lax.broadcasted_iota(jnp.int32, (EXPERT_ROWS, EXPERT_ROWS), 1)
    padded_row = jnp.sum(jnp.where(r8 == c8, padded, 0.0), axis=0, keepdims=True)
    start = jnp.sum(jnp.where(c8 < r8, padded_row, 0.0), axis=1, keepdims=True)
    slot0 = jnp.sum(jnp.where(oh0, start + rank, 0.0), axis=0, keepdims=True)
    slot1 = jnp.sum(jnp.where(oh1, start + rank, 0.0), axis=0, keepdims=True)
    j_io = lax.broadcasted_iota(jnp.int32, (SORT_SLOTS, tm), 0).astype(F32)
    perm = jnp.where(j_io == slot0, 1.0, jnp.where(j_io == slot1, 1.0, 0.0)).astype(BF16)
    as_ref[...] = jnp.dot(perm, ab, preferred_element_type=F32).astype(BF16)
    rows = jnp.concatenate([slot0, slot1, gate0, gate1, jnp.zeros((LANES - 4, tm), F32)], axis=0)
    meta_ref[...] = rows.T
    cnt_ref[...] = jnp.concatenate([jnp.broadcast_to(padded, (EXPERT_ROWS, LANES)),
                                    jnp.broadcast_to(start, (EXPERT_ROWS, LANES))], axis=0)


def moe_sort(x_lat, x_ctx, mods, layer, g, wrt, *, rows_per_batch):
    tm = SORT_TOKENS
    n_lat = x_lat.shape[0] // tm
    n_ctx = 0 if x_ctx is None else x_ctx.shape[0] // tm
    nt = n_lat + n_ctx
    if x_ctx is None:
        x_ctx = x_lat
    w_head = wrt.astype(BF16)
    w_split = jnp.concatenate([w_head, (wrt - w_head.astype(F32)).astype(BF16)], axis=0)
    tok = jnp.arange(tm)
    earlier = (tok[:, None] < tok[None, :]).astype(BF16)
    body = functools.partial(_sort_body, tm=tm, rows_per_batch=rows_per_batch, n_lat=n_lat)
    return pl.pallas_call(
        body,
        grid=(nt,),
        in_specs=[pl.BlockSpec((tm, D_MODEL), lambda i: (jnp.minimum(i, n_lat - 1), 0)),
                  pl.BlockSpec((tm, D_MODEL), lambda i: (jnp.maximum(i - n_lat, 0), 0)),
                  _mod_spec(layer),
                  pl.BlockSpec((1, D_MODEL), lambda i: (0, 0)),
                  pl.BlockSpec((2 * EXPERT_ROWS, D_MODEL), lambda i: (0, 0)),
                  pl.BlockSpec((tm, tm), lambda i: (0, 0))],
        out_specs=[pl.BlockSpec((SORT_SLOTS, D_MODEL), lambda i: (i, 0)),
                   pl.BlockSpec((tm, LANES), lambda i: (i, 0)),
                   pl.BlockSpec((None, 2 * EXPERT_ROWS, LANES), lambda i: (i, 0, 0))],
        out_shape=[jax.ShapeDtypeStruct((nt * SORT_SLOTS, D_MODEL), BF16),
                   jax.ShapeDtypeStruct((nt * tm, LANES), F32),
                   jax.ShapeDtypeStruct((nt, 2 * EXPERT_ROWS, LANES), F32)],
        compiler_params=_cparams("arbitrary"),
        name="moe_sort",
    )(x_lat, x_ctx, mods, g.reshape(1, D_MODEL), w_split, earlier)


def moe_group_table(cnt):
    nt = cnt.shape[0]
    padded = cnt[:, :N_EXPERTS, 0].astype(jnp.int32)
    start = cnt[:, EXPERT_ROWS:EXPERT_ROWS + N_EXPERTS, 0].astype(jnp.int32)
    groups = padded // ROW_GROUP
    cum = jnp.cumsum(groups, axis=0)
    tot = cum[-1]
    blocks = (tot + GROUPS_PER_BLOCK - 1) // GROUPS_PER_BLOCK
    bend = jnp.cumsum(blocks)
    bstart = bend - blocks
    n_blocks = (nt * SORT_SLOTS // ROW_GROUP + N_EXPERTS * (GROUPS_PER_BLOCK - 1)) // GROUPS_PER_BLOCK
    bi = jnp.arange(n_blocks, dtype=jnp.int32)
    block_e = jnp.minimum(jnp.searchsorted(bend, bi, side="right"), N_EXPERTS - 1).astype(jnp.int32)
    q = (bi - bstart[block_e])[:, None] * GROUPS_PER_BLOCK + jnp.arange(GROUPS_PER_BLOCK, dtype=jnp.int32)[None, :]
    n_valid = jnp.clip(tot[block_e] - (bi - bstart[block_e]) * GROUPS_PER_BLOCK, 0, GROUPS_PER_BLOCK).astype(jnp.int32)
    cum_e = cum.T[block_e]
    tile = jnp.sum((cum_e[:, None, :] <= q[:, :, None]).astype(jnp.int32), axis=2)
    tile = jnp.minimum(tile, nt - 1)
    before = jnp.take_along_axis(cum_e - groups.T[block_e], tile, axis=1)
    first = jnp.take_along_axis(start.T[block_e], tile, axis=1)
    rows = tile * SORT_SLOTS + first + (q - before) * ROW_GROUP
    valid = jnp.arange(GROUPS_PER_BLOCK, dtype=jnp.int32)[None, :] < n_valid[:, None]
    rows = jnp.where(valid, rows, 0).astype(jnp.int32)
    n_used = bend[-1].astype(jnp.int32).reshape(1)
    return block_e, n_used, n_valid, rows.reshape(n_blocks, 1, GROUPS_PER_BLOCK)


def _moe2_body(be_ref, nu_ref, nv_ref, row_ref, rown_ref, as_hbm, w1_ref, w3_ref, w2_ref, ys_hbm,
               x_ref, acc_ref, y_ref, gsem, ssem, *, nf):
    i = pl.program_id(0)
    f = pl.program_id(1)
    n_used = nu_ref[0]
    slot = i % 2

    def gather_copy(s, gidx, row):
        row = pl.multiple_of(row, ROW_GROUP)
        dst = pl.multiple_of(gidx * ROW_GROUP, ROW_GROUP)
        return pltpu.make_async_copy(as_hbm.at[pl.ds(row, ROW_GROUP), :], x_ref.at[s, pl.ds(dst, ROW_GROUP), :],
                                     gsem.at[s])

    def scatter_copy(s, gidx, row):
        row = pl.multiple_of(row, ROW_GROUP)
        src = pl.multiple_of(gidx * ROW_GROUP, ROW_GROUP)
        return pltpu.make_async_copy(y_ref.at[s, pl.ds(src, ROW_GROUP), :], ys_hbm.at[pl.ds(row, ROW_GROUP), :],
                                     ssem.at[s])

    def loop(n, fn):
        def body(r, c):
            fn(r)
            return c
        lax.fori_loop(0, n, body, 0)

    @pl.when(i < n_used)
    def _():
        @pl.when(f == 0)
        def _():
            @pl.when(i == 0)
            def _():
                x_ref[...] = jnp.zeros_like(x_ref)
                loop(nv_ref[0], lambda r: gather_copy(0, r, row_ref[0, r]).start())

            loop(nv_ref[i], lambda r: gather_copy(slot, r, 0).wait())

            @pl.when(i + 1 < n_used)
            def _():
                loop(nv_ref[i + 1], lambda r: gather_copy(1 - slot, r, rown_ref[0, r]).start())

        x = x_ref[slot]
        u = jnp.dot(x, w1_ref[...], preferred_element_type=F32)
        v = jnp.dot(x, w3_ref[...], preferred_element_type=F32)
        p = (_silu(u) * v).astype(BF16)
        y = jnp.dot(p, w2_ref[...], preferred_element_type=F32)

        @pl.when(f == 0)
        def _():
            acc_ref[...] = y

        @pl.when(jnp.logical_and(f > 0, f < nf - 1))
        def _():
            acc_ref[...] += y

        @pl.when(f == nf - 1)
        def _():
            y_ref[slot] = (acc_ref[...] + y).astype(BF16)
            loop(nv_ref[i], lambda r: scatter_copy(slot, r, row_ref[0, r]).start())

            @pl.when(i > 0)
            def _():
                loop(nv_ref[i - 1], lambda r: scatter_copy(1 - slot, r, 0).wait())

            @pl.when(i == n_used - 1)
            def _():
                loop(nv_ref[i], lambda r: scatter_copy(slot, r, 0).wait())


def moe_experts_sorted(a_sorted, block_e, n_used, n_valid, rows, w1, w3, w2, w_layer, *, tf=1792):
    n_blocks = rows.shape[0]
    tm = MOE_ROWS
    f_dim = w1.shape[3]
    nf = f_dim // tf
    assert nf >= 2
    body = functools.partial(_moe2_body, nf=nf)

    def wmap(kind):
        def index_map(i, f, be, nu, nv):
            live = i < nu[0]
            ff = jnp.where(live, f, nf - 1)
            ii = jnp.where(live, i, nu[0] - 1)
            return (w_layer, be[ii], 0, ff) if kind == "up" else (w_layer, be[ii], ff, 0)
        return index_map

    idx_spec = pl.BlockSpec((None, 1, GROUPS_PER_BLOCK), lambda i, f, be, nu, nv: (i, 0, 0), memory_space=pltpu.SMEM)
    next_spec = pl.BlockSpec((None, 1, GROUPS_PER_BLOCK),
                             lambda i, f, be, nu, nv: (jnp.minimum(i + 1, n_blocks - 1), 0, 0),
                             memory_space=pltpu.SMEM)
    grid_spec = pltpu.PrefetchScalarGridSpec(
        num_scalar_prefetch=3,
        grid=(n_blocks, nf),
        in_specs=[idx_spec, next_spec,
                  pl.BlockSpec(memory_space=pl.ANY),
                  pl.BlockSpec((None, None, D_MODEL, tf), wmap("up")),
                  pl.BlockSpec((None, None, D_MODEL, tf), wmap("up")),
                  pl.BlockSpec((None, None, tf, D_MODEL), wmap("down"))],
        out_specs=pl.BlockSpec(memory_space=pl.ANY),
        scratch_shapes=[pltpu.VMEM((2, tm, D_MODEL), BF16), pltpu.VMEM((tm, D_MODEL), F32),
                        pltpu.VMEM((2, tm, D_MODEL), BF16),
                        pltpu.SemaphoreType.DMA((2,)), pltpu.SemaphoreType.DMA((2,))],
    )
    return pl.pallas_call(
        body,
        grid_spec=grid_spec,
        out_shape=jax.ShapeDtypeStruct(a_sorted.shape, BF16),
        input_output_aliases={5: 0},
        compiler_params=_cparams("arbitrary", "arbitrary"),
        name="moe_experts",
    )(block_e, n_used, n_valid, rows, rows, a_sorted, w1, w3, w2)


def _unsort_body(h_ref, ys_ref, meta_ref, mod_ref, fg_ref, o_ref, *, tm, rows_per_batch, tile_off, final):
    i = pl.program_id(0)
    row = ((i * tm) // rows_per_batch) if rows_per_batch else CTX_MOD_ROW
    meta = meta_ref[...]
    slot0 = meta[:, META_SLOT0:META_SLOT0 + 1]
    slot1 = meta[:, META_SLOT1:META_SLOT1 + 1]
    gate0 = meta[:, META_GATE0:META_GATE0 + 1]
    gate1 = meta[:, META_GATE1:META_GATE1 + 1]
    j_io = lax.broadcasted_iota(jnp.int32, (tm, SORT_SLOTS), 1).astype(F32)
    pick = jnp.where(j_io == slot0, gate0, jnp.where(j_io == slot1, gate1, 0.0)).astype(BF16)
    fsum = jnp.dot(pick, ys_ref[...], preferred_element_type=F32)
    out = h_ref[...] + _mod_chunk(mod_ref, row, 5) * fsum
    if final:
        ms = jnp.mean(out * out, axis=-1, keepdims=True)
        out = out * lax.rsqrt(ms + NORM_EPS) * fg_ref[...]
    o_ref[...] = out


def moe_unsort_combine(h, ys, meta, mods, layer, final_g, *, tile_off, rows_per_batch, final):
    tm = SORT_TOKENS
    m = h.shape[0]
    body = functools.partial(_unsort_body, tm=tm, rows_per_batch=rows_per_batch, tile_off=tile_off, final=final)
    return pl.pallas_call(
        body,
        grid=(m // tm,),
        in_specs=[pl.BlockSpec((tm, D_MODEL), lambda i: (i, 0)),
                  pl.BlockSpec((SORT_SLOTS, D_MODEL), lambda i: (tile_off + i, 0)),
                  pl.BlockSpec((tm, LANES), lambda i: (tile_off + i, 0)),
                  _mod_spec(layer),
                  pl.BlockSpec((1, D_MODEL), lambda i: (0, 0))],
        out_specs=pl.BlockSpec((tm, D_MODEL), lambda i: (i, 0)),
        out_shape=jax.ShapeDtypeStruct((m, D_MODEL), F32),
        compiler_params=_cparams("arbitrary"),
        name="moe_combine",
    )(h, ys, meta, mods, final_g.reshape(1, D_MODEL))


def _na_layer(h_lat, h_ctx, mods, layer, g, w_qkv, b_qkv, rpb, w_out, b_out, with_ctx_out):
    b, seq, _ = h_lat.shape
    c = h_ctx.shape[1]
    qscale = jnp.concatenate([jnp.full((D_MODEL,), (D_MODEL // NA_HEADS) ** -0.5, F32), jnp.ones((2 * D_MODEL,), F32)])
    w = (w_qkv * qscale).astype(BF16)
    bias = b_qkv * qscale
    qkv = nm_matmul(h_lat.reshape(b * seq, D_MODEL), mods, layer, g, w, bias, rows_per_batch=seq, sh=0, sc=1)
    qkv_c = nm_matmul(h_ctx.reshape(b * c, D_MODEL), mods, layer, g, w, bias, rows_per_batch=None, sh=0, sc=1)
    qkv = qkv.reshape(b, seq, 3 * D_MODEL)
    qkv_c = qkv_c.reshape(b, c, 3 * D_MODEL)
    o_lat = na_attention(qkv, qkv_c, na_bias_table(rpb))
    wo = w_out.astype(BF16)
    h_lat = mm_residual(o_lat.reshape(b * seq, D_MODEL), h_lat.reshape(b * seq, D_MODEL), mods, layer, wo, b_out,
                        rows_per_batch=seq, gate=2).reshape(b, seq, D_MODEL)
    if with_ctx_out:
        o_ctx = ctx_attention(qkv_c)
        h_ctx = mm_residual(o_ctx.reshape(b * c, D_MODEL), h_ctx.reshape(b * c, D_MODEL), mods, layer, wo, b_out,
                            rows_per_batch=None, gate=2).reshape(b, c, D_MODEL)
    return h_lat, h_ctx


def _mlstm_layer(h_lat, h_ctx, mods, layer, g, w_up, conv_w, conv_b, w_q, w_k, w_v, w_gates, b_gates,
                 gn_w, skip, w_down, with_ctx_out):
    b, seq, _ = h_lat.shape
    c = h_ctx.shape[1]
    inner = MLSTM_INNER
    wu = w_up.astype(BF16)
    zero_b = jnp.zeros((2 * inner,), F32)
    wq_bd = block_diag_weights(w_q).astype(BF16)
    wk_bd = block_diag_weights(w_k).astype(BF16)
    wkt_bd = jnp.swapaxes(wk_bd, 1, 2)
    wv_bd = block_diag_weights(w_v).astype(BF16)
    gxc, gxm = fold_gate_weights(w_q, w_k, w_v, w_gates)
    wd = w_down.astype(BF16)

    def features(h, rows_per_batch):
        n, s, _ = h.shape
        up = nm_matmul(h.reshape(n * s, D_MODEL), mods, layer, g, wu, zero_b, rows_per_batch=rows_per_batch, sh=0, sc=1)
        up = up.reshape(n, s, 2 * inner)
        return up, mlstm_features(up, conv_w, conv_b, wq_bd, wk_bd, wkt_bd, wv_bd, gxc, gxm, b_gates)

    up_c, (q_c, k_c, kt_c, v_c, xc_c, g_c, gt_c) = features(h_ctx, None)
    up_l, (q_l, k_l, kt_l, v_l, xc_l, g_l, gt_l) = features(h_lat, seq)
    hf_c, cf, mf = mlstm_scan(q_c, k_c, kt_c, v_c, g_c, gt_c, None, rev=False)
    hb_c, cb, mb = mlstm_scan(q_c, k_c, kt_c, v_c, g_c, gt_c, None, rev=True)
    hf_l, _, _ = mlstm_scan(q_l, k_l, kt_l, v_l, g_l, gt_l, (cf, mf), rev=False)
    hb_l, _, _ = mlstm_scan(q_l, k_l, kt_l, v_l, g_l, gt_l, (cb, mb), rev=True)
    flat = lambda a: a.reshape(-1, a.shape[-1])
    h_lat = mlstm_output(flat(hf_l), flat(hb_l), flat(xc_l), flat(up_l), flat(h_lat), mods, layer, gn_w, skip, wd,
                         rows_per_batch=seq).reshape(b, seq, D_MODEL)
    if with_ctx_out:
        h_ctx = mlstm_output(flat(hf_c), flat(hb_c), flat(xc_c), flat(up_c), flat(h_ctx), mods, layer, gn_w, skip, wd,
                             rows_per_batch=None).reshape(b, c, D_MODEL)
    return h_lat, h_ctx


def _moe_layer(h_lat, h_ctx, mods, layer, g, w_router, w1, w3, w2, w_layer, final_g, last):
    b, seq, _ = h_lat.shape
    c = h_ctx.shape[1]
    wrt = jnp.pad(w_router.T, ((0, EXPERT_ROWS - N_EXPERTS), (0, 0)))
    hl = h_lat.reshape(b * seq, D_MODEL)
    hc = None if last else h_ctx.reshape(b * c, D_MODEL)
    a_sorted, meta, cnt = moe_sort(hl, hc, mods, layer, g, wrt, rows_per_batch=seq)
    block_e, n_used, n_valid, rows = moe_group_table(cnt)
    ys = moe_experts_sorted(a_sorted, block_e, n_used, n_valid, rows, w1, w3, w2, w_layer)
    h_lat = moe_unsort_combine(hl, ys, meta, mods, layer, final_g, tile_off=0, rows_per_batch=seq,
                               final=last).reshape(b, seq, D_MODEL)
    if not last:
        h_ctx = moe_unsort_combine(hc, ys, meta, mods, layer, final_g, tile_off=(b * seq) // SORT_TOKENS,
                                   rows_per_batch=None, final=False).reshape(b, c, D_MODEL)
    return h_lat, h_ctx


def kernel(x, c, ctx, c_ctx, w_mod, b_mod, norm_g, final_g, na_w_qkv, na_b_qkv, na_rpb, na_w_out, na_b_out,
           pool_w, pool_scale, ml_w_up, ml_conv_w, ml_conv_b, ml_w_q, ml_w_k, ml_w_v, ml_w_gates, ml_b_gates,
           ml_gn_w, ml_skip, ml_w_down, ffn_w1, ffn_w3, ffn_w2, moe_w_router, moe_w1, moe_w3, moe_w2):
    b, seq, _ = x.shape
    n_ctx = ctx.shape[1]
    depth = w_mod.shape[0]
    assert b <= CTX_MOD_ROW
    cond = jnp.zeros((MOD_ROWS, D_MODEL), F32).at[:b].set(c).at[CTX_MOD_ROW].set(c_ctx)
    mods = adaln_all(cond, w_mod, b_mod)
    moe_w1b, moe_w3b, moe_w2b = moe_w1.astype(BF16), moe_w3.astype(BF16), moe_w2.astype(BF16)
    h_lat, h_ctx = x, ctx
    for i in range(depth):
        last = i == depth - 1
        kind = i % 3
        j = i // 3
        g_tok = norm_g[i, 0]
        if kind == 0:
            h_lat, h_ctx = _na_layer(h_lat, h_ctx, mods, i, g_tok, na_w_qkv[j], na_b_qkv[j], na_rpb[j],
                                     na_w_out[j], na_b_out[j], not last)
        elif kind == 1:
            wp = pool_w[j].astype(BF16)
            h_lat = pool_mixer(h_lat, mods, i, g_tok, wp, pool_scale[j], is_ctx=False)
            if not last:
                h_ctx = pool_mixer(h_ctx, mods, i, g_tok, wp, pool_scale[j], is_ctx=True)
        else:
            h_lat, h_ctx = _mlstm_layer(h_lat, h_ctx, mods, i, g_tok, ml_w_up[j], ml_conv_w[j], ml_conv_b[j],
                                        ml_w_q[j], ml_w_k[j], ml_w_v[j], ml_w_gates[j], ml_b_gates[j],
                                        ml_gn_w[j], ml_skip[j], ml_w_down[j], not last)
        e = i // 2
        g_ch = norm_g[i, 1]
        if i % 2 == 0:
            w1, w3, w2 = ffn_w1[e].astype(BF16), ffn_w3[e].astype(BF16), ffn_w2[e].astype(BF16)
            h_lat = ffn_dense(h_lat.reshape(b * seq, D_MODEL), mods, i, g_ch, w1, w3, w2,
                              rows_per_batch=seq).reshape(b, seq, D_MODEL)
            if not last:
                h_ctx = ffn_dense(h_ctx.reshape(b * n_ctx, D_MODEL), mods, i, g_ch, w1, w3, w2,
                                  rows_per_batch=None).reshape(b, n_ctx, D_MODEL)
        else:
            h_lat, h_ctx = _moe_layer(h_lat, h_ctx, mods, i, g_ch, moe_w_router[e], moe_w1b, moe_w3b, moe_w2b, e,
                                      final_g, last)
    return h_lat
```

```python
import functools

import jax
import jax.numpy as jnp
from jax import lax
from jax.experimental import pallas as pl
from jax.experimental.pallas import tpu as pltpu

F32 = jnp.float32
BF16 = jnp.bfloat16

D_MODEL = 1024
N_MOD = 6
NORM_EPS = 1e-6
GRID_W = 64
NA_HEADS = 16
NA_WIN_ROWS = 8
NA_WIN_COLS = 16
POOL_WINDOWS = (2, 4, 8, 16)
POOL_GROUP_DIM = D_MODEL // len(POOL_WINDOWS)
MLSTM_INNER = 2 * D_MODEL
MLSTM_HEADS = 4
MLSTM_HEAD_DIM = MLSTM_INNER // MLSTM_HEADS
MLSTM_CONV = 4
MLSTM_QKV_BLOCK = 4
N_EXPERTS = 8
TOP_K = 2

LANES = 128
MOD_ROWS = 8
CTX_MOD_ROW = 4
VMEM_LIMIT_BYTES = 56 * 1024 * 1024
NEG_BIG = -1e30
SCAN_CHUNK = 256
SCAN_STATE_ROWS = MLSTM_HEAD_DIM + 8
MOE_ROWS = 1024


def _cparams(*sem):
    return pltpu.CompilerParams(dimension_semantics=sem, vmem_limit_bytes=VMEM_LIMIT_BYTES)


def _resident(shape, index_map):
    return pl.BlockSpec(shape, index_map, pipeline_mode=pl.Buffered(1))


def _silu(x):
    return x * jax.nn.sigmoid(x)


def _norm_mod(x, g, shift, scale):
    ms = jnp.mean(x * x, axis=-1, keepdims=True)
    y = x * lax.rsqrt(ms + NORM_EPS) * g
    return y * (1.0 + scale) + shift


def _mod_chunk(mod_ref, row, j):
    return mod_ref[pl.ds(row, 1), pl.ds(j * D_MODEL, D_MODEL)]


def _mod_spec(layer):
    return pl.BlockSpec((None, MOD_ROWS, N_MOD * D_MODEL), lambda *_: (layer, 0, 0))


def _adaln_body(c_ref, w_ref, b_ref, o_ref):
    s = _silu(c_ref[...])
    o_ref[...] = jnp.dot(s, w_ref[...], preferred_element_type=F32) + b_ref[...]


def adaln_all(cond, w_mod, b_mod):
    depth = w_mod.shape[0]
    n = N_MOD * D_MODEL
    tn = 1536
    return pl.pallas_call(
        _adaln_body,
        grid=(depth, n // tn),
        in_specs=[pl.BlockSpec((MOD_ROWS, D_MODEL), lambda l, j: (0, 0)),
                  pl.BlockSpec((None, D_MODEL, tn), lambda l, j: (l, 0, j)),
                  pl.BlockSpec((None, 1, tn), lambda l, j: (l, 0, j))],
        out_specs=pl.BlockSpec((None, MOD_ROWS, tn), lambda l, j: (l, 0, j)),
        out_shape=jax.ShapeDtypeStruct((depth, MOD_ROWS, n), F32),
        compiler_params=_cparams("arbitrary", "arbitrary"),
        name="adaln",
    )(cond, w_mod, b_mod.reshape(depth, 1, n))


def _nm_matmul_body(x_ref, mod_ref, g_ref, w_ref, b_ref, o_ref, *, tm, rows_per_batch, sh, sc, nc):
    i = pl.program_id(0)
    row = (i * tm) // rows_per_batch if rows_per_batch else CTX_MOD_ROW
    a = _norm_mod(x_ref[...], g_ref[...], _mod_chunk(mod_ref, row, sh), _mod_chunk(mod_ref, row, sc)).astype(BF16)
    n = o_ref.shape[1]
    for c in range(n // nc):
        sl = slice(c * nc, (c + 1) * nc)
        y = jnp.dot(a, w_ref[:, sl], preferred_element_type=F32) + b_ref[:, sl]
        o_ref[:, sl] = y.astype(o_ref.dtype)


def nm_matmul(x, mods, layer, g, w, bias, *, rows_per_batch, sh, sc, tm=512, nc=1024, out_dtype=BF16):
    m, n = x.shape[0], w.shape[1]
    tm = min(tm, m)
    body = functools.partial(_nm_matmul_body, tm=tm, rows_per_batch=rows_per_batch, sh=sh, sc=sc, nc=nc)
    return pl.pallas_call(
        body,
        grid=(m // tm,),
        in_specs=[pl.BlockSpec((tm, D_MODEL), lambda i: (i, 0)),
                  _mod_spec(layer),
                  pl.BlockSpec((1, D_MODEL), lambda i: (0, 0)),
                  _resident((D_MODEL, n), lambda i: (0, 0)),
                  pl.BlockSpec((1, n), lambda i: (0, 0))],
        out_specs=pl.BlockSpec((tm, n), lambda i: (i, 0)),
        out_shape=jax.ShapeDtypeStruct((m, n), out_dtype),
        compiler_params=_cparams("arbitrary"),
        name="nm_matmul",
    )(x, mods, g.reshape(1, D_MODEL), w, bias.reshape(1, n))


def _mm_res_body(a_ref, h_ref, mod_ref, w_ref, b_ref, o_ref, *, tm, rows_per_batch, gate):
    i = pl.program_id(0)
    row = (i * tm) // rows_per_batch if rows_per_batch else CTX_MOD_ROW
    y = jnp.dot(a_ref[...], w_ref[...], preferred_element_type=F32) + b_ref[...]
    o_ref[...] = h_ref[...] + _mod_chunk(mod_ref, row, gate) * y


def mm_residual(a, h, mods, layer, w, bias, *, rows_per_batch, gate, tm=512):
    m, k = a.shape
    tm = min(tm, m)
    body = functools.partial(_mm_res_body, tm=tm, rows_per_batch=rows_per_batch, gate=gate)
    return pl.pallas_call(
        body,
        grid=(m // tm,),
        in_specs=[pl.BlockSpec((tm, k), lambda i: (i, 0)),
                  pl.BlockSpec((tm, D_MODEL), lambda i: (i, 0)),
                  _mod_spec(layer),
                  _resident((k, D_MODEL), lambda i: (0, 0)),
                  pl.BlockSpec((1, D_MODEL), lambda i: (0, 0))],
        out_specs=pl.BlockSpec((tm, D_MODEL), lambda i: (i, 0)),
        out_shape=jax.ShapeDtypeStruct((m, D_MODEL), F32),
        compiler_params=_cparams("arbitrary"),
        name="mm_residual",
    )(a, h, mods, w, bias.reshape(1, D_MODEL))


def _ffn_rows(h, mod_ref, row, g_ref, w1_ref, w3_ref, w2_ref):
    a = _norm_mod(h, g_ref[...], _mod_chunk(mod_ref, row, 3), _mod_chunk(mod_ref, row, 4)).astype(BF16)
    u = jnp.dot(a, w1_ref[...], preferred_element_type=F32)
    v = jnp.dot(a, w3_ref[...], preferred_element_type=F32)
    p = (_silu(u) * v).astype(BF16)
    y = jnp.dot(p, w2_ref[...], preferred_element_type=F32)
    return h + _mod_chunk(mod_ref, row, 5) * y


def _ffn_body(h_ref, mod_ref, g_ref, w1_ref, w3_ref, w2_ref, o_ref, *, tm, rows_per_batch):
    i = pl.program_id(0)
    row = (i * tm) // rows_per_batch if rows_per_batch else CTX_MOD_ROW
    o_ref[...] = _ffn_rows(h_ref[...], mod_ref, row, g_ref, w1_ref, w3_ref, w2_ref)


def ffn_dense(h, mods, layer, g, w1, w3, w2, *, rows_per_batch, tm=256):
    m = h.shape[0]
    f = w1.shape[1]
    tm = min(tm, m)
    body = functools.partial(_ffn_body, tm=tm, rows_per_batch=rows_per_batch)
    return pl.pallas_call(
        body,
        grid=(m // tm,),
        in_specs=[pl.BlockSpec((tm, D_MODEL), lambda i: (i, 0)),
                  _mod_spec(layer),
                  pl.BlockSpec((1, D_MODEL), lambda i: (0, 0)),
                  _resident((D_MODEL, f), lambda i: (0, 0)),
                  _resident((D_MODEL, f), lambda i: (0, 0)),
                  _resident((f, D_MODEL), lambda i: (0, 0))],
        out_specs=pl.BlockSpec((tm, D_MODEL), lambda i: (i, 0)),
        out_shape=jax.ShapeDtypeStruct((m, D_MODEL), F32),
        compiler_params=_cparams("arbitrary"),
        name="ffn_dense",
    )(h, mods, g.reshape(1, D_MODEL), w1, w3, w2)


def na_bias_table(rpb):
    h = rpb.shape[0]
    col = jnp.arange(GRID_W)
    c0 = jnp.clip(col - NA_WIN_COLS // 2, 0, GRID_W - NA_WIN_COLS)
    col_ok = (col[None, :] >= c0[:, None]) & (col[None, :] < c0[:, None] + NA_WIN_COLS)
    dcol = jnp.clip(col[None, :] - col[:, None], 1 - NA_WIN_COLS, NA_WIN_COLS - 1) + (NA_WIN_COLS - 1)
    n_drow = 2 * NA_WIN_ROWS - 1
    t = jnp.where(col_ok[None, None], rpb[:, :, dcol].astype(F32), NEG_BIG)
    t = t.transpose(0, 2, 1, 3).reshape(h // 2, 2 * GRID_W, n_drow * GRID_W)
    tiles = [t[:, :, (NA_WIN_ROWS - 1 - off) * GRID_W:(2 * NA_WIN_ROWS - 1 - off) * GRID_W]
             for off in range(NA_WIN_ROWS)]
    return jnp.stack(tiles, axis=1)


def _stack_heads(q):
    lo = lax.broadcasted_iota(jnp.int32, q.shape, 1) < (LANES // 2)
    zero = jnp.zeros_like(q)
    return jnp.concatenate([jnp.where(lo, q, zero), jnp.where(lo, zero, q)], axis=0)


def _unstack_heads(o):
    n = o.shape[0] // 2
    lo = lax.broadcasted_iota(jnp.int32, (n, LANES), 1) < (LANES // 2)
    return jnp.where(lo, o[:n], o[n:])


_NT = (((1,), (1,)), ((), ()))
NA_GROUP = 8


def _na_body(q_ref, k_ref, v_ref, kc_ref, vc_ref, bias_ref, o_ref, *, rows):
    kc = kc_ref[...]
    vc = vc_ref[...]
    kwin = NA_WIN_ROWS * GRID_W

    def scores(r):
        r0 = jnp.clip(r - NA_WIN_ROWS // 2, 0, rows - NA_WIN_ROWS)
        qoff = pl.multiple_of(r * GRID_W, GRID_W)
        koff = pl.multiple_of(r0 * GRID_W, GRID_W)
        qs = _stack_heads(q_ref[pl.ds(qoff, GRID_W), :])
        k = k_ref[pl.ds(koff, kwin), :]
        s_loc = lax.dot_general(qs, k, _NT, preferred_element_type=F32) + bias_ref[r - r0]
        s_ctx = lax.dot_general(qs, kc, _NT, preferred_element_type=F32)
        return qoff, koff, s_loc, s_ctx

    def probs(s_loc, s_ctx):
        m = jnp.maximum(jnp.max(s_loc, axis=-1, keepdims=True), jnp.max(s_ctx, axis=-1, keepdims=True))
        p_loc = jnp.exp(s_loc - m)
        p_ctx = jnp.exp(s_ctx - m)
        l = jnp.sum(p_loc, axis=-1, keepdims=True) + jnp.sum(p_ctx, axis=-1, keepdims=True)
        return p_loc.astype(BF16), p_ctx.astype(BF16), l

    def group(gi, carry):
        sc = [scores(gi * NA_GROUP + u) for u in range(NA_GROUP)]
        pr = [probs(s[2], s[3]) for s in sc]
        for (qoff, koff, _, _), (p_loc, p_ctx, l) in zip(sc, pr):
            v = v_ref[pl.ds(koff, kwin), :]
            o = (jnp.dot(p_loc, v, preferred_element_type=F32)
                 + jnp.dot(p_ctx, vc, preferred_element_type=F32)) / l
            o_ref[pl.ds(qoff, GRID_W), :] = _unstack_heads(o).astype(o_ref.dtype)
        return carry

    lax.fori_loop(0, rows // NA_GROUP, group, 0)


def na_attention(qkv, qkv_ctx, bias):
    b, l, _ = qkv.shape
    c = qkv_ctx.shape[1]
    hp = D_MODEL // LANES
    body = functools.partial(_na_body, rows=l // GRID_W)
    return pl.pallas_call(
        body,
        grid=(b, hp),
        in_specs=[pl.BlockSpec((None, l, LANES), lambda i, j: (i, 0, j)),
                  pl.BlockSpec((None, l, LANES), lambda i, j: (i, 0, hp + j)),
                  pl.BlockSpec((None, l, LANES), lambda i, j: (i, 0, 2 * hp + j)),
                  pl.BlockSpec((None, c, LANES), lambda i, j: (i, 0, hp + j)),
                  pl.BlockSpec((None, c, LANES), lambda i, j: (i, 0, 2 * hp + j)),
                  pl.BlockSpec((None,) + bias.shape[1:], lambda i, j: (j, 0, 0, 0))],
        out_specs=pl.BlockSpec((None, l, LANES), lambda i, j: (i, 0, j)),
        out_shape=jax.ShapeDtypeStruct((b, l, D_MODEL), BF16),
        compiler_params=_cparams("arbitrary", "arbitrary"),
        name="na_attention",
    )(qkv, qkv, qkv, qkv_ctx, qkv_ctx, bias)


def _ctx_attn_body(q_ref, k_ref, v_ref, o_ref):
    qs = _stack_heads(q_ref[...])
    s = lax.dot_general(qs, k_ref[...], _NT, preferred_element_type=F32)
    p = jnp.exp(s - jnp.max(s, axis=-1, keepdims=True))
    l = jnp.sum(p, axis=-1, keepdims=True)
    o = jnp.dot(p.astype(BF16), v_ref[...], preferred_element_type=F32) / l
    o_ref[...] = _unstack_heads(o).astype(o_ref.dtype)


def ctx_attention(qkv_ctx):
    b, c, _ = qkv_ctx.shape
    hp = D_MODEL // LANES
    return pl.pallas_call(
        _ctx_attn_body,
        grid=(b, hp),
        in_specs=[pl.BlockSpec((None, c, LANES), lambda i, j: (i, 0, j)),
                  pl.BlockSpec((None, c, LANES), lambda i, j: (i, 0, hp + j)),
                  pl.BlockSpec((None, c, LANES), lambda i, j: (i, 0, 2 * hp + j))],
        out_specs=pl.BlockSpec((None, c, LANES), lambda i, j: (i, 0, j)),
        out_shape=jax.ShapeDtypeStruct((b, c, D_MODEL), BF16),
        compiler_params=_cparams("arbitrary", "arbitrary"),
        name="ctx_attention",
    )(qkv_ctx, qkv_ctx, qkv_ctx)


POOL_HALO = 8


def _pool_body(prev_ref, cur_ref, next_ref, mod_ref, g_ref, wp_ref, ps_ref, o_ref, *, tl, seq, is_ctx):
    b = pl.program_id(0)
    j = pl.program_id(1)
    row = CTX_MOD_ROW if is_ctx else b
    g = g_ref[...]
    sh = _mod_chunk(mod_ref, row, 0)
    sc = _mod_chunk(mod_ref, row, 1)
    h = cur_ref[...]
    a_cur = _norm_mod(h, g, sh, sc)
    a_prev = _norm_mod(prev_ref[...], g, sh, sc) * (j > 0).astype(F32)
    a_next = _norm_mod(next_ref[...], g, sh, sc) * (j < seq // tl - 1).astype(F32)
    ext = jnp.concatenate([a_prev, a_cur, a_next], axis=0)
    t = j * tl + lax.broadcasted_iota(jnp.int32, (tl, 1), 0)
    outs = []
    for gi, w in enumerate(POOL_WINDOWS):
        sl = slice(gi * POOL_GROUP_DIM, (gi + 1) * POOL_GROUP_DIM)
        p = ext[:, sl]
        step = 1
        while step < w:
            n = p.shape[0]
            p = p[:n - step] + p[step:]
            step *= 2
        off = POOL_HALO - w // 2
        cnt = jnp.minimum(t + w // 2, seq) - jnp.maximum(t - w // 2, 0)
        pooled = p[off:off + tl] / cnt.astype(F32) - a_cur[:, sl]
        outs.append(jnp.dot(pooled.astype(BF16), wp_ref[gi], preferred_element_type=F32))
    y = jnp.concatenate(outs, axis=1) * ps_ref[...]
    o_ref[...] = h + _mod_chunk(mod_ref, row, 2) * y


def pool_mixer(h, mods, layer, g, w_pool, pool_scale, *, is_ctx, tl=512):
    b, seq, _ = h.shape
    tl = min(tl, seq)
    nh = tl // POOL_HALO
    last = seq // POOL_HALO - 1
    body = functools.partial(_pool_body, tl=tl, seq=seq, is_ctx=is_ctx)
    return pl.pallas_call(
        body,
        grid=(b, seq // tl),
        in_specs=[pl.BlockSpec((None, POOL_HALO, D_MODEL), lambda i, j: (i, jnp.maximum(j * nh - 1, 0), 0)),
                  pl.BlockSpec((None, tl, D_MODEL), lambda i, j: (i, j, 0)),
                  pl.BlockSpec((None, POOL_HALO, D_MODEL), lambda i, j: (i, jnp.minimum((j + 1) * nh, last), 0)),
                  _mod_spec(layer),
                  pl.BlockSpec((1, D_MODEL), lambda i, j: (0, 0)),
                  pl.BlockSpec(w_pool.shape, lambda i, j: (0, 0, 0)),
                  pl.BlockSpec((1, D_MODEL), lambda i, j: (0, 0))],
        out_specs=pl.BlockSpec((None, tl, D_MODEL), lambda i, j: (i, j, 0)),
        out_shape=jax.ShapeDtypeStruct(h.shape, F32),
        compiler_params=_cparams("arbitrary", "arbitrary"),
        name="pool_mixer",
    )(h, h, h, mods, g.reshape(1, D_MODEL), w_pool, pool_scale.reshape(1, D_MODEL))


CONV_HALO = 16


def block_diag_weights(w):
    nb = LANES // MLSTM_QKV_BLOCK
    wc = w.reshape(-1, nb, MLSTM_QKV_BLOCK, MLSTM_QKV_BLOCK)
    eye = jnp.eye(nb, dtype=w.dtype)
    bd = jnp.einsum("cnij,nm->cnimj", wc, eye)
    return bd.reshape(-1, LANES, LANES)


def fold_gate_weights(w_q, w_k, w_v, w_gates):
    ng = w_gates.shape[1]
    wg = w_gates.reshape(3, -1, MLSTM_QKV_BLOCK, ng)
    fold = lambda w, part: jnp.einsum("ncd,ndg->ncg", w, wg[part], precision=lax.Precision.HIGHEST).reshape(-1, ng)
    return fold(w_q, 0) + fold(w_k, 1), fold(w_v, 2)


def _ml_feat_body(prev_ref, cur_ref, next_ref, cw_ref, cb_ref, wq_ref, wk_ref, wkt_ref, wv_ref,
                  gxc_ref, gxm_ref, gxct_ref, gxmt_ref, bg_ref, bgt_ref,
                  q_ref, k_ref, kt_ref, v_ref, xc_ref, g_ref, gt_ref, ext_ref, *, tl, seq):
    j = pl.program_id(1)
    cur = cur_ref[...]
    ext_ref[0:CONV_HALO, :] = prev_ref[...].astype(F32) * (j > 0).astype(F32)
    ext_ref[CONV_HALO:CONV_HALO + tl, :] = cur.astype(F32)
    ext_ref[CONV_HALO + tl:, :] = next_ref[...].astype(F32) * (j < seq // tl - 1).astype(F32)
    left = MLSTM_CONV // 2
    xc = cb_ref[...]
    for tap in range(MLSTM_CONV):
        xc = xc + ext_ref[pl.ds(CONV_HALO - left + tap, tl), :] * cw_ref[tap:tap + 1, :]
    xc = _silu(xc)
    xcb = xc.astype(BF16)
    xc_ref[...] = xcb
    t = SCAN_CHUNK
    qscale = MLSTM_HEAD_DIM ** -0.5
    for c in range(MLSTM_INNER // LANES):
        sl = slice(c * LANES, (c + 1) * LANES)
        xs = xcb[:, sl]
        q = jnp.dot(xs, wq_ref[c], preferred_element_type=F32)
        k = jnp.dot(xs, wk_ref[c], preferred_element_type=F32)
        v = jnp.dot(cur[:, sl], wv_ref[c], preferred_element_type=F32)
        q_ref[:, sl] = (q * qscale).astype(BF16)
        k_ref[:, sl] = k.astype(BF16)
        v_ref[:, sl] = v.astype(BF16)
        for cc in range(tl // t):
            kt = lax.dot_general(wkt_ref[c], xs[cc * t:(cc + 1) * t], _NT, preferred_element_type=F32)
            kt_ref[cc, sl, :] = kt.astype(BF16)
    ng = g_ref.shape[1]
    g = (jnp.dot(xcb, gxc_ref[...], preferred_element_type=F32)
         + jnp.dot(cur, gxm_ref[...], preferred_element_type=F32))
    g_ref[...] = g[:, :ng] + bg_ref[...]
    gt = (lax.dot_general(gxct_ref[...], xcb, _NT, preferred_element_type=F32)
          + lax.dot_general(gxmt_ref[...], cur, _NT, preferred_element_type=F32)) + bgt_ref[...]
    for cc in range(tl // t):
        gt_ref[cc] = gt[:, cc * t:(cc + 1) * t]


def mlstm_features(up, conv_w, conv_b, wq_bd, wk_bd, wkt_bd, wv_bd, gxc, gxm, bg, *, tl=512):
    b, seq, _ = up.shape
    tl = min(tl, seq)
    t = SCAN_CHUNK
    nh = tl // CONV_HALO
    last = seq // CONV_HALO - 1
    ng = gxc.shape[1]
    inner = MLSTM_INNER
    body = functools.partial(_ml_feat_body, tl=tl, seq=seq)
    full = lambda a: pl.BlockSpec(a.shape, lambda i, j: (0,) * a.ndim)
    cw = conv_w
    cb = conv_b.reshape(1, inner)
    bgr = bg.reshape(1, ng)
    bgc = bg.reshape(ng, 1)
    pad = lambda w: jnp.pad(w, ((0, 0), (0, LANES - ng))).astype(BF16)
    gxc_p, gxm_p = pad(gxc), pad(gxm)
    gxc_t, gxm_t = gxc.T.astype(BF16), gxm.T.astype(BF16)
    return pl.pallas_call(
        body,
        grid=(b, seq // tl),
        in_specs=[pl.BlockSpec((None, CONV_HALO, inner), lambda i, j: (i, jnp.maximum(j * nh - 1, 0), 0)),
                  pl.BlockSpec((None, tl, inner), lambda i, j: (i, j, 0)),
                  pl.BlockSpec((None, CONV_HALO, inner), lambda i, j: (i, jnp.minimum((j + 1) * nh, last), 0)),
                  full(cw), full(cb), full(wq_bd), full(wk_bd), full(wkt_bd), full(wv_bd),
                  full(gxc_p), full(gxm_p), full(gxc_t), full(gxm_t), full(bgr), full(bgc)],
        out_specs=[pl.BlockSpec((None, tl, inner), lambda i, j: (i, j, 0)),
                   pl.BlockSpec((None, tl, inner), lambda i, j: (i, j, 0)),
                   pl.BlockSpec((None, tl // t, inner, t), lambda i, j: (i, j, 0, 0)),
                   pl.BlockSpec((None, tl, inner), lambda i, j: (i, j, 0)),
                   pl.BlockSpec((None, tl, inner), lambda i, j: (i, j, 0)),
                   pl.BlockSpec((None, tl, ng), lambda i, j: (i, j, 0)),
                   pl.BlockSpec((None, tl // t, ng, t), lambda i, j: (i, j, 0, 0))],
        out_shape=[jax.ShapeDtypeStruct((b, seq, inner), BF16),
                   jax.ShapeDtypeStruct((b, seq, inner), BF16),
                   jax.ShapeDtypeStruct((b, seq // t, inner, t), BF16),
                   jax.ShapeDtypeStruct((b, seq, inner), BF16),
                   jax.ShapeDtypeStruct((b, seq, inner), BF16),
                   jax.ShapeDtypeStruct((b, seq, ng), F32),
                   jax.ShapeDtypeStruct((b, seq // t, ng, t), F32)],
        scratch_shapes=[pltpu.VMEM((tl + 2 * CONV_HALO, inner), F32)],
        compiler_params=_cparams("arbitrary", "arbitrary"),
        name="mlstm_features",
    )(up, up, up, cw, cb, wq_bd, wk_bd, wkt_bd, wv_bd, gxc_p, gxm_p, gxc_t, gxm_t, bgr, bgc)


def _log_sigmoid(x):
    return jnp.minimum(x, 0.0) - jnp.log1p(jnp.exp(-jnp.abs(x)))


def _scan_body(*refs, rev, nchunk, nblk, has_init):
    if has_init:
        q_ref, k_ref, kt_ref, v_ref, g_ref, gt_ref, c0_ref, m0_ref, h_ref, cf_ref, mf_ref, c_sc, m_sc = refs
    else:
        q_ref, k_ref, kt_ref, v_ref, g_ref, gt_ref, h_ref, cf_ref, mf_ref, c_sc, m_sc = refs
    hd = pl.program_id(1)
    j = pl.program_id(2)
    t = SCAN_CHUNK
    dh = MLSTM_HEAD_DIM

    @pl.when(j == 0)
    def _():
        if has_init:
            c_sc[...] = c0_ref[...]
            m_sc[...] = m0_ref[...]
        else:
            c_sc[...] = jnp.zeros_like(c_sc)
            m_sc[...] = jnp.zeros_like(m_sc)

    ci = (2 if rev else 0) * MLSTM_HEADS + hd
    cf = (3 if rev else 1) * MLSTM_HEADS + hd
    ng = g_ref.shape[1]
    lane = lax.broadcasted_iota(jnp.int32, (t, ng), 1)
    r_io = lax.broadcasted_iota(jnp.int32, (t, t), 0)
    c_io = lax.broadcasted_iota(jnp.int32, (t, t), 1)
    seen = (c_io >= r_io) if rev else (c_io <= r_io)
    seen_t = (r_io >= c_io) if rev else (r_io <= c_io)
    seen_f = seen.astype(F32)
    seen_tf = seen_t.astype(F32)

    order = range(nchunk - 1, -1, -1) if rev else range(nchunk)
    for cc in order:
        rows = slice(cc * t, (cc + 1) * t)
        q = q_ref[rows, :]
        kt = kt_ref[cc]
        v = v_ref[rows, :]
        g = g_ref[rows, :]
        i_col = jnp.sum(jnp.where(lane == ci, g, 0.0), axis=1, keepdims=True)
        f_col = jnp.sum(jnp.where(lane == cf, g, 0.0), axis=1, keepdims=True)
        i_row = gt_ref[cc, pl.ds(ci, 1), :]
        f_row = gt_ref[cc, pl.ds(cf, 1), :]
        lf_col = _log_sigmoid(f_col)
        lf_row = _log_sigmoid(f_row)
        b_col = jnp.sum(seen_f * lf_row, axis=1, keepdims=True)
        b_row = jnp.sum(seen_tf * lf_col, axis=0, keepdims=True)
        m_prev = m_sc[0:1, 0:1]
        n_row = c_sc[dh:dh + 1, :]
        dmat = jnp.where(seen, b_col - b_row + i_row, NEG_BIG)
        inter = b_col + m_prev
        m_t = jnp.maximum(inter, jnp.max(dmat, axis=1, keepdims=True))
        a = jnp.dot(q, kt, preferred_element_type=F32) * jnp.exp(dmat - m_t)
        w_int = jnp.exp(inter - m_t)
        cb = c_sc[0:dh, :].astype(BF16)
        num = (jnp.dot(a.astype(BF16), v, preferred_element_type=F32)
               + jnp.dot(q, cb, preferred_element_type=F32) * w_int)
        den = (jnp.sum(a, axis=1, keepdims=True)
               + w_int * jnp.sum(q.astype(F32) * n_row, axis=1, keepdims=True))
        hc = num / jnp.maximum(jnp.abs(den), jnp.exp(-m_t))
        h_ref[rows, :] = hc.astype(h_ref.dtype)
        b_end = jnp.sum(lf_row, axis=1, keepdims=True)
        g_row = b_end - b_row + i_row
        m_new = jnp.maximum(b_end + m_prev, jnp.max(g_row, axis=1, keepdims=True))
        decay = jnp.exp(b_end + m_prev - m_new)
        kw = (kt.astype(F32) * jnp.exp(g_row - m_new)).astype(BF16)
        w_col = jnp.exp(b_end - b_col + i_col - m_new)
        c_sc[0:dh, :] = decay * c_sc[0:dh, :] + jnp.dot(kw, v, preferred_element_type=F32)
        c_sc[dh:dh + 1, :] = decay * n_row + jnp.sum(k_ref[rows, :].astype(F32) * w_col, axis=0, keepdims=True)
        m_sc[...] = jnp.broadcast_to(m_new, m_sc.shape)

    @pl.when(j == nblk - 1)
    def _():
        cf_ref[...] = c_sc[...]
        mf_ref[...] = m_sc[...]


def mlstm_scan(q, k, kt, v, g, gt, state, *, rev, tb=1024):
    b, seq, inner = q.shape
    t = SCAN_CHUNK
    tb = min(tb, seq)
    nblk = seq // tb
    nchunk = tb // t
    dh = MLSTM_HEAD_DIM
    ng = g.shape[2]
    has_init = state is not None
    blk = (lambda j: nblk - 1 - j) if rev else (lambda j: j)
    body = functools.partial(_scan_body, rev=rev, nchunk=nchunk, nblk=nblk, has_init=has_init)
    tok_spec = pl.BlockSpec((None, tb, dh), lambda i, h, j: (i, blk(j), h))
    in_specs = [tok_spec, tok_spec,
                pl.BlockSpec((None, nchunk, dh, t), lambda i, h, j: (i, blk(j), h, 0)),
                tok_spec,
                pl.BlockSpec((None, tb, ng), lambda i, h, j: (i, blk(j), 0)),
                pl.BlockSpec((None, nchunk, ng, t), lambda i, h, j: (i, blk(j), 0, 0))]
    args = [q, k, kt, v, g, gt]
    st_spec_c = pl.BlockSpec((None, None, SCAN_STATE_ROWS, dh), lambda i, h, j: (i, h, 0, 0))
    st_spec_m = pl.BlockSpec((None, None, 8, LANES), lambda i, h, j: (i, h, 0, 0))
    if has_init:
        in_specs += [st_spec_c, st_spec_m]
        args += list(state)
    return pl.pallas_call(
        body,
        grid=(b, MLSTM_HEADS, nblk),
        in_specs=in_specs,
        out_specs=[tok_spec, st_spec_c, st_spec_m],
        out_shape=[jax.ShapeDtypeStruct((b, seq, inner), BF16),
                   jax.ShapeDtypeStruct((b, MLSTM_HEADS, SCAN_STATE_ROWS, dh), F32),
                   jax.ShapeDtypeStruct((b, MLSTM_HEADS, 8, LANES), F32)],
        scratch_shapes=[pltpu.VMEM((SCAN_STATE_ROWS, dh), F32), pltpu.VMEM((8, LANES), F32)],
        compiler_params=_cparams("arbitrary", "arbitrary", "arbitrary"),
        name="mlstm_scan_bwd" if rev else "mlstm_scan_fwd",
    )(*args)


ML_OUT_ROWS = 256


def _ml_out_body(hf_ref, hb_ref, xc_ref, z_ref, h_ref, mod_ref, gn_ref, sk_ref, w_ref, *rest, tm, rows_per_batch):
    o_ref = rest[-1]
    i = pl.program_id(0)
    row = (i * tm) // rows_per_batch if rows_per_batch else CTX_MOD_ROW

    def gated(rows):
        hs = hf_ref[rows, :].astype(F32) + hb_ref[rows, :].astype(F32)
        parts = []
        for hd in range(MLSTM_HEADS):
            x = hs[:, hd * MLSTM_HEAD_DIM:(hd + 1) * MLSTM_HEAD_DIM]
            mu = jnp.mean(x, axis=-1, keepdims=True)
            xm = x - mu
            var = jnp.mean(xm * xm, axis=-1, keepdims=True)
            parts.append(xm * lax.rsqrt(var + NORM_EPS))
        hn = jnp.concatenate(parts, axis=1) * gn_ref[...]
        y = (hn + sk_ref[...] * xc_ref[rows, :].astype(F32)) * _silu(z_ref[rows, :].astype(F32))
        return y.astype(BF16)

    groups = [slice(s * ML_OUT_ROWS, (s + 1) * ML_OUT_ROWS) for s in range(tm // ML_OUT_ROWS)]
    ys = [gated(rows) for rows in groups]
    for rows, y in zip(groups, ys):
        y = jnp.dot(y, w_ref[...], preferred_element_type=F32)
        h1 = h_ref[rows, :] + _mod_chunk(mod_ref, row, 2) * y
        if len(rest) > 1:
            g_ref, w1_ref, w3_ref, w2_ref = rest[:4]
            h1 = _ffn_rows(h1, mod_ref, row, g_ref, w1_ref, w3_ref, w2_ref)
        o_ref[rows, :] = h1


def mlstm_output(hf, hb, xc, up, h, mods, layer, gn_w, skip, w_down, ffn=None, *, rows_per_batch, tm=256):
    m = h.shape[0]
    tm = min(tm, m)
    inner = MLSTM_INNER
    body = functools.partial(_ml_out_body, tm=tm, rows_per_batch=rows_per_batch)
    row_spec = pl.BlockSpec((tm, inner), lambda i: (i, 0))
    in_specs = [row_spec, row_spec, row_spec,
                pl.BlockSpec((tm, inner), lambda i: (i, 1)),
                pl.BlockSpec((tm, D_MODEL), lambda i: (i, 0)),
                _mod_spec(layer),
                pl.BlockSpec((1, inner), lambda i: (0, 0)),
                pl.BlockSpec((1, inner), lambda i: (0, 0)),
                _resident((inner, D_MODEL), lambda i: (0, 0))]
    args = [hf, hb, xc, up, h, mods, gn_w.reshape(1, inner), skip.reshape(1, inner), w_down]
    if ffn is not None:
        g, w1, w3, w2 = ffn
        f = w1.shape[1]
        in_specs += [pl.BlockSpec((1, D_MODEL), lambda i: (0, 0)),
                     _resident((D_MODEL, f), lambda i: (0, 0)),
                     _resident((D_MODEL, f), lambda i: (0, 0)),
                     _resident((f, D_MODEL), lambda i: (0, 0))]
        args += [g.reshape(1, D_MODEL), w1, w3, w2]
    return pl.pallas_call(
        body,
        grid=(m // tm,),
        in_specs=in_specs,
        out_specs=pl.BlockSpec((tm, D_MODEL), lambda i: (i, 0)),
        out_shape=jax.ShapeDtypeStruct((m, D_MODEL), F32),
        compiler_params=_cparams("arbitrary"),
        name="mlstm_output",
    )(*args)


SORT_TOKENS = 512
ROW_GROUP = 16
EXPERT_ROWS = 16
SORT_SLOTS = -(-(TOP_K * SORT_TOKENS + N_EXPERTS * (ROW_GROUP - 1)) // LANES) * LANES
GROUPS_PER_BLOCK = MOE_ROWS // ROW_GROUP
META_SLOT0, META_SLOT1, META_GATE0, META_GATE1 = 0, 1, 2, 3


def _sort_body(xl_ref, xc_ref, mod_ref, g_ref, wrt_ref, earlier_ref, as_ref, meta_ref, cnt_ref, *,
               tm, rows_per_batch, n_lat):
    i = pl.program_id(0)
    is_lat = i < n_lat
    row = jnp.where(is_lat, (i * tm) // rows_per_batch, CTX_MOD_ROW)
    x = jnp.where(is_lat, xl_ref[...], xc_ref[...])
    a = _norm_mod(x, g_ref[...], _mod_chunk(mod_ref, row, 3), _mod_chunk(mod_ref, row, 4))
    e_io = lax.broadcasted_iota(jnp.int32, (EXPERT_ROWS, tm), 0)
    ab = a.astype(BF16)
    a_rem = (a - ab.astype(F32)).astype(BF16)
    wrt = wrt_ref[...]
    l_head = lax.dot_general(wrt, ab, _NT, preferred_element_type=F32)
    lt = (l_head[:EXPERT_ROWS] + l_head[EXPERT_ROWS:]
          + lax.dot_general(wrt[:EXPERT_ROWS], a_rem, _NT, preferred_element_type=F32))
    lt = jnp.where(e_io < N_EXPERTS, lt, -jnp.inf)
    m0 = jnp.max(lt, axis=0, keepdims=True)
    e0 = jnp.min(jnp.where(lt == m0, e_io, EXPERT_ROWS), axis=0, keepdims=True)
    oh0 = e_io == e0
    lt1 = jnp.where(oh0, -jnp.inf, lt)
    m1 = jnp.max(lt1, axis=0, keepdims=True)
    e1 = jnp.min(jnp.where(lt1 == m1, e_io, EXPERT_ROWS), axis=0, keepdims=True)
    oh1 = e_io == e1
    ex = jnp.exp(m1 - m0)
    gate0 = 1.0 / (1.0 + ex)
    gate1 = ex / (1.0 + ex)
    oh = jnp.where(oh0, 1.0, jnp.where(oh1, 1.0, 0.0))
    rank = jnp.dot(oh.astype(BF16), earlier_ref[...], preferred_element_type=F32)
    cnt = jnp.sum(oh, axis=1, keepdims=True)
    padded = jnp.floor((cnt + (ROW_GROUP - 1)) * (1.0 / ROW_GROUP)) * ROW_GROUP
    r8 = lax.broadcasted_iota(jnp.int32, (EXPERT_ROWS, EXPERT_ROWS), 0)
    c8 = lax.broadcasted_iota(jnp.int32, (EXPERT_ROWS, EXPERT_ROWS), 1)
    padded_row = jnp.sum(jnp.where(r8 == c8, padded, 0.0), axis=0, keepdims=True)
    start = jnp.sum(jnp.where(c8 < r8, padded_row, 0.0), axis=1, keepdims=True)
    slot0 = jnp.sum(jnp.where(oh0, start + rank, 0.0), axis=0, keepdims=True)
    slot1 = jnp.sum(jnp.where(oh1, start + rank, 0.0), axis=0, keepdims=True)
    j_io = lax.broadcasted_iota(jnp.int32, (SORT_SLOTS, tm), 0).astype(F32)
    perm = jnp.where(j_io == slot0, 1.0, jnp.where(j_io == slot1, 1.0, 0.0)).astype(BF16)
    as_ref[...] = jnp.dot(perm, ab, preferred_element_type=F32).astype(BF16)
    rows = jnp.concatenate([slot0, slot1, gate0, gate1, jnp.zeros((LANES - 4, tm), F32)], axis=0)
    meta_ref[...] = rows.T
    cnt_ref[...] = jnp.concatenate([jnp.broadcast_to(padded, (EXPERT_ROWS, LANES)),
                                    jnp.broadcast_to(start, (EXPERT_ROWS, LANES))], axis=0)


def moe_sort(x_lat, x_ctx, mods, layer, g, wrt, *, rows_per_batch):
    tm = SORT_TOKENS
    n_lat = x_lat.shape[0] // tm
    n_ctx = 0 if x_ctx is None else x_ctx.shape[0] // tm
    nt = n_lat + n_ctx
    if x_ctx is None:
        x_ctx = x_lat
    w_head = wrt.astype(BF16)
    w_split = jnp.concatenate([w_head, (wrt - w_head.astype(F32)).astype(BF16)], axis=0)
    tok = jnp.arange(tm)
    earlier = (tok[:, None] < tok[None, :]).astype(BF16)
    body = functools.partial(_sort_body, tm=tm, rows_per_batch=rows_per_batch, n_lat=n_lat)
    return pl.pallas_call(
        body,
        grid=(nt,),
        in_specs=[pl.BlockSpec((tm, D_MODEL), lambda i: (jnp.minimum(i, n_lat - 1), 0)),
                  pl.BlockSpec((tm, D_MODEL), lambda i: (jnp.maximum(i - n_lat, 0), 0)),
                  _mod_spec(layer),
                  pl.BlockSpec((1, D_MODEL), lambda i: (0, 0)),
                  pl.BlockSpec((2 * EXPERT_ROWS, D_MODEL), lambda i: (0, 0)),
                  pl.BlockSpec((tm, tm), lambda i: (0, 0))],
        out_specs=[pl.BlockSpec((SORT_SLOTS, D_MODEL), lambda i: (i, 0)),
                   pl.BlockSpec((tm, LANES), lambda i: (i, 0)),
                   pl.BlockSpec((None, 2 * EXPERT_ROWS, LANES), lambda i: (i, 0, 0))],
        out_shape=[jax.ShapeDtypeStruct((nt * SORT_SLOTS, D_MODEL), BF16),
                   jax.ShapeDtypeStruct((nt * tm, LANES), F32),
                   jax.ShapeDtypeStruct((nt, 2 * EXPERT_ROWS, LANES), F32)],
        compiler_params=_cparams("arbitrary"),
        name="moe_sort",
    )(x_lat, x_ctx, mods, g.reshape(1, D_MODEL), w_split, earlier)


def moe_group_table(cnt):
    nt = cnt.shape[0]
    padded = cnt[:, :N_EXPERTS, 0].astype(jnp.int32)
    start = cnt[:, EXPERT_ROWS:EXPERT_ROWS + N_EXPERTS, 0].astype(jnp.int32)
    groups = padded // ROW_GROUP
    cum = jnp.cumsum(groups, axis=0)
    tot = cum[-1]
    blocks = (tot + GROUPS_PER_BLOCK - 1) // GROUPS_PER_BLOCK
    bend = jnp.cumsum(blocks)
    bstart = bend - blocks
    n_blocks = (nt * SORT_SLOTS // ROW_GROUP + N_EXPERTS * (GROUPS_PER_BLOCK - 1)) // GROUPS_PER_BLOCK
    bi = jnp.arange(n_blocks, dtype=jnp.int32)
    block_e = jnp.minimum(jnp.searchsorted(bend, bi, side="right"), N_EXPERTS - 1).astype(jnp.int32)
    q = (bi - bstart[block_e])[:, None] * GROUPS_PER_BLOCK + jnp.arange(GROUPS_PER_BLOCK, dtype=jnp.int32)[None, :]
    n_valid = jnp.clip(tot[block_e] - (bi - bstart[block_e]) * GROUPS_PER_BLOCK, 0, GROUPS_PER_BLOCK).astype(jnp.int32)
    cum_e = cum.T[block_e]
    tile = jnp.sum((cum_e[:, None, :] <= q[:, :, None]).astype(jnp.int32), axis=2)
    tile = jnp.minimum(tile, nt - 1)
    before = jnp.take_along_axis(cum_e - groups.T[block_e], tile, axis=1)
    first = jnp.take_along_axis(start.T[block_e], tile, axis=1)
    rows = tile * SORT_SLOTS + first + (q - before) * ROW_GROUP
    valid = jnp.arange(GROUPS_PER_BLOCK, dtype=jnp.int32)[None, :] < n_valid[:, None]
    rows = jnp.where(valid, rows, 0).astype(jnp.int32)
    n_used = bend[-1].astype(jnp.int32).reshape(1)
    return block_e, n_used, n_valid, rows.reshape(n_blocks, 1, GROUPS_PER_BLOCK)


def _experts_body(be_ref, nu_ref, nv_ref, row_ref, rown_ref, as_hbm, w1_ref, w3_ref, w2_ref, ys_hbm,
               x_ref, acc_ref, y_ref, gsem, ssem, *, nf):
    i = pl.program_id(0)
    f = pl.program_id(1)
    n_used = nu_ref[0]
    slot = i % 2

    def gather_copy(s, gidx, row):
        row = pl.multiple_of(row, ROW_GROUP)
        dst = pl.multiple_of(gidx * ROW_GROUP, ROW_GROUP)
        return pltpu.make_async_copy(as_hbm.at[pl.ds(row, ROW_GROUP), :], x_ref.at[s, pl.ds(dst, ROW_GROUP), :],
                                     gsem.at[s])

    def scatter_copy(s, gidx, row):
        row = pl.multiple_of(row, ROW_GROUP)
        src = pl.multiple_of(gidx * ROW_GROUP, ROW_GROUP)
        return pltpu.make_async_copy(y_ref.at[s, pl.ds(src, ROW_GROUP), :], ys_hbm.at[pl.ds(row, ROW_GROUP), :],
                                     ssem.at[s])

    def loop(n, fn):
        def body(r, c):
            fn(r)
            return c
        lax.fori_loop(0, n, body, 0)

    @pl.when(i < n_used)
    def _():
        @pl.when(f == 0)
        def _():
            @pl.when(i == 0)
            def _():
                x_ref[...] = jnp.zeros_like(x_ref)
                loop(nv_ref[0], lambda r: gather_copy(0, r, row_ref[0, r]).start())

            loop(nv_ref[i], lambda r: gather_copy(slot, r, 0).wait())

            @pl.when(i + 1 < n_used)
            def _():
                loop(nv_ref[i + 1], lambda r: gather_copy(1 - slot, r, rown_ref[0, r]).start())

        x = x_ref[slot]
        u = jnp.dot(x, w1_ref[...], preferred_element_type=F32)
        v = jnp.dot(x, w3_ref[...], preferred_element_type=F32)
        p = (_silu(u) * v).astype(BF16)
        y = jnp.dot(p, w2_ref[...], preferred_element_type=F32)

        @pl.when(f == 0)
        def _():
            acc_ref[...] = y

        @pl.when(jnp.logical_and(f > 0, f < nf - 1))
        def _():
            acc_ref[...] += y

        @pl.when(f == nf - 1)
        def _():
            y_ref[slot] = (acc_ref[...] + y).astype(BF16)
            loop(nv_ref[i], lambda r: scatter_copy(slot, r, row_ref[0, r]).start())

            @pl.when(i > 0)
            def _():
                loop(nv_ref[i - 1], lambda r: scatter_copy(1 - slot, r, 0).wait())

            @pl.when(i == n_used - 1)
            def _():
                loop(nv_ref[i], lambda r: scatter_copy(slot, r, 0).wait())


def moe_experts_sorted(a_sorted, block_e, n_used, n_valid, rows, w1, w3, w2, w_layer, *, tf=1792):
    n_blocks = rows.shape[0]
    tm = MOE_ROWS
    f_dim = w1.shape[3]
    nf = f_dim // tf
    assert nf >= 2
    body = functools.partial(_experts_body, nf=nf)

    def wmap(kind):
        def index_map(i, f, be, nu, nv):
            live = i < nu[0]
            ff = jnp.where(live, f, nf - 1)
            ii = jnp.where(live, i, nu[0] - 1)
            return (w_layer, be[ii], 0, ff) if kind == "up" else (w_layer, be[ii], ff, 0)
        return index_map

    idx_spec = pl.BlockSpec((None, 1, GROUPS_PER_BLOCK), lambda i, f, be, nu, nv: (i, 0, 0), memory_space=pltpu.SMEM)
    next_spec = pl.BlockSpec((None, 1, GROUPS_PER_BLOCK),
                             lambda i, f, be, nu, nv: (jnp.minimum(i + 1, n_blocks - 1), 0, 0),
                             memory_space=pltpu.SMEM)
    grid_spec = pltpu.PrefetchScalarGridSpec(
        num_scalar_prefetch=3,
        grid=(n_blocks, nf),
        in_specs=[idx_spec, next_spec,
                  pl.BlockSpec(memory_space=pl.ANY),
                  pl.BlockSpec((None, None, D_MODEL, tf), wmap("up")),
                  pl.BlockSpec((None, None, D_MODEL, tf), wmap("up")),
                  pl.BlockSpec((None, None, tf, D_MODEL), wmap("down"))],
        out_specs=pl.BlockSpec(memory_space=pl.ANY),
        scratch_shapes=[pltpu.VMEM((2, tm, D_MODEL), BF16), pltpu.VMEM((tm, D_MODEL), F32),
                        pltpu.VMEM((2, tm, D_MODEL), BF16),
                        pltpu.SemaphoreType.DMA((2,)), pltpu.SemaphoreType.DMA((2,))],
    )
    return pl.pallas_call(
        body,
        grid_spec=grid_spec,
        out_shape=jax.ShapeDtypeStruct(a_sorted.shape, BF16),
        input_output_aliases={5: 0},
        compiler_params=_cparams("arbitrary", "arbitrary"),
        name="moe_experts",
    )(block_e, n_used, n_valid, rows, rows, a_sorted, w1, w3, w2)


def _unsort_body(h_ref, ys_ref, meta_ref, mod_ref, fg_ref, o_ref, *, tm, rows_per_batch, final):
    i = pl.program_id(0)
    row = ((i * tm) // rows_per_batch) if rows_per_batch else CTX_MOD_ROW
    meta = meta_ref[...]
    slot0 = meta[:, META_SLOT0:META_SLOT0 + 1]
    slot1 = meta[:, META_SLOT1:META_SLOT1 + 1]
    gate0 = meta[:, META_GATE0:META_GATE0 + 1]
    gate1 = meta[:, META_GATE1:META_GATE1 + 1]
    j_io = lax.broadcasted_iota(jnp.int32, (tm, SORT_SLOTS), 1).astype(F32)
    pick = jnp.where(j_io == slot0, gate0, jnp.where(j_io == slot1, gate1, 0.0)).astype(BF16)
    fsum = jnp.dot(pick, ys_ref[...], preferred_element_type=F32)
    out = h_ref[...] + _mod_chunk(mod_ref, row, 5) * fsum
    if final:
        ms = jnp.mean(out * out, axis=-1, keepdims=True)
        out = out * lax.rsqrt(ms + NORM_EPS) * fg_ref[...]
    o_ref[...] = out


def moe_unsort_combine(h, ys, meta, mods, layer, final_g, *, tile_off, rows_per_batch, final):
    tm = SORT_TOKENS
    m = h.shape[0]
    body = functools.partial(_unsort_body, tm=tm, rows_per_batch=rows_per_batch, final=final)
    return pl.pallas_call(
        body,
        grid=(m // tm,),
        in_specs=[pl.BlockSpec((tm, D_MODEL), lambda i: (i, 0)),
                  pl.BlockSpec((SORT_SLOTS, D_MODEL), lambda i: (tile_off + i, 0)),
                  pl.BlockSpec((tm, LANES), lambda i: (tile_off + i, 0)),
                  _mod_spec(layer),
                  pl.BlockSpec((1, D_MODEL), lambda i: (0, 0))],
        out_specs=pl.BlockSpec((tm, D_MODEL), lambda i: (i, 0)),
        out_shape=jax.ShapeDtypeStruct((m, D_MODEL), F32),
        compiler_params=_cparams("arbitrary"),
        name="moe_combine",
    )(h, ys, meta, mods, final_g.reshape(1, D_MODEL))


def _na_layer(h_lat, h_ctx, mods, layer, g, w_qkv, b_qkv, rpb, w_out, b_out, with_ctx_out):
    b, seq, _ = h_lat.shape
    c = h_ctx.shape[1]
    qscale = jnp.concatenate([jnp.full((D_MODEL,), (D_MODEL // NA_HEADS) ** -0.5, F32), jnp.ones((2 * D_MODEL,), F32)])
    w = (w_qkv * qscale).astype(BF16)
    bias = b_qkv * qscale
    qkv = nm_matmul(h_lat.reshape(b * seq, D_MODEL), mods, layer, g, w, bias, rows_per_batch=seq, sh=0, sc=1)
    qkv_c = nm_matmul(h_ctx.reshape(b * c, D_MODEL), mods, layer, g, w, bias, rows_per_batch=None, sh=0, sc=1)
    qkv = qkv.reshape(b, seq, 3 * D_MODEL)
    qkv_c = qkv_c.reshape(b, c, 3 * D_MODEL)
    o_lat = na_attention(qkv, qkv_c, na_bias_table(rpb))
    wo = w_out.astype(BF16)
    h_lat = mm_residual(o_lat.reshape(b * seq, D_MODEL), h_lat.reshape(b * seq, D_MODEL), mods, layer, wo, b_out,
                        rows_per_batch=seq, gate=2).reshape(b, seq, D_MODEL)
    if with_ctx_out:
        o_ctx = ctx_attention(qkv_c)
        h_ctx = mm_residual(o_ctx.reshape(b * c, D_MODEL), h_ctx.reshape(b * c, D_MODEL), mods, layer, wo, b_out,
                            rows_per_batch=None, gate=2).reshape(b, c, D_MODEL)
    return h_lat, h_ctx


def _mlstm_layer(h_lat, h_ctx, mods, layer, g, w_up, conv_w, conv_b, w_q, w_k, w_v, w_gates, b_gates,
                 gn_w, skip, w_down, ffn, with_ctx_out):
    b, seq, _ = h_lat.shape
    c = h_ctx.shape[1]
    inner = MLSTM_INNER
    wu = w_up.astype(BF16)
    zero_b = jnp.zeros((2 * inner,), F32)
    wq_bd = block_diag_weights(w_q).astype(BF16)
    wk_bd = block_diag_weights(w_k).astype(BF16)
    wkt_bd = jnp.swapaxes(wk_bd, 1, 2)
    wv_bd = block_diag_weights(w_v).astype(BF16)
    gxc, gxm = fold_gate_weights(w_q, w_k, w_v, w_gates)
    wd = w_down.astype(BF16)

    def features(h, rows_per_batch):
        n, s, _ = h.shape
        up = nm_matmul(h.reshape(n * s, D_MODEL), mods, layer, g, wu, zero_b, rows_per_batch=rows_per_batch, sh=0, sc=1)
        up = up.reshape(n, s, 2 * inner)
        return up, mlstm_features(up, conv_w, conv_b, wq_bd, wk_bd, wkt_bd, wv_bd, gxc, gxm, b_gates)

    up_c, (q_c, k_c, kt_c, v_c, xc_c, g_c, gt_c) = features(h_ctx, None)
    up_l, (q_l, k_l, kt_l, v_l, xc_l, g_l, gt_l) = features(h_lat, seq)
    hf_c, cf, mf = mlstm_scan(q_c, k_c, kt_c, v_c, g_c, gt_c, None, rev=False)
    hb_c, cb, mb = mlstm_scan(q_c, k_c, kt_c, v_c, g_c, gt_c, None, rev=True)
    hf_l, _, _ = mlstm_scan(q_l, k_l, kt_l, v_l, g_l, gt_l, (cf, mf), rev=False)
    hb_l, _, _ = mlstm_scan(q_l, k_l, kt_l, v_l, g_l, gt_l, (cb, mb), rev=True)
    flat = lambda a: a.reshape(-1, a.shape[-1])
    h_lat = mlstm_output(flat(hf_l), flat(hb_l), flat(xc_l), flat(up_l), flat(h_lat), mods, layer, gn_w, skip, wd,
                         ffn, rows_per_batch=seq).reshape(b, seq, D_MODEL)
    if with_ctx_out:
        h_ctx = mlstm_output(flat(hf_c), flat(hb_c), flat(xc_c), flat(up_c), flat(h_ctx), mods, layer, gn_w, skip, wd,
                             ffn, rows_per_batch=None).reshape(b, c, D_MODEL)
    return h_lat, h_ctx


def _moe_layer(h_lat, h_ctx, mods, layer, g, w_router, w1, w3, w2, w_layer, final_g, last):
    b, seq, _ = h_lat.shape
    c = h_ctx.shape[1]
    wrt = jnp.pad(w_router.T, ((0, EXPERT_ROWS - N_EXPERTS), (0, 0)))
    hl = h_lat.reshape(b * seq, D_MODEL)
    hc = None if last else h_ctx.reshape(b * c, D_MODEL)
    a_sorted, meta, cnt = moe_sort(hl, hc, mods, layer, g, wrt, rows_per_batch=seq)
    block_e, n_used, n_valid, rows = moe_group_table(cnt)
    ys = moe_experts_sorted(a_sorted, block_e, n_used, n_valid, rows, w1, w3, w2, w_layer)
    h_lat = moe_unsort_combine(hl, ys, meta, mods, layer, final_g, tile_off=0, rows_per_batch=seq,
                               final=last).reshape(b, seq, D_MODEL)
    if not last:
        h_ctx = moe_unsort_combine(hc, ys, meta, mods, layer, final_g, tile_off=(b * seq) // SORT_TOKENS,
                                   rows_per_batch=None, final=False).reshape(b, c, D_MODEL)
    return h_lat, h_ctx


def kernel(x, c, ctx, c_ctx, w_mod, b_mod, norm_g, final_g, na_w_qkv, na_b_qkv, na_rpb, na_w_out, na_b_out,
           pool_w, pool_scale, ml_w_up, ml_conv_w, ml_conv_b, ml_w_q, ml_w_k, ml_w_v, ml_w_gates, ml_b_gates,
           ml_gn_w, ml_skip, ml_w_down, ffn_w1, ffn_w3, ffn_w2, moe_w_router, moe_w1, moe_w3, moe_w2):
    b, seq, _ = x.shape
    n_ctx = ctx.shape[1]
    depth = w_mod.shape[0]
    assert b <= CTX_MOD_ROW
    cond = jnp.zeros((MOD_ROWS, D_MODEL), F32).at[:b].set(c).at[CTX_MOD_ROW].set(c_ctx)
    mods = adaln_all(cond, w_mod, b_mod)
    moe_w1b, moe_w3b, moe_w2b = moe_w1.astype(BF16), moe_w3.astype(BF16), moe_w2.astype(BF16)
    h_lat, h_ctx = x, ctx
    for i in range(depth):
        last = i == depth - 1
        kind = i % 3
        j = i // 3
        g_tok = norm_g[i, 0]
        e = i // 2
        g_ch = norm_g[i, 1]
        dense = i % 2 == 0
        ffn = (g_ch, ffn_w1[e].astype(BF16), ffn_w3[e].astype(BF16), ffn_w2[e].astype(BF16)) if dense else None
        if kind == 0:
            h_lat, h_ctx = _na_layer(h_lat, h_ctx, mods, i, g_tok, na_w_qkv[j], na_b_qkv[j], na_rpb[j],
                                     na_w_out[j], na_b_out[j], not last)
        elif kind == 1:
            wp = pool_w[j].astype(BF16)
            h_lat = pool_mixer(h_lat, mods, i, g_tok, wp, pool_scale[j], is_ctx=False)
            if not last:
                h_ctx = pool_mixer(h_ctx, mods, i, g_tok, wp, pool_scale[j], is_ctx=True)
        else:
            h_lat, h_ctx = _mlstm_layer(h_lat, h_ctx, mods, i, g_tok, ml_w_up[j], ml_conv_w[j], ml_conv_b[j],
                                        ml_w_q[j], ml_w_k[j], ml_w_v[j], ml_w_gates[j], ml_b_gates[j],
                                        ml_gn_w[j], ml_skip[j], ml_w_down[j], ffn, not last)
        if dense:
            if kind != 2:
                h_lat = ffn_dense(h_lat.reshape(b * seq, D_MODEL), mods, i, *ffn,
                                  rows_per_batch=seq).reshape(b, seq, D_MODEL)
                if not last:
                    h_ctx = ffn_dense(h_ctx.reshape(b * n_ctx, D_MODEL), mods, i, *ffn,
                                      rows_per_batch=None).reshape(b, n_ctx, D_MODEL)
        else:
            h_lat, h_ctx = _moe_layer(h_lat, h_ctx, mods, i, g_ch, moe_w_router[e], moe_w1b, moe_w3b, moe_w2b, e,
                                      final_g, last)
    return h_lat
```

```python
import functools

import jax
import jax.numpy as jnp
from jax import lax
from jax.experimental import pallas as pl
from jax.experimental.pallas import tpu as pltpu

F32 = jnp.float32
BF16 = jnp.bfloat16

D_MODEL = 1024
N_MOD = 6
NORM_EPS = 1e-6
GRID_W = 64
NA_HEADS = 16
NA_WIN_ROWS = 8
NA_WIN_COLS = 16
POOL_WINDOWS = (2, 4, 8, 16)
POOL_GROUP_DIM = D_MODEL // len(POOL_WINDOWS)
MLSTM_INNER = 2 * D_MODEL
MLSTM_HEADS = 4
MLSTM_HEAD_DIM = MLSTM_INNER // MLSTM_HEADS
MLSTM_CONV = 4
MLSTM_QKV_BLOCK = 4
N_EXPERTS = 8
TOP_K = 2

LANES = 128
MOD_ROWS = 8
CTX_MOD_ROW = 4
VMEM_LIMIT_BYTES = 56 * 1024 * 1024
NEG_BIG = -1e30
SCAN_CHUNK = 256
SCAN_STATE_ROWS = MLSTM_HEAD_DIM + 8
MOE_ROWS = 1024


def _cparams(*sem):
    return pltpu.CompilerParams(dimension_semantics=sem, vmem_limit_bytes=VMEM_LIMIT_BYTES)


def _resident(shape, index_map):
    return pl.BlockSpec(shape, index_map, pipeline_mode=pl.Buffered(1))


def _silu(x):
    return x * jax.nn.sigmoid(x)


def _norm_mod(x, g, shift, scale):
    ms = jnp.mean(x * x, axis=-1, keepdims=True)
    y = x * lax.rsqrt(ms + NORM_EPS) * g
    return y * (1.0 + scale) + shift


def _mod_chunk(mod_ref, row, j):
    return mod_ref[pl.ds(row, 1), pl.ds(j * D_MODEL, D_MODEL)]


def _mod_spec(layer):
    return pl.BlockSpec((None, MOD_ROWS, N_MOD * D_MODEL), lambda *_: (layer, 0, 0))


def _adaln_body(c_ref, w_ref, b_ref, o_ref):
    s = _silu(c_ref[...])
    o_ref[...] = jnp.dot(s, w_ref[...], preferred_element_type=F32) + b_ref[...]


def adaln_all(cond, w_mod, b_mod):
    depth = w_mod.shape[0]
    n = N_MOD * D_MODEL
    tn = 1536
    return pl.pallas_call(
        _adaln_body,
        grid=(depth, n // tn),
        in_specs=[pl.BlockSpec((MOD_ROWS, D_MODEL), lambda l, j: (0, 0)),
                  pl.BlockSpec((None, D_MODEL, tn), lambda l, j: (l, 0, j)),
                  pl.BlockSpec((None, 1, tn), lambda l, j: (l, 0, j))],
        out_specs=pl.BlockSpec((None, MOD_ROWS, tn), lambda l, j: (l, 0, j)),
        out_shape=jax.ShapeDtypeStruct((depth, MOD_ROWS, n), F32),
        compiler_params=_cparams("arbitrary", "arbitrary"),
        name="adaln",
    )(cond, w_mod, b_mod.reshape(depth, 1, n))


NM_COLS = 1024


def _nm_project(x, mod_ref, row, g_ref, w_ref, b_ref, o_ref, sh, sc):
    a = _norm_mod(x, g_ref[...], _mod_chunk(mod_ref, row, sh), _mod_chunk(mod_ref, row, sc)).astype(BF16)
    for c in range(o_ref.shape[1] // NM_COLS):
        sl = slice(c * NM_COLS, (c + 1) * NM_COLS)
        y = jnp.dot(a, w_ref[:, sl], preferred_element_type=F32) + b_ref[:, sl]
        o_ref[:, sl] = y.astype(o_ref.dtype)


def _nm_matmul_body(x_ref, mod_ref, g_ref, w_ref, b_ref, o_ref, *, tm, rows_per_batch, sh, sc):
    i = pl.program_id(0)
    row = (i * tm) // rows_per_batch if rows_per_batch else CTX_MOD_ROW
    _nm_project(x_ref[...], mod_ref, row, g_ref, w_ref, b_ref, o_ref, sh, sc)


def nm_matmul(x, mods, layer, g, w, bias, *, rows_per_batch, sh, sc, tm=512, out_dtype=BF16):
    m, n = x.shape[0], w.shape[1]
    tm = min(tm, m)
    body = functools.partial(_nm_matmul_body, tm=tm, rows_per_batch=rows_per_batch, sh=sh, sc=sc)
    return pl.pallas_call(
        body,
        grid=(m // tm,),
        in_specs=[pl.BlockSpec((tm, D_MODEL), lambda i: (i, 0)),
                  _mod_spec(layer),
                  pl.BlockSpec((1, D_MODEL), lambda i: (0, 0)),
                  _resident((D_MODEL, n), lambda i: (0, 0)),
                  pl.BlockSpec((1, n), lambda i: (0, 0))],
        out_specs=pl.BlockSpec((tm, n), lambda i: (i, 0)),
        out_shape=jax.ShapeDtypeStruct((m, n), out_dtype),
        compiler_params=_cparams("arbitrary"),
        name="nm_matmul",
    )(x, mods, g.reshape(1, D_MODEL), w, bias.reshape(1, n))


def _mm_res_body(a_ref, h_ref, mod_ref, w_ref, b_ref, *rest, tm, rows_per_batch, gate):
    o_ref = rest[-1]
    i = pl.program_id(0)
    row = (i * tm) // rows_per_batch if rows_per_batch else CTX_MOD_ROW
    y = jnp.dot(a_ref[...], w_ref[...], preferred_element_type=F32) + b_ref[...]
    h1 = h_ref[...] + _mod_chunk(mod_ref, row, gate) * y
    if len(rest) > 1:
        h1 = _ffn_rows(h1, mod_ref, row, *rest[:4])
    o_ref[...] = h1


def mm_residual(a, h, mods, layer, w, bias, ffn=None, *, rows_per_batch, gate):
    m, k = a.shape
    tm = min(512 if ffn is None else 256, m)
    body = functools.partial(_mm_res_body, tm=tm, rows_per_batch=rows_per_batch, gate=gate)
    in_specs = [pl.BlockSpec((tm, k), lambda i: (i, 0)),
                pl.BlockSpec((tm, D_MODEL), lambda i: (i, 0)),
                _mod_spec(layer),
                _resident((k, D_MODEL), lambda i: (0, 0)),
                pl.BlockSpec((1, D_MODEL), lambda i: (0, 0))]
    args = [a, h, mods, w, bias.reshape(1, D_MODEL)]
    if ffn is not None:
        in_specs += _ffn_specs(ffn)
        args += _ffn_args(ffn)
    return pl.pallas_call(
        body,
        grid=(m // tm,),
        in_specs=in_specs,
        out_specs=pl.BlockSpec((tm, D_MODEL), lambda i: (i, 0)),
        out_shape=jax.ShapeDtypeStruct((m, D_MODEL), F32),
        compiler_params=_cparams("arbitrary"),
        name="mm_residual",
    )(*args)


def _ffn_rows(h, mod_ref, row, g_ref, w1_ref, w3_ref, w2_ref):
    a = _norm_mod(h, g_ref[...], _mod_chunk(mod_ref, row, 3), _mod_chunk(mod_ref, row, 4)).astype(BF16)
    u = jnp.dot(a, w1_ref[...], preferred_element_type=F32)
    v = jnp.dot(a, w3_ref[...], preferred_element_type=F32)
    p = (_silu(u) * v).astype(BF16)
    y = jnp.dot(p, w2_ref[...], preferred_element_type=F32)
    return h + _mod_chunk(mod_ref, row, 5) * y


def _ffn_specs(ffn):
    f = ffn[1].shape[1]
    return [pl.BlockSpec((1, D_MODEL), lambda i: (0, 0)),
            _resident((D_MODEL, f), lambda i: (0, 0)),
            _resident((D_MODEL, f), lambda i: (0, 0)),
            _resident((f, D_MODEL), lambda i: (0, 0))]


def _ffn_args(ffn):
    g, w1, w3, w2 = ffn
    return [g.reshape(1, D_MODEL), w1, w3, w2]


def _ffn_body(h_ref, mod_ref, g_ref, w1_ref, w3_ref, w2_ref, o_ref, *, tm, rows_per_batch):
    i = pl.program_id(0)
    row = (i * tm) // rows_per_batch if rows_per_batch else CTX_MOD_ROW
    o_ref[...] = _ffn_rows(h_ref[...], mod_ref, row, g_ref, w1_ref, w3_ref, w2_ref)


def ffn_dense(h, mods, layer, g, w1, w3, w2, *, rows_per_batch, tm=256):
    m = h.shape[0]
    f = w1.shape[1]
    tm = min(tm, m)
    body = functools.partial(_ffn_body, tm=tm, rows_per_batch=rows_per_batch)
    return pl.pallas_call(
        body,
        grid=(m // tm,),
        in_specs=[pl.BlockSpec((tm, D_MODEL), lambda i: (i, 0)),
                  _mod_spec(layer),
                  pl.BlockSpec((1, D_MODEL), lambda i: (0, 0)),
                  _resident((D_MODEL, f), lambda i: (0, 0)),
                  _resident((D_MODEL, f), lambda i: (0, 0)),
                  _resident((f, D_MODEL), lambda i: (0, 0))],
        out_specs=pl.BlockSpec((tm, D_MODEL), lambda i: (i, 0)),
        out_shape=jax.ShapeDtypeStruct((m, D_MODEL), F32),
        compiler_params=_cparams("arbitrary"),
        name="ffn_dense",
    )(h, mods, g.reshape(1, D_MODEL), w1, w3, w2)


def na_bias_table(rpb):
    h = rpb.shape[0]
    col = jnp.arange(GRID_W)
    c0 = jnp.clip(col - NA_WIN_COLS // 2, 0, GRID_W - NA_WIN_COLS)
    col_ok = (col[None, :] >= c0[:, None]) & (col[None, :] < c0[:, None] + NA_WIN_COLS)
    dcol = jnp.clip(col[None, :] - col[:, None], 1 - NA_WIN_COLS, NA_WIN_COLS - 1) + (NA_WIN_COLS - 1)
    n_drow = 2 * NA_WIN_ROWS - 1
    t = jnp.where(col_ok[None, None], rpb[:, :, dcol].astype(F32), NEG_BIG)
    t = t.transpose(0, 2, 1, 3).reshape(h // 2, 2 * GRID_W, n_drow * GRID_W)
    tiles = [t[:, :, (NA_WIN_ROWS - 1 - off) * GRID_W:(2 * NA_WIN_ROWS - 1 - off) * GRID_W]
             for off in range(NA_WIN_ROWS)]
    return jnp.stack(tiles, axis=1)


def _stack_heads(q):
    lo = lax.broadcasted_iota(jnp.int32, q.shape, 1) < (LANES // 2)
    zero = jnp.zeros_like(q)
    return jnp.concatenate([jnp.where(lo, q, zero), jnp.where(lo, zero, q)], axis=0)


def _unstack_heads(o):
    n = o.shape[0] // 2
    lo = lax.broadcasted_iota(jnp.int32, (n, LANES), 1) < (LANES // 2)
    return jnp.where(lo, o[:n], o[n:])


_NT = (((1,), (1,)), ((), ()))
NA_GROUP = 16


def _na_body(q_ref, k_ref, v_ref, kc_ref, vc_ref, bias_ref, o_ref, *, rows):
    kc = kc_ref[...]
    vc = vc_ref[...]
    kwin = NA_WIN_ROWS * GRID_W

    def scores(r):
        r0 = jnp.clip(r - NA_WIN_ROWS // 2, 0, rows - NA_WIN_ROWS)
        qoff = pl.multiple_of(r * GRID_W, GRID_W)
        koff = pl.multiple_of(r0 * GRID_W, GRID_W)
        qs = _stack_heads(q_ref[pl.ds(qoff, GRID_W), :])
        k = k_ref[pl.ds(koff, kwin), :]
        s_loc = lax.dot_general(qs, k, _NT, preferred_element_type=F32) + bias_ref[r - r0]
        s_ctx = lax.dot_general(qs, kc, _NT, preferred_element_type=F32)
        return qoff, koff, s_loc, s_ctx

    def probs(s_loc, s_ctx):
        m = jnp.maximum(jnp.max(s_loc, axis=-1, keepdims=True), jnp.max(s_ctx, axis=-1, keepdims=True))
        p_loc = jnp.exp(s_loc - m)
        p_ctx = jnp.exp(s_ctx - m)
        l = jnp.sum(p_loc, axis=-1, keepdims=True) + jnp.sum(p_ctx, axis=-1, keepdims=True)
        return p_loc.astype(BF16), p_ctx.astype(BF16), l

    def group(gi, carry):
        sc = [scores(gi * NA_GROUP + u) for u in range(NA_GROUP)]
        pr = [probs(s[2], s[3]) for s in sc]
        for (qoff, koff, _, _), (p_loc, p_ctx, l) in zip(sc, pr):
            v = v_ref[pl.ds(koff, kwin), :]
            o = (jnp.dot(p_loc, v, preferred_element_type=F32)
                 + jnp.dot(p_ctx, vc, preferred_element_type=F32)) / l
            o_ref[pl.ds(qoff, GRID_W), :] = _unstack_heads(o).astype(o_ref.dtype)
        return carry

    lax.fori_loop(0, rows // NA_GROUP, group, 0)


def na_attention(qkv, qkv_ctx, bias):
    b, l, _ = qkv.shape
    c = qkv_ctx.shape[1]
    hp = D_MODEL // LANES
    body = functools.partial(_na_body, rows=l // GRID_W)
    return pl.pallas_call(
        body,
        grid=(b, hp),
        in_specs=[pl.BlockSpec((None, l, LANES), lambda i, j: (i, 0, j)),
                  pl.BlockSpec((None, l, LANES), lambda i, j: (i, 0, hp + j)),
                  pl.BlockSpec((None, l, LANES), lambda i, j: (i, 0, 2 * hp + j)),
                  pl.BlockSpec((None, c, LANES), lambda i, j: (i, 0, hp + j)),
                  pl.BlockSpec((None, c, LANES), lambda i, j: (i, 0, 2 * hp + j)),
                  pl.BlockSpec((None,) + bias.shape[1:], lambda i, j: (j, 0, 0, 0))],
        out_specs=pl.BlockSpec((None, l, LANES), lambda i, j: (i, 0, j)),
        out_shape=jax.ShapeDtypeStruct((b, l, D_MODEL), BF16),
        compiler_params=_cparams("arbitrary", "arbitrary"),
        name="na_attention",
    )(qkv, qkv, qkv, qkv_ctx, qkv_ctx, bias)


def _ctx_attn_body(q_ref, k_ref, v_ref, o_ref):
    qs = _stack_heads(q_ref[...])
    s = lax.dot_general(qs, k_ref[...], _NT, preferred_element_type=F32)
    p = jnp.exp(s - jnp.max(s, axis=-1, keepdims=True))
    l = jnp.sum(p, axis=-1, keepdims=True)
    o = jnp.dot(p.astype(BF16), v_ref[...], preferred_element_type=F32) / l
    o_ref[...] = _unstack_heads(o).astype(o_ref.dtype)


def ctx_attention(qkv_ctx):
    b, c, _ = qkv_ctx.shape
    hp = D_MODEL // LANES
    return pl.pallas_call(
        _ctx_attn_body,
        grid=(b, hp),
        in_specs=[pl.BlockSpec((None, c, LANES), lambda i, j: (i, 0, j)),
                  pl.BlockSpec((None, c, LANES), lambda i, j: (i, 0, hp + j)),
                  pl.BlockSpec((None, c, LANES), lambda i, j: (i, 0, 2 * hp + j))],
        out_specs=pl.BlockSpec((None, c, LANES), lambda i, j: (i, 0, j)),
        out_shape=jax.ShapeDtypeStruct((b, c, D_MODEL), BF16),
        compiler_params=_cparams("arbitrary", "arbitrary"),
        name="ctx_attention",
    )(qkv_ctx, qkv_ctx, qkv_ctx)


POOL_HALO = 8


def _pool_body(prev_ref, cur_ref, next_ref, mod_ref, g_ref, wp_ref, ps_ref, o_ref, *, tl, seq, is_ctx):
    b = pl.program_id(0)
    j = pl.program_id(1)
    row = CTX_MOD_ROW if is_ctx else b
    g = g_ref[...]
    sh = _mod_chunk(mod_ref, row, 0)
    sc = _mod_chunk(mod_ref, row, 1)
    h = cur_ref[...]
    a_cur = _norm_mod(h, g, sh, sc)
    a_prev = _norm_mod(prev_ref[...], g, sh, sc) * (j > 0).astype(F32)
    a_next = _norm_mod(next_ref[...], g, sh, sc) * (j < seq // tl - 1).astype(F32)
    ext = jnp.concatenate([a_prev, a_cur, a_next], axis=0)
    t = j * tl + lax.broadcasted_iota(jnp.int32, (tl, 1), 0)
    outs = []
    for gi, w in enumerate(POOL_WINDOWS):
        sl = slice(gi * POOL_GROUP_DIM, (gi + 1) * POOL_GROUP_DIM)
        p = ext[:, sl]
        step = 1
        while step < w:
            n = p.shape[0]
            p = p[:n - step] + p[step:]
            step *= 2
        off = POOL_HALO - w // 2
        cnt = jnp.minimum(t + w // 2, seq) - jnp.maximum(t - w // 2, 0)
        pooled = p[off:off + tl] / cnt.astype(F32) - a_cur[:, sl]
        outs.append(jnp.dot(pooled.astype(BF16), wp_ref[gi], preferred_element_type=F32))
    y = jnp.concatenate(outs, axis=1) * ps_ref[...]
    o_ref[...] = h + _mod_chunk(mod_ref, row, 2) * y


def pool_mixer(h, mods, layer, g, w_pool, pool_scale, *, is_ctx, tl=512):
    b, seq, _ = h.shape
    tl = min(tl, seq)
    nh = tl // POOL_HALO
    last = seq // POOL_HALO - 1
    body = functools.partial(_pool_body, tl=tl, seq=seq, is_ctx=is_ctx)
    return pl.pallas_call(
        body,
        grid=(b, seq // tl),
        in_specs=[pl.BlockSpec((None, POOL_HALO, D_MODEL), lambda i, j: (i, jnp.maximum(j * nh - 1, 0), 0)),
                  pl.BlockSpec((None, tl, D_MODEL), lambda i, j: (i, j, 0)),
                  pl.BlockSpec((None, POOL_HALO, D_MODEL), lambda i, j: (i, jnp.minimum((j + 1) * nh, last), 0)),
                  _mod_spec(layer),
                  pl.BlockSpec((1, D_MODEL), lambda i, j: (0, 0)),
                  pl.BlockSpec(w_pool.shape, lambda i, j: (0, 0, 0)),
                  pl.BlockSpec((1, D_MODEL), lambda i, j: (0, 0))],
        out_specs=pl.BlockSpec((None, tl, D_MODEL), lambda i, j: (i, j, 0)),
        out_shape=jax.ShapeDtypeStruct(h.shape, F32),
        compiler_params=_cparams("arbitrary", "arbitrary"),
        name="pool_mixer",
    )(h, h, h, mods, g.reshape(1, D_MODEL), w_pool, pool_scale.reshape(1, D_MODEL))


CONV_HALO = 16


def block_diag_weights(w):
    nb = LANES // MLSTM_QKV_BLOCK
    wc = w.reshape(-1, nb, MLSTM_QKV_BLOCK, MLSTM_QKV_BLOCK)
    eye = jnp.eye(nb, dtype=w.dtype)
    bd = jnp.einsum("cnij,nm->cnimj", wc, eye)
    return bd.reshape(-1, LANES, LANES)


def fold_gate_weights(w_q, w_k, w_v, w_gates):
    ng = w_gates.shape[1]
    wg = w_gates.reshape(3, -1, MLSTM_QKV_BLOCK, ng)
    fold = lambda w, part: jnp.einsum("ncd,ndg->ncg", w, wg[part], precision=lax.Precision.HIGHEST).reshape(-1, ng)
    return fold(w_q, 0) + fold(w_k, 1), fold(w_v, 2)


def _ml_feat_body(prev_ref, cur_ref, next_ref, cw_ref, cb_ref, wq_ref, wk_ref, wkt_ref, wv_ref,
                  gxc_ref, gxm_ref, gxct_ref, gxmt_ref, bg_ref, bgt_ref,
                  q_ref, k_ref, kt_ref, v_ref, xc_ref, g_ref, gt_ref, ext_ref, *, tl, seq):
    j = pl.program_id(1)
    cur = cur_ref[...]
    ext_ref[0:CONV_HALO, :] = prev_ref[...].astype(F32) * (j > 0).astype(F32)
    ext_ref[CONV_HALO:CONV_HALO + tl, :] = cur.astype(F32)
    ext_ref[CONV_HALO + tl:, :] = next_ref[...].astype(F32) * (j < seq // tl - 1).astype(F32)
    left = MLSTM_CONV // 2
    xc = cb_ref[...]
    for tap in range(MLSTM_CONV):
        xc = xc + ext_ref[pl.ds(CONV_HALO - left + tap, tl), :] * cw_ref[tap:tap + 1, :]
    xc = _silu(xc)
    xcb = xc.astype(BF16)
    xc_ref[...] = xcb
    t = SCAN_CHUNK
    qscale = MLSTM_HEAD_DIM ** -0.5
    for c in range(MLSTM_INNER // LANES):
        sl = slice(c * LANES, (c + 1) * LANES)
        xs = xcb[:, sl]
        q = jnp.dot(xs, wq_ref[c], preferred_element_type=F32)
        k = jnp.dot(xs, wk_ref[c], preferred_element_type=F32)
        v = jnp.dot(cur[:, sl], wv_ref[c], preferred_element_type=F32)
        q_ref[:, sl] = (q * qscale).astype(BF16)
        k_ref[:, sl] = k.astype(BF16)
        v_ref[:, sl] = v.astype(BF16)
        for cc in range(tl // t):
            kt = lax.dot_general(wkt_ref[c], xs[cc * t:(cc + 1) * t], _NT, preferred_element_type=F32)
            kt_ref[cc, sl, :] = kt.astype(BF16)
    ng = g_ref.shape[1]
    g = (jnp.dot(xcb, gxc_ref[...], preferred_element_type=F32)
         + jnp.dot(cur, gxm_ref[...], preferred_element_type=F32))
    g_ref[...] = g[:, :ng] + bg_ref[...]
    gt = (lax.dot_general(gxct_ref[...], xcb, _NT, preferred_element_type=F32)
          + lax.dot_general(gxmt_ref[...], cur, _NT, preferred_element_type=F32)) + bgt_ref[...]
    for cc in range(tl // t):
        gt_ref[cc] = gt[:, cc * t:(cc + 1) * t]


def mlstm_features(up, conv_w, conv_b, wq_bd, wk_bd, wkt_bd, wv_bd, gxc, gxm, bg, *, tl=512):
    b, seq, _ = up.shape
    tl = min(tl, seq)
    t = SCAN_CHUNK
    nh = tl // CONV_HALO
    last = seq // CONV_HALO - 1
    ng = gxc.shape[1]
    inner = MLSTM_INNER
    body = functools.partial(_ml_feat_body, tl=tl, seq=seq)
    full = lambda a: pl.BlockSpec(a.shape, lambda i, j: (0,) * a.ndim)
    cw = conv_w
    cb = conv_b.reshape(1, inner)
    bgr = bg.reshape(1, ng)
    bgc = bg.reshape(ng, 1)
    pad = lambda w: jnp.pad(w, ((0, 0), (0, LANES - ng))).astype(BF16)
    gxc_p, gxm_p = pad(gxc), pad(gxm)
    gxc_t, gxm_t = gxc.T.astype(BF16), gxm.T.astype(BF16)
    return pl.pallas_call(
        body,
        grid=(b, seq // tl),
        in_specs=[pl.BlockSpec((None, CONV_HALO, inner), lambda i, j: (i, jnp.maximum(j * nh - 1, 0), 0)),
                  pl.BlockSpec((None, tl, inner), lambda i, j: (i, j, 0)),
                  pl.BlockSpec((None, CONV_HALO, inner), lambda i, j: (i, jnp.minimum((j + 1) * nh, last), 0)),
                  full(cw), full(cb), full(wq_bd), full(wk_bd), full(wkt_bd), full(wv_bd),
                  full(gxc_p), full(gxm_p), full(gxc_t), full(gxm_t), full(bgr), full(bgc)],
        out_specs=[pl.BlockSpec((None, tl, inner), lambda i, j: (i, j, 0)),
                   pl.BlockSpec((None, tl, inner), lambda i, j: (i, j, 0)),
                   pl.BlockSpec((None, tl // t, inner, t), lambda i, j: (i, j, 0, 0)),
                   pl.BlockSpec((None, tl, inner), lambda i, j: (i, j, 0)),
                   pl.BlockSpec((None, tl, inner), lambda i, j: (i, j, 0)),
                   pl.BlockSpec((None, tl, ng), lambda i, j: (i, j, 0)),
                   pl.BlockSpec((None, tl // t, ng, t), lambda i, j: (i, j, 0, 0))],
        out_shape=[jax.ShapeDtypeStruct((b, seq, inner), BF16),
                   jax.ShapeDtypeStruct((b, seq, inner), BF16),
                   jax.ShapeDtypeStruct((b, seq // t, inner, t), BF16),
                   jax.ShapeDtypeStruct((b, seq, inner), BF16),
                   jax.ShapeDtypeStruct((b, seq, inner), BF16),
                   jax.ShapeDtypeStruct((b, seq, ng), F32),
                   jax.ShapeDtypeStruct((b, seq // t, ng, t), F32)],
        scratch_shapes=[pltpu.VMEM((tl + 2 * CONV_HALO, inner), F32)],
        compiler_params=_cparams("arbitrary", "arbitrary"),
        name="mlstm_features",
    )(up, up, up, cw, cb, wq_bd, wk_bd, wkt_bd, wv_bd, gxc_p, gxm_p, gxc_t, gxm_t, bgr, bgc)


def _log_sigmoid(x):
    return jnp.minimum(x, 0.0) - jnp.log1p(jnp.exp(-jnp.abs(x)))


def _scan_body(*refs, rev, nchunk, nblk, has_init):
    if has_init:
        q_ref, k_ref, kt_ref, v_ref, g_ref, gt_ref, c0_ref, m0_ref, h_ref, cf_ref, mf_ref, c_sc, m_sc = refs
    else:
        q_ref, k_ref, kt_ref, v_ref, g_ref, gt_ref, h_ref, cf_ref, mf_ref, c_sc, m_sc = refs
    hd = pl.program_id(1)
    j = pl.program_id(2)
    t = SCAN_CHUNK
    dh = MLSTM_HEAD_DIM

    @pl.when(j == 0)
    def _():
        if has_init:
            c_sc[...] = c0_ref[...]
            m_sc[...] = m0_ref[...]
        else:
            c_sc[...] = jnp.zeros_like(c_sc)
            m_sc[...] = jnp.zeros_like(m_sc)

    ci = (2 if rev else 0) * MLSTM_HEADS + hd
    cf = (3 if rev else 1) * MLSTM_HEADS + hd
    ng = g_ref.shape[1]
    lane = lax.broadcasted_iota(jnp.int32, (t, ng), 1)
    r_io = lax.broadcasted_iota(jnp.int32, (t, t), 0)
    c_io = lax.broadcasted_iota(jnp.int32, (t, t), 1)
    seen = (c_io >= r_io) if rev else (c_io <= r_io)
    seen_t = (r_io >= c_io) if rev else (r_io <= c_io)
    seen_f = seen.astype(F32)
    seen_tf = seen_t.astype(F32)

    order = range(nchunk - 1, -1, -1) if rev else range(nchunk)
    for cc in order:
        rows = slice(cc * t, (cc + 1) * t)
        q = q_ref[rows, :]
        kt = kt_ref[cc]
        v = v_ref[rows, :]
        g = g_ref[rows, :]
        i_col = jnp.sum(jnp.where(lane == ci, g, 0.0), axis=1, keepdims=True)
        f_col = jnp.sum(jnp.where(lane == cf, g, 0.0), axis=1, keepdims=True)
        i_row = gt_ref[cc, pl.ds(ci, 1), :]
        f_row = gt_ref[cc, pl.ds(cf, 1), :]
        lf_col = _log_sigmoid(f_col)
        lf_row = _log_sigmoid(f_row)
        b_col = jnp.sum(seen_f * lf_row, axis=1, keepdims=True)
        b_row = jnp.sum(seen_tf * lf_col, axis=0, keepdims=True)
        m_prev = m_sc[0:1, 0:1]
        n_row = c_sc[dh:dh + 1, :]
        dmat = jnp.where(seen, b_col - b_row + i_row, NEG_BIG)
        inter = b_col + m_prev
        m_t = jnp.maximum(inter, jnp.max(dmat, axis=1, keepdims=True))
        a = jnp.dot(q, kt, preferred_element_type=F32) * jnp.exp(dmat - m_t)
        w_int = jnp.exp(inter - m_t)
        cb = c_sc[0:dh, :].astype(BF16)
        num = (jnp.dot(a.astype(BF16), v, preferred_element_type=F32)
               + jnp.dot(q, cb, preferred_element_type=F32) * w_int)
        den = (jnp.sum(a, axis=1, keepdims=True)
               + w_int * jnp.sum(q.astype(F32) * n_row, axis=1, keepdims=True))
        hc = num / jnp.maximum(jnp.abs(den), jnp.exp(-m_t))
        h_ref[rows, :] = hc.astype(h_ref.dtype)
        b_end = jnp.sum(lf_row, axis=1, keepdims=True)
        g_row = b_end - b_row + i_row
        m_new = jnp.maximum(b_end + m_prev, jnp.max(g_row, axis=1, keepdims=True))
        decay = jnp.exp(b_end + m_prev - m_new)
        kw = (kt.astype(F32) * jnp.exp(g_row - m_new)).astype(BF16)
        w_col = jnp.exp(b_end - b_col + i_col - m_new)
        c_sc[0:dh, :] = decay * c_sc[0:dh, :] + jnp.dot(kw, v, preferred_element_type=F32)
        c_sc[dh:dh + 1, :] = decay * n_row + jnp.sum(k_ref[rows, :].astype(F32) * w_col, axis=0, keepdims=True)
        m_sc[...] = jnp.broadcast_to(m_new, m_sc.shape)

    @pl.when(j == nblk - 1)
    def _():
        cf_ref[...] = c_sc[...]
        mf_ref[...] = m_sc[...]


def mlstm_scan(q, k, kt, v, g, gt, state, *, rev, tb=1024):
    b, seq, inner = q.shape
    t = SCAN_CHUNK
    tb = min(tb, seq)
    nblk = seq // tb
    nchunk = tb // t
    dh = MLSTM_HEAD_DIM
    ng = g.shape[2]
    has_init = state is not None
    blk = (lambda j: nblk - 1 - j) if rev else (lambda j: j)
    body = functools.partial(_scan_body, rev=rev, nchunk=nchunk, nblk=nblk, has_init=has_init)
    tok_spec = pl.BlockSpec((None, tb, dh), lambda i, h, j: (i, blk(j), h))
    in_specs = [tok_spec, tok_spec,
                pl.BlockSpec((None, nchunk, dh, t), lambda i, h, j: (i, blk(j), h, 0)),
                tok_spec,
                pl.BlockSpec((None, tb, ng), lambda i, h, j: (i, blk(j), 0)),
                pl.BlockSpec((None, nchunk, ng, t), lambda i, h, j: (i, blk(j), 0, 0))]
    args = [q, k, kt, v, g, gt]
    st_spec_c = pl.BlockSpec((None, None, SCAN_STATE_ROWS, dh), lambda i, h, j: (i, h, 0, 0))
    st_spec_m = pl.BlockSpec((None, None, 8, LANES), lambda i, h, j: (i, h, 0, 0))
    if has_init:
        in_specs += [st_spec_c, st_spec_m]
        args += list(state)
    return pl.pallas_call(
        body,
        grid=(b, MLSTM_HEADS, nblk),
        in_specs=in_specs,
        out_specs=[tok_spec, st_spec_c, st_spec_m],
        out_shape=[jax.ShapeDtypeStruct((b, seq, inner), BF16),
                   jax.ShapeDtypeStruct((b, MLSTM_HEADS, SCAN_STATE_ROWS, dh), F32),
                   jax.ShapeDtypeStruct((b, MLSTM_HEADS, 8, LANES), F32)],
        scratch_shapes=[pltpu.VMEM((SCAN_STATE_ROWS, dh), F32), pltpu.VMEM((8, LANES), F32)],
        compiler_params=_cparams("arbitrary", "arbitrary", "arbitrary"),
        name="mlstm_scan_bwd" if rev else "mlstm_scan_fwd",
    )(*args)


ML_OUT_ROWS = 256


def _ml_out_body(hf_ref, hb_ref, xc_ref, z_ref, h_ref, mod_ref, gn_ref, sk_ref, w_ref, *rest, tm, rows_per_batch):
    o_ref = rest[-1]
    i = pl.program_id(0)
    row = (i * tm) // rows_per_batch if rows_per_batch else CTX_MOD_ROW

    def gated(rows):
        hs = hf_ref[rows, :].astype(F32) + hb_ref[rows, :].astype(F32)
        parts = []
        for hd in range(MLSTM_HEADS):
            x = hs[:, hd * MLSTM_HEAD_DIM:(hd + 1) * MLSTM_HEAD_DIM]
            mu = jnp.mean(x, axis=-1, keepdims=True)
            xm = x - mu
            var = jnp.mean(xm * xm, axis=-1, keepdims=True)
            parts.append(xm * lax.rsqrt(var + NORM_EPS))
        hn = jnp.concatenate(parts, axis=1) * gn_ref[...]
        y = (hn + sk_ref[...] * xc_ref[rows, :].astype(F32)) * _silu(z_ref[rows, :].astype(F32))
        return y.astype(BF16)

    groups = [slice(s * ML_OUT_ROWS, (s + 1) * ML_OUT_ROWS) for s in range(tm // ML_OUT_ROWS)]
    ys = [gated(rows) for rows in groups]
    for rows, y in zip(groups, ys):
        y = jnp.dot(y, w_ref[...], preferred_element_type=F32)
        h1 = h_ref[rows, :] + _mod_chunk(mod_ref, row, 2) * y
        if len(rest) > 1:
            g_ref, w1_ref, w3_ref, w2_ref = rest[:4]
            h1 = _ffn_rows(h1, mod_ref, row, g_ref, w1_ref, w3_ref, w2_ref)
        o_ref[rows, :] = h1


def mlstm_output(hf, hb, xc, up, h, mods, layer, gn_w, skip, w_down, ffn=None, *, rows_per_batch, tm=256):
    m = h.shape[0]
    tm = min(tm, m)
    inner = MLSTM_INNER
    body = functools.partial(_ml_out_body, tm=tm, rows_per_batch=rows_per_batch)
    row_spec = pl.BlockSpec((tm, inner), lambda i: (i, 0))
    in_specs = [row_spec, row_spec, row_spec,
                pl.BlockSpec((tm, inner), lambda i: (i, 1)),
                pl.BlockSpec((tm, D_MODEL), lambda i: (i, 0)),
                _mod_spec(layer),
                pl.BlockSpec((1, inner), lambda i: (0, 0)),
                pl.BlockSpec((1, inner), lambda i: (0, 0)),
                _resident((inner, D_MODEL), lambda i: (0, 0))]
    args = [hf, hb, xc, up, h, mods, gn_w.reshape(1, inner), skip.reshape(1, inner), w_down]
    if ffn is not None:
        in_specs += _ffn_specs(ffn)
        args += _ffn_args(ffn)
    return pl.pallas_call(
        body,
        grid=(m // tm,),
        in_specs=in_specs,
        out_specs=pl.BlockSpec((tm, D_MODEL), lambda i: (i, 0)),
        out_shape=jax.ShapeDtypeStruct((m, D_MODEL), F32),
        compiler_params=_cparams("arbitrary"),
        name="mlstm_output",
    )(*args)


SORT_TOKENS = 512
ROW_GROUP = 16
EXPERT_ROWS = 16
SORT_SLOTS = -(-(TOP_K * SORT_TOKENS + N_EXPERTS * (ROW_GROUP - 1)) // LANES) * LANES
GROUPS_PER_BLOCK = MOE_ROWS // ROW_GROUP
META_SLOT0, META_SLOT1, META_GATE0, META_GATE1 = 0, 1, 2, 3


def _sort_body(xl_ref, xc_ref, mod_ref, g_ref, wrt_ref, earlier_ref, as_ref, meta_ref, cnt_ref, *,
               tm, rows_per_batch, n_lat):
    i = pl.program_id(0)
    is_lat = i < n_lat
    row = jnp.where(is_lat, (i * tm) // rows_per_batch, CTX_MOD_ROW)
    x = jnp.where(is_lat, xl_ref[...], xc_ref[...])
    a = _norm_mod(x, g_ref[...], _mod_chunk(mod_ref, row, 3), _mod_chunk(mod_ref, row, 4))
    e_io = lax.broadcasted_iota(jnp.int32, (EXPERT_ROWS, tm), 0)
    ab = a.astype(BF16)
    a_rem = (a - ab.astype(F32)).astype(BF16)
    wrt = wrt_ref[...]
    l_head = lax.dot_general(wrt, ab, _NT, preferred_element_type=F32)
    lt = (l_head[:EXPERT_ROWS] + l_head[EXPERT_ROWS:]
          + lax.dot_general(wrt[:EXPERT_ROWS], a_rem, _NT, preferred_element_type=F32))
    lt = jnp.where(e_io < N_EXPERTS, lt, -jnp.inf)
    m0 = jnp.max(lt, axis=0, keepdims=True)
    e0 = jnp.min(jnp.where(lt == m0, e_io, EXPERT_ROWS), axis=0, keepdims=True)
    oh0 = e_io == e0
    lt1 = jnp.where(oh0, -jnp.inf, lt)
    m1 = jnp.max(lt1, axis=0, keepdims=True)
    e1 = jnp.min(jnp.where(lt1 == m1, e_io, EXPERT_ROWS), axis=0, keepdims=True)
    oh1 = e_io == e1
    ex = jnp.exp(m1 - m0)
    gate0 = 1.0 / (1.0 + ex)
    gate1 = ex / (1.0 + ex)
    oh = jnp.where(oh0, 1.0, jnp.where(oh1, 1.0, 0.0))
    rank = jnp.dot(oh.astype(BF16), earlier_ref[...], preferred_element_type=F32)
    cnt = jnp.sum(oh, axis=1, keepdims=True)
    padded = jnp.floor((cnt + (ROW_GROUP - 1)) * (1.0 / ROW_GROUP)) * ROW_GROUP
    r8 = lax.broadcasted_iota(jnp.int32, (EXPERT_ROWS, EXPERT_ROWS), 0)
    c8 = lax.broadcasted_iota(jnp.int32, (EXPERT_ROWS, EXPERT_ROWS), 1)
    padded_row = jnp.sum(jnp.where(r8 == c8, padded, 0.0), axis=0, keepdims=True)
    start = jnp.sum(jnp.where(c8 < r8, padded_row, 0.0), axis=1, keepdims=True)
    slot0 = jnp.sum(jnp.where(oh0, start + rank, 0.0), axis=0, keepdims=True)
    slot1 = jnp.sum(jnp.where(oh1, start + rank, 0.0), axis=0, keepdims=True)
    j_io = lax.broadcasted_iota(jnp.int32, (SORT_SLOTS, tm), 0).astype(F32)
    perm = jnp.where(j_io == slot0, 1.0, jnp.where(j_io == slot1, 1.0, 0.0)).astype(BF16)
    as_ref[...] = jnp.dot(perm, ab, preferred_element_type=F32).astype(BF16)
    rows = jnp.concatenate([slot0, slot1, gate0, gate1, jnp.zeros((LANES - 4, tm), F32)], axis=0)
    meta_ref[...] = rows.T
    cnt_ref[...] = jnp.concatenate([jnp.broadcast_to(padded, (EXPERT_ROWS, LANES)),
                                    jnp.broadcast_to(start, (EXPERT_ROWS, LANES))], axis=0)


def moe_sort(x_lat, x_ctx, mods, layer, g, wrt, *, rows_per_batch):
    tm = SORT_TOKENS
    n_lat = x_lat.shape[0] // tm
    n_ctx = 0 if x_ctx is None else x_ctx.shape[0] // tm
    nt = n_lat + n_ctx
    if x_ctx is None:
        x_ctx = x_lat
    w_head = wrt.astype(BF16)
    w_split = jnp.concatenate([w_head, (wrt - w_head.astype(F32)).astype(BF16)], axis=0)
    tok = jnp.arange(tm)
    earlier = (tok[:, None] < tok[None, :]).astype(BF16)
    body = functools.partial(_sort_body, tm=tm, rows_per_batch=rows_per_batch, n_lat=n_lat)
    return pl.pallas_call(
        body,
        grid=(nt,),
        in_specs=[pl.BlockSpec((tm, D_MODEL), lambda i: (jnp.minimum(i, n_lat - 1), 0)),
                  pl.BlockSpec((tm, D_MODEL), lambda i: (jnp.maximum(i - n_lat, 0), 0)),
                  _mod_spec(layer),
                  pl.BlockSpec((1, D_MODEL), lambda i: (0, 0)),
                  pl.BlockSpec((2 * EXPERT_ROWS, D_MODEL), lambda i: (0, 0)),
                  pl.BlockSpec((tm, tm), lambda i: (0, 0))],
        out_specs=[pl.BlockSpec((SORT_SLOTS, D_MODEL), lambda i: (i, 0)),
                   pl.BlockSpec((tm, LANES), lambda i: (i, 0)),
                   pl.BlockSpec((None, 2 * EXPERT_ROWS, LANES), lambda i: (i, 0, 0))],
        out_shape=[jax.ShapeDtypeStruct((nt * SORT_SLOTS, D_MODEL), BF16),
                   jax.ShapeDtypeStruct((nt * tm, LANES), F32),
                   jax.ShapeDtypeStruct((nt, 2 * EXPERT_ROWS, LANES), F32)],
        compiler_params=_cparams("arbitrary"),
        name="moe_sort",
    )(x_lat, x_ctx, mods, g.reshape(1, D_MODEL), w_split, earlier)


def moe_group_table(cnt):
    nt = cnt.shape[0]
    padded = cnt[:, :N_EXPERTS, 0].astype(jnp.int32)
    start = cnt[:, EXPERT_ROWS:EXPERT_ROWS + N_EXPERTS, 0].astype(jnp.int32)
    groups = padded // ROW_GROUP
    cum = jnp.cumsum(groups, axis=0)
    tot = cum[-1]
    blocks = (tot + GROUPS_PER_BLOCK - 1) // GROUPS_PER_BLOCK
    bend = jnp.cumsum(blocks)
    bstart = bend - blocks
    n_blocks = (nt * SORT_SLOTS // ROW_GROUP + N_EXPERTS * (GROUPS_PER_BLOCK - 1)) // GROUPS_PER_BLOCK
    bi = jnp.arange(n_blocks, dtype=jnp.int32)
    block_e = jnp.minimum(jnp.searchsorted(bend, bi, side="right"), N_EXPERTS - 1).astype(jnp.int32)
    q = (bi - bstart[block_e])[:, None] * GROUPS_PER_BLOCK + jnp.arange(GROUPS_PER_BLOCK, dtype=jnp.int32)[None, :]
    n_valid = jnp.clip(tot[block_e] - (bi - bstart[block_e]) * GROUPS_PER_BLOCK, 0, GROUPS_PER_BLOCK).astype(jnp.int32)
    cum_e = cum.T[block_e]
    tile = jnp.sum((cum_e[:, None, :] <= q[:, :, None]).astype(jnp.int32), axis=2)
    tile = jnp.minimum(tile, nt - 1)
    before = jnp.take_along_axis(cum_e - groups.T[block_e], tile, axis=1)
    first = jnp.take_along_axis(start.T[block_e], tile, axis=1)
    rows = tile * SORT_SLOTS + first + (q - before) * ROW_GROUP
    valid = jnp.arange(GROUPS_PER_BLOCK, dtype=jnp.int32)[None, :] < n_valid[:, None]
    rows = jnp.where(valid, rows, 0).astype(jnp.int32)
    n_used = bend[-1].astype(jnp.int32).reshape(1)
    return block_e, n_used, n_valid, rows.reshape(n_blocks, 1, GROUPS_PER_BLOCK)


def _experts_body(be_ref, nu_ref, nv_ref, row_ref, rown_ref, as_hbm, w1_ref, w3_ref, w2_ref, ys_hbm,
               x_ref, acc_ref, y_ref, gsem, ssem, *, nf):
    i = pl.program_id(0)
    f = pl.program_id(1)
    n_used = nu_ref[0]
    slot = i % 2

    def gather_copy(s, gidx, row):
        row = pl.multiple_of(row, ROW_GROUP)
        dst = pl.multiple_of(gidx * ROW_GROUP, ROW_GROUP)
        return pltpu.make_async_copy(as_hbm.at[pl.ds(row, ROW_GROUP), :], x_ref.at[s, pl.ds(dst, ROW_GROUP), :],
                                     gsem.at[s])

    def scatter_copy(s, gidx, row):
        row = pl.multiple_of(row, ROW_GROUP)
        src = pl.multiple_of(gidx * ROW_GROUP, ROW_GROUP)
        return pltpu.make_async_copy(y_ref.at[s, pl.ds(src, ROW_GROUP), :], ys_hbm.at[pl.ds(row, ROW_GROUP), :],
                                     ssem.at[s])

    def loop(n, fn):
        def body(r, c):
            fn(r)
            return c
        lax.fori_loop(0, n, body, 0)

    @pl.when(i < n_used)
    def _():
        @pl.when(f == 0)
        def _():
            @pl.when(i == 0)
            def _():
                x_ref[...] = jnp.zeros_like(x_ref)
                loop(nv_ref[0], lambda r: gather_copy(0, r, row_ref[0, r]).start())

            loop(nv_ref[i], lambda r: gather_copy(slot, r, 0).wait())

            @pl.when(i + 1 < n_used)
            def _():
                loop(nv_ref[i + 1], lambda r: gather_copy(1 - slot, r, rown_ref[0, r]).start())

        x = x_ref[slot]
        u = jnp.dot(x, w1_ref[...], preferred_element_type=F32)
        v = jnp.dot(x, w3_ref[...], preferred_element_type=F32)
        p = (_silu(u) * v).astype(BF16)
        y = jnp.dot(p, w2_ref[...], preferred_element_type=F32)

        @pl.when(f == 0)
        def _():
            acc_ref[...] = y

        @pl.when(jnp.logical_and(f > 0, f < nf - 1))
        def _():
            acc_ref[...] += y

        @pl.when(f == nf - 1)
        def _():
            y_ref[slot] = (acc_ref[...] + y).astype(BF16)
            loop(nv_ref[i], lambda r: scatter_copy(slot, r, row_ref[0, r]).start())

            @pl.when(i > 0)
            def _():
                loop(nv_ref[i - 1], lambda r: scatter_copy(1 - slot, r, 0).wait())

            @pl.when(i == n_used - 1)
            def _():
                loop(nv_ref[i], lambda r: scatter_copy(slot, r, 0).wait())


def moe_experts_sorted(a_sorted, block_e, n_used, n_valid, rows, w1, w3, w2, w_layer, *, tf=1792):
    n_blocks = rows.shape[0]
    tm = MOE_ROWS
    f_dim = w1.shape[3]
    nf = f_dim // tf
    assert nf >= 2
    body = functools.partial(_experts_body, nf=nf)

    def wmap(kind):
        def index_map(i, f, be, nu, nv):
            live = i < nu[0]
            ff = jnp.where(live, f, nf - 1)
            ii = jnp.where(live, i, nu[0] - 1)
            return (w_layer, be[ii], 0, ff) if kind == "up" else (w_layer, be[ii], ff, 0)
        return index_map

    idx_spec = pl.BlockSpec((None, 1, GROUPS_PER_BLOCK), lambda i, f, be, nu, nv: (i, 0, 0), memory_space=pltpu.SMEM)
    next_spec = pl.BlockSpec((None, 1, GROUPS_PER_BLOCK),
                             lambda i, f, be, nu, nv: (jnp.minimum(i + 1, n_blocks - 1), 0, 0),
                             memory_space=pltpu.SMEM)
    grid_spec = pltpu.PrefetchScalarGridSpec(
        num_scalar_prefetch=3,
        grid=(n_blocks, nf),
        in_specs=[idx_spec, next_spec,
                  pl.BlockSpec(memory_space=pl.ANY),
                  pl.BlockSpec((None, None, D_MODEL, tf), wmap("up")),
                  pl.BlockSpec((None, None, D_MODEL, tf), wmap("up")),
                  pl.BlockSpec((None, None, tf, D_MODEL), wmap("down"))],
        out_specs=pl.BlockSpec(memory_space=pl.ANY),
        scratch_shapes=[pltpu.VMEM((2, tm, D_MODEL), BF16), pltpu.VMEM((tm, D_MODEL), F32),
                        pltpu.VMEM((2, tm, D_MODEL), BF16),
                        pltpu.SemaphoreType.DMA((2,)), pltpu.SemaphoreType.DMA((2,))],
    )
    return pl.pallas_call(
        body,
        grid_spec=grid_spec,
        out_shape=jax.ShapeDtypeStruct(a_sorted.shape, BF16),
        input_output_aliases={5: 0},
        compiler_params=_cparams("arbitrary", "arbitrary"),
        name="moe_experts",
    )(block_e, n_used, n_valid, rows, rows, a_sorted, w1, w3, w2)


def _unsort_body(h_ref, ys_ref, meta_ref, mod_ref, fg_ref, *rest, tm, rows_per_batch, final):
    o_ref = rest[-1] if len(rest) == 1 else rest[-2]
    i = pl.program_id(0)
    row = ((i * tm) // rows_per_batch) if rows_per_batch else CTX_MOD_ROW
    meta = meta_ref[...]
    slot0 = meta[:, META_SLOT0:META_SLOT0 + 1]
    slot1 = meta[:, META_SLOT1:META_SLOT1 + 1]
    gate0 = meta[:, META_GATE0:META_GATE0 + 1]
    gate1 = meta[:, META_GATE1:META_GATE1 + 1]
    j_io = lax.broadcasted_iota(jnp.int32, (tm, SORT_SLOTS), 1).astype(F32)
    pick = jnp.where(j_io == slot0, gate0, jnp.where(j_io == slot1, gate1, 0.0)).astype(BF16)
    fsum = jnp.dot(pick, ys_ref[...], preferred_element_type=F32)
    out = h_ref[...] + _mod_chunk(mod_ref, row, 5) * fsum
    if final:
        ms = jnp.mean(out * out, axis=-1, keepdims=True)
        out = out * lax.rsqrt(ms + NORM_EPS) * fg_ref[...]
    o_ref[...] = out
    if len(rest) > 1:
        mod2_ref, g2_ref, w_ref, b_ref, _, p_ref = rest
        _nm_project(out, mod2_ref, row, g2_ref, w_ref, b_ref, p_ref, 0, 1)


def moe_unsort_combine(h, ys, meta, mods, layer, final_g, proj=None, *, tile_off, rows_per_batch, final):
    tm = SORT_TOKENS
    m = h.shape[0]
    body = functools.partial(_unsort_body, tm=tm, rows_per_batch=rows_per_batch, final=final)
    in_specs = [pl.BlockSpec((tm, D_MODEL), lambda i: (i, 0)),
                pl.BlockSpec((SORT_SLOTS, D_MODEL), lambda i: (tile_off + i, 0)),
                pl.BlockSpec((tm, LANES), lambda i: (tile_off + i, 0)),
                _mod_spec(layer),
                pl.BlockSpec((1, D_MODEL), lambda i: (0, 0))]
    args = [h, ys, meta, mods, final_g.reshape(1, D_MODEL)]
    out_specs = pl.BlockSpec((tm, D_MODEL), lambda i: (i, 0))
    out_shape = jax.ShapeDtypeStruct((m, D_MODEL), F32)
    if proj is not None:
        g2, w, bias = proj
        n = w.shape[1]
        in_specs += [_mod_spec(layer + 1),
                     pl.BlockSpec((1, D_MODEL), lambda i: (0, 0)),
                     _resident((D_MODEL, n), lambda i: (0, 0)),
                     pl.BlockSpec((1, n), lambda i: (0, 0))]
        args += [mods, g2.reshape(1, D_MODEL), w, bias.reshape(1, n)]
        out_specs = [out_specs, pl.BlockSpec((tm, n), lambda i: (i, 0))]
        out_shape = [out_shape, jax.ShapeDtypeStruct((m, n), BF16)]
    return pl.pallas_call(
        body,
        grid=(m // tm,),
        in_specs=in_specs,
        out_specs=out_specs,
        out_shape=out_shape,
        compiler_params=_cparams("arbitrary"),
        name="moe_combine",
    )(*args)


def _na_layer(h_lat, h_ctx, mods, layer, g, w_qkv, b_qkv, rpb, w_out, b_out, ffn, with_ctx_out):
    b, seq, _ = h_lat.shape
    c = h_ctx.shape[1]
    qscale = jnp.concatenate([jnp.full((D_MODEL,), (D_MODEL // NA_HEADS) ** -0.5, F32), jnp.ones((2 * D_MODEL,), F32)])
    w = (w_qkv * qscale).astype(BF16)
    bias = b_qkv * qscale
    qkv = nm_matmul(h_lat.reshape(b * seq, D_MODEL), mods, layer, g, w, bias, rows_per_batch=seq, sh=0, sc=1)
    qkv_c = nm_matmul(h_ctx.reshape(b * c, D_MODEL), mods, layer, g, w, bias, rows_per_batch=None, sh=0, sc=1)
    qkv = qkv.reshape(b, seq, 3 * D_MODEL)
    qkv_c = qkv_c.reshape(b, c, 3 * D_MODEL)
    o_lat = na_attention(qkv, qkv_c, na_bias_table(rpb))
    wo = w_out.astype(BF16)
    h_lat = mm_residual(o_lat.reshape(b * seq, D_MODEL), h_lat.reshape(b * seq, D_MODEL), mods, layer, wo, b_out,
                        ffn, rows_per_batch=seq, gate=2).reshape(b, seq, D_MODEL)
    if with_ctx_out:
        o_ctx = ctx_attention(qkv_c)
        h_ctx = mm_residual(o_ctx.reshape(b * c, D_MODEL), h_ctx.reshape(b * c, D_MODEL), mods, layer, wo, b_out,
                            ffn, rows_per_batch=None, gate=2).reshape(b, c, D_MODEL)
    return h_lat, h_ctx


def mlstm_up_proj(w_up):
    return w_up.astype(BF16), jnp.zeros((w_up.shape[1],), F32)


def _mlstm_layer(h_lat, h_ctx, mods, layer, g, w_up, conv_w, conv_b, w_q, w_k, w_v, w_gates, b_gates,
                 gn_w, skip, w_down, ffn, ups, with_ctx_out):
    b, seq, _ = h_lat.shape
    c = h_ctx.shape[1]
    inner = MLSTM_INNER
    wq_bd = block_diag_weights(w_q).astype(BF16)
    wk_bd = block_diag_weights(w_k).astype(BF16)
    wkt_bd = jnp.swapaxes(wk_bd, 1, 2)
    wv_bd = block_diag_weights(w_v).astype(BF16)
    gxc, gxm = fold_gate_weights(w_q, w_k, w_v, w_gates)
    wd = w_down.astype(BF16)

    def features(h, up, rows_per_batch):
        n, s, _ = h.shape
        if up is None:
            up = nm_matmul(h.reshape(n * s, D_MODEL), mods, layer, g, *mlstm_up_proj(w_up),
                           rows_per_batch=rows_per_batch, sh=0, sc=1)
        up = up.reshape(n, s, 2 * inner)
        return up, mlstm_features(up, conv_w, conv_b, wq_bd, wk_bd, wkt_bd, wv_bd, gxc, gxm, b_gates)

    up_l, up_c = ups if ups is not None else (None, None)
    up_c, (q_c, k_c, kt_c, v_c, xc_c, g_c, gt_c) = features(h_ctx, up_c, None)
    up_l, (q_l, k_l, kt_l, v_l, xc_l, g_l, gt_l) = features(h_lat, up_l, seq)
    hf_c, cf, mf = mlstm_scan(q_c, k_c, kt_c, v_c, g_c, gt_c, None, rev=False)
    hb_c, cb, mb = mlstm_scan(q_c, k_c, kt_c, v_c, g_c, gt_c, None, rev=True)
    hf_l, _, _ = mlstm_scan(q_l, k_l, kt_l, v_l, g_l, gt_l, (cf, mf), rev=False)
    hb_l, _, _ = mlstm_scan(q_l, k_l, kt_l, v_l, g_l, gt_l, (cb, mb), rev=True)
    flat = lambda a: a.reshape(-1, a.shape[-1])
    h_lat = mlstm_output(flat(hf_l), flat(hb_l), flat(xc_l), flat(up_l), flat(h_lat), mods, layer, gn_w, skip, wd,
                         ffn, rows_per_batch=seq).reshape(b, seq, D_MODEL)
    if with_ctx_out:
        h_ctx = mlstm_output(flat(hf_c), flat(hb_c), flat(xc_c), flat(up_c), flat(h_ctx), mods, layer, gn_w, skip, wd,
                             ffn, rows_per_batch=None).reshape(b, c, D_MODEL)
    return h_lat, h_ctx


def _moe_layer(h_lat, h_ctx, mods, layer, g, w_router, w1, w3, w2, w_layer, final_g, proj, last):
    b, seq, _ = h_lat.shape
    c = h_ctx.shape[1]
    wrt = jnp.pad(w_router.T, ((0, EXPERT_ROWS - N_EXPERTS), (0, 0)))
    hl = h_lat.reshape(b * seq, D_MODEL)
    hc = None if last else h_ctx.reshape(b * c, D_MODEL)
    a_sorted, meta, cnt = moe_sort(hl, hc, mods, layer, g, wrt, rows_per_batch=seq)
    block_e, n_used, n_valid, rows = moe_group_table(cnt)
    ys = moe_experts_sorted(a_sorted, block_e, n_used, n_valid, rows, w1, w3, w2, w_layer)
    res_l = moe_unsort_combine(hl, ys, meta, mods, layer, final_g, proj, tile_off=0, rows_per_batch=seq, final=last)
    res_c = None
    if not last:
        res_c = moe_unsort_combine(hc, ys, meta, mods, layer, final_g, proj, tile_off=(b * seq) // SORT_TOKENS,
                                   rows_per_batch=None, final=False)
    if proj is None:
        h_lat = res_l.reshape(b, seq, D_MODEL)
        h_ctx = h_ctx if res_c is None else res_c.reshape(b, c, D_MODEL)
        return h_lat, h_ctx, None
    return res_l[0].reshape(b, seq, D_MODEL), res_c[0].reshape(b, c, D_MODEL), (res_l[1], res_c[1])


def kernel(x, c, ctx, c_ctx, w_mod, b_mod, norm_g, final_g, na_w_qkv, na_b_qkv, na_rpb, na_w_out, na_b_out,
           pool_w, pool_scale, ml_w_up, ml_conv_w, ml_conv_b, ml_w_q, ml_w_k, ml_w_v, ml_w_gates, ml_b_gates,
           ml_gn_w, ml_skip, ml_w_down, ffn_w1, ffn_w3, ffn_w2, moe_w_router, moe_w1, moe_w3, moe_w2):
    b, seq, _ = x.shape
    n_ctx = ctx.shape[1]
    depth = w_mod.shape[0]
    assert b <= CTX_MOD_ROW
    cond = jnp.zeros((MOD_ROWS, D_MODEL), F32).at[:b].set(c).at[CTX_MOD_ROW].set(c_ctx)
    mods = adaln_all(cond, w_mod, b_mod)
    moe_w1b, moe_w3b, moe_w2b = moe_w1.astype(BF16), moe_w3.astype(BF16), moe_w2.astype(BF16)
    h_lat, h_ctx = x, ctx
    ups = None
    for i in range(depth):
        last = i == depth - 1
        kind = i % 3
        j = i // 3
        g_tok = norm_g[i, 0]
        e = i // 2
        g_ch = norm_g[i, 1]
        dense = i % 2 == 0
        ffn = (g_ch, ffn_w1[e].astype(BF16), ffn_w3[e].astype(BF16), ffn_w2[e].astype(BF16)) if dense else None
        if kind == 0:
            h_lat, h_ctx = _na_layer(h_lat, h_ctx, mods, i, g_tok, na_w_qkv[j], na_b_qkv[j], na_rpb[j],
                                     na_w_out[j], na_b_out[j], ffn, not last)
        elif kind == 1:
            wp = pool_w[j].astype(BF16)
            h_lat = pool_mixer(h_lat, mods, i, g_tok, wp, pool_scale[j], is_ctx=False)
            if not last:
                h_ctx = pool_mixer(h_ctx, mods, i, g_tok, wp, pool_scale[j], is_ctx=True)
        else:
            h_lat, h_ctx = _mlstm_layer(h_lat, h_ctx, mods, i, g_tok, ml_w_up[j], ml_conv_w[j], ml_conv_b[j],
                                        ml_w_q[j], ml_w_k[j], ml_w_v[j], ml_w_gates[j], ml_b_gates[j],
                                        ml_gn_w[j], ml_skip[j], ml_w_down[j], ffn, ups, not last)
            ups = None
        if dense:
            if kind == 1:
                h_lat = ffn_dense(h_lat.reshape(b * seq, D_MODEL), mods, i, *ffn,
                                  rows_per_batch=seq).reshape(b, seq, D_MODEL)
                if not last:
                    h_ctx = ffn_dense(h_ctx.reshape(b * n_ctx, D_MODEL), mods, i, *ffn,
                                      rows_per_batch=None).reshape(b, n_ctx, D_MODEL)
        else:
            proj = None
            if not last and (i + 1) % 3 == 2:
                proj = (norm_g[i + 1, 0],) + mlstm_up_proj(ml_w_up[(i + 1) // 3])
            h_lat, h_ctx, ups = _moe_layer(h_lat, h_ctx, mods, i, g_ch, moe_w_router[e], moe_w1b, moe_w3b, moe_w2b, e,
                                           final_g, proj, last)
    return h_lat
```

```python
import functools

import jax
import jax.numpy as jnp
from jax import lax
from jax.experimental import pallas as pl
from jax.experimental.pallas import tpu as pltpu

F32 = jnp.float32
BF16 = jnp.bfloat16

D_MODEL = 1024
N_MOD = 6
NORM_EPS = 1e-6
GRID_W = 64
NA_HEADS = 16
NA_WIN_ROWS = 8
NA_WIN_COLS = 16
POOL_WINDOWS = (2, 4, 8, 16)
POOL_GROUP_DIM = D_MODEL // len(POOL_WINDOWS)
MLSTM_INNER = 2 * D_MODEL
MLSTM_HEADS = 4
MLSTM_HEAD_DIM = MLSTM_INNER // MLSTM_HEADS
MLSTM_CONV = 4
MLSTM_QKV_BLOCK = 4
N_EXPERTS = 8
TOP_K = 2

LANES = 128
MOD_ROWS = 8
CTX_MOD_ROW = 4
VMEM_LIMIT_BYTES = 56 * 1024 * 1024
NEG_BIG = -1e30
SCAN_CHUNK = 256
SCAN_STATE_ROWS = MLSTM_HEAD_DIM + 8
MOE_ROWS = 1024


def _cparams(*sem):
    return pltpu.CompilerParams(dimension_semantics=sem, vmem_limit_bytes=VMEM_LIMIT_BYTES)


def _resident(shape, index_map):
    return pl.BlockSpec(shape, index_map, pipeline_mode=pl.Buffered(1))


def _silu(x):
    return x * jax.nn.sigmoid(x)


def _norm_mod(x, g, shift, scale):
    ms = jnp.mean(x * x, axis=-1, keepdims=True)
    y = x * lax.rsqrt(ms + NORM_EPS) * g
    return y * (1.0 + scale) + shift


def _mod_chunk(mod_ref, row, j):
    return mod_ref[pl.ds(row, 1), pl.ds(j * D_MODEL, D_MODEL)]


def _mod_spec(layer):
    return pl.BlockSpec((None, MOD_ROWS, N_MOD * D_MODEL), lambda *_: (layer, 0, 0))


def _adaln_body(c_ref, w_ref, b_ref, o_ref):
    s = _silu(c_ref[...])
    o_ref[...] = jnp.dot(s, w_ref[...], preferred_element_type=F32) + b_ref[...]


def adaln_all(cond, w_mod, b_mod):
    depth = w_mod.shape[0]
    n = N_MOD * D_MODEL
    tn = 1536
    return pl.pallas_call(
        _adaln_body,
        grid=(depth, n // tn),
        in_specs=[pl.BlockSpec((MOD_ROWS, D_MODEL), lambda l, j: (0, 0)),
                  pl.BlockSpec((None, D_MODEL, tn), lambda l, j: (l, 0, j)),
                  pl.BlockSpec((None, 1, tn), lambda l, j: (l, 0, j))],
        out_specs=pl.BlockSpec((None, MOD_ROWS, tn), lambda l, j: (l, 0, j)),
        out_shape=jax.ShapeDtypeStruct((depth, MOD_ROWS, n), F32),
        compiler_params=_cparams("arbitrary", "arbitrary"),
        name="adaln",
    )(cond, w_mod, b_mod.reshape(depth, 1, n))


NM_COLS = 1024


def _nm_project(x, mod_ref, row, g_ref, w_ref, b_ref, o_ref, sh, sc):
    a = _norm_mod(x, g_ref[...], _mod_chunk(mod_ref, row, sh), _mod_chunk(mod_ref, row, sc)).astype(BF16)
    for c in range(o_ref.shape[1] // NM_COLS):
        sl = slice(c * NM_COLS, (c + 1) * NM_COLS)
        y = jnp.dot(a, w_ref[:, sl], preferred_element_type=F32) + b_ref[:, sl]
        o_ref[:, sl] = y.astype(o_ref.dtype)


def _nm_matmul_body(x_ref, mod_ref, g_ref, w_ref, b_ref, o_ref, *, tm, rows_per_batch, sh, sc):
    i = pl.program_id(0)
    row = (i * tm) // rows_per_batch if rows_per_batch else CTX_MOD_ROW
    _nm_project(x_ref[...], mod_ref, row, g_ref, w_ref, b_ref, o_ref, sh, sc)


def nm_matmul(x, mods, layer, g, w, bias, *, rows_per_batch, sh, sc, tm=512, out_dtype=BF16):
    m, n = x.shape[0], w.shape[1]
    tm = min(tm, m)
    body = functools.partial(_nm_matmul_body, tm=tm, rows_per_batch=rows_per_batch, sh=sh, sc=sc)
    return pl.pallas_call(
        body,
        grid=(m // tm,),
        in_specs=[pl.BlockSpec((tm, D_MODEL), lambda i: (i, 0)),
                  _mod_spec(layer),
                  pl.BlockSpec((1, D_MODEL), lambda i: (0, 0)),
                  _resident((D_MODEL, n), lambda i: (0, 0)),
                  pl.BlockSpec((1, n), lambda i: (0, 0))],
        out_specs=pl.BlockSpec((tm, n), lambda i: (i, 0)),
        out_shape=jax.ShapeDtypeStruct((m, n), out_dtype),
        compiler_params=_cparams("arbitrary"),
        name="nm_matmul",
    )(x, mods, g.reshape(1, D_MODEL), w, bias.reshape(1, n))


def _mm_res_body(a_ref, h_ref, mod_ref, w_ref, b_ref, *rest, tm, rows_per_batch, gate):
    o_ref = rest[-1]
    i = pl.program_id(0)
    row = (i * tm) // rows_per_batch if rows_per_batch else CTX_MOD_ROW
    y = jnp.dot(a_ref[...], w_ref[...], preferred_element_type=F32) + b_ref[...]
    h1 = h_ref[...] + _mod_chunk(mod_ref, row, gate) * y
    if len(rest) > 1:
        h1 = _ffn_rows(h1, mod_ref, row, *rest[:4])
    o_ref[...] = h1


def mm_residual(a, h, mods, layer, w, bias, ffn=None, *, rows_per_batch, gate):
    m, k = a.shape
    tm = min(512 if ffn is None else 256, m)
    body = functools.partial(_mm_res_body, tm=tm, rows_per_batch=rows_per_batch, gate=gate)
    in_specs = [pl.BlockSpec((tm, k), lambda i: (i, 0)),
                pl.BlockSpec((tm, D_MODEL), lambda i: (i, 0)),
                _mod_spec(layer),
                _resident((k, D_MODEL), lambda i: (0, 0)),
                pl.BlockSpec((1, D_MODEL), lambda i: (0, 0))]
    args = [a, h, mods, w, bias.reshape(1, D_MODEL)]
    if ffn is not None:
        in_specs += _ffn_specs(ffn)
        args += _ffn_args(ffn)
    return pl.pallas_call(
        body,
        grid=(m // tm,),
        in_specs=in_specs,
        out_specs=pl.BlockSpec((tm, D_MODEL), lambda i: (i, 0)),
        out_shape=jax.ShapeDtypeStruct((m, D_MODEL), F32),
        compiler_params=_cparams("arbitrary"),
        name="mm_residual",
    )(*args)


def _ffn_rows(h, mod_ref, row, g_ref, w1_ref, w3_ref, w2_ref):
    a = _norm_mod(h, g_ref[...], _mod_chunk(mod_ref, row, 3), _mod_chunk(mod_ref, row, 4)).astype(BF16)
    u = jnp.dot(a, w1_ref[...], preferred_element_type=F32)
    v = jnp.dot(a, w3_ref[...], preferred_element_type=F32)
    p = (_silu(u) * v).astype(BF16)
    y = jnp.dot(p, w2_ref[...], preferred_element_type=F32)
    return h + _mod_chunk(mod_ref, row, 5) * y


def _ffn_specs(ffn):
    f = ffn[1].shape[1]
    return [pl.BlockSpec((1, D_MODEL), lambda i: (0, 0)),
            _resident((D_MODEL, f), lambda i: (0, 0)),
            _resident((D_MODEL, f), lambda i: (0, 0)),
            _resident((f, D_MODEL), lambda i: (0, 0))]


def _ffn_args(ffn):
    g, w1, w3, w2 = ffn
    return [g.reshape(1, D_MODEL), w1, w3, w2]


def _ffn_body(h_ref, mod_ref, g_ref, w1_ref, w3_ref, w2_ref, o_ref, *, tm, rows_per_batch):
    i = pl.program_id(0)
    row = (i * tm) // rows_per_batch if rows_per_batch else CTX_MOD_ROW
    o_ref[...] = _ffn_rows(h_ref[...], mod_ref, row, g_ref, w1_ref, w3_ref, w2_ref)


def ffn_dense(h, mods, layer, g, w1, w3, w2, *, rows_per_batch, tm=256):
    m = h.shape[0]
    f = w1.shape[1]
    tm = min(tm, m)
    body = functools.partial(_ffn_body, tm=tm, rows_per_batch=rows_per_batch)
    return pl.pallas_call(
        body,
        grid=(m // tm,),
        in_specs=[pl.BlockSpec((tm, D_MODEL), lambda i: (i, 0)),
                  _mod_spec(layer),
                  pl.BlockSpec((1, D_MODEL), lambda i: (0, 0)),
                  _resident((D_MODEL, f), lambda i: (0, 0)),
                  _resident((D_MODEL, f), lambda i: (0, 0)),
                  _resident((f, D_MODEL), lambda i: (0, 0))],
        out_specs=pl.BlockSpec((tm, D_MODEL), lambda i: (i, 0)),
        out_shape=jax.ShapeDtypeStruct((m, D_MODEL), F32),
        compiler_params=_cparams("arbitrary"),
        name="ffn_dense",
    )(h, mods, g.reshape(1, D_MODEL), w1, w3, w2)


def na_bias_table(rpb):
    h = rpb.shape[0]
    col = jnp.arange(GRID_W)
    c0 = jnp.clip(col - NA_WIN_COLS // 2, 0, GRID_W - NA_WIN_COLS)
    col_ok = (col[None, :] >= c0[:, None]) & (col[None, :] < c0[:, None] + NA_WIN_COLS)
    dcol = jnp.clip(col[None, :] - col[:, None], 1 - NA_WIN_COLS, NA_WIN_COLS - 1) + (NA_WIN_COLS - 1)
    n_drow = 2 * NA_WIN_ROWS - 1
    t = jnp.where(col_ok[None, None], rpb[:, :, dcol].astype(F32), NEG_BIG)
    t = t.transpose(0, 2, 1, 3).reshape(h // 2, 2 * GRID_W, n_drow * GRID_W)
    tiles = [t[:, :, (NA_WIN_ROWS - 1 - off) * GRID_W:(2 * NA_WIN_ROWS - 1 - off) * GRID_W]
             for off in range(NA_WIN_ROWS)]
    return jnp.stack(tiles, axis=1)


def _stack_heads(q):
    lo = lax.broadcasted_iota(jnp.int32, q.shape, 1) < (LANES // 2)
    zero = jnp.zeros_like(q)
    return jnp.concatenate([jnp.where(lo, q, zero), jnp.where(lo, zero, q)], axis=0)


def _unstack_heads(o):
    n = o.shape[0] // 2
    lo = lax.broadcasted_iota(jnp.int32, (n, LANES), 1) < (LANES // 2)
    return jnp.where(lo, o[:n], o[n:])


_NT = (((1,), (1,)), ((), ()))
NA_GROUP = 16


def _na_body(q_ref, k_ref, v_ref, kc_ref, vc_ref, bias_ref, o_ref, *, rows):
    kc = kc_ref[...]
    vc = vc_ref[...]
    kwin = NA_WIN_ROWS * GRID_W

    def scores(r):
        r0 = jnp.clip(r - NA_WIN_ROWS // 2, 0, rows - NA_WIN_ROWS)
        qoff = pl.multiple_of(r * GRID_W, GRID_W)
        koff = pl.multiple_of(r0 * GRID_W, GRID_W)
        qs = _stack_heads(q_ref[pl.ds(qoff, GRID_W), :])
        k = k_ref[pl.ds(koff, kwin), :]
        s_loc = lax.dot_general(qs, k, _NT, preferred_element_type=F32) + bias_ref[r - r0]
        s_ctx = lax.dot_general(qs, kc, _NT, preferred_element_type=F32)
        return qoff, koff, s_loc, s_ctx

    def probs(s_loc, s_ctx):
        m = jnp.maximum(jnp.max(s_loc, axis=-1, keepdims=True), jnp.max(s_ctx, axis=-1, keepdims=True))
        p_loc = jnp.exp(s_loc - m)
        p_ctx = jnp.exp(s_ctx - m)
        l = jnp.sum(p_loc, axis=-1, keepdims=True) + jnp.sum(p_ctx, axis=-1, keepdims=True)
        return p_loc.astype(BF16), p_ctx.astype(BF16), l

    def group(gi, carry):
        sc = [scores(gi * NA_GROUP + u) for u in range(NA_GROUP)]
        pr = [probs(s[2], s[3]) for s in sc]
        for (qoff, koff, _, _), (p_loc, p_ctx, l) in zip(sc, pr):
            v = v_ref[pl.ds(koff, kwin), :]
            o = (jnp.dot(p_loc, v, preferred_element_type=F32)
                 + jnp.dot(p_ctx, vc, preferred_element_type=F32)) / l
            o_ref[pl.ds(qoff, GRID_W), :] = _unstack_heads(o).astype(o_ref.dtype)
        return carry

    lax.fori_loop(0, rows // NA_GROUP, group, 0)


def na_attention(qkv, qkv_ctx, bias):
    b, l, _ = qkv.shape
    c = qkv_ctx.shape[1]
    hp = D_MODEL // LANES
    body = functools.partial(_na_body, rows=l // GRID_W)
    return pl.pallas_call(
        body,
        grid=(b, hp),
        in_specs=[pl.BlockSpec((None, l, LANES), lambda i, j: (i, 0, j)),
                  pl.BlockSpec((None, l, LANES), lambda i, j: (i, 0, hp + j)),
                  pl.BlockSpec((None, l, LANES), lambda i, j: (i, 0, 2 * hp + j)),
                  pl.BlockSpec((None, c, LANES), lambda i, j: (i, 0, hp + j)),
                  pl.BlockSpec((None, c, LANES), lambda i, j: (i, 0, 2 * hp + j)),
                  pl.BlockSpec((None,) + bias.shape[1:], lambda i, j: (j, 0, 0, 0))],
        out_specs=pl.BlockSpec((None, l, LANES), lambda i, j: (i, 0, j)),
        out_shape=jax.ShapeDtypeStruct((b, l, D_MODEL), BF16),
        compiler_params=_cparams("arbitrary", "arbitrary"),
        name="na_attention",
    )(qkv, qkv, qkv, qkv_ctx, qkv_ctx, bias)


def _ctx_attn_body(q_ref, k_ref, v_ref, o_ref):
    qs = _stack_heads(q_ref[...])
    s = lax.dot_general(qs, k_ref[...], _NT, preferred_element_type=F32)
    p = jnp.exp(s - jnp.max(s, axis=-1, keepdims=True))
    l = jnp.sum(p, axis=-1, keepdims=True)
    o = jnp.dot(p.astype(BF16), v_ref[...], preferred_element_type=F32) / l
    o_ref[...] = _unstack_heads(o).astype(o_ref.dtype)


def ctx_attention(qkv_ctx):
    b, c, _ = qkv_ctx.shape
    hp = D_MODEL // LANES
    return pl.pallas_call(
        _ctx_attn_body,
        grid=(b, hp),
        in_specs=[pl.BlockSpec((None, c, LANES), lambda i, j: (i, 0, j)),
                  pl.BlockSpec((None, c, LANES), lambda i, j: (i, 0, hp + j)),
                  pl.BlockSpec((None, c, LANES), lambda i, j: (i, 0, 2 * hp + j))],
        out_specs=pl.BlockSpec((None, c, LANES), lambda i, j: (i, 0, j)),
        out_shape=jax.ShapeDtypeStruct((b, c, D_MODEL), BF16),
        compiler_params=_cparams("arbitrary", "arbitrary"),
        name="ctx_attention",
    )(qkv_ctx, qkv_ctx, qkv_ctx)


POOL_HALO = 8


def _pool_body(prev_ref, cur_ref, next_ref, mod_ref, g_ref, wp_ref, ps_ref, o_ref, *, tl, seq, is_ctx):
    b = pl.program_id(0)
    j = pl.program_id(1)
    row = CTX_MOD_ROW if is_ctx else b
    g = g_ref[...]
    sh = _mod_chunk(mod_ref, row, 0)
    sc = _mod_chunk(mod_ref, row, 1)
    h = cur_ref[...]
    a_cur = _norm_mod(h, g, sh, sc)
    a_prev = _norm_mod(prev_ref[...], g, sh, sc) * (j > 0).astype(F32)
    a_next = _norm_mod(next_ref[...], g, sh, sc) * (j < seq // tl - 1).astype(F32)
    ext = jnp.concatenate([a_prev, a_cur, a_next], axis=0)
    t = j * tl + lax.broadcasted_iota(jnp.int32, (tl, 1), 0)
    outs = []
    for gi, w in enumerate(POOL_WINDOWS):
        sl = slice(gi * POOL_GROUP_DIM, (gi + 1) * POOL_GROUP_DIM)
        p = ext[:, sl]
        step = 1
        while step < w:
            n = p.shape[0]
            p = p[:n - step] + p[step:]
            step *= 2
        off = POOL_HALO - w // 2
        cnt = jnp.minimum(t + w // 2, seq) - jnp.maximum(t - w // 2, 0)
        pooled = p[off:off + tl] / cnt.astype(F32) - a_cur[:, sl]
        outs.append(jnp.dot(pooled.astype(BF16), wp_ref[gi], preferred_element_type=F32))
    y = jnp.concatenate(outs, axis=1) * ps_ref[...]
    o_ref[...] = h + _mod_chunk(mod_ref, row, 2) * y


def pool_mixer(h, mods, layer, g, w_pool, pool_scale, *, is_ctx, tl=512):
    b, seq, _ = h.shape
    tl = min(tl, seq)
    nh = tl // POOL_HALO
    last = seq // POOL_HALO - 1
    body = functools.partial(_pool_body, tl=tl, seq=seq, is_ctx=is_ctx)
    return pl.pallas_call(
        body,
        grid=(b, seq // tl),
        in_specs=[pl.BlockSpec((None, POOL_HALO, D_MODEL), lambda i, j: (i, jnp.maximum(j * nh - 1, 0), 0)),
                  pl.BlockSpec((None, tl, D_MODEL), lambda i, j: (i, j, 0)),
                  pl.BlockSpec((None, POOL_HALO, D_MODEL), lambda i, j: (i, jnp.minimum((j + 1) * nh, last), 0)),
                  _mod_spec(layer),
                  pl.BlockSpec((1, D_MODEL), lambda i, j: (0, 0)),
                  pl.BlockSpec(w_pool.shape, lambda i, j: (0, 0, 0)),
                  pl.BlockSpec((1, D_MODEL), lambda i, j: (0, 0))],
        out_specs=pl.BlockSpec((None, tl, D_MODEL), lambda i, j: (i, j, 0)),
        out_shape=jax.ShapeDtypeStruct(h.shape, F32),
        compiler_params=_cparams("arbitrary", "arbitrary"),
        name="pool_mixer",
    )(h, h, h, mods, g.reshape(1, D_MODEL), w_pool, pool_scale.reshape(1, D_MODEL))


CONV_HALO = 16


def block_diag_weights(w):
    nb = LANES // MLSTM_QKV_BLOCK
    wc = w.reshape(-1, nb, MLSTM_QKV_BLOCK, MLSTM_QKV_BLOCK)
    eye = jnp.eye(nb, dtype=w.dtype)
    bd = jnp.einsum("cnij,nm->cnimj", wc, eye)
    return bd.reshape(-1, LANES, LANES)


def fold_gate_weights(w_q, w_k, w_v, w_gates):
    ng = w_gates.shape[1]
    wg = w_gates.reshape(3, -1, MLSTM_QKV_BLOCK, ng)
    fold = lambda w, part: jnp.einsum("ncd,ndg->ncg", w, wg[part], precision=lax.Precision.HIGHEST).reshape(-1, ng)
    return fold(w_q, 0) + fold(w_k, 1), fold(w_v, 2)


def _ml_feat_body(prev_ref, cur_ref, next_ref, cw_ref, cb_ref, wq_ref, wk_ref, wkt_ref, wv_ref,
                  gxc_ref, gxm_ref, gxct_ref, gxmt_ref, bg_ref, bgt_ref,
                  q_ref, k_ref, kt_ref, v_ref, xc_ref, g_ref, gt_ref, ext_ref, *, tl, seq):
    j = pl.program_id(1)
    cur = cur_ref[...]
    ext_ref[0:CONV_HALO, :] = prev_ref[...].astype(F32) * (j > 0).astype(F32)
    ext_ref[CONV_HALO:CONV_HALO + tl, :] = cur.astype(F32)
    ext_ref[CONV_HALO + tl:, :] = next_ref[...].astype(F32) * (j < seq // tl - 1).astype(F32)
    left = MLSTM_CONV // 2
    xc = cb_ref[...]
    for tap in range(MLSTM_CONV):
        xc = xc + ext_ref[pl.ds(CONV_HALO - left + tap, tl), :] * cw_ref[tap:tap + 1, :]
    xc = _silu(xc)
    xcb = xc.astype(BF16)
    xc_ref[...] = xcb
    t = SCAN_CHUNK
    qscale = MLSTM_HEAD_DIM ** -0.5
    for c in range(MLSTM_INNER // LANES):
        sl = slice(c * LANES, (c + 1) * LANES)
        xs = xcb[:, sl]
        q = jnp.dot(xs, wq_ref[c], preferred_element_type=F32)
        k = jnp.dot(xs, wk_ref[c], preferred_element_type=F32)
        v = jnp.dot(cur[:, sl], wv_ref[c], preferred_element_type=F32)
        q_ref[:, sl] = (q * qscale).astype(BF16)
        k_ref[:, sl] = k.astype(BF16)
        v_ref[:, sl] = v.astype(BF16)
        for cc in range(tl // t):
            kt = lax.dot_general(wkt_ref[c], xs[cc * t:(cc + 1) * t], _NT, preferred_element_type=F32)
            kt_ref[cc, sl, :] = kt.astype(BF16)
    ng = g_ref.shape[1]
    g = (jnp.dot(xcb, gxc_ref[...], preferred_element_type=F32)
         + jnp.dot(cur, gxm_ref[...], preferred_element_type=F32))
    g_ref[...] = g[:, :ng] + bg_ref[...]
    gt = (lax.dot_general(gxct_ref[...], xcb, _NT, preferred_element_type=F32)
          + lax.dot_general(gxmt_ref[...], cur, _NT, preferred_element_type=F32)) + bgt_ref[...]
    for cc in range(tl // t):
        gt_ref[cc] = gt[:, cc * t:(cc + 1) * t]


def mlstm_features(up, conv_w, conv_b, wq_bd, wk_bd, wkt_bd, wv_bd, gxc, gxm, bg, *, tl=512):
    b, seq, _ = up.shape
    tl = min(tl, seq)
    t = SCAN_CHUNK
    nh = tl // CONV_HALO
    last = seq // CONV_HALO - 1
    ng = gxc.shape[1]
    inner = MLSTM_INNER
    body = functools.partial(_ml_feat_body, tl=tl, seq=seq)
    full = lambda a: pl.BlockSpec(a.shape, lambda i, j: (0,) * a.ndim)
    cw = conv_w
    cb = conv_b.reshape(1, inner)
    bgr = bg.reshape(1, ng)
    bgc = bg.reshape(ng, 1)
    pad = lambda w: jnp.pad(w, ((0, 0), (0, LANES - ng))).astype(BF16)
    gxc_p, gxm_p = pad(gxc), pad(gxm)
    gxc_t, gxm_t = gxc.T.astype(BF16), gxm.T.astype(BF16)
    return pl.pallas_call(
        body,
        grid=(b, seq // tl),
        in_specs=[pl.BlockSpec((None, CONV_HALO, inner), lambda i, j: (i, jnp.maximum(j * nh - 1, 0), 0)),
                  pl.BlockSpec((None, tl, inner), lambda i, j: (i, j, 0)),
                  pl.BlockSpec((None, CONV_HALO, inner), lambda i, j: (i, jnp.minimum((j + 1) * nh, last), 0)),
                  full(cw), full(cb), full(wq_bd), full(wk_bd), full(wkt_bd), full(wv_bd),
                  full(gxc_p), full(gxm_p), full(gxc_t), full(gxm_t), full(bgr), full(bgc)],
        out_specs=[pl.BlockSpec((None, tl, inner), lambda i, j: (i, j, 0)),
                   pl.BlockSpec((None, tl, inner), lambda i, j: (i, j, 0)),
                   pl.BlockSpec((None, tl // t, inner, t), lambda i, j: (i, j, 0, 0)),
                   pl.BlockSpec((None, tl, inner), lambda i, j: (i, j, 0)),
                   pl.BlockSpec((None, tl, inner), lambda i, j: (i, j, 0)),
                   pl.BlockSpec((None, tl, ng), lambda i, j: (i, j, 0)),
                   pl.BlockSpec((None, tl // t, ng, t), lambda i, j: (i, j, 0, 0))],
        out_shape=[jax.ShapeDtypeStruct((b, seq, inner), BF16),
                   jax.ShapeDtypeStruct((b, seq, inner), BF16),
                   jax.ShapeDtypeStruct((b, seq // t, inner, t), BF16),
                   jax.ShapeDtypeStruct((b, seq, inner), BF16),
                   jax.ShapeDtypeStruct((b, seq, inner), BF16),
                   jax.ShapeDtypeStruct((b, seq, ng), F32),
                   jax.ShapeDtypeStruct((b, seq // t, ng, t), F32)],
        scratch_shapes=[pltpu.VMEM((tl + 2 * CONV_HALO, inner), F32)],
        compiler_params=_cparams("arbitrary", "arbitrary"),
        name="mlstm_features",
    )(up, up, up, cw, cb, wq_bd, wk_bd, wkt_bd, wv_bd, gxc_p, gxm_p, gxc_t, gxm_t, bgr, bgc)


def _log_sigmoid(x):
    return jnp.minimum(x, 0.0) - jnp.log1p(jnp.exp(-jnp.abs(x)))


def _scan_body(*refs, rev, nchunk, nblk, has_init):
    if has_init:
        q_ref, k_ref, kt_ref, v_ref, g_ref, gt_ref, c0_ref, m0_ref, h_ref, cf_ref, mf_ref, c_sc, m_sc = refs
    else:
        q_ref, k_ref, kt_ref, v_ref, g_ref, gt_ref, h_ref, cf_ref, mf_ref, c_sc, m_sc = refs
    hd = pl.program_id(1)
    j = pl.program_id(2)
    t = SCAN_CHUNK
    dh = MLSTM_HEAD_DIM

    @pl.when(j == 0)
    def _():
        if has_init:
            c_sc[...] = c0_ref[...]
            m_sc[...] = m0_ref[...]
        else:
            c_sc[...] = jnp.zeros_like(c_sc)
            m_sc[...] = jnp.zeros_like(m_sc)

    ci = (2 if rev else 0) * MLSTM_HEADS + hd
    cf = (3 if rev else 1) * MLSTM_HEADS + hd
    ng = g_ref.shape[1]
    lane = lax.broadcasted_iota(jnp.int32, (t, ng), 1)
    r_io = lax.broadcasted_iota(jnp.int32, (t, t), 0)
    c_io = lax.broadcasted_iota(jnp.int32, (t, t), 1)
    seen = (c_io >= r_io) if rev else (c_io <= r_io)
    seen_t = (r_io >= c_io) if rev else (r_io <= c_io)
    seen_f = seen.astype(F32)
    seen_tf = seen_t.astype(F32)

    order = range(nchunk - 1, -1, -1) if rev else range(nchunk)
    for cc in order:
        rows = slice(cc * t, (cc + 1) * t)
        q = q_ref[rows, :]
        kt = kt_ref[cc]
        v = v_ref[rows, :]
        g = g_ref[rows, :]
        i_col = jnp.sum(jnp.where(lane == ci, g, 0.0), axis=1, keepdims=True)
        f_col = jnp.sum(jnp.where(lane == cf, g, 0.0), axis=1, keepdims=True)
        i_row = gt_ref[cc, pl.ds(ci, 1), :]
        f_row = gt_ref[cc, pl.ds(cf, 1), :]
        lf_col = _log_sigmoid(f_col)
        lf_row = _log_sigmoid(f_row)
        b_col = jnp.sum(seen_f * lf_row, axis=1, keepdims=True)
        b_row = jnp.sum(seen_tf * lf_col, axis=0, keepdims=True)
        m_prev = m_sc[0:1, 0:1]
        n_row = c_sc[dh:dh + 1, :]
        dmat = jnp.where(seen, b_col - b_row + i_row, NEG_BIG)
        inter = b_col + m_prev
        m_t = jnp.maximum(inter, jnp.max(dmat, axis=1, keepdims=True))
        a = jnp.dot(q, kt, preferred_element_type=F32) * jnp.exp(dmat - m_t)
        w_int = jnp.exp(inter - m_t)
        cb = c_sc[0:dh, :].astype(BF16)
        num = (jnp.dot(a.astype(BF16), v, preferred_element_type=F32)
               + jnp.dot(q, cb, preferred_element_type=F32) * w_int)
        den = (jnp.sum(a, axis=1, keepdims=True)
               + w_int * jnp.sum(q.astype(F32) * n_row, axis=1, keepdims=True))
        hc = num / jnp.maximum(jnp.abs(den), jnp.exp(-m_t))
        h_ref[rows, :] = hc.astype(h_ref.dtype)
        b_end = jnp.sum(lf_row, axis=1, keepdims=True)
        g_row = b_end - b_row + i_row
        m_new = jnp.maximum(b_end + m_prev, jnp.max(g_row, axis=1, keepdims=True))
        decay = jnp.exp(b_end + m_prev - m_new)
        kw = (kt.astype(F32) * jnp.exp(g_row - m_new)).astype(BF16)
        w_col = jnp.exp(b_end - b_col + i_col - m_new)
        c_sc[0:dh, :] = decay * c_sc[0:dh, :] + jnp.dot(kw, v, preferred_element_type=F32)
        c_sc[dh:dh + 1, :] = decay * n_row + jnp.sum(k_ref[rows, :].astype(F32) * w_col, axis=0, keepdims=True)
        m_sc[...] = jnp.broadcast_to(m_new, m_sc.shape)

    @pl.when(j == nblk - 1)
    def _():
        cf_ref[...] = c_sc[...]
        mf_ref[...] = m_sc[...]


def mlstm_scan(q, k, kt, v, g, gt, state, *, rev, tb=1024):
    b, seq, inner = q.shape
    t = SCAN_CHUNK
    tb = min(tb, seq)
    nblk = seq // tb
    nchunk = tb // t
    dh = MLSTM_HEAD_DIM
    ng = g.shape[2]
    has_init = state is not None
    blk = (lambda j: nblk - 1 - j) if rev else (lambda j: j)
    body = functools.partial(_scan_body, rev=rev, nchunk=nchunk, nblk=nblk, has_init=has_init)
    tok_spec = pl.BlockSpec((None, tb, dh), lambda i, h, j: (i, blk(j), h))
    in_specs = [tok_spec, tok_spec,
                pl.BlockSpec((None, nchunk, dh, t), lambda i, h, j: (i, blk(j), h, 0)),
                tok_spec,
                pl.BlockSpec((None, tb, ng), lambda i, h, j: (i, blk(j), 0)),
                pl.BlockSpec((None, nchunk, ng, t), lambda i, h, j: (i, blk(j), 0, 0))]
    args = [q, k, kt, v, g, gt]
    st_spec_c = pl.BlockSpec((None, None, SCAN_STATE_ROWS, dh), lambda i, h, j: (i, h, 0, 0))
    st_spec_m = pl.BlockSpec((None, None, 8, LANES), lambda i, h, j: (i, h, 0, 0))
    if has_init:
        in_specs += [st_spec_c, st_spec_m]
        args += list(state)
    return pl.pallas_call(
        body,
        grid=(b, MLSTM_HEADS, nblk),
        in_specs=in_specs,
        out_specs=[tok_spec, st_spec_c, st_spec_m],
        out_shape=[jax.ShapeDtypeStruct((b, seq, inner), BF16),
                   jax.ShapeDtypeStruct((b, MLSTM_HEADS, SCAN_STATE_ROWS, dh), F32),
                   jax.ShapeDtypeStruct((b, MLSTM_HEADS, 8, LANES), F32)],
        scratch_shapes=[pltpu.VMEM((SCAN_STATE_ROWS, dh), F32), pltpu.VMEM((8, LANES), F32)],
        compiler_params=_cparams("arbitrary", "arbitrary", "arbitrary"),
        name="mlstm_scan_bwd" if rev else "mlstm_scan_fwd",
    )(*args)


ML_OUT_ROWS = 256


def _ml_out_body(hf_ref, hb_ref, xc_ref, z_ref, h_ref, mod_ref, gn_ref, sk_ref, w_ref, *rest, tm, rows_per_batch):
    o_ref = rest[-1]
    i = pl.program_id(0)
    row = (i * tm) // rows_per_batch if rows_per_batch else CTX_MOD_ROW

    def gated(rows):
        hs = hf_ref[rows, :].astype(F32) + hb_ref[rows, :].astype(F32)
        parts = []
        for hd in range(MLSTM_HEADS):
            x = hs[:, hd * MLSTM_HEAD_DIM:(hd + 1) * MLSTM_HEAD_DIM]
            mu = jnp.mean(x, axis=-1, keepdims=True)
            xm = x - mu
            var = jnp.mean(xm * xm, axis=-1, keepdims=True)
            parts.append(xm * lax.rsqrt(var + NORM_EPS))
        hn = jnp.concatenate(parts, axis=1) * gn_ref[...]
        y = (hn + sk_ref[...] * xc_ref[rows, :].astype(F32)) * _silu(z_ref[rows, :].astype(F32))
        return y.astype(BF16)

    groups = [slice(s * ML_OUT_ROWS, (s + 1) * ML_OUT_ROWS) for s in range(tm // ML_OUT_ROWS)]
    ys = [gated(rows) for rows in groups]
    for rows, y in zip(groups, ys):
        y = jnp.dot(y, w_ref[...], preferred_element_type=F32)
        h1 = h_ref[rows, :] + _mod_chunk(mod_ref, row, 2) * y
        if len(rest) > 1:
            g_ref, w1_ref, w3_ref, w2_ref = rest[:4]
            h1 = _ffn_rows(h1, mod_ref, row, g_ref, w1_ref, w3_ref, w2_ref)
        o_ref[rows, :] = h1


def mlstm_output(hf, hb, xc, up, h, mods, layer, gn_w, skip, w_down, ffn=None, *, rows_per_batch, tm=256):
    m = h.shape[0]
    tm = min(tm, m)
    inner = MLSTM_INNER
    body = functools.partial(_ml_out_body, tm=tm, rows_per_batch=rows_per_batch)
    row_spec = pl.BlockSpec((tm, inner), lambda i: (i, 0))
    in_specs = [row_spec, row_spec, row_spec,
                pl.BlockSpec((tm, inner), lambda i: (i, 1)),
                pl.BlockSpec((tm, D_MODEL), lambda i: (i, 0)),
                _mod_spec(layer),
                pl.BlockSpec((1, inner), lambda i: (0, 0)),
                pl.BlockSpec((1, inner), lambda i: (0, 0)),
                _resident((inner, D_MODEL), lambda i: (0, 0))]
    args = [hf, hb, xc, up, h, mods, gn_w.reshape(1, inner), skip.reshape(1, inner), w_down]
    if ffn is not None:
        in_specs += _ffn_specs(ffn)
        args += _ffn_args(ffn)
    return pl.pallas_call(
        body,
        grid=(m // tm,),
        in_specs=in_specs,
        out_specs=pl.BlockSpec((tm, D_MODEL), lambda i: (i, 0)),
        out_shape=jax.ShapeDtypeStruct((m, D_MODEL), F32),
        compiler_params=_cparams("arbitrary"),
        name="mlstm_output",
    )(*args)


SORT_TOKENS = 512
ROW_GROUP = 16
EXPERT_ROWS = 16
SORT_SLOTS = -(-(TOP_K * SORT_TOKENS + N_EXPERTS * (ROW_GROUP - 1)) // LANES) * LANES
GROUPS_PER_BLOCK = MOE_ROWS // ROW_GROUP
EXPERT_ROW_BUCKETS = (MOE_ROWS // 2, MOE_ROWS)
META_SLOT0, META_SLOT1, META_GATE0, META_GATE1 = 0, 1, 2, 3


def _sort_body(xl_ref, xc_ref, mod_ref, g_ref, wrt_ref, earlier_ref, *rest, tm, rows_per_batch, n_lat):
    i = pl.program_id(0)
    is_lat = i < n_lat
    row = jnp.where(is_lat, (i * tm) // rows_per_batch, CTX_MOD_ROW)
    x = jnp.where(is_lat, xl_ref[...], xc_ref[...])
    if len(rest) == 3:
        as_ref, meta_ref, cnt_ref = rest
    else:
        o_ref, wo_ref, bo_ref, as_ref, meta_ref, cnt_ref, h1_ref = rest
        y = jnp.dot(o_ref[...], wo_ref[...], preferred_element_type=F32) + bo_ref[...]
        x = x + _mod_chunk(mod_ref, row, 2) * y
        h1_ref[...] = x
    a = _norm_mod(x, g_ref[...], _mod_chunk(mod_ref, row, 3), _mod_chunk(mod_ref, row, 4))
    e_io = lax.broadcasted_iota(jnp.int32, (EXPERT_ROWS, tm), 0)
    ab = a.astype(BF16)
    a_rem = (a - ab.astype(F32)).astype(BF16)
    wrt = wrt_ref[...]
    l_head = lax.dot_general(wrt, ab, _NT, preferred_element_type=F32)
    lt = (l_head[:EXPERT_ROWS] + l_head[EXPERT_ROWS:]
          + lax.dot_general(wrt[:EXPERT_ROWS], a_rem, _NT, preferred_element_type=F32))
    lt = jnp.where(e_io < N_EXPERTS, lt, -jnp.inf)
    m0 = jnp.max(lt, axis=0, keepdims=True)
    e0 = jnp.min(jnp.where(lt == m0, e_io, EXPERT_ROWS), axis=0, keepdims=True)
    oh0 = e_io == e0
    lt1 = jnp.where(oh0, -jnp.inf, lt)
    m1 = jnp.max(lt1, axis=0, keepdims=True)
    e1 = jnp.min(jnp.where(lt1 == m1, e_io, EXPERT_ROWS), axis=0, keepdims=True)
    oh1 = e_io == e1
    ex = jnp.exp(m1 - m0)
    gate0 = 1.0 / (1.0 + ex)
    gate1 = ex / (1.0 + ex)
    oh = jnp.where(oh0, 1.0, jnp.where(oh1, 1.0, 0.0))
    rank = jnp.dot(oh.astype(BF16), earlier_ref[...], preferred_element_type=F32)
    cnt = jnp.sum(oh, axis=1, keepdims=True)
    padded = jnp.floor((cnt + (ROW_GROUP - 1)) * (1.0 / ROW_GROUP)) * ROW_GROUP
    r8 = lax.broadcasted_iota(jnp.int32, (EXPERT_ROWS, EXPERT_ROWS), 0)
    c8 = lax.broadcasted_iota(jnp.int32, (EXPERT_ROWS, EXPERT_ROWS), 1)
    padded_row = jnp.sum(jnp.where(r8 == c8, padded, 0.0), axis=0, keepdims=True)
    start = jnp.sum(jnp.where(c8 < r8, padded_row, 0.0), axis=1, keepdims=True)
    slot0 = jnp.sum(jnp.where(oh0, start + rank, 0.0), axis=0, keepdims=True)
    slot1 = jnp.sum(jnp.where(oh1, start + rank, 0.0), axis=0, keepdims=True)
    j_io = lax.broadcasted_iota(jnp.int32, (SORT_SLOTS, tm), 0).astype(F32)
    perm = jnp.where(j_io == slot0, 1.0, jnp.where(j_io == slot1, 1.0, 0.0)).astype(BF16)
    as_ref[...] = jnp.dot(perm, ab, preferred_element_type=F32).astype(BF16)
    rows = jnp.concatenate([slot0, slot1, gate0, gate1, jnp.zeros((LANES - 4, tm), F32)], axis=0)
    meta_ref[...] = rows.T
    cnt_ref[...] = jnp.concatenate([jnp.broadcast_to(padded, (EXPERT_ROWS, LANES)),
                                    jnp.broadcast_to(start, (EXPERT_ROWS, LANES))], axis=0)


def moe_sort(x_lat, x_ctx, mods, layer, g, wrt, pending=None, *, rows_per_batch):
    tm = SORT_TOKENS
    n_lat = x_lat.shape[0] // tm
    n_ctx = 0 if x_ctx is None else x_ctx.shape[0] // tm
    nt = n_lat + n_ctx
    if x_ctx is None:
        x_ctx = x_lat
    w_head = wrt.astype(BF16)
    w_split = jnp.concatenate([w_head, (wrt - w_head.astype(F32)).astype(BF16)], axis=0)
    tok = jnp.arange(tm)
    earlier = (tok[:, None] < tok[None, :]).astype(BF16)
    body = functools.partial(_sort_body, tm=tm, rows_per_batch=rows_per_batch, n_lat=n_lat)
    in_specs = [pl.BlockSpec((tm, D_MODEL), lambda i: (jnp.minimum(i, n_lat - 1), 0)),
                pl.BlockSpec((tm, D_MODEL), lambda i: (jnp.maximum(i - n_lat, 0), 0)),
                _mod_spec(layer),
                pl.BlockSpec((1, D_MODEL), lambda i: (0, 0)),
                pl.BlockSpec((2 * EXPERT_ROWS, D_MODEL), lambda i: (0, 0)),
                pl.BlockSpec((tm, tm), lambda i: (0, 0))]
    args = [x_lat, x_ctx, mods, g.reshape(1, D_MODEL), w_split, earlier]
    out_specs = [pl.BlockSpec((SORT_SLOTS, D_MODEL), lambda i: (i, 0)),
                 pl.BlockSpec((tm, LANES), lambda i: (i, 0)),
                 pl.BlockSpec((None, 2 * EXPERT_ROWS, LANES), lambda i: (i, 0, 0))]
    out_shape = [jax.ShapeDtypeStruct((nt * SORT_SLOTS, D_MODEL), BF16),
                 jax.ShapeDtypeStruct((nt * tm, LANES), F32),
                 jax.ShapeDtypeStruct((nt, 2 * EXPERT_ROWS, LANES), F32)]
    if pending is not None:
        assert n_ctx == 0
        o, w_out, b_out = pending
        in_specs += [pl.BlockSpec((tm, D_MODEL), lambda i: (i, 0)),
                     _resident((D_MODEL, D_MODEL), lambda i: (0, 0)),
                     pl.BlockSpec((1, D_MODEL), lambda i: (0, 0))]
        args += [o, w_out, b_out.reshape(1, D_MODEL)]
        out_specs.append(pl.BlockSpec((tm, D_MODEL), lambda i: (i, 0)))
        out_shape.append(jax.ShapeDtypeStruct((nt * tm, D_MODEL), F32))
    return pl.pallas_call(
        body,
        grid=(nt,),
        in_specs=in_specs,
        out_specs=out_specs,
        out_shape=out_shape,
        compiler_params=_cparams("arbitrary"),
        name="moe_sort",
    )(*args)


def moe_group_table(cnt):
    nt = cnt.shape[0]
    padded = cnt[:, :N_EXPERTS, 0].astype(jnp.int32)
    start = cnt[:, EXPERT_ROWS:EXPERT_ROWS + N_EXPERTS, 0].astype(jnp.int32)
    groups = padded // ROW_GROUP
    cum = jnp.cumsum(groups, axis=0)
    tot = cum[-1]
    blocks = (tot + GROUPS_PER_BLOCK - 1) // GROUPS_PER_BLOCK
    bend = jnp.cumsum(blocks)
    bstart = bend - blocks
    n_blocks = (nt * SORT_SLOTS // ROW_GROUP + N_EXPERTS * (GROUPS_PER_BLOCK - 1)) // GROUPS_PER_BLOCK
    bi = jnp.arange(n_blocks, dtype=jnp.int32)
    block_e = jnp.minimum(jnp.searchsorted(bend, bi, side="right"), N_EXPERTS - 1).astype(jnp.int32)
    q = (bi - bstart[block_e])[:, None] * GROUPS_PER_BLOCK + jnp.arange(GROUPS_PER_BLOCK, dtype=jnp.int32)[None, :]
    n_valid = jnp.clip(tot[block_e] - (bi - bstart[block_e]) * GROUPS_PER_BLOCK, 0, GROUPS_PER_BLOCK).astype(jnp.int32)
    cum_e = cum.T[block_e]
    tile = jnp.sum((cum_e[:, None, :] <= q[:, :, None]).astype(jnp.int32), axis=2)
    tile = jnp.minimum(tile, nt - 1)
    before = jnp.take_along_axis(cum_e - groups.T[block_e], tile, axis=1)
    first = jnp.take_along_axis(start.T[block_e], tile, axis=1)
    rows = tile * SORT_SLOTS + first + (q - before) * ROW_GROUP
    valid = jnp.arange(GROUPS_PER_BLOCK, dtype=jnp.int32)[None, :] < n_valid[:, None]
    rows = jnp.where(valid, rows, 0).astype(jnp.int32)
    n_used = bend[-1].astype(jnp.int32).reshape(1)
    return block_e, n_used, n_valid, rows.reshape(n_blocks, 1, GROUPS_PER_BLOCK)


def _experts_body(be_ref, nu_ref, nv_ref, row_ref, rown_ref, as_hbm, w1_ref, w3_ref, w2_ref, ys_hbm,
               x_ref, acc_ref, y_ref, gsem, ssem, *, nf):
    i = pl.program_id(0)
    f = pl.program_id(1)
    n_used = nu_ref[0]
    slot = i % 2

    def gather_copy(s, gidx, row):
        row = pl.multiple_of(row, ROW_GROUP)
        dst = pl.multiple_of(gidx * ROW_GROUP, ROW_GROUP)
        return pltpu.make_async_copy(as_hbm.at[pl.ds(row, ROW_GROUP), :], x_ref.at[s, pl.ds(dst, ROW_GROUP), :],
                                     gsem.at[s])

    def scatter_copy(s, gidx, row):
        row = pl.multiple_of(row, ROW_GROUP)
        src = pl.multiple_of(gidx * ROW_GROUP, ROW_GROUP)
        return pltpu.make_async_copy(y_ref.at[s, pl.ds(src, ROW_GROUP), :], ys_hbm.at[pl.ds(row, ROW_GROUP), :],
                                     ssem.at[s])

    def loop(n, fn):
        def body(r, c):
            fn(r)
            return c
        lax.fori_loop(0, n, body, 0)

    @pl.when(i < n_used)
    def _():
        @pl.when(f == 0)
        def _():
            @pl.when(i == 0)
            def _():
                x_ref[...] = jnp.zeros_like(x_ref)
                loop(nv_ref[0], lambda r: gather_copy(0, r, row_ref[0, r]).start())

            loop(nv_ref[i], lambda r: gather_copy(slot, r, 0).wait())

            @pl.when(i + 1 < n_used)
            def _():
                loop(nv_ref[i + 1], lambda r: gather_copy(1 - slot, r, rown_ref[0, r]).start())

        def swiglu_rows(nrows):
            x = x_ref[slot, 0:nrows, :]
            u = jnp.dot(x, w1_ref[...], preferred_element_type=F32)
            v = jnp.dot(x, w3_ref[...], preferred_element_type=F32)
            p = (_silu(u) * v).astype(BF16)
            y = jnp.dot(p, w2_ref[...], preferred_element_type=F32)

            @pl.when(f == 0)
            def _():
                acc_ref[0:nrows, :] = y

            @pl.when(jnp.logical_and(f > 0, f < nf - 1))
            def _():
                acc_ref[0:nrows, :] += y

            @pl.when(f == nf - 1)
            def _():
                y_ref[slot, 0:nrows, :] = (acc_ref[0:nrows, :] + y).astype(BF16)

        n_real = nv_ref[i] * ROW_GROUP
        for lo, hi in zip((0,) + EXPERT_ROW_BUCKETS[:-1], EXPERT_ROW_BUCKETS):
            @pl.when(jnp.logical_and(n_real > lo, n_real <= hi))
            def _(hi=hi):
                swiglu_rows(hi)

        @pl.when(f == nf - 1)
        def _():
            loop(nv_ref[i], lambda r: scatter_copy(slot, r, row_ref[0, r]).start())

            @pl.when(i > 0)
            def _():
                loop(nv_ref[i - 1], lambda r: scatter_copy(1 - slot, r, 0).wait())

            @pl.when(i == n_used - 1)
            def _():
                loop(nv_ref[i], lambda r: scatter_copy(slot, r, 0).wait())


def moe_experts_sorted(a_sorted, block_e, n_used, n_valid, rows, w1, w3, w2, w_layer, *, tf=1792):
    n_blocks = rows.shape[0]
    tm = MOE_ROWS
    f_dim = w1.shape[3]
    nf = f_dim // tf
    assert nf >= 2
    body = functools.partial(_experts_body, nf=nf)

    def wmap(kind):
        def index_map(i, f, be, nu, nv):
            live = i < nu[0]
            ff = jnp.where(live, f, nf - 1)
            ii = jnp.where(live, i, nu[0] - 1)
            return (w_layer, be[ii], 0, ff) if kind == "up" else (w_layer, be[ii], ff, 0)
        return index_map

    idx_spec = pl.BlockSpec((None, 1, GROUPS_PER_BLOCK), lambda i, f, be, nu, nv: (i, 0, 0), memory_space=pltpu.SMEM)
    next_spec = pl.BlockSpec((None, 1, GROUPS_PER_BLOCK),
                             lambda i, f, be, nu, nv: (jnp.minimum(i + 1, n_blocks - 1), 0, 0),
                             memory_space=pltpu.SMEM)
    grid_spec = pltpu.PrefetchScalarGridSpec(
        num_scalar_prefetch=3,
        grid=(n_blocks, nf),
        in_specs=[idx_spec, next_spec,
                  pl.BlockSpec(memory_space=pl.ANY),
                  pl.BlockSpec((None, None, D_MODEL, tf), wmap("up")),
                  pl.BlockSpec((None, None, D_MODEL, tf), wmap("up")),
                  pl.BlockSpec((None, None, tf, D_MODEL), wmap("down"))],
        out_specs=pl.BlockSpec(memory_space=pl.ANY),
        scratch_shapes=[pltpu.VMEM((2, tm, D_MODEL), BF16), pltpu.VMEM((tm, D_MODEL), F32),
                        pltpu.VMEM((2, tm, D_MODEL), BF16),
                        pltpu.SemaphoreType.DMA((2,)), pltpu.SemaphoreType.DMA((2,))],
    )
    return pl.pallas_call(
        body,
        grid_spec=grid_spec,
        out_shape=jax.ShapeDtypeStruct(a_sorted.shape, BF16),
        input_output_aliases={5: 0},
        compiler_params=_cparams("arbitrary", "arbitrary"),
        name="moe_experts",
    )(block_e, n_used, n_valid, rows, rows, a_sorted, w1, w3, w2)


def _unsort_body(h_ref, ys_ref, meta_ref, mod_ref, fg_ref, *rest, tm, rows_per_batch, final):
    o_ref = rest[-1] if len(rest) == 1 else rest[-2]
    i = pl.program_id(0)
    row = ((i * tm) // rows_per_batch) if rows_per_batch else CTX_MOD_ROW
    meta = meta_ref[...]
    slot0 = meta[:, META_SLOT0:META_SLOT0 + 1]
    slot1 = meta[:, META_SLOT1:META_SLOT1 + 1]
    gate0 = meta[:, META_GATE0:META_GATE0 + 1]
    gate1 = meta[:, META_GATE1:META_GATE1 + 1]
    j_io = lax.broadcasted_iota(jnp.int32, (tm, SORT_SLOTS), 1).astype(F32)
    pick = jnp.where(j_io == slot0, gate0, jnp.where(j_io == slot1, gate1, 0.0)).astype(BF16)
    fsum = jnp.dot(pick, ys_ref[...], preferred_element_type=F32)
    out = h_ref[...] + _mod_chunk(mod_ref, row, 5) * fsum
    if final:
        ms = jnp.mean(out * out, axis=-1, keepdims=True)
        out = out * lax.rsqrt(ms + NORM_EPS) * fg_ref[...]
    o_ref[...] = out
    if len(rest) > 1:
        mod2_ref, g2_ref, w_ref, b_ref, _, p_ref = rest
        _nm_project(out, mod2_ref, row, g2_ref, w_ref, b_ref, p_ref, 0, 1)


def moe_unsort_combine(h, ys, meta, mods, layer, final_g, proj=None, *, tile_off, rows_per_batch, final):
    tm = SORT_TOKENS
    m = h.shape[0]
    body = functools.partial(_unsort_body, tm=tm, rows_per_batch=rows_per_batch, final=final)
    in_specs = [pl.BlockSpec((tm, D_MODEL), lambda i: (i, 0)),
                pl.BlockSpec((SORT_SLOTS, D_MODEL), lambda i: (tile_off + i, 0)),
                pl.BlockSpec((tm, LANES), lambda i: (tile_off + i, 0)),
                _mod_spec(layer),
                pl.BlockSpec((1, D_MODEL), lambda i: (0, 0))]
    args = [h, ys, meta, mods, final_g.reshape(1, D_MODEL)]
    out_specs = pl.BlockSpec((tm, D_MODEL), lambda i: (i, 0))
    out_shape = jax.ShapeDtypeStruct((m, D_MODEL), F32)
    if proj is not None:
        g2, w, bias = proj
        n = w.shape[1]
        in_specs += [_mod_spec(layer + 1),
                     pl.BlockSpec((1, D_MODEL), lambda i: (0, 0)),
                     _resident((D_MODEL, n), lambda i: (0, 0)),
                     pl.BlockSpec((1, n), lambda i: (0, 0))]
        args += [mods, g2.reshape(1, D_MODEL), w, bias.reshape(1, n)]
        out_specs = [out_specs, pl.BlockSpec((tm, n), lambda i: (i, 0))]
        out_shape = [out_shape, jax.ShapeDtypeStruct((m, n), BF16)]
    return pl.pallas_call(
        body,
        grid=(m // tm,),
        in_specs=in_specs,
        out_specs=out_specs,
        out_shape=out_shape,
        compiler_params=_cparams("arbitrary"),
        name="moe_combine",
    )(*args)


def _na_layer(h_lat, h_ctx, mods, layer, g, w_qkv, b_qkv, rpb, w_out, b_out, ffn, defer_out, with_ctx_out):
    b, seq, _ = h_lat.shape
    c = h_ctx.shape[1]
    qscale = jnp.concatenate([jnp.full((D_MODEL,), (D_MODEL // NA_HEADS) ** -0.5, F32), jnp.ones((2 * D_MODEL,), F32)])
    w = (w_qkv * qscale).astype(BF16)
    bias = b_qkv * qscale
    qkv = nm_matmul(h_lat.reshape(b * seq, D_MODEL), mods, layer, g, w, bias, rows_per_batch=seq, sh=0, sc=1)
    qkv_c = nm_matmul(h_ctx.reshape(b * c, D_MODEL), mods, layer, g, w, bias, rows_per_batch=None, sh=0, sc=1)
    qkv = qkv.reshape(b, seq, 3 * D_MODEL)
    qkv_c = qkv_c.reshape(b, c, 3 * D_MODEL)
    o_lat = na_attention(qkv, qkv_c, na_bias_table(rpb))
    wo = w_out.astype(BF16)
    pending = None
    if defer_out:
        pending = (o_lat.reshape(b * seq, D_MODEL), wo, b_out)
    else:
        h_lat = mm_residual(o_lat.reshape(b * seq, D_MODEL), h_lat.reshape(b * seq, D_MODEL), mods, layer, wo, b_out,
                            ffn, rows_per_batch=seq, gate=2).reshape(b, seq, D_MODEL)
    if with_ctx_out:
        o_ctx = ctx_attention(qkv_c)
        h_ctx = mm_residual(o_ctx.reshape(b * c, D_MODEL), h_ctx.reshape(b * c, D_MODEL), mods, layer, wo, b_out,
                            ffn, rows_per_batch=None, gate=2).reshape(b, c, D_MODEL)
    return h_lat, h_ctx, pending


def mlstm_up_proj(w_up):
    return w_up.astype(BF16), jnp.zeros((w_up.shape[1],), F32)


def _mlstm_layer(h_lat, h_ctx, mods, layer, g, w_up, conv_w, conv_b, w_q, w_k, w_v, w_gates, b_gates,
                 gn_w, skip, w_down, ffn, ups, with_ctx_out):
    b, seq, _ = h_lat.shape
    c = h_ctx.shape[1]
    inner = MLSTM_INNER
    wq_bd = block_diag_weights(w_q).astype(BF16)
    wk_bd = block_diag_weights(w_k).astype(BF16)
    wkt_bd = jnp.swapaxes(wk_bd, 1, 2)
    wv_bd = block_diag_weights(w_v).astype(BF16)
    gxc, gxm = fold_gate_weights(w_q, w_k, w_v, w_gates)
    wd = w_down.astype(BF16)

    def features(h, up, rows_per_batch):
        n, s, _ = h.shape
        if up is None:
            up = nm_matmul(h.reshape(n * s, D_MODEL), mods, layer, g, *mlstm_up_proj(w_up),
                           rows_per_batch=rows_per_batch, sh=0, sc=1)
        up = up.reshape(n, s, 2 * inner)
        return up, mlstm_features(up, conv_w, conv_b, wq_bd, wk_bd, wkt_bd, wv_bd, gxc, gxm, b_gates)

    up_l, up_c = ups if ups is not None else (None, None)
    up_c, (q_c, k_c, kt_c, v_c, xc_c, g_c, gt_c) = features(h_ctx, up_c, None)
    up_l, (q_l, k_l, kt_l, v_l, xc_l, g_l, gt_l) = features(h_lat, up_l, seq)
    hf_c, cf, mf = mlstm_scan(q_c, k_c, kt_c, v_c, g_c, gt_c, None, rev=False)
    hb_c, cb, mb = mlstm_scan(q_c, k_c, kt_c, v_c, g_c, gt_c, None, rev=True)
    hf_l, _, _ = mlstm_scan(q_l, k_l, kt_l, v_l, g_l, gt_l, (cf, mf), rev=False)
    hb_l, _, _ = mlstm_scan(q_l, k_l, kt_l, v_l, g_l, gt_l, (cb, mb), rev=True)
    flat = lambda a: a.reshape(-1, a.shape[-1])
    h_lat = mlstm_output(flat(hf_l), flat(hb_l), flat(xc_l), flat(up_l), flat(h_lat), mods, layer, gn_w, skip, wd,
                         ffn, rows_per_batch=seq).reshape(b, seq, D_MODEL)
    if with_ctx_out:
        h_ctx = mlstm_output(flat(hf_c), flat(hb_c), flat(xc_c), flat(up_c), flat(h_ctx), mods, layer, gn_w, skip, wd,
                             ffn, rows_per_batch=None).reshape(b, c, D_MODEL)
    return h_lat, h_ctx


def _moe_layer(h_lat, h_ctx, mods, layer, g, w_router, w1, w3, w2, w_layer, final_g, proj, pending, last):
    b, seq, _ = h_lat.shape
    c = h_ctx.shape[1]
    wrt = jnp.pad(w_router.T, ((0, EXPERT_ROWS - N_EXPERTS), (0, 0)))
    hl = h_lat.reshape(b * seq, D_MODEL)
    hc = None if last else h_ctx.reshape(b * c, D_MODEL)
    a_sorted, meta, cnt, *updated = moe_sort(hl, hc, mods, layer, g, wrt, pending, rows_per_batch=seq)
    if updated:
        hl = updated[0]
    block_e, n_used, n_valid, rows = moe_group_table(cnt)
    ys = moe_experts_sorted(a_sorted, block_e, n_used, n_valid, rows, w1, w3, w2, w_layer)
    res_l = moe_unsort_combine(hl, ys, meta, mods, layer, final_g, proj, tile_off=0, rows_per_batch=seq, final=last)
    res_c = None
    if not last:
        res_c = moe_unsort_combine(hc, ys, meta, mods, layer, final_g, proj, tile_off=(b * seq) // SORT_TOKENS,
                                   rows_per_batch=None, final=False)
    if proj is None:
        h_lat = res_l.reshape(b, seq, D_MODEL)
        h_ctx = h_ctx if res_c is None else res_c.reshape(b, c, D_MODEL)
        return h_lat, h_ctx, None
    return res_l[0].reshape(b, seq, D_MODEL), res_c[0].reshape(b, c, D_MODEL), (res_l[1], res_c[1])


def kernel(x, c, ctx, c_ctx, w_mod, b_mod, norm_g, final_g, na_w_qkv, na_b_qkv, na_rpb, na_w_out, na_b_out,
           pool_w, pool_scale, ml_w_up, ml_conv_w, ml_conv_b, ml_w_q, ml_w_k, ml_w_v, ml_w_gates, ml_b_gates,
           ml_gn_w, ml_skip, ml_w_down, ffn_w1, ffn_w3, ffn_w2, moe_w_router, moe_w1, moe_w3, moe_w2):
    b, seq, _ = x.shape
    n_ctx = ctx.shape[1]
    depth = w_mod.shape[0]
    assert b <= CTX_MOD_ROW
    cond = jnp.zeros((MOD_ROWS, D_MODEL), F32).at[:b].set(c).at[CTX_MOD_ROW].set(c_ctx)
    mods = adaln_all(cond, w_mod, b_mod)
    moe_w1b, moe_w3b, moe_w2b = moe_w1.astype(BF16), moe_w3.astype(BF16), moe_w2.astype(BF16)
    h_lat, h_ctx = x, ctx
    ups = None
    for i in range(depth):
        last = i == depth - 1
        kind = i % 3
        j = i // 3
        g_tok = norm_g[i, 0]
        e = i // 2
        g_ch = norm_g[i, 1]
        dense = i % 2 == 0
        ffn = (g_ch, ffn_w1[e].astype(BF16), ffn_w3[e].astype(BF16), ffn_w2[e].astype(BF16)) if dense else None
        pending = None
        if kind == 0:
            h_lat, h_ctx, pending = _na_layer(h_lat, h_ctx, mods, i, g_tok, na_w_qkv[j], na_b_qkv[j], na_rpb[j],
                                              na_w_out[j], na_b_out[j], ffn, last and not dense, not last)
        elif kind == 1:
            wp = pool_w[j].astype(BF16)
            h_lat = pool_mixer(h_lat, mods, i, g_tok, wp, pool_scale[j], is_ctx=False)
            if not last:
                h_ctx = pool_mixer(h_ctx, mods, i, g_tok, wp, pool_scale[j], is_ctx=True)
        else:
            h_lat, h_ctx = _mlstm_layer(h_lat, h_ctx, mods, i, g_tok, ml_w_up[j], ml_conv_w[j], ml_conv_b[j],
                                        ml_w_q[j], ml_w_k[j], ml_w_v[j], ml_w_gates[j], ml_b_gates[j],
                                        ml_gn_w[j], ml_skip[j], ml_w_down[j], ffn, ups, not last)
            ups = None
        if dense:
            if kind == 1:
                h_lat = ffn_dense(h_lat.reshape(b * seq, D_MODEL), mods, i, *ffn,
                                  rows_per_batch=seq).reshape(b, seq, D_MODEL)
                if not last:
                    h_ctx = ffn_dense(h_ctx.reshape(b * n_ctx, D_MODEL), mods, i, *ffn,
                                      rows_per_batch=None).reshape(b, n_ctx, D_MODEL)
        else:
            proj = None
            if not last and (i + 1) % 3 == 2:
                proj = (norm_g[i + 1, 0],) + mlstm_up_proj(ml_w_up[(i + 1) // 3])
            h_lat, h_ctx, ups = _moe_layer(h_lat, h_ctx, mods, i, g_ch, moe_w_router[e], moe_w1b, moe_w3b, moe_w2b, e,
                                           final_g, proj, pending, last)
    return h_lat
```

```python
import functools

import jax
import jax.numpy as jnp
from jax import lax
from jax.experimental import pallas as pl
from jax.experimental.pallas import tpu as pltpu

F32 = jnp.float32
BF16 = jnp.bfloat16

D_MODEL = 1024
N_MOD = 6
NORM_EPS = 1e-6
GRID_W = 64
NA_HEADS = 16
NA_WIN_ROWS = 8
NA_WIN_COLS = 16
POOL_WINDOWS = (2, 4, 8, 16)
POOL_GROUP_DIM = D_MODEL // len(POOL_WINDOWS)
MLSTM_INNER = 2 * D_MODEL
MLSTM_HEADS = 4
MLSTM_HEAD_DIM = MLSTM_INNER // MLSTM_HEADS
MLSTM_CONV = 4
MLSTM_QKV_BLOCK = 4
N_EXPERTS = 8
TOP_K = 2

LANES = 128
MOD_ROWS = 8
CTX_MOD_ROW = 4
VMEM_LIMIT_BYTES = 56 * 1024 * 1024
NEG_BIG = -1e30
SCAN_CHUNK = 256
SCAN_STATE_ROWS = MLSTM_HEAD_DIM + 8
MOE_ROWS = 1024


def _cparams(*sem):
    return pltpu.CompilerParams(dimension_semantics=sem, vmem_limit_bytes=VMEM_LIMIT_BYTES)


def _resident(shape, index_map):
    return pl.BlockSpec(shape, index_map, pipeline_mode=pl.Buffered(1))


def _silu(x):
    return x * jax.nn.sigmoid(x)


def _norm_mod(x, g, shift, scale):
    ms = jnp.mean(x * x, axis=-1, keepdims=True)
    y = x * lax.rsqrt(ms + NORM_EPS) * g
    return y * (1.0 + scale) + shift


def _mod_chunk(mod_ref, row, j):
    return mod_ref[pl.ds(row, 1), pl.ds(j * D_MODEL, D_MODEL)]


def _mod_spec(layer):
    return pl.BlockSpec((None, MOD_ROWS, N_MOD * D_MODEL), lambda *_: (layer, 0, 0))


def _adaln_body(c_ref, w_ref, b_ref, o_ref):
    s = _silu(c_ref[...])
    o_ref[...] = jnp.dot(s, w_ref[...], preferred_element_type=F32) + b_ref[...]


def adaln_all(cond, w_mod, b_mod):
    depth = w_mod.shape[0]
    n = N_MOD * D_MODEL
    tn = 1536
    return pl.pallas_call(
        _adaln_body,
        grid=(depth, n // tn),
        in_specs=[pl.BlockSpec((MOD_ROWS, D_MODEL), lambda l, j: (0, 0)),
                  pl.BlockSpec((None, D_MODEL, tn), lambda l, j: (l, 0, j)),
                  pl.BlockSpec((None, 1, tn), lambda l, j: (l, 0, j))],
        out_specs=pl.BlockSpec((None, MOD_ROWS, tn), lambda l, j: (l, 0, j)),
        out_shape=jax.ShapeDtypeStruct((depth, MOD_ROWS, n), F32),
        compiler_params=_cparams("arbitrary", "arbitrary"),
        name="adaln",
    )(cond, w_mod, b_mod.reshape(depth, 1, n))


NM_COLS = 1024


def _nm_project(x, mod_ref, row, g_ref, w_ref, b_ref, o_ref, sh, sc):
    a = _norm_mod(x, g_ref[...], _mod_chunk(mod_ref, row, sh), _mod_chunk(mod_ref, row, sc)).astype(BF16)
    for c in range(o_ref.shape[1] // NM_COLS):
        sl = slice(c * NM_COLS, (c + 1) * NM_COLS)
        y = jnp.dot(a, w_ref[:, sl], preferred_element_type=F32) + b_ref[:, sl]
        o_ref[:, sl] = y.astype(o_ref.dtype)


def _nm_matmul_body(x_ref, mod_ref, g_ref, w_ref, b_ref, o_ref, *, tm, rows_per_batch, sh, sc):
    i = pl.program_id(0)
    row = (i * tm) // rows_per_batch if rows_per_batch else CTX_MOD_ROW
    _nm_project(x_ref[...], mod_ref, row, g_ref, w_ref, b_ref, o_ref, sh, sc)


def nm_matmul(x, mods, layer, g, w, bias, *, rows_per_batch, sh, sc, tm=512, out_dtype=BF16):
    m, n = x.shape[0], w.shape[1]
    tm = min(tm, m)
    body = functools.partial(_nm_matmul_body, tm=tm, rows_per_batch=rows_per_batch, sh=sh, sc=sc)
    return pl.pallas_call(
        body,
        grid=(m // tm,),
        in_specs=[pl.BlockSpec((tm, D_MODEL), lambda i: (i, 0)),
                  _mod_spec(layer),
                  pl.BlockSpec((1, D_MODEL), lambda i: (0, 0)),
                  _resident((D_MODEL, n), lambda i: (0, 0)),
                  pl.BlockSpec((1, n), lambda i: (0, 0))],
        out_specs=pl.BlockSpec((tm, n), lambda i: (i, 0)),
        out_shape=jax.ShapeDtypeStruct((m, n), out_dtype),
        compiler_params=_cparams("arbitrary"),
        name="nm_matmul",
    )(x, mods, g.reshape(1, D_MODEL), w, bias.reshape(1, n))


def _mm_res_body(a_ref, h_ref, mod_ref, w_ref, b_ref, *rest, tm, rows_per_batch, gate):
    o_ref = rest[-1]
    i = pl.program_id(0)
    row = (i * tm) // rows_per_batch if rows_per_batch else CTX_MOD_ROW
    y = jnp.dot(a_ref[...], w_ref[...], preferred_element_type=F32) + b_ref[...]
    h1 = h_ref[...] + _mod_chunk(mod_ref, row, gate) * y
    if len(rest) > 1:
        h1 = _ffn_rows(h1, mod_ref, row, *rest[:4])
    o_ref[...] = h1


def mm_residual(a, h, mods, layer, w, bias, ffn=None, *, rows_per_batch, gate):
    m, k = a.shape
    tm = min(512 if ffn is None else 256, m)
    body = functools.partial(_mm_res_body, tm=tm, rows_per_batch=rows_per_batch, gate=gate)
    in_specs = [pl.BlockSpec((tm, k), lambda i: (i, 0)),
                pl.BlockSpec((tm, D_MODEL), lambda i: (i, 0)),
                _mod_spec(layer),
                _resident((k, D_MODEL), lambda i: (0, 0)),
                pl.BlockSpec((1, D_MODEL), lambda i: (0, 0))]
    args = [a, h, mods, w, bias.reshape(1, D_MODEL)]
    if ffn is not None:
        in_specs += _ffn_specs(ffn)
        args += _ffn_args(ffn)
    return pl.pallas_call(
        body,
        grid=(m // tm,),
        in_specs=in_specs,
        out_specs=pl.BlockSpec((tm, D_MODEL), lambda i: (i, 0)),
        out_shape=jax.ShapeDtypeStruct((m, D_MODEL), F32),
        compiler_params=_cparams("arbitrary"),
        name="mm_residual",
    )(*args)


def _ffn_rows(h, mod_ref, row, g_ref, w1_ref, w3_ref, w2_ref):
    a = _norm_mod(h, g_ref[...], _mod_chunk(mod_ref, row, 3), _mod_chunk(mod_ref, row, 4)).astype(BF16)
    u = jnp.dot(a, w1_ref[...], preferred_element_type=F32)
    v = jnp.dot(a, w3_ref[...], preferred_element_type=F32)
    p = (_silu(u) * v).astype(BF16)
    y = jnp.dot(p, w2_ref[...], preferred_element_type=F32)
    return h + _mod_chunk(mod_ref, row, 5) * y


def _ffn_specs(ffn):
    f = ffn[1].shape[1]
    return [pl.BlockSpec((1, D_MODEL), lambda i: (0, 0)),
            _resident((D_MODEL, f), lambda i: (0, 0)),
            _resident((D_MODEL, f), lambda i: (0, 0)),
            _resident((f, D_MODEL), lambda i: (0, 0))]


def _ffn_args(ffn):
    g, w1, w3, w2 = ffn
    return [g.reshape(1, D_MODEL), w1, w3, w2]


def _ffn_body(h_ref, mod_ref, g_ref, w1_ref, w3_ref, w2_ref, o_ref, *, tm, rows_per_batch):
    i = pl.program_id(0)
    row = (i * tm) // rows_per_batch if rows_per_batch else CTX_MOD_ROW
    o_ref[...] = _ffn_rows(h_ref[...], mod_ref, row, g_ref, w1_ref, w3_ref, w2_ref)


def ffn_dense(h, mods, layer, g, w1, w3, w2, *, rows_per_batch, tm=256):
    m = h.shape[0]
    f = w1.shape[1]
    tm = min(tm, m)
    body = functools.partial(_ffn_body, tm=tm, rows_per_batch=rows_per_batch)
    return pl.pallas_call(
        body,
        grid=(m // tm,),
        in_specs=[pl.BlockSpec((tm, D_MODEL), lambda i: (i, 0)),
                  _mod_spec(layer),
                  pl.BlockSpec((1, D_MODEL), lambda i: (0, 0)),
                  _resident((D_MODEL, f), lambda i: (0, 0)),
                  _resident((D_MODEL, f), lambda i: (0, 0)),
                  _resident((f, D_MODEL), lambda i: (0, 0))],
        out_specs=pl.BlockSpec((tm, D_MODEL), lambda i: (i, 0)),
        out_shape=jax.ShapeDtypeStruct((m, D_MODEL), F32),
        compiler_params=_cparams("arbitrary"),
        name="ffn_dense",
    )(h, mods, g.reshape(1, D_MODEL), w1, w3, w2)


def na_bias_table(rpb):
    h = rpb.shape[0]
    col = jnp.arange(GRID_W)
    c0 = jnp.clip(col - NA_WIN_COLS // 2, 0, GRID_W - NA_WIN_COLS)
    col_ok = (col[None, :] >= c0[:, None]) & (col[None, :] < c0[:, None] + NA_WIN_COLS)
    n_drow = 2 * NA_WIN_ROWS - 1
    strip = jnp.concatenate([jnp.repeat(rpb[:, :, :1], GRID_W - NA_WIN_COLS, axis=2), rpb,
                             jnp.repeat(rpb[:, :, -1:], GRID_W - NA_WIN_COLS, axis=2)], axis=2)
    rel = jnp.stack([strip[:, :, GRID_W - 1 - q:2 * GRID_W - 1 - q] for q in range(GRID_W)], axis=2)
    t = jnp.where(col_ok[None, None], rel.astype(F32), NEG_BIG)
    t = t.transpose(0, 2, 1, 3).reshape(h // 2, 2 * GRID_W, n_drow * GRID_W)
    tiles = [t[:, :, (NA_WIN_ROWS - 1 - off) * GRID_W:(2 * NA_WIN_ROWS - 1 - off) * GRID_W]
             for off in range(NA_WIN_ROWS)]
    return jnp.stack(tiles, axis=1)


def _stack_heads(q):
    lo = lax.broadcasted_iota(jnp.int32, q.shape, 1) < (LANES // 2)
    zero = jnp.zeros_like(q)
    return jnp.concatenate([jnp.where(lo, q, zero), jnp.where(lo, zero, q)], axis=0)


def _unstack_heads(o):
    n = o.shape[0] // 2
    lo = lax.broadcasted_iota(jnp.int32, (n, LANES), 1) < (LANES // 2)
    return jnp.where(lo, o[:n], o[n:])


_NT = (((1,), (1,)), ((), ()))
NA_GROUP = 16


def _na_body(q_ref, k_ref, v_ref, kc_ref, vc_ref, bias_ref, o_ref, *, rows):
    kc = kc_ref[...]
    vc = vc_ref[...]
    kwin = NA_WIN_ROWS * GRID_W

    def scores(r):
        r0 = jnp.clip(r - NA_WIN_ROWS // 2, 0, rows - NA_WIN_ROWS)
        qoff = pl.multiple_of(r * GRID_W, GRID_W)
        koff = pl.multiple_of(r0 * GRID_W, GRID_W)
        qs = _stack_heads(q_ref[pl.ds(qoff, GRID_W), :])
        k = k_ref[pl.ds(koff, kwin), :]
        s_loc = lax.dot_general(qs, k, _NT, preferred_element_type=F32) + bias_ref[r - r0]
        s_ctx = lax.dot_general(qs, kc, _NT, preferred_element_type=F32)
        return qoff, koff, s_loc, s_ctx

    def probs(s_loc, s_ctx):
        m = jnp.maximum(jnp.max(s_loc, axis=-1, keepdims=True), jnp.max(s_ctx, axis=-1, keepdims=True))
        p_loc = jnp.exp(s_loc - m)
        p_ctx = jnp.exp(s_ctx - m)
        l = jnp.sum(p_loc, axis=-1, keepdims=True) + jnp.sum(p_ctx, axis=-1, keepdims=True)
        return p_loc.astype(BF16), p_ctx.astype(BF16), l

    def group(gi, carry):
        sc = [scores(gi * NA_GROUP + u) for u in range(NA_GROUP)]
        pr = [probs(s[2], s[3]) for s in sc]
        for (qoff, koff, _, _), (p_loc, p_ctx, l) in zip(sc, pr):
            v = v_ref[pl.ds(koff, kwin), :]
            o = (jnp.dot(p_loc, v, preferred_element_type=F32)
                 + jnp.dot(p_ctx, vc, preferred_element_type=F32)) / l
            o_ref[pl.ds(qoff, GRID_W), :] = _unstack_heads(o).astype(o_ref.dtype)
        return carry

    lax.fori_loop(0, rows // NA_GROUP, group, 0)


def na_attention(qkv, qkv_ctx, bias):
    b, l, _ = qkv.shape
    c = qkv_ctx.shape[1]
    hp = D_MODEL // LANES
    body = functools.partial(_na_body, rows=l // GRID_W)
    return pl.pallas_call(
        body,
        grid=(b, hp),
        in_specs=[pl.BlockSpec((None, l, LANES), lambda i, j: (i, 0, j)),
                  pl.BlockSpec((None, l, LANES), lambda i, j: (i, 0, hp + j)),
                  pl.BlockSpec((None, l, LANES), lambda i, j: (i, 0, 2 * hp + j)),
                  pl.BlockSpec((None, c, LANES), lambda i, j: (i, 0, hp + j)),
                  pl.BlockSpec((None, c, LANES), lambda i, j: (i, 0, 2 * hp + j)),
                  pl.BlockSpec((None,) + bias.shape[1:], lambda i, j: (j, 0, 0, 0))],
        out_specs=pl.BlockSpec((None, l, LANES), lambda i, j: (i, 0, j)),
        out_shape=jax.ShapeDtypeStruct((b, l, D_MODEL), BF16),
        compiler_params=_cparams("arbitrary", "arbitrary"),
        name="na_attention",
    )(qkv, qkv, qkv, qkv_ctx, qkv_ctx, bias)


def _ctx_attn_body(q_ref, k_ref, v_ref, o_ref):
    qs = _stack_heads(q_ref[...])
    s = lax.dot_general(qs, k_ref[...], _NT, preferred_element_type=F32)
    p = jnp.exp(s - jnp.max(s, axis=-1, keepdims=True))
    l = jnp.sum(p, axis=-1, keepdims=True)
    o = jnp.dot(p.astype(BF16), v_ref[...], preferred_element_type=F32) / l
    o_ref[...] = _unstack_heads(o).astype(o_ref.dtype)


def ctx_attention(qkv_ctx):
    b, c, _ = qkv_ctx.shape
    hp = D_MODEL // LANES
    return pl.pallas_call(
        _ctx_attn_body,
        grid=(b, hp),
        in_specs=[pl.BlockSpec((None, c, LANES), lambda i, j: (i, 0, j)),
                  pl.BlockSpec((None, c, LANES), lambda i, j: (i, 0, hp + j)),
                  pl.BlockSpec((None, c, LANES), lambda i, j: (i, 0, 2 * hp + j))],
        out_specs=pl.BlockSpec((None, c, LANES), lambda i, j: (i, 0, j)),
        out_shape=jax.ShapeDtypeStruct((b, c, D_MODEL), BF16),
        compiler_params=_cparams("arbitrary", "arbitrary"),
        name="ctx_attention",
    )(qkv_ctx, qkv_ctx, qkv_ctx)


POOL_HALO = 8


def _pool_body(prev_ref, cur_ref, next_ref, mod_ref, g_ref, wp_ref, ps_ref, o_ref, *, tl, seq, is_ctx):
    b = pl.program_id(0)
    j = pl.program_id(1)
    row = CTX_MOD_ROW if is_ctx else b
    g = g_ref[...]
    sh = _mod_chunk(mod_ref, row, 0)
    sc = _mod_chunk(mod_ref, row, 1)
    h = cur_ref[...]
    a_cur = _norm_mod(h, g, sh, sc)
    a_prev = _norm_mod(prev_ref[...], g, sh, sc) * (j > 0).astype(F32)
    a_next = _norm_mod(next_ref[...], g, sh, sc) * (j < seq // tl - 1).astype(F32)
    ext = jnp.concatenate([a_prev, a_cur, a_next], axis=0)
    t = j * tl + lax.broadcasted_iota(jnp.int32, (tl, 1), 0)
    outs = []
    for gi, w in enumerate(POOL_WINDOWS):
        sl = slice(gi * POOL_GROUP_DIM, (gi + 1) * POOL_GROUP_DIM)
        p = ext[:, sl]
        step = 1
        while step < w:
            n = p.shape[0]
            p = p[:n - step] + p[step:]
            step *= 2
        off = POOL_HALO - w // 2
        cnt = jnp.minimum(t + w // 2, seq) - jnp.maximum(t - w // 2, 0)
        pooled = p[off:off + tl] / cnt.astype(F32) - a_cur[:, sl]
        outs.append(jnp.dot(pooled.astype(BF16), wp_ref[gi], preferred_element_type=F32))
    y = jnp.concatenate(outs, axis=1) * ps_ref[...]
    o_ref[...] = h + _mod_chunk(mod_ref, row, 2) * y


def pool_mixer(h, mods, layer, g, w_pool, pool_scale, *, is_ctx, tl=512):
    b, seq, _ = h.shape
    tl = min(tl, seq)
    nh = tl // POOL_HALO
    last = seq // POOL_HALO - 1
    body = functools.partial(_pool_body, tl=tl, seq=seq, is_ctx=is_ctx)
    return pl.pallas_call(
        body,
        grid=(b, seq // tl),
        in_specs=[pl.BlockSpec((None, POOL_HALO, D_MODEL), lambda i, j: (i, jnp.maximum(j * nh - 1, 0), 0)),
                  pl.BlockSpec((None, tl, D_MODEL), lambda i, j: (i, j, 0)),
                  pl.BlockSpec((None, POOL_HALO, D_MODEL), lambda i, j: (i, jnp.minimum((j + 1) * nh, last), 0)),
                  _mod_spec(layer),
                  pl.BlockSpec((1, D_MODEL), lambda i, j: (0, 0)),
                  pl.BlockSpec(w_pool.shape, lambda i, j: (0, 0, 0)),
                  pl.BlockSpec((1, D_MODEL), lambda i, j: (0, 0))],
        out_specs=pl.BlockSpec((None, tl, D_MODEL), lambda i, j: (i, j, 0)),
        out_shape=jax.ShapeDtypeStruct(h.shape, F32),
        compiler_params=_cparams("arbitrary", "arbitrary"),
        name="pool_mixer",
    )(h, h, h, mods, g.reshape(1, D_MODEL), w_pool, pool_scale.reshape(1, D_MODEL))


CONV_HALO = 16


def block_diag_weights(w):
    nb = LANES // MLSTM_QKV_BLOCK
    wc = w.reshape(-1, nb, MLSTM_QKV_BLOCK, MLSTM_QKV_BLOCK)
    eye = jnp.eye(nb, dtype=w.dtype)
    bd = jnp.einsum("cnij,nm->cnimj", wc, eye)
    return bd.reshape(-1, LANES, LANES)


def fold_gate_weights(w_q, w_k, w_v, w_gates):
    ng = w_gates.shape[1]
    wg = w_gates.reshape(3, -1, MLSTM_QKV_BLOCK, ng)
    fold = lambda w, part: jnp.einsum("ncd,ndg->ncg", w, wg[part], precision=lax.Precision.HIGHEST).reshape(-1, ng)
    return fold(w_q, 0) + fold(w_k, 1), fold(w_v, 2)


def _ml_feat_body(prev_ref, cur_ref, next_ref, cw_ref, cb_ref, wq_ref, wk_ref, wkt_ref, wv_ref,
                  gxc_ref, gxm_ref, gxct_ref, gxmt_ref, bg_ref, bgt_ref,
                  q_ref, k_ref, kt_ref, v_ref, xc_ref, g_ref, gt_ref, ext_ref, *, tl, seq):
    j = pl.program_id(1)
    cur = cur_ref[...]
    ext_ref[0:CONV_HALO, :] = prev_ref[...].astype(F32) * (j > 0).astype(F32)
    ext_ref[CONV_HALO:CONV_HALO + tl, :] = cur.astype(F32)
    ext_ref[CONV_HALO + tl:, :] = next_ref[...].astype(F32) * (j < seq // tl - 1).astype(F32)
    left = MLSTM_CONV // 2
    xc = cb_ref[...]
    for tap in range(MLSTM_CONV):
        xc = xc + ext_ref[pl.ds(CONV_HALO - left + tap, tl), :] * cw_ref[tap:tap + 1, :]
    xc = _silu(xc)
    xcb = xc.astype(BF16)
    xc_ref[...] = xcb
    t = SCAN_CHUNK
    qscale = MLSTM_HEAD_DIM ** -0.5
    for c in range(MLSTM_INNER // LANES):
        sl = slice(c * LANES, (c + 1) * LANES)
        xs = xcb[:, sl]
        q = jnp.dot(xs, wq_ref[c], preferred_element_type=F32)
        k = jnp.dot(xs, wk_ref[c], preferred_element_type=F32)
        v = jnp.dot(cur[:, sl], wv_ref[c], preferred_element_type=F32)
        q_ref[:, sl] = (q * qscale).astype(BF16)
        k_ref[:, sl] = k.astype(BF16)
        v_ref[:, sl] = v.astype(BF16)
        for cc in range(tl // t):
            kt = lax.dot_general(wkt_ref[c], xs[cc * t:(cc + 1) * t], _NT, preferred_element_type=F32)
            kt_ref[cc, sl, :] = kt.astype(BF16)
    ng = g_ref.shape[1]
    g = (jnp.dot(xcb, gxc_ref[...], preferred_element_type=F32)
         + jnp.dot(cur, gxm_ref[...], preferred_element_type=F32))
    g_ref[...] = g[:, :ng] + bg_ref[...]
    gt = (lax.dot_general(gxct_ref[...], xcb, _NT, preferred_element_type=F32)
          + lax.dot_general(gxmt_ref[...], cur, _NT, preferred_element_type=F32)) + bgt_ref[...]
    for cc in range(tl // t):
        gt_ref[cc] = gt[:, cc * t:(cc + 1) * t]


def mlstm_features(up, conv_w, conv_b, wq_bd, wk_bd, wkt_bd, wv_bd, gxc, gxm, bg, *, tl=512):
    b, seq, _ = up.shape
    tl = min(tl, seq)
    t = SCAN_CHUNK
    nh = tl // CONV_HALO
    last = seq // CONV_HALO - 1
    ng = gxc.shape[1]
    inner = MLSTM_INNER
    body = functools.partial(_ml_feat_body, tl=tl, seq=seq)
    full = lambda a: pl.BlockSpec(a.shape, lambda i, j: (0,) * a.ndim)
    cw = conv_w
    cb = conv_b.reshape(1, inner)
    bgr = bg.reshape(1, ng)
    bgc = bg.reshape(ng, 1)
    pad = lambda w: jnp.pad(w, ((0, 0), (0, LANES - ng))).astype(BF16)
    gxc_p, gxm_p = pad(gxc), pad(gxm)
    gxc_t, gxm_t = gxc.T.astype(BF16), gxm.T.astype(BF16)
    return pl.pallas_call(
        body,
        grid=(b, seq // tl),
        in_specs=[pl.BlockSpec((None, CONV_HALO, inner), lambda i, j: (i, jnp.maximum(j * nh - 1, 0), 0)),
                  pl.BlockSpec((None, tl, inner), lambda i, j: (i, j, 0)),
                  pl.BlockSpec((None, CONV_HALO, inner), lambda i, j: (i, jnp.minimum((j + 1) * nh, last), 0)),
                  full(cw), full(cb), full(wq_bd), full(wk_bd), full(wkt_bd), full(wv_bd),
                  full(gxc_p), full(gxm_p), full(gxc_t), full(gxm_t), full(bgr), full(bgc)],
        out_specs=[pl.BlockSpec((None, tl, inner), lambda i, j: (i, j, 0)),
                   pl.BlockSpec((None, tl, inner), lambda i, j: (i, j, 0)),
                   pl.BlockSpec((None, tl // t, inner, t), lambda i, j: (i, j, 0, 0)),
                   pl.BlockSpec((None, tl, inner), lambda i, j: (i, j, 0)),
                   pl.BlockSpec((None, tl, inner), lambda i, j: (i, j, 0)),
                   pl.BlockSpec((None, tl, ng), lambda i, j: (i, j, 0)),
                   pl.BlockSpec((None, tl // t, ng, t), lambda i, j: (i, j, 0, 0))],
        out_shape=[jax.ShapeDtypeStruct((b, seq, inner), BF16),
                   jax.ShapeDtypeStruct((b, seq, inner), BF16),
                   jax.ShapeDtypeStruct((b, seq // t, inner, t), BF16),
                   jax.ShapeDtypeStruct((b, seq, inner), BF16),
                   jax.ShapeDtypeStruct((b, seq, inner), BF16),
                   jax.ShapeDtypeStruct((b, seq, ng), F32),
                   jax.ShapeDtypeStruct((b, seq // t, ng, t), F32)],
        scratch_shapes=[pltpu.VMEM((tl + 2 * CONV_HALO, inner), F32)],
        compiler_params=_cparams("arbitrary", "arbitrary"),
        name="mlstm_features",
    )(up, up, up, cw, cb, wq_bd, wk_bd, wkt_bd, wv_bd, gxc_p, gxm_p, gxc_t, gxm_t, bgr, bgc)


def _log_sigmoid(x):
    return jnp.minimum(x, 0.0) - jnp.log1p(jnp.exp(-jnp.abs(x)))


def _scan_body(*refs, rev, nchunk, nblk, has_init):
    if has_init:
        q_ref, k_ref, kt_ref, v_ref, g_ref, gt_ref, c0_ref, m0_ref, h_ref, cf_ref, mf_ref, c_sc, m_sc = refs
    else:
        q_ref, k_ref, kt_ref, v_ref, g_ref, gt_ref, h_ref, cf_ref, mf_ref, c_sc, m_sc = refs
    hd = pl.program_id(1)
    j = pl.program_id(2)
    t = SCAN_CHUNK
    dh = MLSTM_HEAD_DIM

    @pl.when(j == 0)
    def _():
        if has_init:
            c_sc[...] = c0_ref[...]
            m_sc[...] = m0_ref[...]
        else:
            c_sc[...] = jnp.zeros_like(c_sc)
            m_sc[...] = jnp.zeros_like(m_sc)

    ci = (2 if rev else 0) * MLSTM_HEADS + hd
    cf = (3 if rev else 1) * MLSTM_HEADS + hd
    ng = g_ref.shape[1]
    lane = lax.broadcasted_iota(jnp.int32, (t, ng), 1)
    r_io = lax.broadcasted_iota(jnp.int32, (t, t), 0)
    c_io = lax.broadcasted_iota(jnp.int32, (t, t), 1)
    seen = (c_io >= r_io) if rev else (c_io <= r_io)
    seen_t = (r_io >= c_io) if rev else (r_io <= c_io)
    seen_f = seen.astype(F32)
    seen_tf = seen_t.astype(F32)

    order = range(nchunk - 1, -1, -1) if rev else range(nchunk)
    for cc in order:
        rows = slice(cc * t, (cc + 1) * t)
        q = q_ref[rows, :]
        kt = kt_ref[cc]
        v = v_ref[rows, :]
        g = g_ref[rows, :]
        i_col = jnp.sum(jnp.where(lane == ci, g, 0.0), axis=1, keepdims=True)
        f_col = jnp.sum(jnp.where(lane == cf, g, 0.0), axis=1, keepdims=True)
        i_row = gt_ref[cc, pl.ds(ci, 1), :]
        f_row = gt_ref[cc, pl.ds(cf, 1), :]
        lf_col = _log_sigmoid(f_col)
        lf_row = _log_sigmoid(f_row)
        b_col = jnp.sum(seen_f * lf_row, axis=1, keepdims=True)
        b_row = jnp.sum(seen_tf * lf_col, axis=0, keepdims=True)
        m_prev = m_sc[0:1, 0:1]
        n_row = c_sc[dh:dh + 1, :]
        dmat = jnp.where(seen, b_col - b_row + i_row, NEG_BIG)
        inter = b_col + m_prev
        m_t = jnp.maximum(inter, jnp.max(dmat, axis=1, keepdims=True))
        a = jnp.dot(q, kt, preferred_element_type=F32) * jnp.exp(dmat - m_t)
        w_int = jnp.exp(inter - m_t)
        cb = c_sc[0:dh, :].astype(BF16)
        num = (jnp.dot(a.astype(BF16), v, preferred_element_type=F32)
               + jnp.dot(q, cb, preferred_element_type=F32) * w_int)
        den = (jnp.sum(a, axis=1, keepdims=True)
               + w_int * jnp.sum(q.astype(F32) * n_row, axis=1, keepdims=True))
        hc = num / jnp.maximum(jnp.abs(den), jnp.exp(-m_t))
        h_ref[rows, :] = hc.astype(h_ref.dtype)
        b_end = jnp.sum(lf_row, axis=1, keepdims=True)
        g_row = b_end - b_row + i_row
        m_new = jnp.maximum(b_end + m_prev, jnp.max(g_row, axis=1, keepdims=True))
        decay = jnp.exp(b_end + m_prev - m_new)
        kw = (kt.astype(F32) * jnp.exp(g_row - m_new)).astype(BF16)
        w_col = jnp.exp(b_end - b_col + i_col - m_new)
        c_sc[0:dh, :] = decay * c_sc[0:dh, :] + jnp.dot(kw, v, preferred_element_type=F32)
        c_sc[dh:dh + 1, :] = decay * n_row + jnp.sum(k_ref[rows, :].astype(F32) * w_col, axis=0, keepdims=True)
        m_sc[...] = jnp.broadcast_to(m_new, m_sc.shape)

    @pl.when(j == nblk - 1)
    def _():
        cf_ref[...] = c_sc[...]
        mf_ref[...] = m_sc[...]


def mlstm_scan(q, k, kt, v, g, gt, state, *, rev, tb=1024):
    b, seq, inner = q.shape
    t = SCAN_CHUNK
    tb = min(tb, seq)
    nblk = seq // tb
    nchunk = tb // t
    dh = MLSTM_HEAD_DIM
    ng = g.shape[2]
    has_init = state is not None
    blk = (lambda j: nblk - 1 - j) if rev else (lambda j: j)
    body = functools.partial(_scan_body, rev=rev, nchunk=nchunk, nblk=nblk, has_init=has_init)
    tok_spec = pl.BlockSpec((None, tb, dh), lambda i, h, j: (i, blk(j), h))
    in_specs = [tok_spec, tok_spec,
                pl.BlockSpec((None, nchunk, dh, t), lambda i, h, j: (i, blk(j), h, 0)),
                tok_spec,
                pl.BlockSpec((None, tb, ng), lambda i, h, j: (i, blk(j), 0)),
                pl.BlockSpec((None, nchunk, ng, t), lambda i, h, j: (i, blk(j), 0, 0))]
    args = [q, k, kt, v, g, gt]
    st_spec_c = pl.BlockSpec((None, None, SCAN_STATE_ROWS, dh), lambda i, h, j: (i, h, 0, 0))
    st_spec_m = pl.BlockSpec((None, None, 8, LANES), lambda i, h, j: (i, h, 0, 0))
    if has_init:
        in_specs += [st_spec_c, st_spec_m]
        args += list(state)
    return pl.pallas_call(
        body,
        grid=(b, MLSTM_HEADS, nblk),
        in_specs=in_specs,
        out_specs=[tok_spec, st_spec_c, st_spec_m],
        out_shape=[jax.ShapeDtypeStruct((b, seq, inner), BF16),
                   jax.ShapeDtypeStruct((b, MLSTM_HEADS, SCAN_STATE_ROWS, dh), F32),
                   jax.ShapeDtypeStruct((b, MLSTM_HEADS, 8, LANES), F32)],
        scratch_shapes=[pltpu.VMEM((SCAN_STATE_ROWS, dh), F32), pltpu.VMEM((8, LANES), F32)],
        compiler_params=_cparams("arbitrary", "arbitrary", "arbitrary"),
        name="mlstm_scan_bwd" if rev else "mlstm_scan_fwd",
    )(*args)


ML_OUT_ROWS = 256


def _ml_out_body(hf_ref, hb_ref, xc_ref, z_ref, h_ref, mod_ref, gn_ref, sk_ref, w_ref, *rest, tm, rows_per_batch):
    o_ref = rest[-1]
    i = pl.program_id(0)
    row = (i * tm) // rows_per_batch if rows_per_batch else CTX_MOD_ROW

    def gated(rows):
        hs = hf_ref[rows, :].astype(F32) + hb_ref[rows, :].astype(F32)
        parts = []
        for hd in range(MLSTM_HEADS):
            x = hs[:, hd * MLSTM_HEAD_DIM:(hd + 1) * MLSTM_HEAD_DIM]
            mu = jnp.mean(x, axis=-1, keepdims=True)
            xm = x - mu
            var = jnp.mean(xm * xm, axis=-1, keepdims=True)
            parts.append(xm * lax.rsqrt(var + NORM_EPS))
        hn = jnp.concatenate(parts, axis=1) * gn_ref[...]
        y = (hn + sk_ref[...] * xc_ref[rows, :].astype(F32)) * _silu(z_ref[rows, :].astype(F32))
        return y.astype(BF16)

    groups = [slice(s * ML_OUT_ROWS, (s + 1) * ML_OUT_ROWS) for s in range(tm // ML_OUT_ROWS)]
    ys = [gated(rows) for rows in groups]
    for rows, y in zip(groups, ys):
        y = jnp.dot(y, w_ref[...], preferred_element_type=F32)
        h1 = h_ref[rows, :] + _mod_chunk(mod_ref, row, 2) * y
        if len(rest) > 1:
            g_ref, w1_ref, w3_ref, w2_ref = rest[:4]
            h1 = _ffn_rows(h1, mod_ref, row, g_ref, w1_ref, w3_ref, w2_ref)
        o_ref[rows, :] = h1


def mlstm_output(hf, hb, xc, up, h, mods, layer, gn_w, skip, w_down, ffn=None, *, rows_per_batch, tm=256):
    m = h.shape[0]
    tm = min(tm, m)
    inner = MLSTM_INNER
    body = functools.partial(_ml_out_body, tm=tm, rows_per_batch=rows_per_batch)
    row_spec = pl.BlockSpec((tm, inner), lambda i: (i, 0))
    in_specs = [row_spec, row_spec, row_spec,
                pl.BlockSpec((tm, inner), lambda i: (i, 1)),
                pl.BlockSpec((tm, D_MODEL), lambda i: (i, 0)),
                _mod_spec(layer),
                pl.BlockSpec((1, inner), lambda i: (0, 0)),
                pl.BlockSpec((1, inner), lambda i: (0, 0)),
                _resident((inner, D_MODEL), lambda i: (0, 0))]
    args = [hf, hb, xc, up, h, mods, gn_w.reshape(1, inner), skip.reshape(1, inner), w_down]
    if ffn is not None:
        in_specs += _ffn_specs(ffn)
        args += _ffn_args(ffn)
    return pl.pallas_call(
        body,
        grid=(m // tm,),
        in_specs=in_specs,
        out_specs=pl.BlockSpec((tm, D_MODEL), lambda i: (i, 0)),
        out_shape=jax.ShapeDtypeStruct((m, D_MODEL), F32),
        compiler_params=_cparams("arbitrary"),
        name="mlstm_output",
    )(*args)


SORT_TOKENS = 512
ROW_GROUP = 16
EXPERT_ROWS = 16
SORT_SLOTS = -(-(TOP_K * SORT_TOKENS + N_EXPERTS * (ROW_GROUP - 1)) // LANES) * LANES
GROUPS_PER_BLOCK = MOE_ROWS // ROW_GROUP
EXPERT_ROW_BUCKETS = (MOE_ROWS // 2, MOE_ROWS)
META_SLOT0, META_SLOT1, META_GATE0, META_GATE1 = 0, 1, 2, 3


def _sort_body(xl_ref, xc_ref, mod_ref, g_ref, wrt_ref, earlier_ref, *rest, tm, rows_per_batch, n_lat):
    i = pl.program_id(0)
    is_lat = i < n_lat
    row = jnp.where(is_lat, (i * tm) // rows_per_batch, CTX_MOD_ROW)
    x = jnp.where(is_lat, xl_ref[...], xc_ref[...])
    if len(rest) == 3:
        as_ref, meta_ref, cnt_ref = rest
    else:
        o_ref, wo_ref, bo_ref, as_ref, meta_ref, cnt_ref, h1_ref = rest
        y = jnp.dot(o_ref[...], wo_ref[...], preferred_element_type=F32) + bo_ref[...]
        x = x + _mod_chunk(mod_ref, row, 2) * y
        h1_ref[...] = x
    a = _norm_mod(x, g_ref[...], _mod_chunk(mod_ref, row, 3), _mod_chunk(mod_ref, row, 4))
    e_io = lax.broadcasted_iota(jnp.int32, (EXPERT_ROWS, tm), 0)
    ab = a.astype(BF16)
    a_rem = (a - ab.astype(F32)).astype(BF16)
    wrt = wrt_ref[...]
    l_head = lax.dot_general(wrt, ab, _NT, preferred_element_type=F32)
    lt = (l_head[:EXPERT_ROWS] + l_head[EXPERT_ROWS:]
          + lax.dot_general(wrt[:EXPERT_ROWS], a_rem, _NT, preferred_element_type=F32))
    lt = jnp.where(e_io < N_EXPERTS, lt, -jnp.inf)
    m0 = jnp.max(lt, axis=0, keepdims=True)
    e0 = jnp.min(jnp.where(lt == m0, e_io, EXPERT_ROWS), axis=0, keepdims=True)
    oh0 = e_io == e0
    lt1 = jnp.where(oh0, -jnp.inf, lt)
    m1 = jnp.max(lt1, axis=0, keepdims=True)
    e1 = jnp.min(jnp.where(lt1 == m1, e_io, EXPERT_ROWS), axis=0, keepdims=True)
    oh1 = e_io == e1
    ex = jnp.exp(m1 - m0)
    gate0 = 1.0 / (1.0 + ex)
    gate1 = ex / (1.0 + ex)
    oh = jnp.where(oh0, 1.0, jnp.where(oh1, 1.0, 0.0))
    rank = jnp.dot(oh.astype(BF16), earlier_ref[...], preferred_element_type=F32)
    cnt = jnp.sum(oh, axis=1, keepdims=True)
    padded = jnp.floor((cnt + (ROW_GROUP - 1)) * (1.0 / ROW_GROUP)) * ROW_GROUP
    r8 = lax.broadcasted_iota(jnp.int32, (EXPERT_ROWS, EXPERT_ROWS), 0)
    c8 = lax.broadcasted_iota(jnp.int32, (EXPERT_ROWS, EXPERT_ROWS), 1)
    padded_row = jnp.sum(jnp.where(r8 == c8, padded, 0.0), axis=0, keepdims=True)
    start = jnp.sum(jnp.where(c8 < r8, padded_row, 0.0), axis=1, keepdims=True)
    slot0 = jnp.sum(jnp.where(oh0, start + rank, 0.0), axis=0, keepdims=True)
    slot1 = jnp.sum(jnp.where(oh1, start + rank, 0.0), axis=0, keepdims=True)
    j_io = lax.broadcasted_iota(jnp.int32, (SORT_SLOTS, tm), 0).astype(F32)
    perm = jnp.where(j_io == slot0, 1.0, jnp.where(j_io == slot1, 1.0, 0.0)).astype(BF16)
    as_ref[...] = jnp.dot(perm, ab, preferred_element_type=F32).astype(BF16)
    rows = jnp.concatenate([slot0, slot1, gate0, gate1, jnp.zeros((LANES - 4, tm), F32)], axis=0)
    meta_ref[...] = rows.T
    cnt_ref[...] = jnp.concatenate([jnp.broadcast_to(padded, (EXPERT_ROWS, LANES)),
                                    jnp.broadcast_to(start, (EXPERT_ROWS, LANES))], axis=0)


def moe_sort(x_lat, x_ctx, mods, layer, g, wrt, pending=None, *, rows_per_batch):
    tm = SORT_TOKENS
    n_lat = x_lat.shape[0] // tm
    n_ctx = 0 if x_ctx is None else x_ctx.shape[0] // tm
    nt = n_lat + n_ctx
    if x_ctx is None:
        x_ctx = x_lat
    w_head = wrt.astype(BF16)
    w_split = jnp.concatenate([w_head, (wrt - w_head.astype(F32)).astype(BF16)], axis=0)
    tok = jnp.arange(tm)
    earlier = (tok[:, None] < tok[None, :]).astype(BF16)
    body = functools.partial(_sort_body, tm=tm, rows_per_batch=rows_per_batch, n_lat=n_lat)
    in_specs = [pl.BlockSpec((tm, D_MODEL), lambda i: (jnp.minimum(i, n_lat - 1), 0)),
                pl.BlockSpec((tm, D_MODEL), lambda i: (jnp.maximum(i - n_lat, 0), 0)),
                _mod_spec(layer),
                pl.BlockSpec((1, D_MODEL), lambda i: (0, 0)),
                pl.BlockSpec((2 * EXPERT_ROWS, D_MODEL), lambda i: (0, 0)),
                pl.BlockSpec((tm, tm), lambda i: (0, 0))]
    args = [x_lat, x_ctx, mods, g.reshape(1, D_MODEL), w_split, earlier]
    out_specs = [pl.BlockSpec((SORT_SLOTS, D_MODEL), lambda i: (i, 0)),
                 pl.BlockSpec((tm, LANES), lambda i: (i, 0)),
                 pl.BlockSpec((None, 2 * EXPERT_ROWS, LANES), lambda i: (i, 0, 0))]
    out_shape = [jax.ShapeDtypeStruct((nt * SORT_SLOTS, D_MODEL), BF16),
                 jax.ShapeDtypeStruct((nt * tm, LANES), F32),
                 jax.ShapeDtypeStruct((nt, 2 * EXPERT_ROWS, LANES), F32)]
    if pending is not None:
        assert n_ctx == 0
        o, w_out, b_out = pending
        in_specs += [pl.BlockSpec((tm, D_MODEL), lambda i: (i, 0)),
                     _resident((D_MODEL, D_MODEL), lambda i: (0, 0)),
                     pl.BlockSpec((1, D_MODEL), lambda i: (0, 0))]
        args += [o, w_out, b_out.reshape(1, D_MODEL)]
        out_specs.append(pl.BlockSpec((tm, D_MODEL), lambda i: (i, 0)))
        out_shape.append(jax.ShapeDtypeStruct((nt * tm, D_MODEL), F32))
    return pl.pallas_call(
        body,
        grid=(nt,),
        in_specs=in_specs,
        out_specs=out_specs,
        out_shape=out_shape,
        compiler_params=_cparams("arbitrary"),
        name="moe_sort",
    )(*args)


def moe_group_table(cnt):
    nt = cnt.shape[0]
    padded = cnt[:, :N_EXPERTS, 0].astype(jnp.int32)
    start = cnt[:, EXPERT_ROWS:EXPERT_ROWS + N_EXPERTS, 0].astype(jnp.int32)
    groups = padded // ROW_GROUP
    cum = jnp.cumsum(groups, axis=0)
    tot = cum[-1]
    blocks = (tot + GROUPS_PER_BLOCK - 1) // GROUPS_PER_BLOCK
    bend = jnp.cumsum(blocks)
    bstart = bend - blocks
    n_blocks = (nt * SORT_SLOTS // ROW_GROUP + N_EXPERTS * (GROUPS_PER_BLOCK - 1)) // GROUPS_PER_BLOCK
    bi = jnp.arange(n_blocks, dtype=jnp.int32)
    block_e = jnp.minimum(jnp.sum((bend[None, :] <= bi[:, None]).astype(jnp.int32), axis=1), N_EXPERTS - 1)
    q = (bi - bstart[block_e])[:, None] * GROUPS_PER_BLOCK + jnp.arange(GROUPS_PER_BLOCK, dtype=jnp.int32)[None, :]
    n_valid = jnp.clip(tot[block_e] - (bi - bstart[block_e]) * GROUPS_PER_BLOCK, 0, GROUPS_PER_BLOCK).astype(jnp.int32)
    cum_e = cum.T[block_e]
    tile = jnp.sum((cum_e[:, None, :] <= q[:, :, None]).astype(jnp.int32), axis=2)
    tile = jnp.minimum(tile, nt - 1)
    before = jnp.take_along_axis(cum_e - groups.T[block_e], tile, axis=1)
    first = jnp.take_along_axis(start.T[block_e], tile, axis=1)
    rows = tile * SORT_SLOTS + first + (q - before) * ROW_GROUP
    valid = jnp.arange(GROUPS_PER_BLOCK, dtype=jnp.int32)[None, :] < n_valid[:, None]
    rows = jnp.where(valid, rows, 0).astype(jnp.int32)
    n_used = bend[-1].astype(jnp.int32).reshape(1)
    return block_e, n_used, n_valid, rows.reshape(n_blocks, 1, GROUPS_PER_BLOCK)


def _experts_body(be_ref, nu_ref, nv_ref, row_ref, rown_ref, as_hbm, w1_ref, w3_ref, w2_ref, ys_hbm,
               x_ref, acc_ref, y_ref, gsem, ssem, *, nf):
    i = pl.program_id(0)
    f = pl.program_id(1)
    n_used = nu_ref[0]
    slot = i % 2

    def gather_copy(s, gidx, row):
        row = pl.multiple_of(row, ROW_GROUP)
        dst = pl.multiple_of(gidx * ROW_GROUP, ROW_GROUP)
        return pltpu.make_async_copy(as_hbm.at[pl.ds(row, ROW_GROUP), :], x_ref.at[s, pl.ds(dst, ROW_GROUP), :],
                                     gsem.at[s])

    def scatter_copy(s, gidx, row):
        row = pl.multiple_of(row, ROW_GROUP)
        src = pl.multiple_of(gidx * ROW_GROUP, ROW_GROUP)
        return pltpu.make_async_copy(y_ref.at[s, pl.ds(src, ROW_GROUP), :], ys_hbm.at[pl.ds(row, ROW_GROUP), :],
                                     ssem.at[s])

    def loop(n, fn):
        def body(r, c):
            fn(r)
            return c
        lax.fori_loop(0, n, body, 0)

    @pl.when(i < n_used)
    def _():
        @pl.when(f == 0)
        def _():
            @pl.when(i == 0)
            def _():
                x_ref[...] = jnp.zeros_like(x_ref)
                loop(nv_ref[0], lambda r: gather_copy(0, r, row_ref[0, r]).start())

            loop(nv_ref[i], lambda r: gather_copy(slot, r, 0).wait())

            @pl.when(i + 1 < n_used)
            def _():
                loop(nv_ref[i + 1], lambda r: gather_copy(1 - slot, r, rown_ref[0, r]).start())

        def swiglu_rows(nrows):
            x = x_ref[slot, 0:nrows, :]
            u = jnp.dot(x, w1_ref[...], preferred_element_type=F32)
            v = jnp.dot(x, w3_ref[...], preferred_element_type=F32)
            p = (_silu(u) * v).astype(BF16)
            y = jnp.dot(p, w2_ref[...], preferred_element_type=F32)

            @pl.when(f == 0)
            def _():
                acc_ref[0:nrows, :] = y

            @pl.when(jnp.logical_and(f > 0, f < nf - 1))
            def _():
                acc_ref[0:nrows, :] += y

            @pl.when(f == nf - 1)
            def _():
                y_ref[slot, 0:nrows, :] = (acc_ref[0:nrows, :] + y).astype(BF16)

        n_real = nv_ref[i] * ROW_GROUP
        for lo, hi in zip((0,) + EXPERT_ROW_BUCKETS[:-1], EXPERT_ROW_BUCKETS):
            @pl.when(jnp.logical_and(n_real > lo, n_real <= hi))
            def _(hi=hi):
                swiglu_rows(hi)

        @pl.when(f == nf - 1)
        def _():
            loop(nv_ref[i], lambda r: scatter_copy(slot, r, row_ref[0, r]).start())

            @pl.when(i > 0)
            def _():
                loop(nv_ref[i - 1], lambda r: scatter_copy(1 - slot, r, 0).wait())

            @pl.when(i == n_used - 1)
            def _():
                loop(nv_ref[i], lambda r: scatter_copy(slot, r, 0).wait())


def moe_experts_sorted(a_sorted, block_e, n_used, n_valid, rows, w1, w3, w2, w_layer, *, tf=1792):
    n_blocks = rows.shape[0]
    tm = MOE_ROWS
    f_dim = w1.shape[3]
    nf = f_dim // tf
    assert nf >= 2
    body = functools.partial(_experts_body, nf=nf)

    def wmap(kind):
        def index_map(i, f, be, nu, nv):
            live = i < nu[0]
            ff = jnp.where(live, f, nf - 1)
            ii = jnp.where(live, i, nu[0] - 1)
            return (w_layer, be[ii], 0, ff) if kind == "up" else (w_layer, be[ii], ff, 0)
        return index_map

    idx_spec = pl.BlockSpec((None, 1, GROUPS_PER_BLOCK), lambda i, f, be, nu, nv: (i, 0, 0), memory_space=pltpu.SMEM)
    next_spec = pl.BlockSpec((None, 1, GROUPS_PER_BLOCK),
                             lambda i, f, be, nu, nv: (jnp.minimum(i + 1, n_blocks - 1), 0, 0),
                             memory_space=pltpu.SMEM)
    grid_spec = pltpu.PrefetchScalarGridSpec(
        num_scalar_prefetch=3,
        grid=(n_blocks, nf),
        in_specs=[idx_spec, next_spec,
                  pl.BlockSpec(memory_space=pl.ANY),
                  pl.BlockSpec((None, None, D_MODEL, tf), wmap("up")),
                  pl.BlockSpec((None, None, D_MODEL, tf), wmap("up")),
                  pl.BlockSpec((None, None, tf, D_MODEL), wmap("down"))],
        out_specs=pl.BlockSpec(memory_space=pl.ANY),
        scratch_shapes=[pltpu.VMEM((2, tm, D_MODEL), BF16), pltpu.VMEM((tm, D_MODEL), F32),
                        pltpu.VMEM((2, tm, D_MODEL), BF16),
                        pltpu.SemaphoreType.DMA((2,)), pltpu.SemaphoreType.DMA((2,))],
    )
    return pl.pallas_call(
        body,
        grid_spec=grid_spec,
        out_shape=jax.ShapeDtypeStruct(a_sorted.shape, BF16),
        input_output_aliases={5: 0},
        compiler_params=_cparams("arbitrary", "arbitrary"),
        name="moe_experts",
    )(block_e, n_used, n_valid, rows, rows, a_sorted, w1, w3, w2)


def _unsort_body(h_ref, ys_ref, meta_ref, mod_ref, fg_ref, *rest, tm, rows_per_batch, final):
    o_ref = rest[-1] if len(rest) == 1 else rest[-2]
    i = pl.program_id(0)
    row = ((i * tm) // rows_per_batch) if rows_per_batch else CTX_MOD_ROW
    meta = meta_ref[...]
    slot0 = meta[:, META_SLOT0:META_SLOT0 + 1]
    slot1 = meta[:, META_SLOT1:META_SLOT1 + 1]
    gate0 = meta[:, META_GATE0:META_GATE0 + 1]
    gate1 = meta[:, META_GATE1:META_GATE1 + 1]
    j_io = lax.broadcasted_iota(jnp.int32, (tm, SORT_SLOTS), 1).astype(F32)
    pick = jnp.where(j_io == slot0, gate0, jnp.where(j_io == slot1, gate1, 0.0)).astype(BF16)
    fsum = jnp.dot(pick, ys_ref[...], preferred_element_type=F32)
    out = h_ref[...] + _mod_chunk(mod_ref, row, 5) * fsum
    if final:
        ms = jnp.mean(out * out, axis=-1, keepdims=True)
        out = out * lax.rsqrt(ms + NORM_EPS) * fg_ref[...]
    o_ref[...] = out
    if len(rest) > 1:
        mod2_ref, g2_ref, w_ref, b_ref, _, p_ref = rest
        _nm_project(out, mod2_ref, row, g2_ref, w_ref, b_ref, p_ref, 0, 1)


def moe_unsort_combine(h, ys, meta, mods, layer, final_g, proj=None, *, tile_off, rows_per_batch, final):
    tm = SORT_TOKENS
    m = h.shape[0]
    body = functools.partial(_unsort_body, tm=tm, rows_per_batch=rows_per_batch, final=final)
    in_specs = [pl.BlockSpec((tm, D_MODEL), lambda i: (i, 0)),
                pl.BlockSpec((SORT_SLOTS, D_MODEL), lambda i: (tile_off + i, 0)),
                pl.BlockSpec((tm, LANES), lambda i: (tile_off + i, 0)),
                _mod_spec(layer),
                pl.BlockSpec((1, D_MODEL), lambda i: (0, 0))]
    args = [h, ys, meta, mods, final_g.reshape(1, D_MODEL)]
    out_specs = pl.BlockSpec((tm, D_MODEL), lambda i: (i, 0))
    out_shape = jax.ShapeDtypeStruct((m, D_MODEL), F32)
    if proj is not None:
        g2, w, bias = proj
        n = w.shape[1]
        in_specs += [_mod_spec(layer + 1),
                     pl.BlockSpec((1, D_MODEL), lambda i: (0, 0)),
                     _resident((D_MODEL, n), lambda i: (0, 0)),
                     pl.BlockSpec((1, n), lambda i: (0, 0))]
        args += [mods, g2.reshape(1, D_MODEL), w, bias.reshape(1, n)]
        out_specs = [out_specs, pl.BlockSpec((tm, n), lambda i: (i, 0))]
        out_shape = [out_shape, jax.ShapeDtypeStruct((m, n), BF16)]
    return pl.pallas_call(
        body,
        grid=(m // tm,),
        in_specs=in_specs,
        out_specs=out_specs,
        out_shape=out_shape,
        compiler_params=_cparams("arbitrary"),
        name="moe_combine",
    )(*args)


def _na_layer(h_lat, h_ctx, mods, layer, g, w_qkv, b_qkv, rpb, w_out, b_out, ffn, defer_out, with_ctx_out):
    b, seq, _ = h_lat.shape
    c = h_ctx.shape[1]
    qscale = jnp.concatenate([jnp.full((D_MODEL,), (D_MODEL // NA_HEADS) ** -0.5, F32), jnp.ones((2 * D_MODEL,), F32)])
    w = (w_qkv * qscale).astype(BF16)
    bias = b_qkv * qscale
    qkv = nm_matmul(h_lat.reshape(b * seq, D_MODEL), mods, layer, g, w, bias, rows_per_batch=seq, sh=0, sc=1)
    qkv_c = nm_matmul(h_ctx.reshape(b * c, D_MODEL), mods, layer, g, w, bias, rows_per_batch=None, sh=0, sc=1)
    qkv = qkv.reshape(b, seq, 3 * D_MODEL)
    qkv_c = qkv_c.reshape(b, c, 3 * D_MODEL)
    o_lat = na_attention(qkv, qkv_c, na_bias_table(rpb))
    wo = w_out.astype(BF16)
    pending = None
    if defer_out:
        pending = (o_lat.reshape(b * seq, D_MODEL), wo, b_out)
    else:
        h_lat = mm_residual(o_lat.reshape(b * seq, D_MODEL), h_lat.reshape(b * seq, D_MODEL), mods, layer, wo, b_out,
                            ffn, rows_per_batch=seq, gate=2).reshape(b, seq, D_MODEL)
    if with_ctx_out:
        o_ctx = ctx_attention(qkv_c)
        h_ctx = mm_residual(o_ctx.reshape(b * c, D_MODEL), h_ctx.reshape(b * c, D_MODEL), mods, layer, wo, b_out,
                            ffn, rows_per_batch=None, gate=2).reshape(b, c, D_MODEL)
    return h_lat, h_ctx, pending


def mlstm_up_proj(w_up):
    return w_up.astype(BF16), jnp.zeros((w_up.shape[1],), F32)


def _mlstm_layer(h_lat, h_ctx, mods, layer, g, w_up, conv_w, conv_b, w_q, w_k, w_v, w_gates, b_gates,
                 gn_w, skip, w_down, ffn, ups, with_ctx_out):
    b, seq, _ = h_lat.shape
    c = h_ctx.shape[1]
    inner = MLSTM_INNER
    wq_bd = block_diag_weights(w_q).astype(BF16)
    wk_bd = block_diag_weights(w_k).astype(BF16)
    wkt_bd = jnp.swapaxes(wk_bd, 1, 2)
    wv_bd = block_diag_weights(w_v).astype(BF16)
    gxc, gxm = fold_gate_weights(w_q, w_k, w_v, w_gates)
    wd = w_down.astype(BF16)

    def features(h, up, rows_per_batch):
        n, s, _ = h.shape
        if up is None:
            up = nm_matmul(h.reshape(n * s, D_MODEL), mods, layer, g, *mlstm_up_proj(w_up),
                           rows_per_batch=rows_per_batch, sh=0, sc=1)
        up = up.reshape(n, s, 2 * inner)
        return up, mlstm_features(up, conv_w, conv_b, wq_bd, wk_bd, wkt_bd, wv_bd, gxc, gxm, b_gates)

    up_l, up_c = ups if ups is not None else (None, None)
    up_c, (q_c, k_c, kt_c, v_c, xc_c, g_c, gt_c) = features(h_ctx, up_c, None)
    up_l, (q_l, k_l, kt_l, v_l, xc_l, g_l, gt_l) = features(h_lat, up_l, seq)
    hf_c, cf, mf = mlstm_scan(q_c, k_c, kt_c, v_c, g_c, gt_c, None, rev=False)
    hb_c, cb, mb = mlstm_scan(q_c, k_c, kt_c, v_c, g_c, gt_c, None, rev=True)
    hf_l, _, _ = mlstm_scan(q_l, k_l, kt_l, v_l, g_l, gt_l, (cf, mf), rev=False)
    hb_l, _, _ = mlstm_scan(q_l, k_l, kt_l, v_l, g_l, gt_l, (cb, mb), rev=True)
    flat = lambda a: a.reshape(-1, a.shape[-1])
    h_lat = mlstm_output(flat(hf_l), flat(hb_l), flat(xc_l), flat(up_l), flat(h_lat), mods, layer, gn_w, skip, wd,
                         ffn, rows_per_batch=seq).reshape(b, seq, D_MODEL)
    if with_ctx_out:
        h_ctx = mlstm_output(flat(hf_c), flat(hb_c), flat(xc_c), flat(up_c), flat(h_ctx), mods, layer, gn_w, skip, wd,
                             ffn, rows_per_batch=None).reshape(b, c, D_MODEL)
    return h_lat, h_ctx


def _moe_layer(h_lat, h_ctx, mods, layer, g, w_router, w1, w3, w2, w_layer, final_g, proj, pending, last):
    b, seq, _ = h_lat.shape
    c = h_ctx.shape[1]
    wrt = jnp.pad(w_router.T, ((0, EXPERT_ROWS - N_EXPERTS), (0, 0)))
    hl = h_lat.reshape(b * seq, D_MODEL)
    hc = None if last else h_ctx.reshape(b * c, D_MODEL)
    a_sorted, meta, cnt, *updated = moe_sort(hl, hc, mods, layer, g, wrt, pending, rows_per_batch=seq)
    if updated:
        hl = updated[0]
    block_e, n_used, n_valid, rows = moe_group_table(cnt)
    ys = moe_experts_sorted(a_sorted, block_e, n_used, n_valid, rows, w1, w3, w2, w_layer)
    res_l = moe_unsort_combine(hl, ys, meta, mods, layer, final_g, proj, tile_off=0, rows_per_batch=seq, final=last)
    res_c = None
    if not last:
        res_c = moe_unsort_combine(hc, ys, meta, mods, layer, final_g, proj, tile_off=(b * seq) // SORT_TOKENS,
                                   rows_per_batch=None, final=False)
    if proj is None:
        h_lat = res_l.reshape(b, seq, D_MODEL)
        h_ctx = h_ctx if res_c is None else res_c.reshape(b, c, D_MODEL)
        return h_lat, h_ctx, None
    return res_l[0].reshape(b, seq, D_MODEL), res_c[0].reshape(b, c, D_MODEL), (res_l[1], res_c[1])


def kernel(x, c, ctx, c_ctx, w_mod, b_mod, norm_g, final_g, na_w_qkv, na_b_qkv, na_rpb, na_w_out, na_b_out,
           pool_w, pool_scale, ml_w_up, ml_conv_w, ml_conv_b, ml_w_q, ml_w_k, ml_w_v, ml_w_gates, ml_b_gates,
           ml_gn_w, ml_skip, ml_w_down, ffn_w1, ffn_w3, ffn_w2, moe_w_router, moe_w1, moe_w3, moe_w2):
    b, seq, _ = x.shape
    n_ctx = ctx.shape[1]
    depth = w_mod.shape[0]
    assert b <= CTX_MOD_ROW
    cond = jnp.zeros((MOD_ROWS, D_MODEL), F32).at[:b].set(c).at[CTX_MOD_ROW].set(c_ctx)
    mods = adaln_all(cond, w_mod, b_mod)
    moe_w1b, moe_w3b, moe_w2b = moe_w1.astype(BF16), moe_w3.astype(BF16), moe_w2.astype(BF16)
    h_lat, h_ctx = x, ctx
    ups = None
    for i in range(depth):
        last = i == depth - 1
        kind = i % 3
        j = i // 3
        g_tok = norm_g[i, 0]
        e = i // 2
        g_ch = norm_g[i, 1]
        dense = i % 2 == 0
        ffn = (g_ch, ffn_w1[e].astype(BF16), ffn_w3[e].astype(BF16), ffn_w2[e].astype(BF16)) if dense else None
        pending = None
        if kind == 0:
            h_lat, h_ctx, pending = _na_layer(h_lat, h_ctx, mods, i, g_tok, na_w_qkv[j], na_b_qkv[j], na_rpb[j],
                                              na_w_out[j], na_b_out[j], ffn, last and not dense, not last)
        elif kind == 1:
            wp = pool_w[j].astype(BF16)
            h_lat = pool_mixer(h_lat, mods, i, g_tok, wp, pool_scale[j], is_ctx=False)
            if not last:
                h_ctx = pool_mixer(h_ctx, mods, i, g_tok, wp, pool_scale[j], is_ctx=True)
        else:
            h_lat, h_ctx = _mlstm_layer(h_lat, h_ctx, mods, i, g_tok, ml_w_up[j], ml_conv_w[j], ml_conv_b[j],
                                        ml_w_q[j], ml_w_k[j], ml_w_v[j], ml_w_gates[j], ml_b_gates[j],
                                        ml_gn_w[j], ml_skip[j], ml_w_down[j], ffn, ups, not last)
            ups = None
        if dense:
            if kind == 1:
                h_lat = ffn_dense(h_lat.reshape(b * seq, D_MODEL), mods, i, *ffn,
                                  rows_per_batch=seq).reshape(b, seq, D_MODEL)
                if not last:
                    h_ctx = ffn_dense(h_ctx.reshape(b * n_ctx, D_MODEL), mods, i, *ffn,
                                      rows_per_batch=None).reshape(b, n_ctx, D_MODEL)
        else:
            proj = None
            if not last and (i + 1) % 3 == 2:
                proj = (norm_g[i + 1, 0],) + mlstm_up_proj(ml_w_up[(i + 1) // 3])
            h_lat, h_ctx, ups = _moe_layer(h_lat, h_ctx, mods, i, g_ch, moe_w_router[e], moe_w1b, moe_w3b, moe_w2b, e,
                                           final_g, proj, pending, last)
    return h_lat
```
